```python
import jax, jax.numpy as jnp
from jax import lax
import numpy as np

D_MODEL = 1024
BATCH = 8
SEQ = 2048
DEPTH = 4

ROPE_THETA = 10000.0
ROPE_DIM = 64
EPS = 1e-6
NEG_INF = -1e30
ADA_CHUNKS = 6

MLA_HEADS = 8
MLA_Q_RANK = 768
MLA_KV_RANK = 512
MLA_NOPE = 128
MLA_ROPE = ROPE_DIM
MLA_V = 128
MLA_Q_BLOCK = 128

DIL_PAIRS = ((128, 1), (512, 4), (2048, 16))
DIL_GROUPS = len(DIL_PAIRS)
DIL_HEADS = 8
DIL_HEAD_DIM = ROPE_DIM
DIL_BLOCK = 64
DIL_W = DIL_GROUPS * DIL_HEADS * DIL_HEAD_DIM

RET_HEADS = 8
RET_QK_DIM = ROPE_DIM
RET_V_DIM = 2 * RET_QK_DIM
RET_CHUNK = 128
RET_QK_W = RET_HEADS * RET_QK_DIM
RET_V_W = RET_HEADS * RET_V_DIM

IN_SPLITS = (MLA_Q_RANK, MLA_KV_RANK, MLA_ROPE,
             DIL_W, DIL_W, DIL_W,
             RET_QK_W, RET_QK_W, RET_V_W, RET_V_W,
             D_MODEL, D_MODEL, D_MODEL)
IN_COLS = sum(IN_SPLITS)

D_FF = 2816
N_EXPERTS = 8
TOP_K = 2
D_FF_EXPERT = 3584
MOE_BLOCK = 256
N_DENSE = (DEPTH + 1) // 2
N_MOE = DEPTH // 2

kernel_name = 'hybrid_mla_dilated_retention_moe_encoder'


def rmsnorm(x, g):
    xf = x.astype(jnp.float32)
    y = xf * lax.rsqrt(jnp.mean(xf * xf, axis=-1, keepdims=True) + EPS)
    return (y * g.astype(jnp.float32)).astype(x.dtype)


def rope_tables(positions):
    inv_freq = ROPE_THETA ** (-jnp.arange(0, ROPE_DIM, 2, dtype=jnp.float32) / ROPE_DIM)
    ang = positions.astype(jnp.float32)[..., None] * inv_freq
    return jnp.cos(ang), jnp.sin(ang)


def apply_rope(t, cos, sin):
    c = cos[:, :, None, :].astype(t.dtype)
    s = sin[:, :, None, :].astype(t.dtype)
    t1, t2 = jnp.split(t, 2, axis=-1)
    return jnp.concatenate([t1 * c - t2 * s, t2 * c + t1 * s], axis=-1)


def split_columns(t, sizes):
    out, start = [], 0
    for n in sizes:
        out.append(t[..., start:start + n])
        start += n
    return out


def mla_attention(q_nope, q_rope, k_nope, k_rope, v):
    B_, S_, H, _ = q_nope.shape
    nqb = S_ // MLA_Q_BLOCK
    scale = (MLA_NOPE + MLA_ROPE) ** -0.5

    def blocks(t):
        return jnp.moveaxis(t.reshape((B_, nqb, MLA_Q_BLOCK) + t.shape[2:]), 1, 0)

    def attend(qs):
        qn, qr = qs
        s = (jnp.einsum('bqhd,bkhd->bhqk', qn, k_nope)
             + jnp.einsum('bqhr,bkr->bhqk', qr, k_rope))
        p = jax.nn.softmax(s.astype(jnp.float32) * scale, axis=-1).astype(v.dtype)
        return jnp.einsum('bhqk,bkhd->bqhd', p, v)

    o = lax.map(attend, (blocks(q_nope), blocks(q_rope)))
    return jnp.moveaxis(o, 0, 1).reshape(B_, S_, H * MLA_V)


def dilated_attention(q, k, v, dilation, band):
    B_, S_, H, dh = q.shape
    L = S_ // dilation
    nb = -(-L // DIL_BLOCK)
    Lp = nb * DIL_BLOCK

    def sub(t):
        t = t.reshape(B_, L, dilation, H, dh).transpose(0, 2, 1, 3, 4)
        return jnp.pad(t, ((0, 0), (0, 0), (0, Lp - L), (0, 0), (0, 0)))

    def neighbours(t):
        tp = jnp.pad(t, ((0, 0), (0, 0), (DIL_BLOCK, DIL_BLOCK), (0, 0), (0, 0)))
        tp = tp.reshape(B_, dilation, nb + 2, DIL_BLOCK, H, dh)
        return jnp.concatenate([tp[:, :, :-2], tp[:, :, 1:-1], tp[:, :, 2:]], axis=3)

    qb = sub(q).reshape(B_, dilation, nb, DIL_BLOCK, H, dh)
    kw = neighbours(sub(k))
    vw = neighbours(sub(v))
    jq = jnp.arange(nb)[:, None] * DIL_BLOCK + jnp.arange(DIL_BLOCK)[None, :]
    jk = (jnp.arange(nb)[:, None] - 1) * DIL_BLOCK + jnp.arange(3 * DIL_BLOCK)[None, :]
    valid = ((jnp.abs(jq[:, :, None] - jk[:, None, :]) <= band)
             & (jk[:, None, :] >= 0) & (jk[:, None, :] < L))
    s = jnp.einsum('brnqhd,brnkhd->brnhqk', qb, kw).astype(jnp.float32) * (dh ** -0.5)
    s = jnp.where(valid[None, None, :, None], s, NEG_INF)
    lse = jax.nn.logsumexp(s, axis=-1, keepdims=True)
    p = jnp.exp(s - lse).astype(v.dtype)
    o = jnp.einsum('brnhqk,brnkhd->brnqhd', p, vw)
    lse = lse[..., 0].transpose(0, 1, 2, 4, 3)

    def unsub(t):
        t = t.reshape((B_, dilation, Lp) + t.shape[4:])[:, :, :L]
        t = jnp.moveaxis(t, 1, 2)
        return t.reshape((B_, S_) + t.shape[3:])

    return unsub(o), unsub(lse)


def retention_scan(q, k, v, log_gamma, inclusive):
    B_, S_, H, dk = q.shape
    dv = v.shape[-1]
    C = RET_CHUNK
    nc = S_ // C

    def chunks(t):
        return t.reshape(B_, nc, C, H, t.shape[-1]).transpose(1, 0, 3, 2, 4)

    idx = jnp.arange(C, dtype=jnp.float32)
    diff = idx[:, None] - idx[None, :]
    mask = (diff >= 0) if inclusive else (diff > 0)
    intra = jnp.where(mask, jnp.exp(log_gamma[:, None, None] * jnp.where(mask, diff, 0.0)), 0.0)
    q_dec = jnp.exp(log_gamma[:, None] * (idx + 1.0))[:, :, None]
    k_dec = jnp.exp(log_gamma[:, None] * (C - 1.0 - idx))[:, :, None]
    c_dec = jnp.exp(log_gamma * C)[:, None, None]

    def step(state, inp):
        qc, kc, vc = inp
        s = jnp.einsum('bhqd,bhkd->bhqk', qc, kc) * intra
        y = (jnp.einsum('bhqk,bhkv->bhqv', s, vc)
             + jnp.einsum('bhqd,bhdv->bhqv', qc, state) * q_dec)
        state = c_dec * state + jnp.einsum('bhkd,bhkv->bhdv', kc * k_dec, vc)
        return state, y

    init = jnp.zeros((B_, H, dk, dv), jnp.float32)
    _, ys = lax.scan(step, init, (chunks(q), chunks(k), chunks(v)))
    return ys.transpose(1, 0, 3, 2, 4).reshape(B_, S_, H, dv)


def token_mixer(h, cos, sin, w_in, q_norm, w_uq, kv_norm, w_ukv, ret_log_decay, ret_norm,
                w_br_mla, w_br_dil, w_br_ret, w_out):
    B_, S_, _ = h.shape
    proj = jnp.einsum('bsd,dn->bsn', h, w_in)
    (cq, ckv, kr, dq, dk, dv, rq, rk, rv, rg, ga, gb, gc) = split_columns(proj, IN_SPLITS)

    cq = rmsnorm(cq, q_norm)
    ckv = rmsnorm(ckv, kv_norm)
    q = jnp.einsum('bsr,rn->bsn', cq, w_uq).reshape(B_, S_, MLA_HEADS, MLA_NOPE + MLA_ROPE)
    q_nope = q[..., :MLA_NOPE]
    q_rope = apply_rope(q[..., MLA_NOPE:], cos, sin)
    kv = jnp.einsum('bsr,rn->bsn', ckv, w_ukv).reshape(B_, S_, MLA_HEADS, MLA_NOPE + MLA_V)
    k_nope, v_mla = kv[..., :MLA_NOPE], kv[..., MLA_NOPE:]
    k_rope = apply_rope(kr[:, :, None, :], cos, sin)[:, :, 0]
    y_mla = mla_attention(q_nope, q_rope, k_nope, k_rope, v_mla)

    n_dh = DIL_GROUPS * DIL_HEADS
    qd = apply_rope(dq.reshape(B_, S_, n_dh, DIL_HEAD_DIM), cos, sin)
    kd = apply_rope(dk.reshape(B_, S_, n_dh, DIL_HEAD_DIM), cos, sin)
    qd = qd.reshape(B_, S_, DIL_GROUPS, DIL_HEADS, DIL_HEAD_DIM)
    kd = kd.reshape(B_, S_, DIL_GROUPS, DIL_HEADS, DIL_HEAD_DIM)
    vd = dv.reshape(B_, S_, DIL_GROUPS, DIL_HEADS, DIL_HEAD_DIM)
    outs, lses = [], []
    for g, (window, dilation) in enumerate(DIL_PAIRS):
        o, lse = dilated_attention(qd[:, :, g], kd[:, :, g], vd[:, :, g], dilation, window // (2 * dilation))
        outs.append(o)
        lses.append(lse)
    outs = jnp.stack(outs)
    wts = jax.nn.softmax(jnp.stack(lses), axis=0)
    y_dil = jnp.sum(wts[..., None].astype(outs.dtype) * outs, axis=0).reshape(B_, S_, DIL_HEADS * DIL_HEAD_DIM)

    rq = apply_rope(rq.reshape(B_, S_, RET_HEADS, RET_QK_DIM), cos, sin).astype(jnp.float32)
    rk = (apply_rope(rk.reshape(B_, S_, RET_HEADS, RET_QK_DIM), cos, sin).astype(jnp.float32)
          * (RET_QK_DIM ** -0.5))
    rv = rv.reshape(B_, S_, RET_HEADS, RET_V_DIM).astype(jnp.float32)
    ld = ret_log_decay.astype(jnp.float32)
    fwd = retention_scan(rq, rk, rv, ld[0], True)
    bwd = jnp.flip(retention_scan(jnp.flip(rq, 1), jnp.flip(rk, 1), jnp.flip(rv, 1), ld[1], False), 1)
    yr = fwd + bwd
    mu = jnp.mean(yr, axis=-1, keepdims=True)
    var = jnp.mean(jnp.square(yr - mu), axis=-1, keepdims=True)
    yr = ((yr - mu) * lax.rsqrt(var + EPS)).reshape(B_, S_, RET_V_W) * ret_norm.astype(jnp.float32)
    y_ret = (jax.nn.silu(rg.astype(jnp.float32)) * yr).astype(h.dtype)

    merged = (jax.nn.sigmoid(ga) * jnp.dot(y_mla, w_br_mla)
              + jax.nn.sigmoid(gb) * jnp.dot(y_dil, w_br_dil)
              + jax.nn.sigmoid(gc) * jnp.dot(y_ret, w_br_ret))
    return jnp.dot(merged, w_out)


def swiglu(h, w1, w3, w2):
    return jnp.dot(jax.nn.silu(jnp.dot(h, w1)) * jnp.dot(h, w3), w2)


def moe_swiglu(h, w_router, w1, w3, w2):
    B_, S_, D = h.shape
    t = h.reshape(-1, D)
    T = t.shape[0]
    logits = jnp.dot(t, w_router).astype(jnp.float32)
    top_val, top_idx = lax.top_k(logits, TOP_K)
    top_w = jax.nn.softmax(top_val, axis=-1)
    A = T * TOP_K
    e_flat = top_idx.reshape(-1)
    tok_flat = jnp.arange(A, dtype=jnp.int32) // TOP_K
    w_flat = top_w.reshape(-1)
    order = jnp.argsort(e_flat)
    e_sorted = e_flat[order]
    counts = jnp.bincount(e_flat, length=N_EXPERTS)
    starts = jnp.cumsum(counts) - counts
    padded = (counts + MOE_BLOCK - 1) // MOE_BLOCK * MOE_BLOCK
    pends = jnp.cumsum(padded)
    pstarts = pends - padded
    dest = pstarts[e_sorted] + (jnp.arange(A) - starts[e_sorted])
    NB = -(-A // MOE_BLOCK) + N_EXPERTS
    R = NB * MOE_BLOCK
    row_tok = jnp.zeros((R,), jnp.int32).at[dest].set(tok_flat[order])
    row_w = jnp.zeros((R,), jnp.float32).at[dest].set(w_flat[order])
    block_expert = jnp.minimum(jnp.searchsorted(pends, jnp.arange(NB) * MOE_BLOCK, side='right'), N_EXPERTS - 1)

    def expert_block(args):
        tok, e = args
        xb = t[tok]
        hb = jax.nn.silu(jnp.dot(xb, w1[e])) * jnp.dot(xb, w3[e])
        return jnp.dot(hb, w2[e])

    yb = lax.map(expert_block, (row_tok.reshape(NB, MOE_BLOCK), block_expert)).reshape(R, D)
    y = jax.ops.segment_sum(yb * row_w[:, None].astype(yb.dtype), row_tok, num_segments=T)
    return y.reshape(B_, S_, D)


def setup_inputs(seed: int = 0) -> dict:
    key = jax.random.key(seed)
    keys = iter(jax.random.split(key, 32))
    f32 = jnp.float32

    def w(shape, fan_in, scale=1.0):
        return jax.random.normal(next(keys), shape, f32) * (scale * fan_in ** -0.5)

    def gain(shape):
        return 1.0 + 0.05 * jax.random.normal(next(keys), shape, f32)

    x = jax.random.normal(next(keys), (BATCH, SEQ, D_MODEL), f32)
    c = jax.random.normal(next(keys), (BATCH, D_MODEL), f32)
    positions = (jnp.arange(SEQ, dtype=jnp.int32)[None, :]
                 + jax.random.randint(next(keys), (BATCH, 1), 0, 4096, dtype=jnp.int32))
    ada_w = w((DEPTH, D_MODEL, ADA_CHUNKS * D_MODEL), D_MODEL, 0.5)
    ada_b = 0.01 * jax.random.normal(next(keys), (DEPTH, ADA_CHUNKS * D_MODEL), f32)
    norm_mix = gain((DEPTH, D_MODEL))
    norm_ffn = gain((DEPTH, D_MODEL))
    w_in = w((DEPTH, D_MODEL, IN_COLS), D_MODEL)
    mla_q_norm = gain((DEPTH, MLA_Q_RANK))
    mla_w_uq = w((DEPTH, MLA_Q_RANK, MLA_HEADS * (MLA_NOPE + MLA_ROPE)), MLA_Q_RANK)
    mla_kv_norm = gain((DEPTH, MLA_KV_RANK))
    mla_w_ukv = w((DEPTH, MLA_KV_RANK, MLA_HEADS * (MLA_NOPE + MLA_V)), MLA_KV_RANK)
    expo = -(5.0 + jnp.arange(RET_HEADS, dtype=f32)) + 0.1 * jax.random.normal(next(keys), (DEPTH, 2, RET_HEADS), f32)
    ret_log_decay = jnp.log1p(-jnp.exp2(expo))
    ret_norm = gain((DEPTH, RET_V_W))
    w_br_mla = w((DEPTH, MLA_HEADS * MLA_V, D_MODEL), MLA_HEADS * MLA_V)
    w_br_dil = w((DEPTH, DIL_HEADS * DIL_HEAD_DIM, D_MODEL), DIL_HEADS * DIL_HEAD_DIM)
    w_br_ret = w((DEPTH, RET_V_W, D_MODEL), RET_V_W)
    w_out = w((DEPTH, D_MODEL, D_MODEL), D_MODEL)
    ffn_w1 = w((N_DENSE, D_MODEL, D_FF), D_MODEL)
    ffn_w3 = w((N_DENSE, D_MODEL, D_FF), D_MODEL)
    ffn_w2 = w((N_DENSE, D_FF, D_MODEL), D_FF)
    moe_router = w((N_MOE, D_MODEL, N_EXPERTS), D_MODEL)
    moe_w1 = w((N_MOE, N_EXPERTS, D_MODEL, D_FF_EXPERT), D_MODEL)
    moe_w3 = w((N_MOE, N_EXPERTS, D_MODEL, D_FF_EXPERT), D_MODEL)
    moe_w2 = w((N_MOE, N_EXPERTS, D_FF_EXPERT, D_MODEL), D_FF_EXPERT)
    final_norm = gain((D_MODEL,))
    return {'x': x, 'c': c, 'positions': positions, 'ada_w': ada_w, 'ada_b': ada_b,
            'norm_mix': norm_mix, 'norm_ffn': norm_ffn, 'w_in': w_in,
            'mla_q_norm': mla_q_norm, 'mla_w_uq': mla_w_uq, 'mla_kv_norm': mla_kv_norm, 'mla_w_ukv': mla_w_ukv,
            'ret_log_decay': ret_log_decay, 'ret_norm': ret_norm,
            'w_br_mla': w_br_mla, 'w_br_dil': w_br_dil, 'w_br_ret': w_br_ret, 'w_out': w_out,
            'ffn_w1': ffn_w1, 'ffn_w3': ffn_w3, 'ffn_w2': ffn_w2,
            'moe_router': moe_router, 'moe_w1': moe_w1, 'moe_w3': moe_w3, 'moe_w2': moe_w2,
            'final_norm': final_norm}


def reference(x, c, positions, ada_w, ada_b, norm_mix, norm_ffn, w_in,
              mla_q_norm, mla_w_uq, mla_kv_norm, mla_w_ukv, ret_log_decay, ret_norm,
              w_br_mla, w_br_dil, w_br_ret, w_out, ffn_w1, ffn_w3, ffn_w2,
              moe_router, moe_w1, moe_w3, moe_w2, final_norm):
    cos, sin = rope_tables(positions)
    c_act = jax.nn.silu(c)
    for layer in range(DEPTH):
        mod = jnp.dot(c_act, ada_w[layer]) + ada_b[layer]
        shift1, scale1, gate1, shift2, scale2, gate2 = jnp.split(mod[:, None, :], ADA_CHUNKS, axis=-1)
        h = rmsnorm(x, norm_mix[layer]) * (1.0 + scale1) + shift1
        x = x + gate1 * token_mixer(h, cos, sin, w_in[layer], mla_q_norm[layer], mla_w_uq[layer],
                                    mla_kv_norm[layer], mla_w_ukv[layer], ret_log_decay[layer],
                                    ret_norm[layer], w_br_mla[layer], w_br_dil[layer],
                                    w_br_ret[layer], w_out[layer])
        h = rmsnorm(x, norm_ffn[layer]) * (1.0 + scale2) + shift2
        if layer % 2 == 0:
            i = layer // 2
            f = swiglu(h, ffn_w1[i], ffn_w3[i], ffn_w2[i])
        else:
            i = layer // 2
            f = moe_swiglu(h, moe_router[i], moe_w1[i], moe_w3[i], moe_w2[i])
        x = x + gate2 * f
    return rmsnorm(x, final_norm)
```

```python
import functools

import jax
import jax.numpy as jnp
from jax import lax
from jax.experimental import pallas as pl
from jax.experimental.pallas import tpu as pltpu

F32 = jnp.float32
BF16 = jnp.bfloat16

EPS = 1e-6
NEG_INF = -1e30
ROPE_THETA = 10000.0
ROPE_DIM = 64
ADA_CHUNKS = 6

MLA_HEADS = 8
MLA_Q_RANK = 768
MLA_KV_RANK = 512
MLA_NOPE = 128
MLA_ROPE = ROPE_DIM
MLA_V = 128
MLA_QK = MLA_NOPE + MLA_ROPE

DIL_PAIRS = ((128, 1), (512, 4), (2048, 16))
DIL_HEADS = 8
DIL_HEAD_DIM = ROPE_DIM
DIL_GROUP_W = DIL_HEADS * DIL_HEAD_DIM
DIL_QBLOCK = 128

RET_HEADS = 8
RET_QK_DIM = ROPE_DIM
RET_V_DIM = 2 * RET_QK_DIM
RET_CHUNK = 128
RET_QK_W = RET_HEADS * RET_QK_DIM
RET_V_W = RET_HEADS * RET_V_DIM

N_EXPERTS = 8
TOP_K = 2
MOE_BLOCK = 256

LANES = 128
V7X_VMEM_LIMIT_BYTES = 56 * 1024 * 1024

ROW_TILE = 512
MERGE_TILE = 256
COMBINE_TILE = 256
MLA_Q_TILE = 512


def _params(n_grid_dims):
    return pltpu.CompilerParams(
        dimension_semantics=("arbitrary",) * n_grid_dims,
        vmem_limit_bytes=V7X_VMEM_LIMIT_BYTES,
    )


def _resident(shape):
    zeros = (0,) * len(shape)
    return pl.BlockSpec(shape, lambda *_: zeros, pipeline_mode=pl.Buffered(1))


def _silu(v):
    return v * jax.nn.sigmoid(v)


def _norm_mod(x, gain, scale, shift):
    ms = jnp.mean(x * x, axis=-1, keepdims=True)
    y = x * lax.rsqrt(ms + EPS) * gain
    return y * (1.0 + scale) + shift


def _rms(v, gain):
    ms = jnp.mean(v * v, axis=-1, keepdims=True)
    return v * lax.rsqrt(ms + EPS) * gain


def _rope(t, cos_t, sin_t):
    lane = lax.broadcasted_iota(jnp.int32, (t.shape[0], LANES), 1)
    first_half = (lane % ROPE_DIM) < (ROPE_DIM // 2)
    out = []
    for j in range(t.shape[1] // LANES):
        c = t[:, j * LANES:(j + 1) * LANES]
        rot = jnp.where(first_half,
                        pltpu.roll(c, LANES - ROPE_DIM // 2, 1),
                        pltpu.roll(c, ROPE_DIM // 2, 1))
        out.append(c * cos_t + rot * sin_t)
    return out[0] if len(out) == 1 else jnp.concatenate(out, axis=1)


def _dot(a, b):
    return jnp.dot(a, b, preferred_element_type=F32)


def _dot_nt(a, b):
    return lax.dot_general(a, b, (((1,), (1,)), ((), ())), preferred_element_type=F32)


def _dot_tn(a, b):
    return lax.dot_general(a, b, (((0,), (0,)), ((), ())), preferred_element_type=F32)


def _ada_kernel(c_ref, w_ref, b_ref, o_ref):
    o_ref[0] = jnp.dot(_silu(c_ref[...]), w_ref[0], preferred_element_type=F32,
                       precision=lax.Precision.HIGHEST) + b_ref[0]


def _ada(c, ada_w, ada_b):
    depth, d_model, n = ada_w.shape
    batch = c.shape[0]
    tn = n // 4
    out = pl.pallas_call(
        _ada_kernel,
        grid=(depth, n // tn),
        in_specs=[
            pl.BlockSpec((batch, d_model), lambda l, j: (0, 0)),
            pl.BlockSpec((1, d_model, tn), lambda l, j: (l, 0, j)),
            pl.BlockSpec((1, 1, tn), lambda l, j: (l, 0, j)),
        ],
        out_specs=pl.BlockSpec((1, batch, tn), lambda l, j: (l, 0, j)),
        out_shape=jax.ShapeDtypeStruct((depth, batch, n), F32),
        compiler_params=_params(2),
        name="ada",
    )(c, ada_w, ada_b.reshape(depth, 1, n))
    return out.reshape(depth, batch, ADA_CHUNKS, d_model)


def _mla_prep_kernel(x_ref, mod_ref, g_ref, wa_ref, qn_ref, kvn_ref, wuq_ref, wukv_ref, cos_ref, sin_ref,
                     q_ref, k_ref, v_ref):
    mod = mod_ref[0]
    h = _norm_mod(x_ref[...], g_ref[...], mod[1:2], mod[0:1]).astype(BF16)
    p = _dot(h, wa_ref[...])
    cq = _rms(p[:, :MLA_Q_RANK], qn_ref[...]).astype(BF16)
    ckv = _rms(p[:, MLA_Q_RANK:MLA_Q_RANK + MLA_KV_RANK], kvn_ref[...]).astype(BF16)
    cos_t = cos_ref[0]
    sin_t = sin_ref[0]
    kr = _rope(p[:, MLA_Q_RANK + MLA_KV_RANK:], cos_t, sin_t)
    q = _dot(cq, wuq_ref[...])
    kv = _dot(ckv, wukv_ref[...])
    nope_w = MLA_HEADS * MLA_NOPE
    qr = _rope(q[:, nope_w:], cos_t, sin_t)
    scale = MLA_QK ** -0.5
    kr_b = kr[:, :MLA_ROPE].astype(BF16)
    for hd in range(MLA_HEADS):
        q_ref[0, hd, :, :MLA_NOPE] = (q[:, hd * MLA_NOPE:(hd + 1) * MLA_NOPE] * scale).astype(BF16)
        q_ref[0, hd, :, MLA_NOPE:] = (qr[:, hd * MLA_ROPE:(hd + 1) * MLA_ROPE] * scale).astype(BF16)
        base = hd * (MLA_NOPE + MLA_V)
        k_ref[0, hd, :, :MLA_NOPE] = kv[:, base:base + MLA_NOPE].astype(BF16)
        k_ref[0, hd, :, MLA_NOPE:] = kr_b
        v_ref[0, hd] = kv[:, base + MLA_NOPE:base + MLA_NOPE + MLA_V].astype(BF16)


def _mla_prep(x, mod, gain, wa, qn, kvn, wuq, wukv, cos_t, sin_t, batch, seq):
    t, d_model = x.shape
    tm = ROW_TILE
    per_b = seq // tm
    row = lambda i: (i, 0)
    hs = lambda i: (i // per_b, 0, i % per_b, 0)
    return pl.pallas_call(
        _mla_prep_kernel,
        grid=(t // tm,),
        in_specs=[
            pl.BlockSpec((tm, d_model), row),
            pl.BlockSpec((1, ADA_CHUNKS, d_model), lambda i: (i // per_b, 0, 0)),
            _resident((1, d_model)),
            _resident(wa.shape),
            _resident((1, MLA_Q_RANK)),
            _resident((1, MLA_KV_RANK)),
            _resident(wuq.shape),
            _resident(wukv.shape),
            pl.BlockSpec((1, tm, LANES), lambda i: (i // per_b, i % per_b, 0)),
            pl.BlockSpec((1, tm, LANES), lambda i: (i // per_b, i % per_b, 0)),
        ],
        out_specs=[
            pl.BlockSpec((1, MLA_HEADS, tm, MLA_QK), hs),
            pl.BlockSpec((1, MLA_HEADS, tm, MLA_QK), hs),
            pl.BlockSpec((1, MLA_HEADS, tm, MLA_V), hs),
        ],
        out_shape=[
            jax.ShapeDtypeStruct((batch, MLA_HEADS, seq, MLA_QK), BF16),
            jax.ShapeDtypeStruct((batch, MLA_HEADS, seq, MLA_QK), BF16),
            jax.ShapeDtypeStruct((batch, MLA_HEADS, seq, MLA_V), BF16),
        ],
        compiler_params=_params(1),
        name="mla_prep",
    )(x, mod, gain, wa, qn, kvn, wuq, wukv, cos_t, sin_t)


def _mla_attn_kernel(q_ref, k_ref, v_ref, o_ref):
    s = _dot_nt(q_ref[0, 0], k_ref[0, 0])
    m = jnp.max(s, axis=-1, keepdims=True)
    p = jnp.exp(s - m)
    l = jnp.sum(p, axis=-1, keepdims=True)
    o = _dot(p.astype(BF16), v_ref[0, 0])
    o_ref[...] = (o / l).astype(o_ref.dtype)


def _mla_attn(q, k, v):
    batch, heads, seq, _ = q.shape
    tq = MLA_Q_TILE
    nq = seq // tq
    return pl.pallas_call(
        _mla_attn_kernel,
        grid=(batch, heads, nq),
        in_specs=[
            pl.BlockSpec((1, 1, tq, MLA_QK), lambda b, h, i: (b, h, i, 0)),
            pl.BlockSpec((1, 1, seq, MLA_QK), lambda b, h, i: (b, h, 0, 0)),
            pl.BlockSpec((1, 1, seq, MLA_V), lambda b, h, i: (b, h, 0, 0)),
        ],
        out_specs=pl.BlockSpec((tq, MLA_V), lambda b, h, i: (b * nq + i, h)),
        out_shape=jax.ShapeDtypeStruct((batch * seq, heads * MLA_V), BF16),
        compiler_params=_params(3),
        name="mla_attn",
    )(q, k, v)


def _dil_prep_kernel(x_ref, mod_ref, g_ref, w_ref, cos_ref, sin_ref, *refs):
    out_refs = refs[:3 * len(DIL_PAIRS)]
    h_scr = refs[3 * len(DIL_PAIRS)]
    mod = mod_ref[0]
    h32 = _norm_mod(x_ref[...], g_ref[...], mod[1:2], mod[0:1])
    n_lane_blocks = h32.shape[1] // LANES
    for j in range(n_lane_blocks):
        h_scr[j] = h32[:, j * LANES:(j + 1) * LANES]
    tm = h32.shape[0]
    gw = DIL_GROUP_W
    for g, (_, dil) in enumerate(DIL_PAIRS):
        n = tm // dil
        if dil == 1:
            hp, cos_t, sin_t = h32, cos_ref[0], sin_ref[0]
        else:
            hp = jnp.concatenate(
                [jnp.concatenate([h_scr[j, pl.ds(r, n, stride=dil), :] for j in range(n_lane_blocks)], axis=1)
                 for r in range(dil)], axis=0)
            cos_t = jnp.concatenate([cos_ref[0, pl.ds(r, n, stride=dil), :] for r in range(dil)], axis=0)
            sin_t = jnp.concatenate([sin_ref[0, pl.ds(r, n, stride=dil), :] for r in range(dil)], axis=0)
        pg = _dot(hp.astype(BF16), w_ref[:, g * 3 * gw:(g + 1) * 3 * gw])
        qg = (_rope(pg[:, :gw], cos_t, sin_t) * (DIL_HEAD_DIM ** -0.5)).astype(BF16)
        kg = _rope(pg[:, gw:2 * gw], cos_t, sin_t).astype(BF16)
        vg = pg[:, 2 * gw:].astype(BF16)
        q_ref, k_ref, v_ref = out_refs[3 * g:3 * g + 3]
        for r in range(dil):
            q_ref[0, r] = qg[r * n:(r + 1) * n]
            k_ref[0, r] = kg[r * n:(r + 1) * n]
            v_ref[0, r] = vg[r * n:(r + 1) * n]


def _dil_prep(x, mod, gain, w, cos_t, sin_t, batch, seq):
    t, d_model = x.shape
    tm = ROW_TILE
    per_b = seq // tm
    out_specs, out_shape = [], []
    for _, dil in DIL_PAIRS:
        for _ in range(3):
            out_specs.append(pl.BlockSpec((1, dil, tm // dil, DIL_GROUP_W),
                                          lambda i: (i // per_b, 0, i % per_b, 0)))
            out_shape.append(jax.ShapeDtypeStruct((batch, dil, seq // dil, DIL_GROUP_W), BF16))
    return pl.pallas_call(
        _dil_prep_kernel,
        grid=(t // tm,),
        in_specs=[
            pl.BlockSpec((tm, d_model), lambda i: (i, 0)),
            pl.BlockSpec((1, ADA_CHUNKS, d_model), lambda i: (i // per_b, 0, 0)),
            _resident((1, d_model)),
            _resident(w.shape),
            pl.BlockSpec((1, tm, LANES), lambda i: (i // per_b, i % per_b, 0)),
            pl.BlockSpec((1, tm, LANES), lambda i: (i // per_b, i % per_b, 0)),
        ],
        out_specs=out_specs,
        out_shape=out_shape,
        scratch_shapes=[pltpu.VMEM((d_model // LANES, tm, LANES), F32)],
        compiler_params=_params(1),
        name="dil_prep",
    )(x, mod, gain, w, cos_t, sin_t)


def _dil_attn_kernel(q_ref, k_ref, v_ref, o_ref, lse_ref, *, dil, length, band):
    qb_rows = DIL_QBLOCK
    nb = length // qb_rows
    kw = min(length, 2 * qb_rows)
    lane = lax.broadcasted_iota(jnp.int32, (qb_rows, LANES), 1)
    low = lane < DIL_HEAD_DIM
    row_i = lax.broadcasted_iota(jnp.int32, (qb_rows, kw), 0)
    col_i = lax.broadcasted_iota(jnp.int32, (qb_rows, kw), 1)

    def tile(idx, carry):
        r = idx // nb
        q0 = pl.multiple_of((idx % nb) * qb_rows, qb_rows)
        ks = pl.multiple_of(jnp.clip(q0 - band, 0, length - kw), band)
        qt = q_ref[0, r, pl.ds(q0, qb_rows), :]
        kt = k_ref[0, r, pl.ds(ks, kw), :]
        vt = v_ref[0, r, pl.ds(ks, kw), :]
        valid = jnp.abs((q0 + row_i) - (ks + col_i)) <= band
        for hp in range(DIL_GROUP_W // LANES):
            cols = slice(hp * LANES, (hp + 1) * LANES)
            qp, kp, vp = qt[:, cols], kt[:, cols], vt[:, cols]
            outs, lses = [], []
            for sel in (low, jnp.logical_not(low)):
                s = _dot_nt(jnp.where(sel, qp, jnp.zeros_like(qp)), kp)
                s = jnp.where(valid, s, NEG_INF)
                m = jnp.max(s, axis=-1, keepdims=True)
                p = jnp.exp(s - m)
                l = jnp.sum(p, axis=-1, keepdims=True)
                outs.append(_dot(p.astype(BF16), vp) / l)
                lses.append(m + jnp.log(l))
            o_ref[0, r, pl.ds(q0, qb_rows), cols] = jnp.where(low, outs[0], outs[1]).astype(o_ref.dtype)
            lse_ref[0, r, pl.ds(q0, qb_rows), cols] = jnp.where(low, lses[0], lses[1])
        return carry

    lax.fori_loop(0, dil * nb, tile, 0)


def _dil_attn(q, k, v, window, dil):
    batch, _, length, gw = q.shape
    band = window // (2 * dil)
    assert band * 2 == DIL_QBLOCK and length % DIL_QBLOCK == 0
    spec = pl.BlockSpec((1, dil, length, gw), lambda b: (b, 0, 0, 0))
    return pl.pallas_call(
        functools.partial(_dil_attn_kernel, dil=dil, length=length, band=band),
        grid=(batch,),
        in_specs=[spec, spec, spec],
        out_specs=[spec, spec],
        out_shape=[jax.ShapeDtypeStruct(q.shape, BF16), jax.ShapeDtypeStruct(q.shape, F32)],
        compiler_params=_params(1),
        name=f"dil_attn_d{dil}",
    )(q, k, v)


def _ret_prep_kernel(x_ref, mod_ref, g_ref, w_ref, cos_ref, sin_ref, q_ref, k_ref, v_ref):
    mod = mod_ref[0]
    h = _norm_mod(x_ref[...], g_ref[...], mod[1:2], mod[0:1]).astype(BF16)
    p = _dot(h, w_ref[...])
    cos_t, sin_t = cos_ref[0], sin_ref[0]
    q_ref[...] = _rope(p[:, :RET_QK_W], cos_t, sin_t).astype(BF16)
    k_ref[...] = (_rope(p[:, RET_QK_W:2 * RET_QK_W], cos_t, sin_t) * (RET_QK_DIM ** -0.5)).astype(BF16)
    v_ref[...] = p[:, 2 * RET_QK_W:].astype(BF16)


def _ret_prep(x, mod, gain, w, cos_t, sin_t, seq):
    t, d_model = x.shape
    tm = ROW_TILE
    per_b = seq // tm
    row = lambda i: (i, 0)
    return pl.pallas_call(
        _ret_prep_kernel,
        grid=(t // tm,),
        in_specs=[
            pl.BlockSpec((tm, d_model), row),
            pl.BlockSpec((1, ADA_CHUNKS, d_model), lambda i: (i // per_b, 0, 0)),
            _resident((1, d_model)),
            _resident(w.shape),
            pl.BlockSpec((1, tm, LANES), lambda i: (i // per_b, i % per_b, 0)),
            pl.BlockSpec((1, tm, LANES), lambda i: (i // per_b, i % per_b, 0)),
        ],
        out_specs=[pl.BlockSpec((tm, RET_QK_W), row), pl.BlockSpec((tm, RET_QK_W), row),
                   pl.BlockSpec((tm, RET_V_W), row)],
        out_shape=[jax.ShapeDtypeStruct((t, RET_QK_W), BF16), jax.ShapeDtypeStruct((t, RET_QK_W), BF16),
                   jax.ShapeDtypeStruct((t, RET_V_W), BF16)],
        compiler_params=_params(1),
        name="ret_prep",
    )(x, mod, gain, w, cos_t, sin_t)


def _ret_scan_kernel(ld_ref, q_ref, k_ref, v_ref, gn_ref, o_ref,
                     yf_ref, st_ref, intra_ref, qdec_ref, kdec_ref, cdec_ref):
    c = RET_CHUNK
    seq = q_ref.shape[1]
    nc = seq // c
    n_pairs = RET_HEADS // 2
    pair_v = 2 * RET_V_DIM
    ia = lax.broadcasted_iota(jnp.int32, (c, c), 0)
    ib = lax.broadcasted_iota(jnp.int32, (c, c), 1)
    diff = (ia - ib).astype(F32)
    idx_q = lax.broadcasted_iota(jnp.int32, (c, pair_v), 0).astype(F32)
    lane_q = lax.broadcasted_iota(jnp.int32, (c, pair_v), 1)
    idx_k = lax.broadcasted_iota(jnp.int32, (c, LANES), 0).astype(F32)
    lane_k = lax.broadcasted_iota(jnp.int32, (c, LANES), 1)
    low_k = lane_k < RET_QK_DIM
    srow = lax.broadcasted_iota(jnp.int32, (LANES, pair_v), 0)
    scol = lax.broadcasted_iota(jnp.int32, (LANES, pair_v), 1)
    blk0 = jnp.logical_and(srow < RET_QK_DIM, scol < RET_V_DIM)
    blk1 = jnp.logical_and(srow >= RET_QK_DIM, scol >= RET_V_DIM)
    diag = jnp.where(jnp.logical_or(blk0, blk1), 1.0, 0.0).astype(F32)

    for dr in range(2):
        for hd in range(RET_HEADS):
            lg = ld_ref[dr, hd]
            if dr == 0:
                mask = diff >= 0
                dist = jnp.where(mask, diff, 0.0)
            else:
                mask = diff < 0
                dist = jnp.where(mask, -diff, 0.0)
            intra_ref[dr, hd] = jnp.where(mask, jnp.exp(lg * dist), 0.0)
        for hp in range(n_pairs):
            lg0 = ld_ref[dr, 2 * hp]
            lg1 = ld_ref[dr, 2 * hp + 1]
            q_exp = idx_q + 1.0 if dr == 0 else c - idx_q
            k_exp = c - 1.0 - idx_k if dr == 0 else idx_k
            qdec_ref[dr, hp] = jnp.exp(jnp.where(lane_q < RET_V_DIM, lg0, lg1) * q_exp)
            kdec_ref[dr, hp] = jnp.exp(jnp.where(low_k, lg0, lg1) * k_exp)
            cdec_ref[dr, hp] = jnp.where(blk0, jnp.exp(lg0 * c), jnp.where(blk1, jnp.exp(lg1 * c), 0.0))

    def chunk(dr, n):
        r0 = pl.multiple_of(n * c, c)
        qc = q_ref[0, pl.ds(r0, c), :]
        kc = k_ref[0, pl.ds(r0, c), :]
        vc = v_ref[0, pl.ds(r0, c), :]
        ys = []
        for hp in range(n_pairs):
            qp = qc[:, hp * LANES:(hp + 1) * LANES]
            kp = kc[:, hp * LANES:(hp + 1) * LANES]
            vp = vc[:, hp * pair_v:(hp + 1) * pair_v]
            zero = jnp.zeros_like(qp)
            s0 = _dot_nt(jnp.where(low_k, qp, zero), kp) * intra_ref[dr, 2 * hp]
            s1 = _dot_nt(jnp.where(low_k, zero, qp), kp) * intra_ref[dr, 2 * hp + 1]
            y_intra = jnp.concatenate([_dot(s0.astype(BF16), vp[:, :RET_V_DIM]),
                                       _dot(s1.astype(BF16), vp[:, RET_V_DIM:])], axis=1)
            state = st_ref[hp]
            y_inter = _dot(qp, state.astype(BF16)) * qdec_ref[dr, hp]
            ys.append(y_intra + y_inter)
            kd = (kp.astype(F32) * kdec_ref[dr, hp]).astype(BF16)
            st_ref[hp] = cdec_ref[dr, hp] * state + diag * _dot_tn(kd, vp)
        return r0, ys

    st_ref[...] = jnp.zeros_like(st_ref)

    def fwd(n, carry):
        r0, ys = chunk(0, n)
        for hp in range(n_pairs):
            yf_ref[pl.ds(r0, c), hp * pair_v:(hp + 1) * pair_v] = ys[hp]
        return carry

    lax.fori_loop(0, nc, fwd, 0)
    st_ref[...] = jnp.zeros_like(st_ref)

    def bwd(i, carry):
        r0, ys = chunk(1, nc - 1 - i)
        for hp in range(n_pairs):
            y = ys[hp] + yf_ref[pl.ds(r0, c), hp * pair_v:(hp + 1) * pair_v]
            for half in range(2):
                cols = slice(hp * pair_v + half * RET_V_DIM, hp * pair_v + (half + 1) * RET_V_DIM)
                yh = y[:, half * RET_V_DIM:(half + 1) * RET_V_DIM]
                mu = jnp.mean(yh, axis=-1, keepdims=True)
                dev = yh - mu
                var = jnp.mean(dev * dev, axis=-1, keepdims=True)
                o_ref[0, pl.ds(r0, c), cols] = (dev * lax.rsqrt(var + EPS) * gn_ref[:, cols]).astype(o_ref.dtype)
        return carry

    lax.fori_loop(0, nc, bwd, 0)


def _ret_scan(log_decay, q, k, v, ret_norm, batch, seq):
    c = RET_CHUNK
    n_pairs = RET_HEADS // 2
    q3 = q.reshape(batch, seq, RET_QK_W)
    k3 = k.reshape(batch, seq, RET_QK_W)
    v3 = v.reshape(batch, seq, RET_V_W)
    out = pl.pallas_call(
        _ret_scan_kernel,
        grid=(batch,),
        in_specs=[
            pl.BlockSpec(memory_space=pltpu.SMEM),
            pl.BlockSpec((1, seq, RET_QK_W), lambda b: (b, 0, 0)),
            pl.BlockSpec((1, seq, RET_QK_W), lambda b: (b, 0, 0)),
            pl.BlockSpec((1, seq, RET_V_W), lambda b: (b, 0, 0)),
            _resident((1, RET_V_W)),
        ],
        out_specs=pl.BlockSpec((1, seq, RET_V_W), lambda b: (b, 0, 0)),
        out_shape=jax.ShapeDtypeStruct((batch, seq, RET_V_W), BF16),
        scratch_shapes=[
            pltpu.VMEM((seq, RET_V_W), F32),
            pltpu.VMEM((n_pairs, LANES, 2 * RET_V_DIM), F32),
            pltpu.VMEM((2, RET_HEADS, c, c), F32),
            pltpu.VMEM((2, n_pairs, c, 2 * RET_V_DIM), F32),
            pltpu.VMEM((2, n_pairs, c, LANES), F32),
            pltpu.VMEM((2, n_pairs, LANES, 2 * RET_V_DIM), F32),
        ],
        compiler_params=_params(1),
        name="ret_scan",
    )(log_decay, q3, k3, v3, ret_norm)
    return out.reshape(batch * seq, RET_V_W)


def _merge_kernel(x_ref, mod_ref, g_ref, wg_ref, ymla_ref, yret_ref,
                  o0_ref, l0_ref, o1_ref, l1_ref, o2_ref, l2_ref,
                  wbm_ref, wbd_ref, wbr_ref, wout_ref, out_ref, nat_o, nat_l):
    x = x_ref[...]
    mod = mod_ref[0]
    h = _norm_mod(x, g_ref[...], mod[1:2], mod[0:1]).astype(BF16)
    gates = _dot(h, wg_ref[...])
    d_model = x.shape[1]
    tm = x.shape[0]

    o_nat = [o0_ref[0, 0].astype(F32)]
    l_nat = [l0_ref[0, 0]]
    for g, (o_ref, l_ref) in enumerate(((o1_ref, l1_ref), (o2_ref, l2_ref))):
        dil = DIL_PAIRS[g + 1][1]
        n = tm // dil
        n_lane_blocks = DIL_GROUP_W // LANES
        for r in range(dil):
            o_r = o_ref[0, r].astype(F32)
            l_r = l_ref[0, r]
            for j in range(n_lane_blocks):
                nat_o[g, j, pl.ds(r, n, stride=dil), :] = o_r[:, j * LANES:(j + 1) * LANES]
                nat_l[g, j, pl.ds(r, n, stride=dil), :] = l_r[:, j * LANES:(j + 1) * LANES]
        o_nat.append(jnp.concatenate([nat_o[g, j] for j in range(n_lane_blocks)], axis=1))
        l_nat.append(jnp.concatenate([nat_l[g, j] for j in range(n_lane_blocks)], axis=1))
    m = jnp.maximum(jnp.maximum(l_nat[0], l_nat[1]), l_nat[2])
    ws = [jnp.exp(l - m) for l in l_nat]
    y_dil = (ws[0] * o_nat[0] + ws[1] * o_nat[1] + ws[2] * o_nat[2]) / (ws[0] + ws[1] + ws[2])

    y_ret = (_silu(gates[:, :RET_V_W]) * yret_ref[...].astype(F32)).astype(BF16)
    ga = gates[:, RET_V_W:RET_V_W + d_model]
    gb = gates[:, RET_V_W + d_model:RET_V_W + 2 * d_model]
    gc = gates[:, RET_V_W + 2 * d_model:]
    merged = (jax.nn.sigmoid(ga) * _dot(ymla_ref[...], wbm_ref[...])
              + jax.nn.sigmoid(gb) * _dot(y_dil.astype(BF16), wbd_ref[...])
              + jax.nn.sigmoid(gc) * _dot(y_ret, wbr_ref[...]))
    out_ref[...] = x + mod[2:3] * _dot(merged.astype(BF16), wout_ref[...])


def _merge_out(x, mod, gain, wg, y_mla, y_ret, dil_outs, wbm, wbd, wbr, wout, batch, seq):
    t, d_model = x.shape
    tm = MERGE_TILE
    per_b = seq // tm
    row = lambda i: (i, 0)
    dil_specs, dil_args = [], []
    for (_, dil), (o, lse) in zip(DIL_PAIRS, dil_outs):
        spec = pl.BlockSpec((1, dil, tm // dil, DIL_GROUP_W), lambda i: (i // per_b, 0, i % per_b, 0))
        dil_specs += [spec, spec]
        dil_args += [o, lse]
    return pl.pallas_call(
        _merge_kernel,
        grid=(t // tm,),
        in_specs=[
            pl.BlockSpec((tm, d_model), row),
            pl.BlockSpec((1, ADA_CHUNKS, d_model), lambda i: (i // per_b, 0, 0)),
            _resident((1, d_model)),
            _resident(wg.shape),
            pl.BlockSpec((tm, y_mla.shape[1]), row),
            pl.BlockSpec((tm, RET_V_W), row),
            *dil_specs,
            _resident(wbm.shape), _resident(wbd.shape), _resident(wbr.shape), _resident(wout.shape),
        ],
        out_specs=pl.BlockSpec((tm, d_model), row),
        out_shape=jax.ShapeDtypeStruct((t, d_model), F32),
        scratch_shapes=[pltpu.VMEM((2, DIL_GROUP_W // LANES, tm, LANES), F32),
                        pltpu.VMEM((2, DIL_GROUP_W // LANES, tm, LANES), F32)],
        compiler_params=_params(1),
        name="merge_out",
    )(x, mod, gain, wg, y_mla, y_ret, *dil_args, wbm, wbd, wbr, wout)


def _ffn_kernel(x_ref, mod_ref, g_ref, w1_ref, w3_ref, w2_ref, out_ref, *, n_chunks):
    x = x_ref[...]
    mod = mod_ref[0]
    h = _norm_mod(x, g_ref[...], mod[4:5], mod[3:4]).astype(BF16)
    d_ff = w1_ref.shape[1]
    fc = d_ff // n_chunks
    acc = jnp.zeros(x.shape, F32)
    for j in range(n_chunks):
        a = _dot(h, w1_ref[:, j * fc:(j + 1) * fc])
        b = _dot(h, w3_ref[:, j * fc:(j + 1) * fc])
        acc = acc + _dot((_silu(a) * b).astype(BF16), w2_ref[j * fc:(j + 1) * fc, :])
    out_ref[...] = x + mod[5:6] * acc


def _ffn_dense(x, mod, gain, w1, w3, w2, seq):
    t, d_model = x.shape
    tm = ROW_TILE
    per_b = seq // tm
    row = lambda i: (i, 0)
    return pl.pallas_call(
        functools.partial(_ffn_kernel, n_chunks=2),
        grid=(t // tm,),
        in_specs=[
            pl.BlockSpec((tm, d_model), row),
            pl.BlockSpec((1, ADA_CHUNKS, d_model), lambda i: (i // per_b, 0, 0)),
            _resident((1, d_model)),
            _resident(w1.shape), _resident(w3.shape), _resident(w2.shape),
        ],
        out_specs=pl.BlockSpec((tm, d_model), row),
        out_shape=jax.ShapeDtypeStruct((t, d_model), F32),
        compiler_params=_params(1),
        name="ffn_dense",
    )(x, mod, gain, w1, w3, w2)


def _route_kernel(x_ref, mod_ref, g_ref, wr_ref, h_ref, route_ref):
    mod = mod_ref[0]
    h = _norm_mod(x_ref[...], g_ref[...], mod[4:5], mod[3:4])
    h_ref[...] = h
    logits = jnp.dot(h, wr_ref[...], preferred_element_type=F32, precision=lax.Precision.HIGHEST)
    lane = lax.broadcasted_iota(jnp.int32, logits.shape, 1).astype(F32)
    lg = jnp.where(lane < N_EXPERTS, logits, -jnp.inf)
    m1 = jnp.max(lg, axis=-1, keepdims=True)
    i1 = jnp.min(jnp.where(lg == m1, lane, float(LANES)), axis=-1, keepdims=True)
    lg2 = jnp.where(lane == i1, -jnp.inf, lg)
    m2 = jnp.max(lg2, axis=-1, keepdims=True)
    i2 = jnp.min(jnp.where(lg2 == m2, lane, float(LANES)), axis=-1, keepdims=True)
    e = jnp.exp(m2 - m1)
    w1 = 1.0 / (1.0 + e)
    w2 = e / (1.0 + e)
    route_ref[...] = jnp.where(lane == 0, i1,
                               jnp.where(lane == 1, i2, jnp.where(lane == 2, w1, jnp.where(lane == 3, w2, 0.0))))


def _moe_route(x, mod, gain, w_router, seq):
    t, d_model = x.shape
    tm = ROW_TILE
    per_b = seq // tm
    row = lambda i: (i, 0)
    wr = jnp.zeros((d_model, LANES), F32).at[:, :N_EXPERTS].set(w_router)
    return pl.pallas_call(
        _route_kernel,
        grid=(t // tm,),
        in_specs=[
            pl.BlockSpec((tm, d_model), row),
            pl.BlockSpec((1, ADA_CHUNKS, d_model), lambda i: (i // per_b, 0, 0)),
            _resident((1, d_model)),
            _resident(wr.shape),
        ],
        out_specs=[pl.BlockSpec((tm, d_model), row), pl.BlockSpec((tm, LANES), row)],
        out_shape=[jax.ShapeDtypeStruct((t, d_model), F32), jax.ShapeDtypeStruct((t, LANES), F32)],
        compiler_params=_params(1),
        name="moe_route",
    )(x, mod, gain, wr)


def _row_copy(src_hbm, dst, src_row, dst_row, sem):
    return pltpu.make_async_copy(src_hbm.at[pl.ds(src_row, 1), :], dst.at[pl.ds(dst_row, 1), :], sem)


def _gather_kernel(tok_ref, h_hbm, xs_hbm, sem):
    bm = tok_ref.shape[2]
    base = pl.program_id(0) * bm

    def issue(r, carry):
        _row_copy(h_hbm, xs_hbm, tok_ref[0, 0, r], base + r, sem).start()
        return carry

    lax.fori_loop(0, bm, issue, 0)
    pltpu.make_async_copy(h_hbm.at[pl.ds(0, bm), :], xs_hbm.at[pl.ds(base, bm), :], sem).wait()


def _moe_gather(row_tok, h):
    nb, _, bm = row_tok.shape
    d_model = h.shape[1]
    return pl.pallas_call(
        _gather_kernel,
        grid=(nb,),
        in_specs=[
            pl.BlockSpec((1, 1, bm), lambda i: (i, 0, 0), memory_space=pltpu.SMEM),
            pl.BlockSpec(memory_space=pl.ANY),
        ],
        out_specs=pl.BlockSpec(memory_space=pl.ANY),
        scratch_shapes=[pltpu.SemaphoreType.DMA(())],
        out_shape=jax.ShapeDtypeStruct((nb * bm, d_model), F32),
        compiler_params=_params(1),
        name="moe_gather",
    )(row_tok, h)


def _expert_kernel(be_ref, nact_ref, x_ref, w1_ref, w3_ref, w2_ref, y_ref, *, n_chunks):
    active = pl.program_id(0) < nact_ref[0]

    @pl.when(jnp.logical_not(active))
    def _():
        y_ref[...] = jnp.zeros_like(y_ref)

    @pl.when(active)
    def _():
        xb = x_ref[...].astype(BF16)
        d_ff = w1_ref.shape[2]
        fc = d_ff // n_chunks
        acc = jnp.zeros(y_ref.shape, F32)
        for j in range(n_chunks):
            a = _dot(xb, w1_ref[0, :, j * fc:(j + 1) * fc])
            b = _dot(xb, w3_ref[0, :, j * fc:(j + 1) * fc])
            acc = acc + _dot((_silu(a) * b).astype(BF16), w2_ref[0, j * fc:(j + 1) * fc, :])
        y_ref[...] = acc


def _moe_expert(block_expert, nact, xs, w1, w3, w2):
    r, d_model = xs.shape
    bm = MOE_BLOCK
    d_ff = w1.shape[2]
    blk = lambda i, be, na: (jnp.minimum(i, na[0] - 1), 0)
    wsel = lambda i, be, na: (be[jnp.minimum(i, na[0] - 1)], 0, 0)
    return pl.pallas_call(
        functools.partial(_expert_kernel, n_chunks=2),
        grid_spec=pltpu.PrefetchScalarGridSpec(
            num_scalar_prefetch=2,
            grid=(r // bm,),
            in_specs=[
                pl.BlockSpec((bm, d_model), blk),
                pl.BlockSpec((1, d_model, d_ff), wsel, pipeline_mode=pl.Buffered(1)),
                pl.BlockSpec((1, d_model, d_ff), wsel, pipeline_mode=pl.Buffered(1)),
                pl.BlockSpec((1, d_ff, d_model), wsel, pipeline_mode=pl.Buffered(1)),
            ],
            out_specs=pl.BlockSpec((bm, d_model), lambda i, be, na: (i, 0)),
        ),
        out_shape=jax.ShapeDtypeStruct((r, d_model), F32),
        compiler_params=_params(1),
        name="moe_expert",
    )(block_expert, nact, xs, w1, w3, w2)


def _combine_kernel(dest_ref, x_ref, mod_ref, route_ref, yb_hbm, out_ref, buf, sem):
    tm = x_ref.shape[0]

    def issue(r, carry):
        _row_copy(yb_hbm, buf, dest_ref[0, 0, r], r, sem).start()
        return carry

    lax.fori_loop(0, TOP_K * tm, issue, 0)
    pltpu.make_async_copy(yb_hbm.at[pl.ds(0, TOP_K * tm), :], buf, sem).wait()
    route = route_ref[...]
    y = route[:, 2:3] * buf[:tm, :] + route[:, 3:4] * buf[tm:, :]
    out_ref[...] = x_ref[...] + mod_ref[0][5:6] * y


def _moe_combine(dest, x, mod, route, yb, seq):
    t, d_model = x.shape
    tm = COMBINE_TILE
    per_b = seq // tm
    row = lambda i: (i, 0)
    return pl.pallas_call(
        _combine_kernel,
        grid=(t // tm,),
        in_specs=[
            pl.BlockSpec((1, 1, TOP_K * tm), lambda i: (i, 0, 0), memory_space=pltpu.SMEM),
            pl.BlockSpec((tm, d_model), row),
            pl.BlockSpec((1, ADA_CHUNKS, d_model), lambda i: (i // per_b, 0, 0)),
            pl.BlockSpec((tm, LANES), row),
            pl.BlockSpec(memory_space=pl.ANY),
        ],
        out_specs=pl.BlockSpec((tm, d_model), row),
        out_shape=jax.ShapeDtypeStruct((t, d_model), F32),
        scratch_shapes=[pltpu.VMEM((TOP_K * tm, d_model), F32), pltpu.SemaphoreType.DMA(())],
        compiler_params=_params(1),
        name="moe_combine",
    )(dest, x, mod, route, yb)


def _moe(x, mod, gain, w_router, w1, w3, w2, seq):
    t, d_model = x.shape
    bm = MOE_BLOCK
    h, route = _moe_route(x, mod, gain, w_router, seq)
    e_flat = route[:, :TOP_K].astype(jnp.int32).reshape(-1)
    n_assign = e_flat.shape[0]
    onehot = (e_flat[:, None] == jnp.arange(N_EXPERTS, dtype=jnp.int32)[None, :]).astype(jnp.int32)
    csum = jnp.cumsum(onehot, axis=0)
    rank = jnp.take_along_axis(csum, e_flat[:, None], axis=1)[:, 0] - 1
    counts = csum[-1]
    padded = (counts + bm - 1) // bm * bm
    pends = jnp.cumsum(padded)
    dest = (pends - padded)[e_flat] + rank
    nb = n_assign // bm + N_EXPERTS
    row_tok = jnp.zeros((nb * bm,), jnp.int32).at[dest].set(jnp.arange(n_assign, dtype=jnp.int32) // TOP_K)
    block_expert = jnp.minimum(
        jnp.searchsorted(pends, jnp.arange(nb, dtype=jnp.int32) * bm, side="right"), N_EXPERTS - 1
    ).astype(jnp.int32)
    nact = (pends[-1:] // bm).astype(jnp.int32)

    xs = _moe_gather(row_tok.reshape(nb, 1, bm), h)
    yb = _moe_expert(block_expert, nact, xs, w1, w3, w2)
    tmc = COMBINE_TILE
    dest_tiles = dest.reshape(t // tmc, tmc, TOP_K).transpose(0, 2, 1).reshape(t // tmc, 1, TOP_K * tmc)
    return _moe_combine(dest_tiles.astype(jnp.int32), x, mod, route, yb, seq)


def _final_kernel(x_ref, g_ref, o_ref):
    o_ref[...] = _rms(x_ref[...], g_ref[...])


def _final_norm(x, gain):
    t, d_model = x.shape
    tm = ROW_TILE
    return pl.pallas_call(
        _final_kernel,
        grid=(t // tm,),
        in_specs=[pl.BlockSpec((tm, d_model), lambda i: (i, 0)), _resident((1, d_model))],
        out_specs=pl.BlockSpec((tm, d_model), lambda i: (i, 0)),
        out_shape=jax.ShapeDtypeStruct((t, d_model), F32),
        compiler_params=_params(1),
        name="final_norm",
    )(x, gain)


def _rope_tables(positions):
    inv_freq = ROPE_THETA ** (-jnp.arange(0, ROPE_DIM, 2, dtype=F32) / ROPE_DIM)
    ang = positions.astype(F32)[..., None] * inv_freq
    cos, sin = jnp.cos(ang), jnp.sin(ang)
    reps = LANES // ROPE_DIM
    cos_t = jnp.tile(jnp.concatenate([cos, cos], axis=-1), (1, 1, reps))
    sin_t = jnp.tile(jnp.concatenate([-sin, sin], axis=-1), (1, 1, reps))
    return cos_t, sin_t


def _split_w_in(w_in):
    d_model = w_in.shape[0]
    sizes = (MLA_Q_RANK, MLA_KV_RANK, MLA_ROPE,
             3 * DIL_GROUP_W, 3 * DIL_GROUP_W, 3 * DIL_GROUP_W,
             RET_QK_W, RET_QK_W, RET_V_W, RET_V_W, d_model, d_model, d_model)
    cols, start = [], 0
    for n in sizes:
        cols.append(w_in[:, start:start + n])
        start += n
    cq, ckv, kr, dq, dk, dv, rq, rk, rv, rg, ga, gb, gc = cols
    w_mla = jnp.concatenate([cq, ckv, kr, kr], axis=1)
    gw = DIL_GROUP_W
    w_dil = jnp.concatenate(
        [m[:, g * gw:(g + 1) * gw] for g in range(len(DIL_PAIRS)) for m in (dq, dk, dv)], axis=1)
    w_ret = jnp.concatenate([rq, rk, rv], axis=1)
    w_gate = jnp.concatenate([rg, ga, gb, gc], axis=1)
    return tuple(m.astype(BF16) for m in (w_mla, w_dil, w_ret, w_gate))


def _split_w_uq(w_uq):
    w = w_uq.reshape(MLA_Q_RANK, MLA_HEADS, MLA_QK)
    nope = w[:, :, :MLA_NOPE].reshape(MLA_Q_RANK, MLA_HEADS * MLA_NOPE)
    rope = w[:, :, MLA_NOPE:].reshape(MLA_Q_RANK, MLA_HEADS * MLA_ROPE)
    return jnp.concatenate([nope, rope], axis=1).astype(BF16)


def kernel(x, c, positions, ada_w, ada_b, norm_mix, norm_ffn, w_in, mla_q_norm, mla_w_uq, mla_kv_norm,
           mla_w_ukv, ret_log_decay, ret_norm, w_br_mla, w_br_dil, w_br_ret, w_out, ffn_w1, ffn_w3, ffn_w2,
           moe_router, moe_w1, moe_w3, moe_w2, final_norm):
    batch, seq, d_model = x.shape
    depth = ada_w.shape[0]
    cos_t, sin_t = _rope_tables(positions)
    mod_all = _ada(c, ada_w, ada_b)
    xt = x.reshape(batch * seq, d_model)
    for layer in range(depth):
        mod = mod_all[layer]
        gmix = norm_mix[layer].reshape(1, d_model)
        w_mla, w_dil, w_ret, w_gate = _split_w_in(w_in[layer])

        q, k, v = _mla_prep(xt, mod, gmix, w_mla, mla_q_norm[layer].reshape(1, -1),
                            mla_kv_norm[layer].reshape(1, -1), _split_w_uq(mla_w_uq[layer]),
                            mla_w_ukv[layer].astype(BF16), cos_t, sin_t, batch, seq)
        y_mla = _mla_attn(q, k, v)

        dil_qkv = _dil_prep(xt, mod, gmix, w_dil, cos_t, sin_t, batch, seq)
        dil_outs = [_dil_attn(*dil_qkv[3 * g:3 * g + 3], window, dil)
                    for g, (window, dil) in enumerate(DIL_PAIRS)]

        rq, rk, rv = _ret_prep(xt, mod, gmix, w_ret, cos_t, sin_t, seq)
        y_ret = _ret_scan(ret_log_decay[layer].astype(F32), rq, rk, rv,
                          ret_norm[layer].reshape(1, -1).astype(F32), batch, seq)

        xt = _merge_out(xt, mod, gmix, w_gate, y_mla, y_ret, dil_outs,
                        w_br_mla[layer].astype(BF16), w_br_dil[layer].astype(BF16),
                        w_br_ret[layer].astype(BF16), w_out[layer].astype(BF16), batch, seq)

        gffn = norm_ffn[layer].reshape(1, d_model)
        i = layer // 2
        if layer % 2 == 0:
            xt = _ffn_dense(xt, mod, gffn, ffn_w1[i].astype(BF16), ffn_w3[i].astype(BF16),
                            ffn_w2[i].astype(BF16), seq)
        else:
            xt = _moe(xt, mod, gffn, moe_router[i], moe_w1[i].astype(BF16), moe_w3[i].astype(BF16),
                      moe_w2[i].astype(BF16), seq)
    return _final_norm(xt, final_norm.reshape(1, d_model)).reshape(batch, seq, d_model)
```

```python
import functools

import jax
import jax.numpy as jnp
from jax import lax
from jax.experimental import pallas as pl
from jax.experimental.pallas import tpu as pltpu

F32 = jnp.float32
BF16 = jnp.bfloat16

EPS = 1e-6
NEG_INF = -1e30
ROPE_THETA = 10000.0
ROPE_DIM = 64
ADA_CHUNKS = 6

MLA_HEADS = 8
MLA_Q_RANK = 768
MLA_KV_RANK = 512
MLA_NOPE = 128
MLA_ROPE = ROPE_DIM
MLA_V = 128
MLA_QK = MLA_NOPE + MLA_ROPE

DIL_PAIRS = ((128, 1), (512, 4), (2048, 16))
DIL_HEADS = 8
DIL_HEAD_DIM = ROPE_DIM
DIL_GROUP_W = DIL_HEADS * DIL_HEAD_DIM
DIL_QBLOCK = 128

RET_HEADS = 8
RET_QK_DIM = ROPE_DIM
RET_V_DIM = 2 * RET_QK_DIM
RET_CHUNK = 128
RET_QK_W = RET_HEADS * RET_QK_DIM
RET_V_W = RET_HEADS * RET_V_DIM

N_EXPERTS = 8
TOP_K = 2
MOE_BLOCK = 256

LANES = 128
V7X_VMEM_LIMIT_BYTES = 56 * 1024 * 1024

ROW_TILE = 512
MERGE_TILE = 256
COMBINE_TILE = 256
MLA_Q_TILE = 512


def _params(n_grid_dims):
    return pltpu.CompilerParams(
        dimension_semantics=("arbitrary",) * n_grid_dims,
        vmem_limit_bytes=V7X_VMEM_LIMIT_BYTES,
    )


def _resident(shape):
    zeros = (0,) * len(shape)
    return pl.BlockSpec(shape, lambda *_: zeros, pipeline_mode=pl.Buffered(1))


def _silu(v):
    return v * jax.nn.sigmoid(v)


def _norm_mod(x, gain, scale, shift):
    ms = jnp.mean(x * x, axis=-1, keepdims=True)
    y = x * lax.rsqrt(ms + EPS) * gain
    return y * (1.0 + scale) + shift


def _rms(v, gain):
    ms = jnp.mean(v * v, axis=-1, keepdims=True)
    return v * lax.rsqrt(ms + EPS) * gain


def _rope(t, cos_t, sin_t):
    lane = lax.broadcasted_iota(jnp.int32, (t.shape[0], LANES), 1)
    first_half = (lane % ROPE_DIM) < (ROPE_DIM // 2)
    out = []
    for j in range(t.shape[1] // LANES):
        c = t[:, j * LANES:(j + 1) * LANES]
        rot = jnp.where(first_half,
                        pltpu.roll(c, LANES - ROPE_DIM // 2, 1),
                        pltpu.roll(c, ROPE_DIM // 2, 1))
        out.append(c * cos_t + rot * sin_t)
    return out[0] if len(out) == 1 else jnp.concatenate(out, axis=1)


def _dot(a, b):
    return jnp.dot(a, b, preferred_element_type=F32)


def _dot_nt(a, b):
    return lax.dot_general(a, b, (((1,), (1,)), ((), ())), preferred_element_type=F32)


def _dot_tn(a, b):
    return lax.dot_general(a, b, (((0,), (0,)), ((), ())), preferred_element_type=F32)


def _ada_kernel(c_ref, w_ref, b_ref, o_ref):
    o_ref[0] = jnp.dot(_silu(c_ref[...]), w_ref[0], preferred_element_type=F32,
                       precision=lax.Precision.HIGHEST) + b_ref[0]


def _ada(c, ada_w, ada_b):
    depth, d_model, n = ada_w.shape
    batch = c.shape[0]
    tn = n // 4
    out = pl.pallas_call(
        _ada_kernel,
        grid=(depth, n // tn),
        in_specs=[
            pl.BlockSpec((batch, d_model), lambda l, j: (0, 0)),
            pl.BlockSpec((1, d_model, tn), lambda l, j: (l, 0, j)),
            pl.BlockSpec((1, 1, tn), lambda l, j: (l, 0, j)),
        ],
        out_specs=pl.BlockSpec((1, batch, tn), lambda l, j: (l, 0, j)),
        out_shape=jax.ShapeDtypeStruct((depth, batch, n), F32),
        compiler_params=_params(2),
        name="ada",
    )(c, ada_w, ada_b.reshape(depth, 1, n))
    return out.reshape(depth, batch, ADA_CHUNKS, d_model)


def _mla_prep_kernel(x_ref, mod_ref, g_ref, wa_ref, qn_ref, kvn_ref, wuq_ref, wukv_ref, cos_ref, sin_ref,
                     q_ref, k_ref, v_ref):
    mod = mod_ref[0]
    h = _norm_mod(x_ref[...], g_ref[...], mod[1:2], mod[0:1]).astype(BF16)
    p = _dot(h, wa_ref[...])
    cq = _rms(p[:, :MLA_Q_RANK], qn_ref[...]).astype(BF16)
    ckv = _rms(p[:, MLA_Q_RANK:MLA_Q_RANK + MLA_KV_RANK], kvn_ref[...]).astype(BF16)
    cos_t = cos_ref[0]
    sin_t = sin_ref[0]
    kr = _rope(p[:, MLA_Q_RANK + MLA_KV_RANK:], cos_t, sin_t)
    q = _dot(cq, wuq_ref[...])
    kv = _dot(ckv, wukv_ref[...])
    nope_w = MLA_HEADS * MLA_NOPE
    qr = _rope(q[:, nope_w:], cos_t, sin_t)
    scale = MLA_QK ** -0.5
    kr_b = kr[:, :MLA_ROPE].astype(BF16)
    for hd in range(MLA_HEADS):
        q_ref[0, hd, :, :MLA_NOPE] = (q[:, hd * MLA_NOPE:(hd + 1) * MLA_NOPE] * scale).astype(BF16)
        q_ref[0, hd, :, MLA_NOPE:] = (qr[:, hd * MLA_ROPE:(hd + 1) * MLA_ROPE] * scale).astype(BF16)
        base = hd * (MLA_NOPE + MLA_V)
        k_ref[0, hd, :, :MLA_NOPE] = kv[:, base:base + MLA_NOPE].astype(BF16)
        k_ref[0, hd, :, MLA_NOPE:] = kr_b
        v_ref[0, hd] = kv[:, base + MLA_NOPE:base + MLA_NOPE + MLA_V].astype(BF16)


def _mla_prep(x, mod, gain, wa, qn, kvn, wuq, wukv, cos_t, sin_t, batch, seq):
    t, d_model = x.shape
    tm = ROW_TILE
    per_b = seq // tm
    row = lambda i: (i, 0)
    hs = lambda i: (i // per_b, 0, i % per_b, 0)
    return pl.pallas_call(
        _mla_prep_kernel,
        grid=(t // tm,),
        in_specs=[
            pl.BlockSpec((tm, d_model), row),
            pl.BlockSpec((1, ADA_CHUNKS, d_model), lambda i: (i // per_b, 0, 0)),
            _resident((1, d_model)),
            _resident(wa.shape),
            _resident((1, MLA_Q_RANK)),
            _resident((1, MLA_KV_RANK)),
            _resident(wuq.shape),
            _resident(wukv.shape),
            pl.BlockSpec((1, tm, LANES), lambda i: (i // per_b, i % per_b, 0)),
            pl.BlockSpec((1, tm, LANES), lambda i: (i // per_b, i % per_b, 0)),
        ],
        out_specs=[
            pl.BlockSpec((1, MLA_HEADS, tm, MLA_QK), hs),
            pl.BlockSpec((1, MLA_HEADS, tm, MLA_QK), hs),
            pl.BlockSpec((1, MLA_HEADS, tm, MLA_V), hs),
        ],
        out_shape=[
            jax.ShapeDtypeStruct((batch, MLA_HEADS, seq, MLA_QK), BF16),
            jax.ShapeDtypeStruct((batch, MLA_HEADS, seq, MLA_QK), BF16),
            jax.ShapeDtypeStruct((batch, MLA_HEADS, seq, MLA_V), BF16),
        ],
        compiler_params=_params(1),
        name="mla_prep",
    )(x, mod, gain, wa, qn, kvn, wuq, wukv, cos_t, sin_t)


def _mla_attn_kernel(q_ref, k_ref, v_ref, o_ref):
    s = _dot_nt(q_ref[0, 0], k_ref[0, 0])
    m = jnp.max(s, axis=-1, keepdims=True)
    p = jnp.exp(s - m)
    l = jnp.sum(p, axis=-1, keepdims=True)
    o = _dot(p.astype(BF16), v_ref[0, 0])
    o_ref[...] = (o / l).astype(o_ref.dtype)


def _mla_attn(q, k, v):
    batch, heads, seq, _ = q.shape
    tq = MLA_Q_TILE
    nq = seq // tq
    return pl.pallas_call(
        _mla_attn_kernel,
        grid=(batch, heads, nq),
        in_specs=[
            pl.BlockSpec((1, 1, tq, MLA_QK), lambda b, h, i: (b, h, i, 0)),
            pl.BlockSpec((1, 1, seq, MLA_QK), lambda b, h, i: (b, h, 0, 0)),
            pl.BlockSpec((1, 1, seq, MLA_V), lambda b, h, i: (b, h, 0, 0)),
        ],
        out_specs=pl.BlockSpec((tq, MLA_V), lambda b, h, i: (b * nq + i, h)),
        out_shape=jax.ShapeDtypeStruct((batch * seq, heads * MLA_V), BF16),
        compiler_params=_params(3),
        name="mla_attn",
    )(q, k, v)


def _dil_prep_kernel(x_ref, mod_ref, g_ref, w_ref, cos_ref, sin_ref, *refs):
    out_refs = refs[:3 * len(DIL_PAIRS)]
    h_scr = refs[3 * len(DIL_PAIRS)]
    mod = mod_ref[0]
    h32 = _norm_mod(x_ref[...], g_ref[...], mod[1:2], mod[0:1])
    n_lane_blocks = h32.shape[1] // LANES
    for j in range(n_lane_blocks):
        h_scr[j] = h32[:, j * LANES:(j + 1) * LANES]
    tm = h32.shape[0]
    gw = DIL_GROUP_W
    for g, (_, dil) in enumerate(DIL_PAIRS):
        n = tm // dil
        if dil == 1:
            hp, cos_t, sin_t = h32, cos_ref[0], sin_ref[0]
        else:
            hp = jnp.concatenate(
                [jnp.concatenate([h_scr[j, pl.ds(r, n, stride=dil), :] for j in range(n_lane_blocks)], axis=1)
                 for r in range(dil)], axis=0)
            cos_t = jnp.concatenate([cos_ref[0, pl.ds(r, n, stride=dil), :] for r in range(dil)], axis=0)
            sin_t = jnp.concatenate([sin_ref[0, pl.ds(r, n, stride=dil), :] for r in range(dil)], axis=0)
        pg = _dot(hp.astype(BF16), w_ref[:, g * 3 * gw:(g + 1) * 3 * gw])
        qg = (_rope(pg[:, :gw], cos_t, sin_t) * (DIL_HEAD_DIM ** -0.5)).astype(BF16)
        kg = _rope(pg[:, gw:2 * gw], cos_t, sin_t).astype(BF16)
        vg = pg[:, 2 * gw:].astype(BF16)
        q_ref, k_ref, v_ref = out_refs[3 * g:3 * g + 3]
        for r in range(dil):
            q_ref[0, r] = qg[r * n:(r + 1) * n]
            k_ref[0, r] = kg[r * n:(r + 1) * n]
            v_ref[0, r] = vg[r * n:(r + 1) * n]


def _dil_prep(x, mod, gain, w, cos_t, sin_t, batch, seq):
    t, d_model = x.shape
    tm = ROW_TILE
    per_b = seq // tm
    out_specs, out_shape = [], []
    for _, dil in DIL_PAIRS:
        for _ in range(3):
            out_specs.append(pl.BlockSpec((1, dil, tm // dil, DIL_GROUP_W),
                                          lambda i: (i // per_b, 0, i % per_b, 0)))
            out_shape.append(jax.ShapeDtypeStruct((batch, dil, seq // dil, DIL_GROUP_W), BF16))
    return pl.pallas_call(
        _dil_prep_kernel,
        grid=(t // tm,),
        in_specs=[
            pl.BlockSpec((tm, d_model), lambda i: (i, 0)),
            pl.BlockSpec((1, ADA_CHUNKS, d_model), lambda i: (i // per_b, 0, 0)),
            _resident((1, d_model)),
            _resident(w.shape),
            pl.BlockSpec((1, tm, LANES), lambda i: (i // per_b, i % per_b, 0)),
            pl.BlockSpec((1, tm, LANES), lambda i: (i // per_b, i % per_b, 0)),
        ],
        out_specs=out_specs,
        out_shape=out_shape,
        scratch_shapes=[pltpu.VMEM((d_model // LANES, tm, LANES), F32)],
        compiler_params=_params(1),
        name="dil_prep",
    )(x, mod, gain, w, cos_t, sin_t)


def _dil_attn_kernel(q_ref, k_ref, v_ref, o_ref, lse_ref, *, dil, length, band):
    qb_rows = DIL_QBLOCK
    nb = length // qb_rows
    kw = min(length, 2 * qb_rows)
    lane = lax.broadcasted_iota(jnp.int32, (qb_rows, LANES), 1)
    low = lane < DIL_HEAD_DIM
    row_i = lax.broadcasted_iota(jnp.int32, (qb_rows, kw), 0)
    col_i = lax.broadcasted_iota(jnp.int32, (qb_rows, kw), 1)

    def tile(idx, carry):
        r = idx // nb
        q0 = pl.multiple_of((idx % nb) * qb_rows, qb_rows)
        ks = pl.multiple_of(jnp.clip(q0 - band, 0, length - kw), band)
        qt = q_ref[0, r, pl.ds(q0, qb_rows), :]
        kt = k_ref[0, r, pl.ds(ks, kw), :]
        vt = v_ref[0, r, pl.ds(ks, kw), :]
        valid = jnp.abs((q0 + row_i) - (ks + col_i)) <= band
        for hp in range(DIL_GROUP_W // LANES):
            cols = slice(hp * LANES, (hp + 1) * LANES)
            qp, kp, vp = qt[:, cols], kt[:, cols], vt[:, cols]
            outs, lses = [], []
            for sel in (low, jnp.logical_not(low)):
                s = _dot_nt(jnp.where(sel, qp, jnp.zeros_like(qp)), kp)
                s = jnp.where(valid, s, NEG_INF)
                m = jnp.max(s, axis=-1, keepdims=True)
                p = jnp.exp(s - m)
                l = jnp.sum(p, axis=-1, keepdims=True)
                outs.append(_dot(p.astype(BF16), vp) / l)
                lses.append(m + jnp.log(l))
            o_ref[0, r, pl.ds(q0, qb_rows), cols] = jnp.where(low, outs[0], outs[1]).astype(o_ref.dtype)
            lse_ref[0, r, pl.ds(q0, qb_rows), cols] = jnp.where(low, lses[0], lses[1])
        return carry

    lax.fori_loop(0, dil * nb, tile, 0)


def _dil_attn(q, k, v, window, dil):
    batch, _, length, gw = q.shape
    band = window // (2 * dil)
    assert band * 2 == DIL_QBLOCK and length % DIL_QBLOCK == 0
    spec = pl.BlockSpec((1, dil, length, gw), lambda b: (b, 0, 0, 0))
    return pl.pallas_call(
        functools.partial(_dil_attn_kernel, dil=dil, length=length, band=band),
        grid=(batch,),
        in_specs=[spec, spec, spec],
        out_specs=[spec, spec],
        out_shape=[jax.ShapeDtypeStruct(q.shape, BF16), jax.ShapeDtypeStruct(q.shape, F32)],
        compiler_params=_params(1),
        name=f"dil_attn_d{dil}",
    )(q, k, v)


def _ret_prep_kernel(x_ref, mod_ref, g_ref, w_ref, cos_ref, sin_ref, q_ref, k_ref, v_ref):
    mod = mod_ref[0]
    h = _norm_mod(x_ref[...], g_ref[...], mod[1:2], mod[0:1]).astype(BF16)
    p = _dot(h, w_ref[...])
    cos_t, sin_t = cos_ref[0], sin_ref[0]
    q_ref[...] = _rope(p[:, :RET_QK_W], cos_t, sin_t).astype(BF16)
    k_ref[...] = (_rope(p[:, RET_QK_W:2 * RET_QK_W], cos_t, sin_t) * (RET_QK_DIM ** -0.5)).astype(BF16)
    v_ref[...] = p[:, 2 * RET_QK_W:].astype(BF16)


def _ret_prep(x, mod, gain, w, cos_t, sin_t, seq):
    t, d_model = x.shape
    tm = ROW_TILE
    per_b = seq // tm
    row = lambda i: (i, 0)
    return pl.pallas_call(
        _ret_prep_kernel,
        grid=(t // tm,),
        in_specs=[
            pl.BlockSpec((tm, d_model), row),
            pl.BlockSpec((1, ADA_CHUNKS, d_model), lambda i: (i // per_b, 0, 0)),
            _resident((1, d_model)),
            _resident(w.shape),
            pl.BlockSpec((1, tm, LANES), lambda i: (i // per_b, i % per_b, 0)),
            pl.BlockSpec((1, tm, LANES), lambda i: (i // per_b, i % per_b, 0)),
        ],
        out_specs=[pl.BlockSpec((tm, RET_QK_W), row), pl.BlockSpec((tm, RET_QK_W), row),
                   pl.BlockSpec((tm, RET_V_W), row)],
        out_shape=[jax.ShapeDtypeStruct((t, RET_QK_W), BF16), jax.ShapeDtypeStruct((t, RET_QK_W), BF16),
                   jax.ShapeDtypeStruct((t, RET_V_W), BF16)],
        compiler_params=_params(1),
        name="ret_prep",
    )(x, mod, gain, w, cos_t, sin_t)


def _ret_scan_kernel(ld_ref, q_ref, k_ref, v_ref, gn_ref, o_ref,
                     yf_ref, st_ref, intra_ref, qdec_ref, kdec_ref, cdec_ref):
    c = RET_CHUNK
    seq = q_ref.shape[1]
    nc = seq // c
    n_pairs = RET_HEADS // 2
    pair_v = 2 * RET_V_DIM
    ia = lax.broadcasted_iota(jnp.int32, (c, c), 0)
    ib = lax.broadcasted_iota(jnp.int32, (c, c), 1)
    diff = (ia - ib).astype(F32)
    idx_q = lax.broadcasted_iota(jnp.int32, (c, pair_v), 0).astype(F32)
    lane_q = lax.broadcasted_iota(jnp.int32, (c, pair_v), 1)
    idx_k = lax.broadcasted_iota(jnp.int32, (c, LANES), 0).astype(F32)
    lane_k = lax.broadcasted_iota(jnp.int32, (c, LANES), 1)
    low_k = lane_k < RET_QK_DIM
    srow = lax.broadcasted_iota(jnp.int32, (LANES, pair_v), 0)
    scol = lax.broadcasted_iota(jnp.int32, (LANES, pair_v), 1)
    blk0 = jnp.logical_and(srow < RET_QK_DIM, scol < RET_V_DIM)
    blk1 = jnp.logical_and(srow >= RET_QK_DIM, scol >= RET_V_DIM)
    diag = jnp.where(jnp.logical_or(blk0, blk1), 1.0, 0.0).astype(F32)

    for dr in range(2):
        for hd in range(RET_HEADS):
            lg = ld_ref[dr, hd]
            if dr == 0:
                mask = diff >= 0
                dist = jnp.where(mask, diff, 0.0)
            else:
                mask = diff < 0
                dist = jnp.where(mask, -diff, 0.0)
            intra_ref[dr, hd] = jnp.where(mask, jnp.exp(lg * dist), 0.0)
        for hp in range(n_pairs):
            lg0 = ld_ref[dr, 2 * hp]
            lg1 = ld_ref[dr, 2 * hp + 1]
            q_exp = idx_q + 1.0 if dr == 0 else c - idx_q
            k_exp = c - 1.0 - idx_k if dr == 0 else idx_k
            qdec_ref[dr, hp] = jnp.exp(jnp.where(lane_q < RET_V_DIM, lg0, lg1) * q_exp)
            kdec_ref[dr, hp] = jnp.exp(jnp.where(low_k, lg0, lg1) * k_exp)
            cdec_ref[dr, hp] = jnp.where(blk0, jnp.exp(lg0 * c), jnp.where(blk1, jnp.exp(lg1 * c), 0.0))

    def chunk(dr, n):
        r0 = pl.multiple_of(n * c, c)
        qc = q_ref[0, pl.ds(r0, c), :]
        kc = k_ref[0, pl.ds(r0, c), :]
        vc = v_ref[0, pl.ds(r0, c), :]
        ys = []
        for hp in range(n_pairs):
            qp = qc[:, hp * LANES:(hp + 1) * LANES]
            kp = kc[:, hp * LANES:(hp + 1) * LANES]
            vp = vc[:, hp * pair_v:(hp + 1) * pair_v]
            zero = jnp.zeros_like(qp)
            s0 = _dot_nt(jnp.where(low_k, qp, zero), kp) * intra_ref[dr, 2 * hp]
            s1 = _dot_nt(jnp.where(low_k, zero, qp), kp) * intra_ref[dr, 2 * hp + 1]
            y_intra = jnp.concatenate([_dot(s0.astype(BF16), vp[:, :RET_V_DIM]),
                                       _dot(s1.astype(BF16), vp[:, RET_V_DIM:])], axis=1)
            state = st_ref[hp]
            y_inter = _dot(qp, state.astype(BF16)) * qdec_ref[dr, hp]
            ys.append(y_intra + y_inter)
            kd = (kp.astype(F32) * kdec_ref[dr, hp]).astype(BF16)
            st_ref[hp] = cdec_ref[dr, hp] * state + diag * _dot_tn(kd, vp)
        return r0, ys

    st_ref[...] = jnp.zeros_like(st_ref)

    def fwd(n, carry):
        r0, ys = chunk(0, n)
        for hp in range(n_pairs):
            yf_ref[pl.ds(r0, c), hp * pair_v:(hp + 1) * pair_v] = ys[hp]
        return carry

    lax.fori_loop(0, nc, fwd, 0)
    st_ref[...] = jnp.zeros_like(st_ref)

    def bwd(i, carry):
        r0, ys = chunk(1, nc - 1 - i)
        for hp in range(n_pairs):
            y = ys[hp] + yf_ref[pl.ds(r0, c), hp * pair_v:(hp + 1) * pair_v]
            for half in range(2):
                cols = slice(hp * pair_v + half * RET_V_DIM, hp * pair_v + (half + 1) * RET_V_DIM)
                yh = y[:, half * RET_V_DIM:(half + 1) * RET_V_DIM]
                mu = jnp.mean(yh, axis=-1, keepdims=True)
                dev = yh - mu
                var = jnp.mean(dev * dev, axis=-1, keepdims=True)
                o_ref[0, pl.ds(r0, c), cols] = (dev * lax.rsqrt(var + EPS) * gn_ref[:, cols]).astype(o_ref.dtype)
        return carry

    lax.fori_loop(0, nc, bwd, 0)


def _ret_scan(log_decay, q, k, v, ret_norm, batch, seq):
    c = RET_CHUNK
    n_pairs = RET_HEADS // 2
    q3 = q.reshape(batch, seq, RET_QK_W)
    k3 = k.reshape(batch, seq, RET_QK_W)
    v3 = v.reshape(batch, seq, RET_V_W)
    out = pl.pallas_call(
        _ret_scan_kernel,
        grid=(batch,),
        in_specs=[
            pl.BlockSpec(memory_space=pltpu.SMEM),
            pl.BlockSpec((1, seq, RET_QK_W), lambda b: (b, 0, 0)),
            pl.BlockSpec((1, seq, RET_QK_W), lambda b: (b, 0, 0)),
            pl.BlockSpec((1, seq, RET_V_W), lambda b: (b, 0, 0)),
            _resident((1, RET_V_W)),
        ],
        out_specs=pl.BlockSpec((1, seq, RET_V_W), lambda b: (b, 0, 0)),
        out_shape=jax.ShapeDtypeStruct((batch, seq, RET_V_W), BF16),
        scratch_shapes=[
            pltpu.VMEM((seq, RET_V_W), F32),
            pltpu.VMEM((n_pairs, LANES, 2 * RET_V_DIM), F32),
            pltpu.VMEM((2, RET_HEADS, c, c), F32),
            pltpu.VMEM((2, n_pairs, c, 2 * RET_V_DIM), F32),
            pltpu.VMEM((2, n_pairs, c, LANES), F32),
            pltpu.VMEM((2, n_pairs, LANES, 2 * RET_V_DIM), F32),
        ],
        compiler_params=_params(1),
        name="ret_scan",
    )(log_decay, q3, k3, v3, ret_norm)
    return out.reshape(batch * seq, RET_V_W)


def _merge_kernel(x_ref, mod_ref, g_ref, wg_ref, ymla_ref, yret_ref,
                  o0_ref, l0_ref, o1_ref, l1_ref, o2_ref, l2_ref,
                  wbm_ref, wbd_ref, wbr_ref, wout_ref, out_ref, nat_o, nat_l):
    x = x_ref[...]
    mod = mod_ref[0]
    h = _norm_mod(x, g_ref[...], mod[1:2], mod[0:1]).astype(BF16)
    gates = _dot(h, wg_ref[...])
    d_model = x.shape[1]
    tm = x.shape[0]

    o_nat = [o0_ref[0, 0].astype(F32)]
    l_nat = [l0_ref[0, 0]]
    for g, (o_ref, l_ref) in enumerate(((o1_ref, l1_ref), (o2_ref, l2_ref))):
        dil = DIL_PAIRS[g + 1][1]
        n = tm // dil
        n_lane_blocks = DIL_GROUP_W // LANES
        for r in range(dil):
            o_r = o_ref[0, r].astype(F32)
            l_r = l_ref[0, r]
            for j in range(n_lane_blocks):
                nat_o[g, j, pl.ds(r, n, stride=dil), :] = o_r[:, j * LANES:(j + 1) * LANES]
                nat_l[g, j, pl.ds(r, n, stride=dil), :] = l_r[:, j * LANES:(j + 1) * LANES]
        o_nat.append(jnp.concatenate([nat_o[g, j] for j in range(n_lane_blocks)], axis=1))
        l_nat.append(jnp.concatenate([nat_l[g, j] for j in range(n_lane_blocks)], axis=1))
    m = jnp.maximum(jnp.maximum(l_nat[0], l_nat[1]), l_nat[2])
    ws = [jnp.exp(l - m) for l in l_nat]
    y_dil = (ws[0] * o_nat[0] + ws[1] * o_nat[1] + ws[2] * o_nat[2]) / (ws[0] + ws[1] + ws[2])

    y_ret = (_silu(gates[:, :RET_V_W]) * yret_ref[...].astype(F32)).astype(BF16)
    ga = gates[:, RET_V_W:RET_V_W + d_model]
    gb = gates[:, RET_V_W + d_model:RET_V_W + 2 * d_model]
    gc = gates[:, RET_V_W + 2 * d_model:]
    merged = (jax.nn.sigmoid(ga) * _dot(ymla_ref[...], wbm_ref[...])
              + jax.nn.sigmoid(gb) * _dot(y_dil.astype(BF16), wbd_ref[...])
              + jax.nn.sigmoid(gc) * _dot(y_ret, wbr_ref[...]))
    out_ref[...] = x + mod[2:3] * _dot(merged.astype(BF16), wout_ref[...])


def _merge_out(x, mod, gain, wg, y_mla, y_ret, dil_outs, wbm, wbd, wbr, wout, batch, seq):
    t, d_model = x.shape
    tm = MERGE_TILE
    per_b = seq // tm
    row = lambda i: (i, 0)
    dil_specs, dil_args = [], []
    for (_, dil), (o, lse) in zip(DIL_PAIRS, dil_outs):
        spec = pl.BlockSpec((1, dil, tm // dil, DIL_GROUP_W), lambda i: (i // per_b, 0, i % per_b, 0))
        dil_specs += [spec, spec]
        dil_args += [o, lse]
    return pl.pallas_call(
        _merge_kernel,
        grid=(t // tm,),
        in_specs=[
            pl.BlockSpec((tm, d_model), row),
            pl.BlockSpec((1, ADA_CHUNKS, d_model), lambda i: (i // per_b, 0, 0)),
            _resident((1, d_model)),
            _resident(wg.shape),
            pl.BlockSpec((tm, y_mla.shape[1]), row),
            pl.BlockSpec((tm, RET_V_W), row),
            *dil_specs,
            _resident(wbm.shape), _resident(wbd.shape), _resident(wbr.shape), _resident(wout.shape),
        ],
        out_specs=pl.BlockSpec((tm, d_model), row),
        out_shape=jax.ShapeDtypeStruct((t, d_model), F32),
        scratch_shapes=[pltpu.VMEM((2, DIL_GROUP_W // LANES, tm, LANES), F32),
                        pltpu.VMEM((2, DIL_GROUP_W // LANES, tm, LANES), F32)],
        compiler_params=_params(1),
        name="merge_out",
    )(x, mod, gain, wg, y_mla, y_ret, *dil_args, wbm, wbd, wbr, wout)


def _ffn_kernel(x_ref, mod_ref, g_ref, w1_ref, w3_ref, w2_ref, out_ref, *, n_chunks):
    x = x_ref[...]
    mod = mod_ref[0]
    h = _norm_mod(x, g_ref[...], mod[4:5], mod[3:4]).astype(BF16)
    d_ff = w1_ref.shape[1]
    fc = d_ff // n_chunks
    acc = jnp.zeros(x.shape, F32)
    for j in range(n_chunks):
        a = _dot(h, w1_ref[:, j * fc:(j + 1) * fc])
        b = _dot(h, w3_ref[:, j * fc:(j + 1) * fc])
        acc = acc + _dot((_silu(a) * b).astype(BF16), w2_ref[j * fc:(j + 1) * fc, :])
    out_ref[...] = x + mod[5:6] * acc


def _ffn_dense(x, mod, gain, w1, w3, w2, seq):
    t, d_model = x.shape
    tm = ROW_TILE
    per_b = seq // tm
    row = lambda i: (i, 0)
    return pl.pallas_call(
        functools.partial(_ffn_kernel, n_chunks=2),
        grid=(t // tm,),
        in_specs=[
            pl.BlockSpec((tm, d_model), row),
            pl.BlockSpec((1, ADA_CHUNKS, d_model), lambda i: (i // per_b, 0, 0)),
            _resident((1, d_model)),
            _resident(w1.shape), _resident(w3.shape), _resident(w2.shape),
        ],
        out_specs=pl.BlockSpec((tm, d_model), row),
        out_shape=jax.ShapeDtypeStruct((t, d_model), F32),
        compiler_params=_params(1),
        name="ffn_dense",
    )(x, mod, gain, w1, w3, w2)


def _route_kernel(x_ref, mod_ref, g_ref, wr_ref, h_ref, route_ref):
    mod = mod_ref[0]
    h = _norm_mod(x_ref[...], g_ref[...], mod[4:5], mod[3:4])
    h_ref[...] = h
    logits = jnp.dot(h, wr_ref[...], preferred_element_type=F32, precision=lax.Precision.HIGHEST)
    lane = lax.broadcasted_iota(jnp.int32, logits.shape, 1).astype(F32)
    lg = jnp.where(lane < N_EXPERTS, logits, -jnp.inf)
    m1 = jnp.max(lg, axis=-1, keepdims=True)
    i1 = jnp.min(jnp.where(lg == m1, lane, float(LANES)), axis=-1, keepdims=True)
    lg2 = jnp.where(lane == i1, -jnp.inf, lg)
    m2 = jnp.max(lg2, axis=-1, keepdims=True)
    i2 = jnp.min(jnp.where(lg2 == m2, lane, float(LANES)), axis=-1, keepdims=True)
    e = jnp.exp(m2 - m1)
    w1 = 1.0 / (1.0 + e)
    w2 = e / (1.0 + e)
    route_ref[...] = jnp.where(lane == 0, i1,
                               jnp.where(lane == 1, i2, jnp.where(lane == 2, w1, jnp.where(lane == 3, w2, 0.0))))


def _moe_route(x, mod, gain, w_router, seq):
    t, d_model = x.shape
    tm = ROW_TILE
    per_b = seq // tm
    row = lambda i: (i, 0)
    wr = jnp.zeros((d_model, LANES), F32).at[:, :N_EXPERTS].set(w_router)
    return pl.pallas_call(
        _route_kernel,
        grid=(t // tm,),
        in_specs=[
            pl.BlockSpec((tm, d_model), row),
            pl.BlockSpec((1, ADA_CHUNKS, d_model), lambda i: (i // per_b, 0, 0)),
            _resident((1, d_model)),
            _resident(wr.shape),
        ],
        out_specs=[pl.BlockSpec((tm, d_model), row), pl.BlockSpec((tm, LANES), row)],
        out_shape=[jax.ShapeDtypeStruct((t, d_model), F32), jax.ShapeDtypeStruct((t, LANES), F32)],
        compiler_params=_params(1),
        name="moe_route",
    )(x, mod, gain, wr)


def _row_copy(src_hbm, dst, src_row, dst_row, sem):
    return pltpu.make_async_copy(src_hbm.at[pl.ds(src_row, 1), :], dst.at[pl.ds(dst_row, 1), :], sem)


def _expert_kernel(be_ref, nact_ref, tok_ref, tok_next_ref, h_hbm, w1_ref, w3_ref, w2_ref, y_ref,
                   xbuf, sems, *, n_chunks):
    i = pl.program_id(0)
    nact = nact_ref[0]
    bm = y_ref.shape[0]
    slot = i % 2

    def gather(tok_blk_ref, dst_slot):
        def issue(r, carry):
            _row_copy(h_hbm, xbuf.at[dst_slot], tok_blk_ref[0, 0, r], r, sems.at[dst_slot]).start()
            return carry

        lax.fori_loop(0, bm, issue, 0, unroll=8)

    @pl.when(i == 0)
    def _():
        gather(tok_ref, 0)

    @pl.when(i + 1 < nact)
    def _():
        gather(tok_next_ref, 1 - slot)

    @pl.when(i >= nact)
    def _():
        y_ref[...] = jnp.zeros_like(y_ref)

    @pl.when(i < nact)
    def _():
        pltpu.make_async_copy(h_hbm.at[pl.ds(0, bm), :], xbuf.at[slot], sems.at[slot]).wait()
        xb = xbuf[slot].astype(BF16)
        d_ff = w1_ref.shape[2]
        fc = d_ff // n_chunks
        acc = jnp.zeros(y_ref.shape, F32)
        for j in range(n_chunks):
            a = _dot(xb, w1_ref[0, :, j * fc:(j + 1) * fc])
            b = _dot(xb, w3_ref[0, :, j * fc:(j + 1) * fc])
            acc = acc + _dot((_silu(a) * b).astype(BF16), w2_ref[0, j * fc:(j + 1) * fc, :])
        y_ref[...] = acc


def _moe_expert(block_expert, nact, row_tok, h, w1, w3, w2):
    nb, _, bm = row_tok.shape
    d_model = h.shape[1]
    d_ff = w1.shape[2]
    wsel = lambda i, be, na: (be[jnp.minimum(i, na[0] - 1)], 0, 0)
    return pl.pallas_call(
        functools.partial(_expert_kernel, n_chunks=2),
        grid_spec=pltpu.PrefetchScalarGridSpec(
            num_scalar_prefetch=2,
            grid=(nb,),
            in_specs=[
                pl.BlockSpec((1, 1, bm), lambda i, be, na: (i, 0, 0), memory_space=pltpu.SMEM),
                pl.BlockSpec((1, 1, bm), lambda i, be, na: (jnp.minimum(i + 1, nb - 1), 0, 0),
                             memory_space=pltpu.SMEM),
                pl.BlockSpec(memory_space=pl.ANY),
                pl.BlockSpec((1, d_model, d_ff), wsel, pipeline_mode=pl.Buffered(1)),
                pl.BlockSpec((1, d_model, d_ff), wsel, pipeline_mode=pl.Buffered(1)),
                pl.BlockSpec((1, d_ff, d_model), wsel, pipeline_mode=pl.Buffered(1)),
            ],
            out_specs=pl.BlockSpec((bm, d_model), lambda i, be, na: (i, 0)),
            scratch_shapes=[pltpu.VMEM((2, bm, d_model), F32), pltpu.SemaphoreType.DMA((2,))],
        ),
        out_shape=jax.ShapeDtypeStruct((nb * bm, d_model), F32),
        compiler_params=_params(1),
        name="moe_expert",
    )(block_expert, nact, row_tok, row_tok, h, w1, w3, w2)


def _combine_kernel(dest_ref, x_ref, mod_ref, route_ref, yb_hbm, out_ref, buf, sem):
    tm = x_ref.shape[0]

    def issue(r, carry):
        _row_copy(yb_hbm, buf, dest_ref[0, 0, r], r, sem).start()
        return carry

    lax.fori_loop(0, TOP_K * tm, issue, 0)
    pltpu.make_async_copy(yb_hbm.at[pl.ds(0, TOP_K * tm), :], buf, sem).wait()
    route = route_ref[...]
    y = route[:, 2:3] * buf[:tm, :] + route[:, 3:4] * buf[tm:, :]
    out_ref[...] = x_ref[...] + mod_ref[0][5:6] * y


def _moe_combine(dest, x, mod, route, yb, seq):
    t, d_model = x.shape
    tm = COMBINE_TILE
    per_b = seq // tm
    row = lambda i: (i, 0)
    return pl.pallas_call(
        _combine_kernel,
        grid=(t // tm,),
        in_specs=[
            pl.BlockSpec((1, 1, TOP_K * tm), lambda i: (i, 0, 0), memory_space=pltpu.SMEM),
            pl.BlockSpec((tm, d_model), row),
            pl.BlockSpec((1, ADA_CHUNKS, d_model), lambda i: (i // per_b, 0, 0)),
            pl.BlockSpec((tm, LANES), row),
            pl.BlockSpec(memory_space=pl.ANY),
        ],
        out_specs=pl.BlockSpec((tm, d_model), row),
        out_shape=jax.ShapeDtypeStruct((t, d_model), F32),
        scratch_shapes=[pltpu.VMEM((TOP_K * tm, d_model), F32), pltpu.SemaphoreType.DMA(())],
        compiler_params=_params(1),
        name="moe_combine",
    )(dest, x, mod, route, yb)


def _moe(x, mod, gain, w_router, w1, w3, w2, seq):
    t, d_model = x.shape
    bm = MOE_BLOCK
    h, route = _moe_route(x, mod, gain, w_router, seq)
    e_flat = route[:, :TOP_K].astype(jnp.int32).reshape(-1)
    n_assign = e_flat.shape[0]
    onehot = (e_flat[:, None] == jnp.arange(N_EXPERTS, dtype=jnp.int32)[None, :]).astype(jnp.int32)
    csum = jnp.cumsum(onehot, axis=0)
    rank = jnp.take_along_axis(csum, e_flat[:, None], axis=1)[:, 0] - 1
    counts = csum[-1]
    padded = (counts + bm - 1) // bm * bm
    pends = jnp.cumsum(padded)
    dest = (pends - padded)[e_flat] + rank
    nb = n_assign // bm + N_EXPERTS
    row_tok = jnp.zeros((nb * bm,), jnp.int32).at[dest].set(jnp.arange(n_assign, dtype=jnp.int32) // TOP_K)
    block_start = jnp.arange(nb, dtype=jnp.int32) * bm
    block_expert = jnp.minimum(
        jnp.sum((pends[None, :] <= block_start[:, None]).astype(jnp.int32), axis=1), N_EXPERTS - 1)
    nact = (pends[-1:] // bm).astype(jnp.int32)

    yb = _moe_expert(block_expert, nact, row_tok.reshape(nb, 1, bm), h, w1, w3, w2)
    tmc = COMBINE_TILE
    dest_tiles = dest.reshape(t // tmc, tmc, TOP_K).transpose(0, 2, 1).reshape(t // tmc, 1, TOP_K * tmc)
    return _moe_combine(dest_tiles.astype(jnp.int32), x, mod, route, yb, seq)


def _final_kernel(x_ref, g_ref, o_ref):
    o_ref[...] = _rms(x_ref[...], g_ref[...])


def _final_norm(x, gain):
    t, d_model = x.shape
    tm = ROW_TILE
    return pl.pallas_call(
        _final_kernel,
        grid=(t // tm,),
        in_specs=[pl.BlockSpec((tm, d_model), lambda i: (i, 0)), _resident((1, d_model))],
        out_specs=pl.BlockSpec((tm, d_model), lambda i: (i, 0)),
        out_shape=jax.ShapeDtypeStruct((t, d_model), F32),
        compiler_params=_params(1),
        name="final_norm",
    )(x, gain)


def _rope_tables(positions):
    inv_freq = ROPE_THETA ** (-jnp.arange(0, ROPE_DIM, 2, dtype=F32) / ROPE_DIM)
    ang = positions.astype(F32)[..., None] * inv_freq
    cos, sin = jnp.cos(ang), jnp.sin(ang)
    reps = LANES // ROPE_DIM
    cos_t = jnp.tile(jnp.concatenate([cos, cos], axis=-1), (1, 1, reps))
    sin_t = jnp.tile(jnp.concatenate([-sin, sin], axis=-1), (1, 1, reps))
    return cos_t, sin_t


def _split_w_in(w_in):
    d_model = w_in.shape[0]
    sizes = (MLA_Q_RANK, MLA_KV_RANK, MLA_ROPE,
             3 * DIL_GROUP_W, 3 * DIL_GROUP_W, 3 * DIL_GROUP_W,
             RET_QK_W, RET_QK_W, RET_V_W, RET_V_W, d_model, d_model, d_model)
    cols, start = [], 0
    for n in sizes:
        cols.append(w_in[:, start:start + n])
        start += n
    cq, ckv, kr, dq, dk, dv, rq, rk, rv, rg, ga, gb, gc = cols
    w_mla = jnp.concatenate([cq, ckv, kr, kr], axis=1)
    gw = DIL_GROUP_W
    w_dil = jnp.concatenate(
        [m[:, g * gw:(g + 1) * gw] for g in range(len(DIL_PAIRS)) for m in (dq, dk, dv)], axis=1)
    w_ret = jnp.concatenate([rq, rk, rv], axis=1)
    w_gate = jnp.concatenate([rg, ga, gb, gc], axis=1)
    return tuple(m.astype(BF16) for m in (w_mla, w_dil, w_ret, w_gate))


def _split_w_uq(w_uq):
    w = w_uq.reshape(MLA_Q_RANK, MLA_HEADS, MLA_QK)
    nope = w[:, :, :MLA_NOPE].reshape(MLA_Q_RANK, MLA_HEADS * MLA_NOPE)
    rope = w[:, :, MLA_NOPE:].reshape(MLA_Q_RANK, MLA_HEADS * MLA_ROPE)
    return jnp.concatenate([nope, rope], axis=1).astype(BF16)


def kernel(x, c, positions, ada_w, ada_b, norm_mix, norm_ffn, w_in, mla_q_norm, mla_w_uq, mla_kv_norm,
           mla_w_ukv, ret_log_decay, ret_norm, w_br_mla, w_br_dil, w_br_ret, w_out, ffn_w1, ffn_w3, ffn_w2,
           moe_router, moe_w1, moe_w3, moe_w2, final_norm):
    batch, seq, d_model = x.shape
    depth = ada_w.shape[0]
    cos_t, sin_t = _rope_tables(positions)
    mod_all = _ada(c, ada_w, ada_b)
    xt = x.reshape(batch * seq, d_model)
    for layer in range(depth):
        mod = mod_all[layer]
        gmix = norm_mix[layer].reshape(1, d_model)
        w_mla, w_dil, w_ret, w_gate = _split_w_in(w_in[layer])

        q, k, v = _mla_prep(xt, mod, gmix, w_mla, mla_q_norm[layer].reshape(1, -1),
                            mla_kv_norm[layer].reshape(1, -1), _split_w_uq(mla_w_uq[layer]),
                            mla_w_ukv[layer].astype(BF16), cos_t, sin_t, batch, seq)
        y_mla = _mla_attn(q, k, v)

        dil_qkv = _dil_prep(xt, mod, gmix, w_dil, cos_t, sin_t, batch, seq)
        dil_outs = [_dil_attn(*dil_qkv[3 * g:3 * g + 3], window, dil)
                    for g, (window, dil) in enumerate(DIL_PAIRS)]

        rq, rk, rv = _ret_prep(xt, mod, gmix, w_ret, cos_t, sin_t, seq)
        y_ret = _ret_scan(ret_log_decay[layer].astype(F32), rq, rk, rv,
                          ret_norm[layer].reshape(1, -1).astype(F32), batch, seq)

        xt = _merge_out(xt, mod, gmix, w_gate, y_mla, y_ret, dil_outs,
                        w_br_mla[layer].astype(BF16), w_br_dil[layer].astype(BF16),
                        w_br_ret[layer].astype(BF16), w_out[layer].astype(BF16), batch, seq)

        gffn = norm_ffn[layer].reshape(1, d_model)
        i = layer // 2
        if layer % 2 == 0:
            xt = _ffn_dense(xt, mod, gffn, ffn_w1[i].astype(BF16), ffn_w3[i].astype(BF16),
                            ffn_w2[i].astype(BF16), seq)
        else:
            xt = _moe(xt, mod, gffn, moe_router[i], moe_w1[i].astype(BF16), moe_w3[i].astype(BF16),
                      moe_w2[i].astype(BF16), seq)
    return _final_norm(xt, final_norm.reshape(1, d_model)).reshape(batch, seq, d_model)
```

```python
import functools

import jax
import jax.numpy as jnp
from jax import lax
from jax.experimental import pallas as pl
from jax.experimental.pallas import tpu as pltpu

F32 = jnp.float32
BF16 = jnp.bfloat16

EPS = 1e-6
NEG_INF = -1e30
ROPE_THETA = 10000.0
ROPE_DIM = 64
ADA_CHUNKS = 6

MLA_HEADS = 8
MLA_Q_RANK = 768
MLA_KV_RANK = 512
MLA_NOPE = 128
MLA_ROPE = ROPE_DIM
MLA_V = 128
MLA_QK = MLA_NOPE + MLA_ROPE

DIL_PAIRS = ((128, 1), (512, 4), (2048, 16))
DIL_HEADS = 8
DIL_HEAD_DIM = ROPE_DIM
DIL_GROUP_W = DIL_HEADS * DIL_HEAD_DIM
DIL_QBLOCK = 128

RET_HEADS = 8
RET_QK_DIM = ROPE_DIM
RET_V_DIM = 2 * RET_QK_DIM
RET_CHUNK = 128
RET_UNROLL = 2
RET_QK_W = RET_HEADS * RET_QK_DIM
RET_V_W = RET_HEADS * RET_V_DIM

N_EXPERTS = 8
TOP_K = 2
MOE_BLOCK = 256

LANES = 128
V7X_VMEM_LIMIT_BYTES = 56 * 1024 * 1024

ROW_TILE = 512
MERGE_TILE = 256
COMBINE_TILE = 256
MLA_Q_TILE = 512
MLA_HEADS_PER_STEP = 4
MLA_KEY_CHUNK = 1024


def _params(n_grid_dims):
    return pltpu.CompilerParams(
        dimension_semantics=("arbitrary",) * n_grid_dims,
        vmem_limit_bytes=V7X_VMEM_LIMIT_BYTES,
    )


def _resident(shape):
    zeros = (0,) * len(shape)
    return pl.BlockSpec(shape, lambda *_: zeros, pipeline_mode=pl.Buffered(1))


def _silu(v):
    return v * jax.nn.sigmoid(v)


def _norm_mod(x, gain, scale, shift):
    ms = jnp.mean(x * x, axis=-1, keepdims=True)
    y = x * lax.rsqrt(ms + EPS) * gain
    return y * (1.0 + scale) + shift


def _rms(v, gain):
    ms = jnp.mean(v * v, axis=-1, keepdims=True)
    return v * lax.rsqrt(ms + EPS) * gain


def _rope(t, cos_t, sin_t):
    lane = lax.broadcasted_iota(jnp.int32, (t.shape[0], LANES), 1)
    first_half = (lane % ROPE_DIM) < (ROPE_DIM // 2)
    out = []
    for j in range(t.shape[1] // LANES):
        c = t[:, j * LANES:(j + 1) * LANES]
        rot = jnp.where(first_half,
                        pltpu.roll(c, LANES - ROPE_DIM // 2, 1),
                        pltpu.roll(c, ROPE_DIM // 2, 1))
        out.append(c * cos_t + rot * sin_t)
    return out[0] if len(out) == 1 else jnp.concatenate(out, axis=1)


def _dot(a, b):
    return jnp.dot(a, b, preferred_element_type=F32)


def _dot_nt(a, b):
    return lax.dot_general(a, b, (((1,), (1,)), ((), ())), preferred_element_type=F32)


def _dot_tn(a, b):
    return lax.dot_general(a, b, (((0,), (0,)), ((), ())), preferred_element_type=F32)


def _ada_kernel(c_ref, w_ref, b_ref, o_ref):
    o_ref[0] = jnp.dot(_silu(c_ref[...]), w_ref[0], preferred_element_type=F32,
                       precision=lax.Precision.HIGHEST) + b_ref[0]


def _ada(c, ada_w, ada_b):
    depth, d_model, n = ada_w.shape
    batch = c.shape[0]
    tn = n // 4
    out = pl.pallas_call(
        _ada_kernel,
        grid=(depth, n // tn),
        in_specs=[
            pl.BlockSpec((batch, d_model), lambda l, j: (0, 0)),
            pl.BlockSpec((1, d_model, tn), lambda l, j: (l, 0, j)),
            pl.BlockSpec((1, 1, tn), lambda l, j: (l, 0, j)),
        ],
        out_specs=pl.BlockSpec((1, batch, tn), lambda l, j: (l, 0, j)),
        out_shape=jax.ShapeDtypeStruct((depth, batch, n), F32),
        compiler_params=_params(2),
        name="ada",
    )(c, ada_w, ada_b.reshape(depth, 1, n))
    return out.reshape(depth, batch, ADA_CHUNKS, d_model)


def _mla_prep_kernel(x_ref, mod_ref, g_ref, wa_ref, qn_ref, kvn_ref, wuq_ref, wukv_ref, cos_ref, sin_ref,
                     q_ref, k_ref, v_ref):
    mod = mod_ref[0]
    h = _norm_mod(x_ref[...], g_ref[...], mod[1:2], mod[0:1]).astype(BF16)
    p = _dot(h, wa_ref[...])
    cq = _rms(p[:, :MLA_Q_RANK], qn_ref[...]).astype(BF16)
    ckv = _rms(p[:, MLA_Q_RANK:MLA_Q_RANK + MLA_KV_RANK], kvn_ref[...]).astype(BF16)
    cos_t = cos_ref[0]
    sin_t = sin_ref[0]
    kr = _rope(p[:, MLA_Q_RANK + MLA_KV_RANK:], cos_t, sin_t)
    q = _dot(cq, wuq_ref[...])
    kv = _dot(ckv, wukv_ref[...])
    nope_w = MLA_HEADS * MLA_NOPE
    qr = _rope(q[:, nope_w:], cos_t, sin_t)
    scale = MLA_QK ** -0.5
    kr_b = kr[:, :MLA_ROPE].astype(BF16)
    for hd in range(MLA_HEADS):
        q_ref[0, hd, :, :MLA_NOPE] = (q[:, hd * MLA_NOPE:(hd + 1) * MLA_NOPE] * scale).astype(BF16)
        q_ref[0, hd, :, MLA_NOPE:] = (qr[:, hd * MLA_ROPE:(hd + 1) * MLA_ROPE] * scale).astype(BF16)
        base = hd * (MLA_NOPE + MLA_V)
        k_ref[0, hd, :, :MLA_NOPE] = kv[:, base:base + MLA_NOPE].astype(BF16)
        k_ref[0, hd, :, MLA_NOPE:] = kr_b
        v_ref[0, hd] = kv[:, base + MLA_NOPE:base + MLA_NOPE + MLA_V].astype(BF16)


def _mla_prep(x, mod, gain, wa, qn, kvn, wuq, wukv, cos_t, sin_t, batch, seq):
    t, d_model = x.shape
    tm = ROW_TILE
    per_b = seq // tm
    row = lambda i: (i, 0)
    hs = lambda i: (i // per_b, 0, i % per_b, 0)
    return pl.pallas_call(
        _mla_prep_kernel,
        grid=(t // tm,),
        in_specs=[
            pl.BlockSpec((tm, d_model), row),
            pl.BlockSpec((1, ADA_CHUNKS, d_model), lambda i: (i // per_b, 0, 0)),
            _resident((1, d_model)),
            _resident(wa.shape),
            _resident((1, MLA_Q_RANK)),
            _resident((1, MLA_KV_RANK)),
            _resident(wuq.shape),
            _resident(wukv.shape),
            pl.BlockSpec((1, tm, LANES), lambda i: (i // per_b, i % per_b, 0)),
            pl.BlockSpec((1, tm, LANES), lambda i: (i // per_b, i % per_b, 0)),
        ],
        out_specs=[
            pl.BlockSpec((1, MLA_HEADS, tm, MLA_QK), hs),
            pl.BlockSpec((1, MLA_HEADS, tm, MLA_QK), hs),
            pl.BlockSpec((1, MLA_HEADS, tm, MLA_V), hs),
        ],
        out_shape=[
            jax.ShapeDtypeStruct((batch, MLA_HEADS, seq, MLA_QK), BF16),
            jax.ShapeDtypeStruct((batch, MLA_HEADS, seq, MLA_QK), BF16),
            jax.ShapeDtypeStruct((batch, MLA_HEADS, seq, MLA_V), BF16),
        ],
        compiler_params=_params(1),
        name="mla_prep",
    )(x, mod, gain, wa, qn, kvn, wuq, wukv, cos_t, sin_t)


def _mla_attn_kernel(q_ref, k_ref, v_ref, o_ref, *, heads_per_step, key_chunk):
    seq = k_ref.shape[2]
    ones = jnp.ones((key_chunk, LANES), BF16)
    for hd in range(heads_per_step):
        q = q_ref[0, hd]
        m = acc = None
        for c in range(seq // key_chunk):
            rows = slice(c * key_chunk, (c + 1) * key_chunk)
            s = _dot_nt(q, k_ref[0, hd, rows, :])
            v_ext = jnp.concatenate([v_ref[0, hd, rows, :], ones], axis=1)
            m_c = jnp.max(s, axis=-1, keepdims=True)
            if c == 0:
                m = m_c
                acc = _dot(jnp.exp(s - m).astype(BF16), v_ext)
            else:
                m_new = jnp.maximum(m, m_c)
                acc = jnp.exp(m - m_new) * acc + _dot(jnp.exp(s - m_new).astype(BF16), v_ext)
                m = m_new
        o_ref[:, hd * MLA_V:(hd + 1) * MLA_V] = (acc[:, :MLA_V] / acc[:, MLA_V:MLA_V + 1]).astype(o_ref.dtype)


def _mla_attn(q, k, v):
    batch, heads, seq, _ = q.shape
    tq = MLA_Q_TILE
    nq = seq // tq
    hps = MLA_HEADS_PER_STEP
    return pl.pallas_call(
        functools.partial(_mla_attn_kernel, heads_per_step=hps, key_chunk=MLA_KEY_CHUNK),
        grid=(batch, heads // hps, nq),
        in_specs=[
            pl.BlockSpec((1, hps, tq, MLA_QK), lambda b, h, i: (b, h, i, 0)),
            pl.BlockSpec((1, hps, seq, MLA_QK), lambda b, h, i: (b, h, 0, 0)),
            pl.BlockSpec((1, hps, seq, MLA_V), lambda b, h, i: (b, h, 0, 0)),
        ],
        out_specs=pl.BlockSpec((tq, hps * MLA_V), lambda b, h, i: (b * nq + i, h)),
        out_shape=jax.ShapeDtypeStruct((batch * seq, heads * MLA_V), BF16),
        compiler_params=_params(3),
        name="mla_attn",
    )(q, k, v)


def _dil_prep_kernel(x_ref, mod_ref, g_ref, w_ref, cos_ref, sin_ref, *refs):
    out_refs = refs[:3 * len(DIL_PAIRS)]
    h_scr = refs[3 * len(DIL_PAIRS)]
    mod = mod_ref[0]
    h32 = _norm_mod(x_ref[...], g_ref[...], mod[1:2], mod[0:1])
    n_lane_blocks = h32.shape[1] // LANES
    for j in range(n_lane_blocks):
        h_scr[j] = h32[:, j * LANES:(j + 1) * LANES]
    tm = h32.shape[0]
    gw = DIL_GROUP_W
    for g, (_, dil) in enumerate(DIL_PAIRS):
        n = tm // dil
        if dil == 1:
            hp, cos_t, sin_t = h32, cos_ref[0], sin_ref[0]
        else:
            hp = jnp.concatenate(
                [jnp.concatenate([h_scr[j, pl.ds(r, n, stride=dil), :] for j in range(n_lane_blocks)], axis=1)
                 for r in range(dil)], axis=0)
            cos_t = jnp.concatenate([cos_ref[0, pl.ds(r, n, stride=dil), :] for r in range(dil)], axis=0)
            sin_t = jnp.concatenate([sin_ref[0, pl.ds(r, n, stride=dil), :] for r in range(dil)], axis=0)
        pg = _dot(hp.astype(BF16), w_ref[:, g * 3 * gw:(g + 1) * 3 * gw])
        qg = (_rope(pg[:, :gw], cos_t, sin_t) * (DIL_HEAD_DIM ** -0.5)).astype(BF16)
        kg = _rope(pg[:, gw:2 * gw], cos_t, sin_t).astype(BF16)
        vg = pg[:, 2 * gw:].astype(BF16)
        q_ref, k_ref, v_ref = out_refs[3 * g:3 * g + 3]
        for r in range(dil):
            q_ref[0, r] = qg[r * n:(r + 1) * n]
            k_ref[0, r] = kg[r * n:(r + 1) * n]
            v_ref[0, r] = vg[r * n:(r + 1) * n]


def _dil_prep(x, mod, gain, w, cos_t, sin_t, batch, seq):
    t, d_model = x.shape
    tm = ROW_TILE
    per_b = seq // tm
    out_specs, out_shape = [], []
    for _, dil in DIL_PAIRS:
        for _ in range(3):
            out_specs.append(pl.BlockSpec((1, dil, tm // dil, DIL_GROUP_W),
                                          lambda i: (i // per_b, 0, i % per_b, 0)))
            out_shape.append(jax.ShapeDtypeStruct((batch, dil, seq // dil, DIL_GROUP_W), BF16))
    return pl.pallas_call(
        _dil_prep_kernel,
        grid=(t // tm,),
        in_specs=[
            pl.BlockSpec((tm, d_model), lambda i: (i, 0)),
            pl.BlockSpec((1, ADA_CHUNKS, d_model), lambda i: (i // per_b, 0, 0)),
            _resident((1, d_model)),
            _resident(w.shape),
            pl.BlockSpec((1, tm, LANES), lambda i: (i // per_b, i % per_b, 0)),
            pl.BlockSpec((1, tm, LANES), lambda i: (i // per_b, i % per_b, 0)),
        ],
        out_specs=out_specs,
        out_shape=out_shape,
        scratch_shapes=[pltpu.VMEM((d_model // LANES, tm, LANES), F32)],
        compiler_params=_params(1),
        name="dil_prep",
    )(x, mod, gain, w, cos_t, sin_t)


def _dil_attn_kernel(q_ref, k_ref, v_ref, o_ref, lse_ref, *, n_rows, length, segment, band):
    qb_rows = DIL_QBLOCK
    kw = 2 * qb_rows
    nb = length // qb_rows
    low_k = lax.broadcasted_iota(jnp.int32, (kw, LANES), 1).astype(F32).astype(BF16) < DIL_HEAD_DIM
    low_q = lax.broadcasted_iota(jnp.int32, (qb_rows, LANES), 1) < DIL_HEAD_DIM
    row_i = lax.broadcasted_iota(jnp.int32, (qb_rows, kw), 0)
    col_i = lax.broadcasted_iota(jnp.int32, (qb_rows, kw), 1)
    zeros_k = jnp.zeros((kw, LANES), BF16)
    ones_k = jnp.ones((kw, LANES), BF16)
    denom_cols = jnp.concatenate([jnp.where(low_k, ones_k, zeros_k), jnp.where(low_k, zeros_k, ones_k)], axis=0)

    def tile(idx, carry):
        r = idx // nb
        q0 = pl.multiple_of((idx % nb) * qb_rows, qb_rows)
        ks = pl.multiple_of(jnp.clip(q0 - band, 0, length - kw), band)
        qt = q_ref[0, r, pl.ds(q0, qb_rows), :]
        kt = k_ref[0, r, pl.ds(ks, kw), :]
        vt = v_ref[0, r, pl.ds(ks, kw), :]
        jq = q0 + row_i
        jk = ks + col_i
        valid = jnp.logical_and(jnp.abs(jq - jk) <= band, jq // segment == jk // segment)
        for hp in range(DIL_GROUP_W // LANES):
            cols = slice(hp * LANES, (hp + 1) * LANES)
            qp, kp, vp = qt[:, cols], kt[:, cols], vt[:, cols]
            k2 = jnp.concatenate([jnp.where(low_k, kp, zeros_k), jnp.where(low_k, zeros_k, kp)], axis=0)
            v2 = jnp.concatenate([jnp.where(low_k, vp, zeros_k), jnp.where(low_k, zeros_k, vp)], axis=0)
            s = _dot_nt(qp, k2)
            s0 = jnp.where(valid, s[:, :kw], NEG_INF)
            s1 = jnp.where(valid, s[:, kw:], NEG_INF)
            m0 = jnp.max(s0, axis=-1, keepdims=True)
            m1 = jnp.max(s1, axis=-1, keepdims=True)
            p = jnp.concatenate([jnp.exp(s0 - m0), jnp.exp(s1 - m1)], axis=1).astype(BF16)
            pv = _dot(p, jnp.concatenate([v2, denom_cols], axis=1))
            den = pv[:, LANES:]
            o_ref[0, r, pl.ds(q0, qb_rows), cols] = (pv[:, :LANES] / den).astype(o_ref.dtype)
            lse_ref[0, r, pl.ds(q0, qb_rows), cols] = jnp.where(low_q, m0, m1) + jnp.log(den)
        return carry

    lax.fori_loop(0, n_rows * nb, tile, 0)


def _dil_attn(q, k, v, window, dil):
    batch, _, seg, gw = q.shape
    band = window // (2 * dil)
    assert band * 2 == DIL_QBLOCK and seg % DIL_QBLOCK == 0
    length = max(seg, 2 * DIL_QBLOCK)
    n_rows = dil * seg // length
    view = (batch, n_rows, length, gw)
    spec = pl.BlockSpec((1, n_rows, length, gw), lambda b: (b, 0, 0, 0))
    o, lse = pl.pallas_call(
        functools.partial(_dil_attn_kernel, n_rows=n_rows, length=length, segment=seg, band=band),
        grid=(batch,),
        in_specs=[spec, spec, spec],
        out_specs=[spec, spec],
        out_shape=[jax.ShapeDtypeStruct(view, BF16), jax.ShapeDtypeStruct(view, F32)],
        compiler_params=_params(1),
        name=f"dil_attn_d{dil}",
    )(q.reshape(view), k.reshape(view), v.reshape(view))
    return o.reshape(q.shape), lse.reshape(q.shape)


def _ret_prep_kernel(x_ref, mod_ref, g_ref, w_ref, cos_ref, sin_ref, q_ref, k_ref, v_ref):
    mod = mod_ref[0]
    h = _norm_mod(x_ref[...], g_ref[...], mod[1:2], mod[0:1]).astype(BF16)
    p = _dot(h, w_ref[...])
    cos_t, sin_t = cos_ref[0], sin_ref[0]
    q_ref[...] = _rope(p[:, :RET_QK_W], cos_t, sin_t).astype(BF16)
    k_ref[...] = (_rope(p[:, RET_QK_W:2 * RET_QK_W], cos_t, sin_t) * (RET_QK_DIM ** -0.5)).astype(BF16)
    v_ref[...] = p[:, 2 * RET_QK_W:].astype(BF16)


def _ret_prep(x, mod, gain, w, cos_t, sin_t, seq):
    t, d_model = x.shape
    tm = ROW_TILE
    per_b = seq // tm
    row = lambda i: (i, 0)
    return pl.pallas_call(
        _ret_prep_kernel,
        grid=(t // tm,),
        in_specs=[
            pl.BlockSpec((tm, d_model), row),
            pl.BlockSpec((1, ADA_CHUNKS, d_model), lambda i: (i // per_b, 0, 0)),
            _resident((1, d_model)),
            _resident(w.shape),
            pl.BlockSpec((1, tm, LANES), lambda i: (i // per_b, i % per_b, 0)),
            pl.BlockSpec((1, tm, LANES), lambda i: (i // per_b, i % per_b, 0)),
        ],
        out_specs=[pl.BlockSpec((tm, RET_QK_W), row), pl.BlockSpec((tm, RET_QK_W), row),
                   pl.BlockSpec((tm, RET_V_W), row)],
        out_shape=[jax.ShapeDtypeStruct((t, RET_QK_W), BF16), jax.ShapeDtypeStruct((t, RET_QK_W), BF16),
                   jax.ShapeDtypeStruct((t, RET_V_W), BF16)],
        compiler_params=_params(1),
        name="ret_prep",
    )(x, mod, gain, w, cos_t, sin_t)


def _ret_scan_kernel(ld_ref, q_ref, k_ref, v_ref, gn_ref, o_ref,
                     yf_ref, st_ref, intra_ref, qdec_ref, kdec_ref, cdec_ref):
    c = RET_CHUNK
    seq = q_ref.shape[1]
    nc = seq // c
    n_pairs = RET_HEADS // 2
    pair_v = 2 * RET_V_DIM
    ia = lax.broadcasted_iota(jnp.int32, (c, c), 0)
    ib = lax.broadcasted_iota(jnp.int32, (c, c), 1)
    diff = (ia - ib).astype(F32)
    idx_q = lax.broadcasted_iota(jnp.int32, (c, pair_v), 0).astype(F32)
    lane_q = lax.broadcasted_iota(jnp.int32, (c, pair_v), 1)
    idx_k = lax.broadcasted_iota(jnp.int32, (c, LANES), 0).astype(F32)
    lane_k = lax.broadcasted_iota(jnp.int32, (c, LANES), 1)
    low_k = lane_k < RET_QK_DIM
    low_k16 = lane_k.astype(F32).astype(BF16) < RET_QK_DIM
    low_v16 = lane_q.astype(F32).astype(BF16) < RET_V_DIM
    srow =lax.broadcasted_iota(jnp.int32, (LANES, pair_v), 0)
    scol = lax.broadcasted_iota(jnp.int32, (LANES, pair_v), 1)
    blk0 = jnp.logical_and(srow < RET_QK_DIM, scol < RET_V_DIM)
    blk1 = jnp.logical_and(srow >= RET_QK_DIM, scol >= RET_V_DIM)
    diag = jnp.where(jnp.logical_or(blk0, blk1), 1.0, 0.0).astype(F32)

    for dr in range(2):
        for hd in range(RET_HEADS):
            lg = ld_ref[dr, hd]
            if dr == 0:
                mask = diff >= 0
                dist = jnp.where(mask, diff, 0.0)
            else:
                mask = diff < 0
                dist = jnp.where(mask, -diff, 0.0)
            intra_ref[dr, hd] = jnp.where(mask, jnp.exp(lg * dist), 0.0)
        for hp in range(n_pairs):
            lg0 = ld_ref[dr, 2 * hp]
            lg1 = ld_ref[dr, 2 * hp + 1]
            q_exp = idx_q + 1.0 if dr == 0 else c - idx_q
            k_exp = c - 1.0 - idx_k if dr == 0 else idx_k
            qdec_ref[dr, hp] = jnp.exp(jnp.where(lane_q < RET_V_DIM, lg0, lg1) * q_exp)
            kdec_ref[dr, hp] = jnp.exp(jnp.where(low_k, lg0, lg1) * k_exp)
            cdec_ref[dr, hp] = jnp.where(blk0, jnp.exp(lg0 * c), jnp.where(blk1, jnp.exp(lg1 * c), 0.0))

    def chunk(dr, n):
        r0 = pl.multiple_of(n * c, c)
        qc = q_ref[0, pl.ds(r0, c), :]
        kc = k_ref[0, pl.ds(r0, c), :]
        vc = v_ref[0, pl.ds(r0, c), :]
        ys = []
        for hp in range(n_pairs):
            qp = qc[:, hp * LANES:(hp + 1) * LANES]
            kp = kc[:, hp * LANES:(hp + 1) * LANES]
            vp = vc[:, hp * pair_v:(hp + 1) * pair_v]
            zk = jnp.zeros_like(kp)
            zv = jnp.zeros_like(vp)
            k2 = jnp.concatenate([jnp.where(low_k16, kp, zk), jnp.where(low_k16, zk, kp)], axis=0)
            v2 = jnp.concatenate([jnp.where(low_v16, vp, zv), jnp.where(low_v16, zv, vp)], axis=0)
            s = _dot_nt(qp, k2) * jnp.concatenate([intra_ref[dr, 2 * hp], intra_ref[dr, 2 * hp + 1]], axis=1)
            y_intra = _dot(s.astype(BF16), v2)
            state = st_ref[hp]
            y_inter = _dot(qp, state.astype(BF16)) * qdec_ref[dr, hp]
            ys.append(y_intra + y_inter)
            kd = (kp.astype(F32) * kdec_ref[dr, hp]).astype(BF16)
            st_ref[hp] = cdec_ref[dr, hp] * state + diag * _dot_tn(kd, vp)
        return r0, ys

    st_ref[...] = jnp.zeros_like(st_ref)

    def fwd(n, carry):
        r0, ys = chunk(0, n)
        for hp in range(n_pairs):
            yf_ref[pl.ds(r0, c), hp * pair_v:(hp + 1) * pair_v] = ys[hp]
        return carry

    lax.fori_loop(0, nc, fwd, 0, unroll=RET_UNROLL)
    st_ref[...] = jnp.zeros_like(st_ref)

    def bwd(i, carry):
        r0, ys = chunk(1, nc - 1 - i)
        for hp in range(n_pairs):
            y = ys[hp] + yf_ref[pl.ds(r0, c), hp * pair_v:(hp + 1) * pair_v]
            for half in range(2):
                cols = slice(hp * pair_v + half * RET_V_DIM, hp * pair_v + (half + 1) * RET_V_DIM)
                yh = y[:, half * RET_V_DIM:(half + 1) * RET_V_DIM]
                mu = jnp.mean(yh, axis=-1, keepdims=True)
                dev = yh - mu
                var = jnp.mean(dev * dev, axis=-1, keepdims=True)
                o_ref[0, pl.ds(r0, c), cols] = (dev * lax.rsqrt(var + EPS) * gn_ref[:, cols]).astype(o_ref.dtype)
        return carry

    lax.fori_loop(0, nc, bwd, 0, unroll=RET_UNROLL)


def _ret_scan(log_decay, q, k, v, ret_norm, batch, seq):
    c = RET_CHUNK
    n_pairs = RET_HEADS // 2
    q3 = q.reshape(batch, seq, RET_QK_W)
    k3 = k.reshape(batch, seq, RET_QK_W)
    v3 = v.reshape(batch, seq, RET_V_W)
    out = pl.pallas_call(
        _ret_scan_kernel,
        grid=(batch,),
        in_specs=[
            pl.BlockSpec(memory_space=pltpu.SMEM),
            pl.BlockSpec((1, seq, RET_QK_W), lambda b: (b, 0, 0)),
            pl.BlockSpec((1, seq, RET_QK_W), lambda b: (b, 0, 0)),
            pl.BlockSpec((1, seq, RET_V_W), lambda b: (b, 0, 0)),
            _resident((1, RET_V_W)),
        ],
        out_specs=pl.BlockSpec((1, seq, RET_V_W), lambda b: (b, 0, 0)),
        out_shape=jax.ShapeDtypeStruct((batch, seq, RET_V_W), BF16),
        scratch_shapes=[
            pltpu.VMEM((seq, RET_V_W), F32),
            pltpu.VMEM((n_pairs, LANES, 2 * RET_V_DIM), F32),
            pltpu.VMEM((2, RET_HEADS, c, c), F32),
            pltpu.VMEM((2, n_pairs, c, 2 * RET_V_DIM), F32),
            pltpu.VMEM((2, n_pairs, c, LANES), F32),
            pltpu.VMEM((2, n_pairs, LANES, 2 * RET_V_DIM), F32),
        ],
        compiler_params=_params(1),
        name="ret_scan",
    )(log_decay, q3, k3, v3, ret_norm)
    return out.reshape(batch * seq, RET_V_W)


def _merge_kernel(x_ref, mod_ref, g_ref, wg_ref, ymla_ref, yret_ref,
                  o0_ref, l0_ref, o1_ref, l1_ref, o2_ref, l2_ref,
                  wbm_ref, wbd_ref, wbr_ref, wout_ref, out_ref, nat_o, nat_l):
    x = x_ref[...]
    mod = mod_ref[0]
    h = _norm_mod(x, g_ref[...], mod[1:2], mod[0:1]).astype(BF16)
    gates = _dot(h, wg_ref[...])
    d_model = x.shape[1]
    tm = x.shape[0]

    o_nat = [o0_ref[0, 0].astype(F32)]
    l_nat = [l0_ref[0, 0]]
    for g, (o_ref, l_ref) in enumerate(((o1_ref, l1_ref), (o2_ref, l2_ref))):
        dil = DIL_PAIRS[g + 1][1]
        n = tm // dil
        n_lane_blocks = DIL_GROUP_W // LANES
        for r in range(dil):
            o_r = o_ref[0, r].astype(F32)
            l_r = l_ref[0, r]
            for j in range(n_lane_blocks):
                nat_o[g, j, pl.ds(r, n, stride=dil), :] = o_r[:, j * LANES:(j + 1) * LANES]
                nat_l[g, j, pl.ds(r, n, stride=dil), :] = l_r[:, j * LANES:(j + 1) * LANES]
        o_nat.append(jnp.concatenate([nat_o[g, j] for j in range(n_lane_blocks)], axis=1))
        l_nat.append(jnp.concatenate([nat_l[g, j] for j in range(n_lane_blocks)], axis=1))
    m = jnp.maximum(jnp.maximum(l_nat[0], l_nat[1]), l_nat[2])
    ws = [jnp.exp(l - m) for l in l_nat]
    y_dil = (ws[0] * o_nat[0] + ws[1] * o_nat[1] + ws[2] * o_nat[2]) / (ws[0] + ws[1] + ws[2])

    y_ret = (_silu(gates[:, :RET_V_W]) * yret_ref[...].astype(F32)).astype(BF16)
    ga = gates[:, RET_V_W:RET_V_W + d_model]
    gb = gates[:, RET_V_W + d_model:RET_V_W + 2 * d_model]
    gc = gates[:, RET_V_W + 2 * d_model:]
    merged = (jax.nn.sigmoid(ga) * _dot(ymla_ref[...], wbm_ref[...])
              + jax.nn.sigmoid(gb) * _dot(y_dil.astype(BF16), wbd_ref[...])
              + jax.nn.sigmoid(gc) * _dot(y_ret, wbr_ref[...]))
    out_ref[...] = x + mod[2:3] * _dot(merged.astype(BF16), wout_ref[...])


def _merge_out(x, mod, gain, wg, y_mla, y_ret, dil_outs, wbm, wbd, wbr, wout, batch, seq):
    t, d_model = x.shape
    tm = MERGE_TILE
    per_b = seq // tm
    row = lambda i: (i, 0)
    dil_specs, dil_args = [], []
    for (_, dil), (o, lse) in zip(DIL_PAIRS, dil_outs):
        spec = pl.BlockSpec((1, dil, tm // dil, DIL_GROUP_W), lambda i: (i // per_b, 0, i % per_b, 0))
        dil_specs += [spec, spec]
        dil_args += [o, lse]
    return pl.pallas_call(
        _merge_kernel,
        grid=(t // tm,),
        in_specs=[
            pl.BlockSpec((tm, d_model), row),
            pl.BlockSpec((1, ADA_CHUNKS, d_model), lambda i: (i // per_b, 0, 0)),
            _resident((1, d_model)),
            _resident(wg.shape),
            pl.BlockSpec((tm, y_mla.shape[1]), row),
            pl.BlockSpec((tm, RET_V_W), row),
            *dil_specs,
            _resident(wbm.shape), _resident(wbd.shape), _resident(wbr.shape), _resident(wout.shape),
        ],
        out_specs=pl.BlockSpec((tm, d_model), row),
        out_shape=jax.ShapeDtypeStruct((t, d_model), F32),
        scratch_shapes=[pltpu.VMEM((2, DIL_GROUP_W // LANES, tm, LANES), F32),
                        pltpu.VMEM((2, DIL_GROUP_W // LANES, tm, LANES), F32)],
        compiler_params=_params(1),
        name="merge_out",
    )(x, mod, gain, wg, y_mla, y_ret, *dil_args, wbm, wbd, wbr, wout)


def _ffn_kernel(x_ref, mod_ref, g_ref, w1_ref, w3_ref, w2_ref, out_ref, *, n_chunks):
    x = x_ref[...]
    mod = mod_ref[0]
    h = _norm_mod(x, g_ref[...], mod[4:5], mod[3:4]).astype(BF16)
    d_ff = w1_ref.shape[1]
    fc = d_ff // n_chunks
    acc = jnp.zeros(x.shape, F32)
    for j in range(n_chunks):
        a = _dot(h, w1_ref[:, j * fc:(j + 1) * fc])
        b = _dot(h, w3_ref[:, j * fc:(j + 1) * fc])
        acc = acc + _dot((_silu(a) * b).astype(BF16), w2_ref[j * fc:(j + 1) * fc, :])
    out_ref[...] = x + mod[5:6] * acc


def _ffn_dense(x, mod, gain, w1, w3, w2, seq):
    t, d_model = x.shape
    tm = ROW_TILE
    per_b = seq // tm
    row = lambda i: (i, 0)
    return pl.pallas_call(
        functools.partial(_ffn_kernel, n_chunks=2),
        grid=(t // tm,),
        in_specs=[
            pl.BlockSpec((tm, d_model), row),
            pl.BlockSpec((1, ADA_CHUNKS, d_model), lambda i: (i // per_b, 0, 0)),
            _resident((1, d_model)),
            _resident(w1.shape), _resident(w3.shape), _resident(w2.shape),
        ],
        out_specs=pl.BlockSpec((tm, d_model), row),
        out_shape=jax.ShapeDtypeStruct((t, d_model), F32),
        compiler_params=_params(1),
        name="ffn_dense",
    )(x, mod, gain, w1, w3, w2)


def _route_kernel(x_ref, mod_ref, g_ref, wr_ref, h_ref, route_ref):
    mod = mod_ref[0]
    h = _norm_mod(x_ref[...], g_ref[...], mod[4:5], mod[3:4])
    h_ref[...] = h
    logits = jnp.dot(h, wr_ref[...], preferred_element_type=F32, precision=lax.Precision.HIGHEST)
    lane = lax.broadcasted_iota(jnp.int32, logits.shape, 1).astype(F32)
    lg = jnp.where(lane < N_EXPERTS, logits, -jnp.inf)
    m1 = jnp.max(lg, axis=-1, keepdims=True)
    i1 = jnp.min(jnp.where(lg == m1, lane, float(LANES)), axis=-1, keepdims=True)
    lg2 = jnp.where(lane == i1, -jnp.inf, lg)
    m2 = jnp.max(lg2, axis=-1, keepdims=True)
    i2 = jnp.min(jnp.where(lg2 == m2, lane, float(LANES)), axis=-1, keepdims=True)
    e = jnp.exp(m2 - m1)
    w1 = 1.0 / (1.0 + e)
    w2 = e / (1.0 + e)
    route_ref[...] = jnp.where(lane == 0, i1,
                               jnp.where(lane == 1, i2, jnp.where(lane == 2, w1, jnp.where(lane == 3, w2, 0.0))))


def _moe_route(x, mod, gain, w_router, seq):
    t, d_model = x.shape
    tm = ROW_TILE
    per_b = seq // tm
    row = lambda i: (i, 0)
    wr = jnp.zeros((d_model, LANES), F32).at[:, :N_EXPERTS].set(w_router)
    return pl.pallas_call(
        _route_kernel,
        grid=(t // tm,),
        in_specs=[
            pl.BlockSpec((tm, d_model), row),
            pl.BlockSpec((1, ADA_CHUNKS, d_model), lambda i: (i // per_b, 0, 0)),
            _resident((1, d_model)),
            _resident(wr.shape),
        ],
        out_specs=[pl.BlockSpec((tm, d_model), row), pl.BlockSpec((tm, LANES), row)],
        out_shape=[jax.ShapeDtypeStruct((t, d_model), F32), jax.ShapeDtypeStruct((t, LANES), F32)],
        compiler_params=_params(1),
        name="moe_route",
    )(x, mod, gain, wr)


def _row_copy(src_hbm, dst, src_row, dst_row, sem):
    return pltpu.make_async_copy(src_hbm.at[pl.ds(src_row, 1), :], dst.at[pl.ds(dst_row, 1), :], sem)


def _expert_kernel(be_ref, nact_ref, tok_ref, tok_next_ref, h_hbm, w1_ref, w3_ref, w2_ref, y_ref,
                   xbuf, sems, *, n_chunks):
    i = pl.program_id(0)
    nact = nact_ref[0]
    bm = y_ref.shape[0]
    slot = i % 2

    def wait_block(s):
        pltpu.make_async_copy(h_hbm.at[pl.ds(0, bm), :], xbuf.at[s], sems.at[s]).wait()

    @pl.when(i == 0)
    def _():
        def issue(r, carry):
            _row_copy(h_hbm, xbuf.at[0], tok_ref[0, 0, r], r, sems.at[0]).start()
            return carry

        lax.fori_loop(0, bm, issue, 0, unroll=8)

    @pl.when(i >= nact)
    def _():
        y_ref[...] = jnp.zeros_like(y_ref)

    @pl.when(i < nact)
    def _():
        wait_block(slot)
        xb = xbuf[slot].astype(BF16)
        for r in range(bm):
            _row_copy(h_hbm, xbuf.at[1 - slot], tok_next_ref[0, 0, r], r, sems.at[1 - slot]).start()
        d_ff = w1_ref.shape[2]
        fc = d_ff // n_chunks
        acc = jnp.zeros(y_ref.shape, F32)
        for j in range(n_chunks):
            a = _dot(xb, w1_ref[0, :, j * fc:(j + 1) * fc])
            b = _dot(xb, w3_ref[0, :, j * fc:(j + 1) * fc])
            acc = acc + _dot((_silu(a) * b).astype(BF16), w2_ref[0, j * fc:(j + 1) * fc, :])
        y_ref[...] = acc

    @pl.when(i == nact - 1)
    def _():
        wait_block(1 - slot)


def _moe_expert(block_expert, nact, row_tok, h, w1, w3, w2):
    nb, _, bm = row_tok.shape
    d_model = h.shape[1]
    d_ff = w1.shape[2]
    wsel = lambda i, be, na: (be[jnp.minimum(i, na[0] - 1)], 0, 0)
    return pl.pallas_call(
        functools.partial(_expert_kernel, n_chunks=2),
        grid_spec=pltpu.PrefetchScalarGridSpec(
            num_scalar_prefetch=2,
            grid=(nb,),
            in_specs=[
                pl.BlockSpec((1, 1, bm), lambda i, be, na: (i, 0, 0), memory_space=pltpu.SMEM),
                pl.BlockSpec((1, 1, bm), lambda i, be, na: (jnp.minimum(i + 1, nb - 1), 0, 0),
                             memory_space=pltpu.SMEM),
                pl.BlockSpec(memory_space=pl.ANY),
                pl.BlockSpec((1, d_model, d_ff), wsel, pipeline_mode=pl.Buffered(1)),
                pl.BlockSpec((1, d_model, d_ff), wsel, pipeline_mode=pl.Buffered(1)),
                pl.BlockSpec((1, d_ff, d_model), wsel, pipeline_mode=pl.Buffered(1)),
            ],
            out_specs=pl.BlockSpec((bm, d_model), lambda i, be, na: (i, 0)),
            scratch_shapes=[pltpu.VMEM((2, bm, d_model), F32), pltpu.SemaphoreType.DMA((2,))],
        ),
        out_shape=jax.ShapeDtypeStruct((nb * bm, d_model), F32),
        compiler_params=_params(1),
        name="moe_expert",
    )(block_expert, nact, row_tok, row_tok, h, w1, w3, w2)


def _combine_kernel(dest_ref, x_ref, mod_ref, route_ref, yb_hbm, out_ref, buf, sem):
    tm = x_ref.shape[0]

    def issue(r, carry):
        _row_copy(yb_hbm, buf, dest_ref[0, 0, r], r, sem).start()
        return carry

    lax.fori_loop(0, TOP_K * tm, issue, 0)
    pltpu.make_async_copy(yb_hbm.at[pl.ds(0, TOP_K * tm), :], buf, sem).wait()
    route = route_ref[...]
    y = route[:, 2:3] * buf[:tm, :] + route[:, 3:4] * buf[tm:, :]
    out_ref[...] = x_ref[...] + mod_ref[0][5:6] * y


def _moe_combine(dest, x, mod, route, yb, seq):
    t, d_model = x.shape
    tm = COMBINE_TILE
    per_b = seq // tm
    row = lambda i: (i, 0)
    return pl.pallas_call(
        _combine_kernel,
        grid=(t // tm,),
        in_specs=[
            pl.BlockSpec((1, 1, TOP_K * tm), lambda i: (i, 0, 0), memory_space=pltpu.SMEM),
            pl.BlockSpec((tm, d_model), row),
            pl.BlockSpec((1, ADA_CHUNKS, d_model), lambda i: (i // per_b, 0, 0)),
            pl.BlockSpec((tm, LANES), row),
            pl.BlockSpec(memory_space=pl.ANY),
        ],
        out_specs=pl.BlockSpec((tm, d_model), row),
        out_shape=jax.ShapeDtypeStruct((t, d_model), F32),
        scratch_shapes=[pltpu.VMEM((TOP_K * tm, d_model), F32), pltpu.SemaphoreType.DMA(())],
        compiler_params=_params(1),
        name="moe_combine",
    )(dest, x, mod, route, yb)


def _moe(x, mod, gain, w_router, w1, w3, w2, seq):
    t, d_model = x.shape
    bm = MOE_BLOCK
    h, route = _moe_route(x, mod, gain, w_router, seq)
    e_flat = route[:, :TOP_K].astype(jnp.int32).reshape(-1)
    n_assign = e_flat.shape[0]
    onehot = (e_flat[:, None] == jnp.arange(N_EXPERTS, dtype=jnp.int32)[None, :]).astype(jnp.int32)
    csum = jnp.cumsum(onehot, axis=0)
    rank = jnp.take_along_axis(csum, e_flat[:, None], axis=1)[:, 0] - 1
    counts = csum[-1]
    padded = (counts + bm - 1) // bm * bm
    pends = jnp.cumsum(padded)
    dest = (pends - padded)[e_flat] + rank
    nb = n_assign // bm + N_EXPERTS
    row_tok = jnp.zeros((nb * bm,), jnp.int32).at[dest].set(jnp.arange(n_assign, dtype=jnp.int32) // TOP_K)
    block_start = jnp.arange(nb, dtype=jnp.int32) * bm
    block_expert = jnp.minimum(
        jnp.sum((pends[None, :] <= block_start[:, None]).astype(jnp.int32), axis=1), N_EXPERTS - 1)
    nact = (pends[-1:] // bm).astype(jnp.int32)

    yb = _moe_expert(block_expert, nact, row_tok.reshape(nb, 1, bm), h, w1, w3, w2)
    tmc = COMBINE_TILE
    dest_tiles = dest.reshape(t // tmc, tmc, TOP_K).transpose(0, 2, 1).reshape(t // tmc, 1, TOP_K * tmc)
    return _moe_combine(dest_tiles.astype(jnp.int32), x, mod, route, yb, seq)


def _final_kernel(x_ref, g_ref, o_ref):
    o_ref[...] = _rms(x_ref[...], g_ref[...])


def _final_norm(x, gain):
    t, d_model = x.shape
    tm = ROW_TILE
    return pl.pallas_call(
        _final_kernel,
        grid=(t // tm,),
        in_specs=[pl.BlockSpec((tm, d_model), lambda i: (i, 0)), _resident((1, d_model))],
        out_specs=pl.BlockSpec((tm, d_model), lambda i: (i, 0)),
        out_shape=jax.ShapeDtypeStruct((t, d_model), F32),
        compiler_params=_params(1),
        name="final_norm",
    )(x, gain)


def _rope_tables(positions):
    inv_freq = ROPE_THETA ** (-jnp.arange(0, ROPE_DIM, 2, dtype=F32) / ROPE_DIM)
    ang = positions.astype(F32)[..., None] * inv_freq
    cos, sin = jnp.cos(ang), jnp.sin(ang)
    reps = LANES // ROPE_DIM
    cos_t = jnp.tile(jnp.concatenate([cos, cos], axis=-1), (1, 1, reps))
    sin_t = jnp.tile(jnp.concatenate([-sin, sin], axis=-1), (1, 1, reps))
    return cos_t, sin_t


def _split_w_in(w_in):
    d_model = w_in.shape[0]
    sizes = (MLA_Q_RANK, MLA_KV_RANK, MLA_ROPE,
             3 * DIL_GROUP_W, 3 * DIL_GROUP_W, 3 * DIL_GROUP_W,
             RET_QK_W, RET_QK_W, RET_V_W, RET_V_W, d_model, d_model, d_model)
    cols, start = [], 0
    for n in sizes:
        cols.append(w_in[:, start:start + n])
        start += n
    cq, ckv, kr, dq, dk, dv, rq, rk, rv, rg, ga, gb, gc = cols
    w_mla = jnp.concatenate([cq, ckv, kr, kr], axis=1)
    gw = DIL_GROUP_W
    w_dil = jnp.concatenate(
        [m[:, g * gw:(g + 1) * gw] for g in range(len(DIL_PAIRS)) for m in (dq, dk, dv)], axis=1)
    w_ret = jnp.concatenate([rq, rk, rv], axis=1)
    w_gate = jnp.concatenate([rg, ga, gb, gc], axis=1)
    return tuple(m.astype(BF16) for m in (w_mla, w_dil, w_ret, w_gate))


def _split_w_uq(w_uq):
    w = w_uq.reshape(MLA_Q_RANK, MLA_HEADS, MLA_QK)
    nope = w[:, :, :MLA_NOPE].reshape(MLA_Q_RANK, MLA_HEADS * MLA_NOPE)
    rope = w[:, :, MLA_NOPE:].reshape(MLA_Q_RANK, MLA_HEADS * MLA_ROPE)
    return jnp.concatenate([nope, rope], axis=1).astype(BF16)


def kernel(x, c, positions, ada_w, ada_b, norm_mix, norm_ffn, w_in, mla_q_norm, mla_w_uq, mla_kv_norm,
           mla_w_ukv, ret_log_decay, ret_norm, w_br_mla, w_br_dil, w_br_ret, w_out, ffn_w1, ffn_w3, ffn_w2,
           moe_router, moe_w1, moe_w3, moe_w2, final_norm):
    batch, seq, d_model = x.shape
    depth = ada_w.shape[0]
    cos_t, sin_t = _rope_tables(positions)
    mod_all = _ada(c, ada_w, ada_b)
    xt = x.reshape(batch * seq, d_model)
    for layer in range(depth):
        mod = mod_all[layer]
        gmix = norm_mix[layer].reshape(1, d_model)
        w_mla, w_dil, w_ret, w_gate = _split_w_in(w_in[layer])

        q, k, v = _mla_prep(xt, mod, gmix, w_mla, mla_q_norm[layer].reshape(1, -1),
                            mla_kv_norm[layer].reshape(1, -1), _split_w_uq(mla_w_uq[layer]),
                            mla_w_ukv[layer].astype(BF16), cos_t, sin_t, batch, seq)
        y_mla = _mla_attn(q, k, v)

        dil_qkv = _dil_prep(xt, mod, gmix, w_dil, cos_t, sin_t, batch, seq)
        dil_outs = [_dil_attn(*dil_qkv[3 * g:3 * g + 3], window, dil)
                    for g, (window, dil) in enumerate(DIL_PAIRS)]

        rq, rk, rv = _ret_prep(xt, mod, gmix, w_ret, cos_t, sin_t, seq)
        y_ret = _ret_scan(ret_log_decay[layer].astype(F32), rq, rk, rv,
                          ret_norm[layer].reshape(1, -1).astype(F32), batch, seq)

        xt = _merge_out(xt, mod, gmix, w_gate, y_mla, y_ret, dil_outs,
                        w_br_mla[layer].astype(BF16), w_br_dil[layer].astype(BF16),
                        w_br_ret[layer].astype(BF16), w_out[layer].astype(BF16), batch, seq)

        gffn = norm_ffn[layer].reshape(1, d_model)
        i = layer // 2
        if layer % 2 == 0:
            xt = _ffn_dense(xt, mod, gffn, ffn_w1[i].astype(BF16), ffn_w3[i].astype(BF16),
                            ffn_w2[i].astype(BF16), seq)
        else:
            xt = _moe(xt, mod, gffn, moe_router[i], moe_w1[i].astype(BF16), moe_w3[i].astype(BF16),
                      moe_w2[i].astype(BF16), seq)
    return _final_norm(xt, final_norm.reshape(1, d_model)).reshape(batch, seq, d_model)
```

```python
import functools

import jax
import jax.numpy as jnp
from jax import lax
from jax.experimental import pallas as pl
from jax.experimental.pallas import tpu as pltpu

F32 = jnp.float32
BF16 = jnp.bfloat16

EPS = 1e-6
NEG_INF = -1e30
ROPE_THETA = 10000.0
ROPE_DIM = 64
ADA_CHUNKS = 6

MLA_HEADS = 8
MLA_Q_RANK = 768
MLA_KV_RANK = 512
MLA_NOPE = 128
MLA_ROPE = ROPE_DIM
MLA_V = 128
MLA_QK = MLA_NOPE + MLA_ROPE

DIL_PAIRS = ((128, 1), (512, 4), (2048, 16))
DIL_HEADS = 8
DIL_HEAD_DIM = ROPE_DIM
DIL_GROUP_W = DIL_HEADS * DIL_HEAD_DIM
DIL_QBLOCK = 128

RET_HEADS = 8
RET_QK_DIM = ROPE_DIM
RET_V_DIM = 2 * RET_QK_DIM
RET_CHUNK = 128
RET_UNROLL = 4
RET_QK_W = RET_HEADS * RET_QK_DIM
RET_V_W = RET_HEADS * RET_V_DIM

N_EXPERTS = 8
TOP_K = 2
MOE_BLOCK = 256

LANES = 128
V7X_VMEM_LIMIT_BYTES = 56 * 1024 * 1024

ROW_TILE = 512
MERGE_TILE = 256
COMBINE_TILE = 256
MLA_Q_TILE = 512
MLA_HEADS_PER_STEP = 4
MLA_KEY_CHUNK = 1024


def _params(n_grid_dims):
    return pltpu.CompilerParams(
        dimension_semantics=("arbitrary",) * n_grid_dims,
        vmem_limit_bytes=V7X_VMEM_LIMIT_BYTES,
    )


def _resident(shape):
    zeros = (0,) * len(shape)
    return pl.BlockSpec(shape, lambda *_: zeros, pipeline_mode=pl.Buffered(1))


def _layer_resident(stacked, layer):
    idx = (layer,) + (0,) * (stacked.ndim - 1)
    return pl.BlockSpec((1,) + stacked.shape[1:], lambda *_: idx, pipeline_mode=pl.Buffered(1))


def _silu(v):
    return v * jax.nn.sigmoid(v)


def _norm_mod(x, gain, scale, shift):
    ms = jnp.mean(x * x, axis=-1, keepdims=True)
    y = x * lax.rsqrt(ms + EPS) * gain
    return y * (1.0 + scale) + shift


def _rms(v, gain):
    ms = jnp.mean(v * v, axis=-1, keepdims=True)
    return v * lax.rsqrt(ms + EPS) * gain


def _rope(t, cos_t, sin_t):
    lane = lax.broadcasted_iota(jnp.int32, (t.shape[0], LANES), 1)
    first_half = (lane % ROPE_DIM) < (ROPE_DIM // 2)
    out = []
    for j in range(t.shape[1] // LANES):
        c = t[:, j * LANES:(j + 1) * LANES]
        rot = jnp.where(first_half,
                        pltpu.roll(c, LANES - ROPE_DIM // 2, 1),
                        pltpu.roll(c, ROPE_DIM // 2, 1))
        out.append(c * cos_t + rot * sin_t)
    return out[0] if len(out) == 1 else jnp.concatenate(out, axis=1)


def _dot(a, b):
    return jnp.dot(a, b, preferred_element_type=F32)


def _dot_nt(a, b):
    return lax.dot_general(a, b, (((1,), (1,)), ((), ())), preferred_element_type=F32)


def _dot_tn(a, b):
    return lax.dot_general(a, b, (((0,), (0,)), ((), ())), preferred_element_type=F32)


def _ada_kernel(c_ref, w_ref, b_ref, o_ref):
    o_ref[0] = jnp.dot(_silu(c_ref[...]), w_ref[0], preferred_element_type=F32,
                       precision=lax.Precision.HIGHEST) + b_ref[0]


def _ada(c, ada_w, ada_b):
    depth, d_model, n = ada_w.shape
    batch = c.shape[0]
    tn = n // 4
    out = pl.pallas_call(
        _ada_kernel,
        grid=(depth, n // tn),
        in_specs=[
            pl.BlockSpec((batch, d_model), lambda l, j: (0, 0)),
            pl.BlockSpec((1, d_model, tn), lambda l, j: (l, 0, j)),
            pl.BlockSpec((1, 1, tn), lambda l, j: (l, 0, j)),
        ],
        out_specs=pl.BlockSpec((1, batch, tn), lambda l, j: (l, 0, j)),
        out_shape=jax.ShapeDtypeStruct((depth, batch, n), F32),
        compiler_params=_params(2),
        name="ada",
    )(c, ada_w, ada_b.reshape(depth, 1, n))
    return out.reshape(depth, batch, ADA_CHUNKS, d_model)


def _mla_prep_kernel(x_ref, mod_ref, g_ref, wa_ref, qn_ref, kvn_ref, wuq_ref, wukv_ref, cos_ref, sin_ref,
                     q_ref, k_ref, v_ref):
    mod = mod_ref[0]
    h = _norm_mod(x_ref[...], g_ref[...], mod[1:2], mod[0:1]).astype(BF16)
    p = _dot(h, wa_ref[0])
    cq = _rms(p[:, :MLA_Q_RANK], qn_ref[...]).astype(BF16)
    ckv = _rms(p[:, MLA_Q_RANK:MLA_Q_RANK + MLA_KV_RANK], kvn_ref[...]).astype(BF16)
    cos_t = cos_ref[0]
    sin_t = sin_ref[0]
    kr = _rope(p[:, MLA_Q_RANK + MLA_KV_RANK:], cos_t, sin_t)
    q = _dot(cq, wuq_ref[0])
    kv = _dot(ckv, wukv_ref[0])
    nope_w = MLA_HEADS * MLA_NOPE
    qr = _rope(q[:, nope_w:], cos_t, sin_t)
    scale = MLA_QK ** -0.5
    kr_b = kr[:, :MLA_ROPE].astype(BF16)
    for hd in range(MLA_HEADS):
        q_ref[0, hd, :, :MLA_NOPE] = (q[:, hd * MLA_NOPE:(hd + 1) * MLA_NOPE] * scale).astype(BF16)
        q_ref[0, hd, :, MLA_NOPE:] = (qr[:, hd * MLA_ROPE:(hd + 1) * MLA_ROPE] * scale).astype(BF16)
        base = hd * (MLA_NOPE + MLA_V)
        k_ref[0, hd, :, :MLA_NOPE] = kv[:, base:base + MLA_NOPE].astype(BF16)
        k_ref[0, hd, :, MLA_NOPE:] = kr_b
        v_ref[0, hd] = kv[:, base + MLA_NOPE:base + MLA_NOPE + MLA_V].astype(BF16)


def _mla_prep(x, mod, gain, wa, qn, kvn, wuq, wukv, cos_t, sin_t, layer, batch, seq):
    t, d_model = x.shape
    tm = ROW_TILE
    per_b = seq // tm
    row = lambda i: (i, 0)
    hs = lambda i: (i // per_b, 0, i % per_b, 0)
    return pl.pallas_call(
        _mla_prep_kernel,
        grid=(t // tm,),
        in_specs=[
            pl.BlockSpec((tm, d_model), row),
            pl.BlockSpec((1, ADA_CHUNKS, d_model), lambda i: (i // per_b, 0, 0)),
            _resident((1, d_model)),
            _layer_resident(wa, layer),
            _resident((1, MLA_Q_RANK)),
            _resident((1, MLA_KV_RANK)),
            _layer_resident(wuq, layer),
            _layer_resident(wukv, layer),
            pl.BlockSpec((1, tm, LANES), lambda i: (i // per_b, i % per_b, 0)),
            pl.BlockSpec((1, tm, LANES), lambda i: (i // per_b, i % per_b, 0)),
        ],
        out_specs=[
            pl.BlockSpec((1, MLA_HEADS, tm, MLA_QK), hs),
            pl.BlockSpec((1, MLA_HEADS, tm, MLA_QK), hs),
            pl.BlockSpec((1, MLA_HEADS, tm, MLA_V), hs),
        ],
        out_shape=[
            jax.ShapeDtypeStruct((batch, MLA_HEADS, seq, MLA_QK), BF16),
            jax.ShapeDtypeStruct((batch, MLA_HEADS, seq, MLA_QK), BF16),
            jax.ShapeDtypeStruct((batch, MLA_HEADS, seq, MLA_V), BF16),
        ],
        compiler_params=_params(1),
        name="mla_prep",
    )(x, mod, gain, wa, qn, kvn, wuq, wukv, cos_t, sin_t)


def _mla_attn_kernel(q_ref, k_ref, v_ref, o_ref, *, heads_per_step, key_chunk):
    seq = k_ref.shape[2]
    ones = jnp.ones((key_chunk, LANES), BF16)
    for hd in range(heads_per_step):
        q = q_ref[0, hd]
        m = acc = None
        for c in range(seq // key_chunk):
            rows = slice(c * key_chunk, (c + 1) * key_chunk)
            s = _dot_nt(q, k_ref[0, hd, rows, :])
            v_ext = jnp.concatenate([v_ref[0, hd, rows, :], ones], axis=1)
            m_c = jnp.max(s, axis=-1, keepdims=True)
            if c == 0:
                m = m_c
                acc = _dot(jnp.exp(s - m).astype(BF16), v_ext)
            else:
                m_new = jnp.maximum(m, m_c)
                acc = jnp.exp(m - m_new) * acc + _dot(jnp.exp(s - m_new).astype(BF16), v_ext)
                m = m_new
        o_ref[:, hd * MLA_V:(hd + 1) * MLA_V] = (acc[:, :MLA_V] / acc[:, MLA_V:MLA_V + 1]).astype(o_ref.dtype)


def _mla_attn(q, k, v):
    batch, heads, seq, _ = q.shape
    tq = MLA_Q_TILE
    nq = seq // tq
    hps = MLA_HEADS_PER_STEP
    return pl.pallas_call(
        functools.partial(_mla_attn_kernel, heads_per_step=hps, key_chunk=MLA_KEY_CHUNK),
        grid=(batch, heads // hps, nq),
        in_specs=[
            pl.BlockSpec((1, hps, tq, MLA_QK), lambda b, h, i: (b, h, i, 0)),
            pl.BlockSpec((1, hps, seq, MLA_QK), lambda b, h, i: (b, h, 0, 0)),
            pl.BlockSpec((1, hps, seq, MLA_V), lambda b, h, i: (b, h, 0, 0)),
        ],
        out_specs=pl.BlockSpec((tq, hps * MLA_V), lambda b, h, i: (b * nq + i, h)),
        out_shape=jax.ShapeDtypeStruct((batch * seq, heads * MLA_V), BF16),
        compiler_params=_params(3),
        name="mla_attn",
    )(q, k, v)


def _dil_prep_kernel(x_ref, mod_ref, g_ref, w_ref, cos_ref, sin_ref, *refs):
    out_refs = refs[:3 * len(DIL_PAIRS)]
    h_scr = refs[3 * len(DIL_PAIRS)]
    mod = mod_ref[0]
    h32 = _norm_mod(x_ref[...], g_ref[...], mod[1:2], mod[0:1])
    n_lane_blocks = h32.shape[1] // LANES
    for j in range(n_lane_blocks):
        h_scr[j] = h32[:, j * LANES:(j + 1) * LANES]
    tm = h32.shape[0]
    gw = DIL_GROUP_W
    for g, (_, dil) in enumerate(DIL_PAIRS):
        n = tm // dil
        if dil == 1:
            hp, cos_t, sin_t = h32, cos_ref[0], sin_ref[0]
        else:
            hp = jnp.concatenate(
                [jnp.concatenate([h_scr[j, pl.ds(r, n, stride=dil), :] for j in range(n_lane_blocks)], axis=1)
                 for r in range(dil)], axis=0)
            cos_t = jnp.concatenate([cos_ref[0, pl.ds(r, n, stride=dil), :] for r in range(dil)], axis=0)
            sin_t = jnp.concatenate([sin_ref[0, pl.ds(r, n, stride=dil), :] for r in range(dil)], axis=0)
        pg = _dot(hp.astype(BF16), w_ref[0, :, g * 3 * gw:(g + 1) * 3 * gw])
        qg = (_rope(pg[:, :gw], cos_t, sin_t) * (DIL_HEAD_DIM ** -0.5)).astype(BF16)
        kg = _rope(pg[:, gw:2 * gw], cos_t, sin_t).astype(BF16)
        vg = pg[:, 2 * gw:].astype(BF16)
        q_ref, k_ref, v_ref = out_refs[3 * g:3 * g + 3]
        for r in range(dil):
            q_ref[0, r] = qg[r * n:(r + 1) * n]
            k_ref[0, r] = kg[r * n:(r + 1) * n]
            v_ref[0, r] = vg[r * n:(r + 1) * n]


def _dil_prep(x, mod, gain, w, cos_t, sin_t, layer, batch, seq):
    t, d_model = x.shape
    tm = ROW_TILE
    per_b = seq // tm
    out_specs, out_shape = [], []
    for _, dil in DIL_PAIRS:
        for _ in range(3):
            out_specs.append(pl.BlockSpec((1, dil, tm // dil, DIL_GROUP_W),
                                          lambda i: (i // per_b, 0, i % per_b, 0)))
            out_shape.append(jax.ShapeDtypeStruct((batch, dil, seq // dil, DIL_GROUP_W), BF16))
    return pl.pallas_call(
        _dil_prep_kernel,
        grid=(t // tm,),
        in_specs=[
            pl.BlockSpec((tm, d_model), lambda i: (i, 0)),
            pl.BlockSpec((1, ADA_CHUNKS, d_model), lambda i: (i // per_b, 0, 0)),
            _resident((1, d_model)),
            _layer_resident(w, layer),
            pl.BlockSpec((1, tm, LANES), lambda i: (i // per_b, i % per_b, 0)),
            pl.BlockSpec((1, tm, LANES), lambda i: (i // per_b, i % per_b, 0)),
        ],
        out_specs=out_specs,
        out_shape=out_shape,
        scratch_shapes=[pltpu.VMEM((d_model // LANES, tm, LANES), F32)],
        compiler_params=_params(1),
        name="dil_prep",
    )(x, mod, gain, w, cos_t, sin_t)


def _dil_attn_kernel(q_ref, k_ref, v_ref, o_ref, lse_ref, *, n_rows, length, segment, band):
    qb_rows = DIL_QBLOCK
    kw = 2 * qb_rows
    nb = length // qb_rows
    low_k = lax.broadcasted_iota(jnp.int32, (kw, LANES), 1).astype(F32).astype(BF16) < DIL_HEAD_DIM
    low_q = lax.broadcasted_iota(jnp.int32, (qb_rows, LANES), 1) < DIL_HEAD_DIM
    row_i = lax.broadcasted_iota(jnp.int32, (qb_rows, kw), 0)
    col_i = lax.broadcasted_iota(jnp.int32, (qb_rows, kw), 1)
    zeros_k = jnp.zeros((kw, LANES), BF16)
    ones_k = jnp.ones((kw, LANES), BF16)
    denom_cols = jnp.concatenate([jnp.where(low_k, ones_k, zeros_k), jnp.where(low_k, zeros_k, ones_k)], axis=0)

    def tile(idx, carry):
        r = idx // nb
        q0 = pl.multiple_of((idx % nb) * qb_rows, qb_rows)
        ks = pl.multiple_of(jnp.clip(q0 - band, 0, length - kw), band)
        qt = q_ref[0, r, pl.ds(q0, qb_rows), :]
        kt = k_ref[0, r, pl.ds(ks, kw), :]
        vt = v_ref[0, r, pl.ds(ks, kw), :]
        jq = q0 + row_i
        jk = ks + col_i
        valid = jnp.logical_and(jnp.abs(jq - jk) <= band, jq // segment == jk // segment)
        for hp in range(DIL_GROUP_W // LANES):
            cols = slice(hp * LANES, (hp + 1) * LANES)
            qp, kp, vp = qt[:, cols], kt[:, cols], vt[:, cols]
            k2 = jnp.concatenate([jnp.where(low_k, kp, zeros_k), jnp.where(low_k, zeros_k, kp)], axis=0)
            v2 = jnp.concatenate([jnp.where(low_k, vp, zeros_k), jnp.where(low_k, zeros_k, vp)], axis=0)
            s = _dot_nt(qp, k2)
            s0 = jnp.where(valid, s[:, :kw], NEG_INF)
            s1 = jnp.where(valid, s[:, kw:], NEG_INF)
            m0 = jnp.max(s0, axis=-1, keepdims=True)
            m1 = jnp.max(s1, axis=-1, keepdims=True)
            p = jnp.concatenate([jnp.exp(s0 - m0), jnp.exp(s1 - m1)], axis=1).astype(BF16)
            pv = _dot(p, jnp.concatenate([v2, denom_cols], axis=1))
            den = pv[:, LANES:]
            o_ref[0, r, pl.ds(q0, qb_rows), cols] = (pv[:, :LANES] / den).astype(o_ref.dtype)
            lse_ref[0, r, pl.ds(q0, qb_rows), cols] = jnp.where(low_q, m0, m1) + jnp.log(den)
        return carry

    lax.fori_loop(0, n_rows * nb, tile, 0, unroll=4)


def _dil_attn(q, k, v, window, dil):
    batch, _, seg, gw = q.shape
    band = window // (2 * dil)
    assert band * 2 == DIL_QBLOCK and seg % DIL_QBLOCK == 0
    length = max(seg, 2 * DIL_QBLOCK)
    n_rows = dil * seg // length
    view = (batch, n_rows, length, gw)
    spec = pl.BlockSpec((1, n_rows, length, gw), lambda b: (b, 0, 0, 0))
    o, lse = pl.pallas_call(
        functools.partial(_dil_attn_kernel, n_rows=n_rows, length=length, segment=seg, band=band),
        grid=(batch,),
        in_specs=[spec, spec, spec],
        out_specs=[spec, spec],
        out_shape=[jax.ShapeDtypeStruct(view, BF16), jax.ShapeDtypeStruct(view, F32)],
        compiler_params=_params(1),
        name=f"dil_attn_d{dil}",
    )(q.reshape(view), k.reshape(view), v.reshape(view))
    return o.reshape(q.shape), lse.reshape(q.shape)


def _ret_prep_kernel(x_ref, mod_ref, g_ref, w_ref, cos_ref, sin_ref, q_ref, k_ref, v_ref):
    mod = mod_ref[0]
    h = _norm_mod(x_ref[...], g_ref[...], mod[1:2], mod[0:1]).astype(BF16)
    p = _dot(h, w_ref[0])
    cos_t, sin_t = cos_ref[0], sin_ref[0]
    q_ref[...] = _rope(p[:, :RET_QK_W], cos_t, sin_t).astype(BF16)
    k_ref[...] = (_rope(p[:, RET_QK_W:2 * RET_QK_W], cos_t, sin_t) * (RET_QK_DIM ** -0.5)).astype(BF16)
    v_ref[...] = p[:, 2 * RET_QK_W:].astype(BF16)


def _ret_prep(x, mod, gain, w, cos_t, sin_t, layer, seq):
    t, d_model = x.shape
    tm = ROW_TILE
    per_b = seq // tm
    row = lambda i: (i, 0)
    return pl.pallas_call(
        _ret_prep_kernel,
        grid=(t // tm,),
        in_specs=[
            pl.BlockSpec((tm, d_model), row),
            pl.BlockSpec((1, ADA_CHUNKS, d_model), lambda i: (i // per_b, 0, 0)),
            _resident((1, d_model)),
            _layer_resident(w, layer),
            pl.BlockSpec((1, tm, LANES), lambda i: (i // per_b, i % per_b, 0)),
            pl.BlockSpec((1, tm, LANES), lambda i: (i // per_b, i % per_b, 0)),
        ],
        out_specs=[pl.BlockSpec((tm, RET_QK_W), row), pl.BlockSpec((tm, RET_QK_W), row),
                   pl.BlockSpec((tm, RET_V_W), row)],
        out_shape=[jax.ShapeDtypeStruct((t, RET_QK_W), BF16), jax.ShapeDtypeStruct((t, RET_QK_W), BF16),
                   jax.ShapeDtypeStruct((t, RET_V_W), BF16)],
        compiler_params=_params(1),
        name="ret_prep",
    )(x, mod, gain, w, cos_t, sin_t)


def _ret_scan_kernel(ld_ref, q_ref, k_ref, v_ref, gn_ref, o_ref,
                     yf_ref, st_ref, intra_ref, qdec_ref, kdec_ref, cdec_ref):
    c = RET_CHUNK
    seq = q_ref.shape[1]
    nc = seq // c
    n_pairs = RET_HEADS // 2
    pair_v = 2 * RET_V_DIM
    ia = lax.broadcasted_iota(jnp.int32, (c, c), 0)
    ib = lax.broadcasted_iota(jnp.int32, (c, c), 1)
    diff = (ia - ib).astype(F32)
    idx_q = lax.broadcasted_iota(jnp.int32, (c, pair_v), 0).astype(F32)
    lane_q = lax.broadcasted_iota(jnp.int32, (c, pair_v), 1)
    idx_k = lax.broadcasted_iota(jnp.int32, (c, LANES), 0).astype(F32)
    lane_k = lax.broadcasted_iota(jnp.int32, (c, LANES), 1)
    low_k = lane_k < RET_QK_DIM
    low_k16 = lane_k.astype(F32).astype(BF16) < RET_QK_DIM
    low_v16 = lane_q.astype(F32).astype(BF16) < RET_V_DIM
    srow =lax.broadcasted_iota(jnp.int32, (LANES, pair_v), 0)
    scol = lax.broadcasted_iota(jnp.int32, (LANES, pair_v), 1)
    blk0 = jnp.logical_and(srow < RET_QK_DIM, scol < RET_V_DIM)
    blk1 = jnp.logical_and(srow >= RET_QK_DIM, scol >= RET_V_DIM)
    diag = jnp.where(jnp.logical_or(blk0, blk1), 1.0, 0.0).astype(F32)

    for dr in range(2):
        for hd in range(RET_HEADS):
            lg = ld_ref[dr, hd]
            if dr == 0:
                mask = diff >= 0
                dist = jnp.where(mask, diff, 0.0)
            else:
                mask = diff < 0
                dist = jnp.where(mask, -diff, 0.0)
            intra_ref[dr, hd] = jnp.where(mask, jnp.exp(lg * dist), 0.0)
        for hp in range(n_pairs):
            lg0 = ld_ref[dr, 2 * hp]
            lg1 = ld_ref[dr, 2 * hp + 1]
            q_exp = idx_q + 1.0 if dr == 0 else c - idx_q
            k_exp = c - 1.0 - idx_k if dr == 0 else idx_k
            qdec_ref[dr, hp] = jnp.exp(jnp.where(lane_q < RET_V_DIM, lg0, lg1) * q_exp)
            kdec_ref[dr, hp] = jnp.exp(jnp.where(low_k, lg0, lg1) * k_exp)
            cdec_ref[dr, hp] = jnp.where(blk0, jnp.exp(lg0 * c), jnp.where(blk1, jnp.exp(lg1 * c), 0.0))

    def chunk(dr, n):
        r0 = pl.multiple_of(n * c, c)
        qc = q_ref[0, pl.ds(r0, c), :]
        kc = k_ref[0, pl.ds(r0, c), :]
        vc = v_ref[0, pl.ds(r0, c), :]
        ys = []
        for hp in range(n_pairs):
            qp = qc[:, hp * LANES:(hp + 1) * LANES]
            kp = kc[:, hp * LANES:(hp + 1) * LANES]
            vp = vc[:, hp * pair_v:(hp + 1) * pair_v]
            zk = jnp.zeros_like(kp)
            zv = jnp.zeros_like(vp)
            k2 = jnp.concatenate([jnp.where(low_k16, kp, zk), jnp.where(low_k16, zk, kp)], axis=0)
            v2 = jnp.concatenate([jnp.where(low_v16, vp, zv), jnp.where(low_v16, zv, vp)], axis=0)
            s = _dot_nt(qp, k2) * jnp.concatenate([intra_ref[dr, 2 * hp], intra_ref[dr, 2 * hp + 1]], axis=1)
            y_intra = _dot(s.astype(BF16), v2)
            state = st_ref[hp]
            y_inter = _dot(qp, state.astype(BF16)) * qdec_ref[dr, hp]
            ys.append(y_intra + y_inter)
            kd = (kp.astype(F32) * kdec_ref[dr, hp]).astype(BF16)
            st_ref[hp] = cdec_ref[dr, hp] * state + diag * _dot_tn(kd, vp)
        return r0, ys

    st_ref[...] = jnp.zeros_like(st_ref)

    def fwd(n, carry):
        r0, ys = chunk(0, n)
        for hp in range(n_pairs):
            yf_ref[pl.ds(r0, c), hp * pair_v:(hp + 1) * pair_v] = ys[hp]
        return carry

    lax.fori_loop(0, nc, fwd, 0, unroll=RET_UNROLL)
    st_ref[...] = jnp.zeros_like(st_ref)

    def bwd(i, carry):
        r0, ys = chunk(1, nc - 1 - i)
        for hp in range(n_pairs):
            y = ys[hp] + yf_ref[pl.ds(r0, c), hp * pair_v:(hp + 1) * pair_v]
            for half in range(2):
                cols = slice(hp * pair_v + half * RET_V_DIM, hp * pair_v + (half + 1) * RET_V_DIM)
                yh = y[:, half * RET_V_DIM:(half + 1) * RET_V_DIM]
                mu = jnp.mean(yh, axis=-1, keepdims=True)
                dev = yh - mu
                var = jnp.mean(dev * dev, axis=-1, keepdims=True)
                o_ref[0, pl.ds(r0, c), cols] = (dev * lax.rsqrt(var + EPS) * gn_ref[:, cols]).astype(o_ref.dtype)
        return carry

    lax.fori_loop(0, nc, bwd, 0, unroll=RET_UNROLL)


def _ret_scan(log_decay, q, k, v, ret_norm, batch, seq):
    c = RET_CHUNK
    n_pairs = RET_HEADS // 2
    q3 = q.reshape(batch, seq, RET_QK_W)
    k3 = k.reshape(batch, seq, RET_QK_W)
    v3 = v.reshape(batch, seq, RET_V_W)
    out = pl.pallas_call(
        _ret_scan_kernel,
        grid=(batch,),
        in_specs=[
            pl.BlockSpec(memory_space=pltpu.SMEM),
            pl.BlockSpec((1, seq, RET_QK_W), lambda b: (b, 0, 0)),
            pl.BlockSpec((1, seq, RET_QK_W), lambda b: (b, 0, 0)),
            pl.BlockSpec((1, seq, RET_V_W), lambda b: (b, 0, 0)),
            _resident((1, RET_V_W)),
        ],
        out_specs=pl.BlockSpec((1, seq, RET_V_W), lambda b: (b, 0, 0)),
        out_shape=jax.ShapeDtypeStruct((batch, seq, RET_V_W), BF16),
        scratch_shapes=[
            pltpu.VMEM((seq, RET_V_W), F32),
            pltpu.VMEM((n_pairs, LANES, 2 * RET_V_DIM), F32),
            pltpu.VMEM((2, RET_HEADS, c, c), F32),
            pltpu.VMEM((2, n_pairs, c, 2 * RET_V_DIM), F32),
            pltpu.VMEM((2, n_pairs, c, LANES), F32),
            pltpu.VMEM((2, n_pairs, LANES, 2 * RET_V_DIM), F32),
        ],
        compiler_params=_params(1),
        name="ret_scan",
    )(log_decay, q3, k3, v3, ret_norm)
    return out.reshape(batch * seq, RET_V_W)


def _merge_kernel(x_ref, mod_ref, g_ref, wg_ref, ymla_ref, yret_ref,
                  o0_ref, l0_ref, o1_ref, l1_ref, o2_ref, l2_ref,
                  wbm_ref, wbd_ref, wbr_ref, wout_ref, out_ref, nat_o, nat_l):
    x = x_ref[...]
    mod = mod_ref[0]
    h = _norm_mod(x, g_ref[...], mod[1:2], mod[0:1]).astype(BF16)
    gates = _dot(h, wg_ref[0])
    d_model = x.shape[1]
    tm = x.shape[0]

    o_nat = [o0_ref[0, 0].astype(F32)]
    l_nat = [l0_ref[0, 0]]
    for g, (o_ref, l_ref) in enumerate(((o1_ref, l1_ref), (o2_ref, l2_ref))):
        dil = DIL_PAIRS[g + 1][1]
        n = tm // dil
        n_lane_blocks = DIL_GROUP_W // LANES
        for r in range(dil):
            o_r = o_ref[0, r].astype(F32)
            l_r = l_ref[0, r]
            for j in range(n_lane_blocks):
                nat_o[g, j, pl.ds(r, n, stride=dil), :] = o_r[:, j * LANES:(j + 1) * LANES]
                nat_l[g, j, pl.ds(r, n, stride=dil), :] = l_r[:, j * LANES:(j + 1) * LANES]
        o_nat.append(jnp.concatenate([nat_o[g, j] for j in range(n_lane_blocks)], axis=1))
        l_nat.append(jnp.concatenate([nat_l[g, j] for j in range(n_lane_blocks)], axis=1))
    m = jnp.maximum(jnp.maximum(l_nat[0], l_nat[1]), l_nat[2])
    ws = [jnp.exp(l - m) for l in l_nat]
    y_dil = (ws[0] * o_nat[0] + ws[1] * o_nat[1] + ws[2] * o_nat[2]) / (ws[0] + ws[1] + ws[2])

    y_ret = (_silu(gates[:, :RET_V_W]) * yret_ref[...].astype(F32)).astype(BF16)
    ga = gates[:, RET_V_W:RET_V_W + d_model]
    gb = gates[:, RET_V_W + d_model:RET_V_W + 2 * d_model]
    gc = gates[:, RET_V_W + 2 * d_model:]
    merged = (jax.nn.sigmoid(ga) * _dot(ymla_ref[...], wbm_ref[0])
              + jax.nn.sigmoid(gb) * _dot(y_dil.astype(BF16), wbd_ref[0])
              + jax.nn.sigmoid(gc) * _dot(y_ret, wbr_ref[0]))
    out_ref[...] = x + mod[2:3] * _dot(merged.astype(BF16), wout_ref[0])


def _merge_out(x, mod, gain, wg, y_mla, y_ret, dil_outs, wbm, wbd, wbr, wout, layer, batch, seq):
    t, d_model = x.shape
    tm = MERGE_TILE
    per_b = seq // tm
    row = lambda i: (i, 0)
    dil_specs, dil_args = [], []
    for (_, dil), (o, lse) in zip(DIL_PAIRS, dil_outs):
        spec = pl.BlockSpec((1, dil, tm // dil, DIL_GROUP_W), lambda i: (i // per_b, 0, i % per_b, 0))
        dil_specs += [spec, spec]
        dil_args += [o, lse]
    return pl.pallas_call(
        _merge_kernel,
        grid=(t // tm,),
        in_specs=[
            pl.BlockSpec((tm, d_model), row),
            pl.BlockSpec((1, ADA_CHUNKS, d_model), lambda i: (i // per_b, 0, 0)),
            _resident((1, d_model)),
            _layer_resident(wg, layer),
            pl.BlockSpec((tm, y_mla.shape[1]), row),
            pl.BlockSpec((tm, RET_V_W), row),
            *dil_specs,
            _layer_resident(wbm, layer), _layer_resident(wbd, layer), _layer_resident(wbr, layer),
            _layer_resident(wout, layer),
        ],
        out_specs=pl.BlockSpec((tm, d_model), row),
        out_shape=jax.ShapeDtypeStruct((t, d_model), F32),
        scratch_shapes=[pltpu.VMEM((2, DIL_GROUP_W // LANES, tm, LANES), F32),
                        pltpu.VMEM((2, DIL_GROUP_W // LANES, tm, LANES), F32)],
        compiler_params=_params(1),
        name="merge_out",
    )(x, mod, gain, wg, y_mla, y_ret, *dil_args, wbm, wbd, wbr, wout)


def _ffn_kernel(x_ref, mod_ref, g_ref, w1_ref, w3_ref, w2_ref, out_ref, *, n_chunks):
    x = x_ref[...]
    mod = mod_ref[0]
    h = _norm_mod(x, g_ref[...], mod[4:5], mod[3:4]).astype(BF16)
    d_ff = w1_ref.shape[2]
    fc = d_ff // n_chunks
    acc = jnp.zeros(x.shape, F32)
    for j in range(n_chunks):
        a = _dot(h, w1_ref[0, :, j * fc:(j + 1) * fc])
        b = _dot(h, w3_ref[0, :, j * fc:(j + 1) * fc])
        acc = acc + _dot((_silu(a) * b).astype(BF16), w2_ref[0, j * fc:(j + 1) * fc, :])
    out_ref[...] = x + mod[5:6] * acc


def _ffn_dense(x, mod, gain, w1, w3, w2, layer, seq):
    t, d_model = x.shape
    tm = ROW_TILE
    per_b = seq // tm
    row = lambda i: (i, 0)
    return pl.pallas_call(
        functools.partial(_ffn_kernel, n_chunks=2),
        grid=(t // tm,),
        in_specs=[
            pl.BlockSpec((tm, d_model), row),
            pl.BlockSpec((1, ADA_CHUNKS, d_model), lambda i: (i // per_b, 0, 0)),
            _resident((1, d_model)),
            _layer_resident(w1, layer), _layer_resident(w3, layer), _layer_resident(w2, layer),
        ],
        out_specs=pl.BlockSpec((tm, d_model), row),
        out_shape=jax.ShapeDtypeStruct((t, d_model), F32),
        compiler_params=_params(1),
        name="ffn_dense",
    )(x, mod, gain, w1, w3, w2)


def _route_kernel(x_ref, mod_ref, g_ref, wr_ref, h_ref, route_ref):
    mod = mod_ref[0]
    h = _norm_mod(x_ref[...], g_ref[...], mod[4:5], mod[3:4])
    h_ref[...] = h
    logits = jnp.dot(h, wr_ref[...], preferred_element_type=F32, precision=lax.Precision.HIGHEST)
    lane = lax.broadcasted_iota(jnp.int32, logits.shape, 1).astype(F32)
    lg = jnp.where(lane < N_EXPERTS, logits, -jnp.inf)
    m1 = jnp.max(lg, axis=-1, keepdims=True)
    i1 = jnp.min(jnp.where(lg == m1, lane, float(LANES)), axis=-1, keepdims=True)
    lg2 = jnp.where(lane == i1, -jnp.inf, lg)
    m2 = jnp.max(lg2, axis=-1, keepdims=True)
    i2 = jnp.min(jnp.where(lg2 == m2, lane, float(LANES)), axis=-1, keepdims=True)
    e = jnp.exp(m2 - m1)
    w1 = 1.0 / (1.0 + e)
    w2 = e / (1.0 + e)
    route_ref[...] = jnp.where(lane == 0, i1,
                               jnp.where(lane == 1, i2, jnp.where(lane == 2, w1, jnp.where(lane == 3, w2, 0.0))))


def _moe_route(x, mod, gain, w_router, seq):
    t, d_model = x.shape
    tm = ROW_TILE
    per_b = seq // tm
    row = lambda i: (i, 0)
    wr = jnp.zeros((d_model, LANES), F32).at[:, :N_EXPERTS].set(w_router)
    return pl.pallas_call(
        _route_kernel,
        grid=(t // tm,),
        in_specs=[
            pl.BlockSpec((tm, d_model), row),
            pl.BlockSpec((1, ADA_CHUNKS, d_model), lambda i: (i // per_b, 0, 0)),
            _resident((1, d_model)),
            _resident(wr.shape),
        ],
        out_specs=[pl.BlockSpec((tm, d_model), row), pl.BlockSpec((tm, LANES), row)],
        out_shape=[jax.ShapeDtypeStruct((t, d_model), F32), jax.ShapeDtypeStruct((t, LANES), F32)],
        compiler_params=_params(1),
        name="moe_route",
    )(x, mod, gain, wr)


def _row_copy(src_hbm, dst, src_row, dst_row, sem):
    return pltpu.make_async_copy(src_hbm.at[pl.ds(src_row, 1), :], dst.at[pl.ds(dst_row, 1), :], sem)


def _expert_kernel(be_ref, nact_ref, tok_ref, tok_next_ref, h_hbm, w1_ref, w3_ref, w2_ref, y_ref,
                   xbuf, sems, *, n_chunks):
    i = pl.program_id(0)
    nact = nact_ref[0]
    bm = y_ref.shape[0]
    slot = i % 2

    def wait_block(s):
        pltpu.make_async_copy(h_hbm.at[pl.ds(0, bm), :], xbuf.at[s], sems.at[s]).wait()

    @pl.when(i == 0)
    def _():
        def issue(r, carry):
            _row_copy(h_hbm, xbuf.at[0], tok_ref[0, 0, r], r, sems.at[0]).start()
            return carry

        lax.fori_loop(0, bm, issue, 0, unroll=8)

    @pl.when(i >= nact)
    def _():
        y_ref[...] = jnp.zeros_like(y_ref)

    @pl.when(i < nact)
    def _():
        wait_block(slot)
        xb = xbuf[slot].astype(BF16)
        for r in range(bm):
            _row_copy(h_hbm, xbuf.at[1 - slot], tok_next_ref[0, 0, r], r, sems.at[1 - slot]).start()
        d_ff = w1_ref.shape[2]
        fc = d_ff // n_chunks
        acc = jnp.zeros(y_ref.shape, F32)
        for j in range(n_chunks):
            a = _dot(xb, w1_ref[0, :, j * fc:(j + 1) * fc])
            b = _dot(xb, w3_ref[0, :, j * fc:(j + 1) * fc])
            acc = acc + _dot((_silu(a) * b).astype(BF16), w2_ref[0, j * fc:(j + 1) * fc, :])
        y_ref[...] = acc

    @pl.when(i == nact - 1)
    def _():
        wait_block(1 - slot)


def _moe_expert(block_expert, nact, row_tok, h, w1, w3, w2, expert_base):
    nb, _, bm = row_tok.shape
    d_model = h.shape[1]
    d_ff = w1.shape[2]
    wsel = lambda i, be, na: (expert_base + be[jnp.minimum(i, na[0] - 1)], 0, 0)
    return pl.pallas_call(
        functools.partial(_expert_kernel, n_chunks=2),
        grid_spec=pltpu.PrefetchScalarGridSpec(
            num_scalar_prefetch=2,
            grid=(nb,),
            in_specs=[
                pl.BlockSpec((1, 1, bm), lambda i, be, na: (i, 0, 0), memory_space=pltpu.SMEM),
                pl.BlockSpec((1, 1, bm), lambda i, be, na: (jnp.minimum(i + 1, nb - 1), 0, 0),
                             memory_space=pltpu.SMEM),
                pl.BlockSpec(memory_space=pl.ANY),
                pl.BlockSpec((1, d_model, d_ff), wsel, pipeline_mode=pl.Buffered(1)),
                pl.BlockSpec((1, d_model, d_ff), wsel, pipeline_mode=pl.Buffered(1)),
                pl.BlockSpec((1, d_ff, d_model), wsel, pipeline_mode=pl.Buffered(1)),
            ],
            out_specs=pl.BlockSpec((bm, d_model), lambda i, be, na: (i, 0)),
            scratch_shapes=[pltpu.VMEM((2, bm, d_model), F32), pltpu.SemaphoreType.DMA((2,))],
        ),
        out_shape=jax.ShapeDtypeStruct((nb * bm, d_model), F32),
        compiler_params=_params(1),
        name="moe_expert",
    )(block_expert, nact, row_tok, row_tok, h, w1, w3, w2)


def _combine_kernel(dest_ref, dest_next_ref, x_ref, mod_ref, route_ref, yb_hbm, out_ref, buf, sems):
    tm = x_ref.shape[0]
    n_rows = TOP_K * tm
    i = pl.program_id(0)
    slot = i % 2

    def issue(rows_ref, s):
        def body(r, carry):
            _row_copy(yb_hbm, buf.at[s], rows_ref[0, 0, r], r, sems.at[s]).start()
            return carry

        lax.fori_loop(0, n_rows, body, 0, unroll=8)

    @pl.when(i == 0)
    def _():
        issue(dest_ref, 0)

    @pl.when(i + 1 < pl.num_programs(0))
    def _():
        issue(dest_next_ref, 1 - slot)

    pltpu.make_async_copy(yb_hbm.at[pl.ds(0, n_rows), :], buf.at[slot], sems.at[slot]).wait()
    route = route_ref[...]
    y = route[:, 2:3] * buf[slot, :tm, :] + route[:, 3:4] * buf[slot, tm:, :]
    out_ref[...] = x_ref[...] + mod_ref[0][5:6] * y


def _moe_combine(dest, x, mod, route, yb, seq):
    t, d_model = x.shape
    tm = COMBINE_TILE
    per_b = seq // tm
    row = lambda i: (i, 0)
    return pl.pallas_call(
        _combine_kernel,
        grid=(t // tm,),
        in_specs=[
            pl.BlockSpec((1, 1, TOP_K * tm), lambda i: (i, 0, 0), memory_space=pltpu.SMEM),
            pl.BlockSpec((1, 1, TOP_K * tm), lambda i: (jnp.minimum(i + 1, t // tm - 1), 0, 0),
                         memory_space=pltpu.SMEM),
            pl.BlockSpec((tm, d_model), row),
            pl.BlockSpec((1, ADA_CHUNKS, d_model), lambda i: (i // per_b, 0, 0)),
            pl.BlockSpec((tm, LANES), row),
            pl.BlockSpec(memory_space=pl.ANY),
        ],
        out_specs=pl.BlockSpec((tm, d_model), row),
        out_shape=jax.ShapeDtypeStruct((t, d_model), F32),
        scratch_shapes=[pltpu.VMEM((2, TOP_K * tm, d_model), F32), pltpu.SemaphoreType.DMA((2,))],
        compiler_params=_params(1),
        name="moe_combine",
    )(dest, dest, x, mod, route, yb)


def _moe(x, mod, gain, w_router, w1, w3, w2, expert_base, seq):
    t, d_model = x.shape
    bm = MOE_BLOCK
    h, route = _moe_route(x, mod, gain, w_router, seq)
    e_flat = route[:, :TOP_K].astype(jnp.int32).reshape(-1)
    n_assign = e_flat.shape[0]
    onehot = (e_flat[:, None] == jnp.arange(N_EXPERTS, dtype=jnp.int32)[None, :]).astype(jnp.int32)
    csum = jnp.cumsum(onehot, axis=0)
    rank = jnp.take_along_axis(csum, e_flat[:, None], axis=1)[:, 0] - 1
    counts = csum[-1]
    padded = (counts + bm - 1) // bm * bm
    pends = jnp.cumsum(padded)
    dest = (pends - padded)[e_flat] + rank
    nb = n_assign // bm + N_EXPERTS
    row_tok = jnp.zeros((nb * bm,), jnp.int32).at[dest].set(jnp.arange(n_assign, dtype=jnp.int32) // TOP_K)
    block_start = jnp.arange(nb, dtype=jnp.int32) * bm
    block_expert = jnp.minimum(
        jnp.sum((pends[None, :] <= block_start[:, None]).astype(jnp.int32), axis=1), N_EXPERTS - 1)
    nact = (pends[-1:] // bm).astype(jnp.int32)

    yb = _moe_expert(block_expert, nact, row_tok.reshape(nb, 1, bm), h, w1, w3, w2, expert_base)
    tmc = COMBINE_TILE
    dest_tiles = dest.reshape(t // tmc, tmc, TOP_K).transpose(0, 2, 1).reshape(t // tmc, 1, TOP_K * tmc)
    return _moe_combine(dest_tiles.astype(jnp.int32), x, mod, route, yb, seq)


def _final_kernel(x_ref, g_ref, o_ref):
    o_ref[...] = _rms(x_ref[...], g_ref[...])


def _final_norm(x, gain):
    t, d_model = x.shape
    tm = ROW_TILE
    return pl.pallas_call(
        _final_kernel,
        grid=(t // tm,),
        in_specs=[pl.BlockSpec((tm, d_model), lambda i: (i, 0)), _resident((1, d_model))],
        out_specs=pl.BlockSpec((tm, d_model), lambda i: (i, 0)),
        out_shape=jax.ShapeDtypeStruct((t, d_model), F32),
        compiler_params=_params(1),
        name="final_norm",
    )(x, gain)


def _rope_tables(positions):
    inv_freq = ROPE_THETA ** (-jnp.arange(0, ROPE_DIM, 2, dtype=F32) / ROPE_DIM)
    ang = positions.astype(F32)[..., None] * inv_freq
    cos, sin = jnp.cos(ang), jnp.sin(ang)
    reps = LANES // ROPE_DIM
    cos_t = jnp.tile(jnp.concatenate([cos, cos], axis=-1), (1, 1, reps))
    sin_t = jnp.tile(jnp.concatenate([-sin, sin], axis=-1), (1, 1, reps))
    return cos_t, sin_t


def _split_w_in(w_in):
    d_model = w_in.shape[1]
    sizes = (MLA_Q_RANK, MLA_KV_RANK, MLA_ROPE,
             3 * DIL_GROUP_W, 3 * DIL_GROUP_W, 3 * DIL_GROUP_W,
             RET_QK_W, RET_QK_W, RET_V_W, RET_V_W, d_model, d_model, d_model)
    w_in = w_in.astype(BF16)
    cols, start = [], 0
    for n in sizes:
        cols.append(w_in[..., start:start + n])
        start += n
    cq, ckv, kr, dq, dk, dv, rq, rk, rv, rg, ga, gb, gc = cols
    w_mla = jnp.concatenate([cq, ckv, kr, kr], axis=-1)
    gw = DIL_GROUP_W
    w_dil = jnp.concatenate(
        [m[..., g * gw:(g + 1) * gw] for g in range(len(DIL_PAIRS)) for m in (dq, dk, dv)], axis=-1)
    w_ret = jnp.concatenate([rq, rk, rv], axis=-1)
    w_gate = jnp.concatenate([rg, ga, gb, gc], axis=-1)
    return w_mla, w_dil, w_ret, w_gate


def _split_w_uq(w_uq):
    depth = w_uq.shape[0]
    w = w_uq.astype(BF16).reshape(depth, MLA_Q_RANK, MLA_HEADS, MLA_QK)
    nope = w[..., :MLA_NOPE].reshape(depth, MLA_Q_RANK, MLA_HEADS * MLA_NOPE)
    rope = w[..., MLA_NOPE:].reshape(depth, MLA_Q_RANK, MLA_HEADS * MLA_ROPE)
    return jnp.concatenate([nope, rope], axis=-1)


def kernel(x, c, positions, ada_w, ada_b, norm_mix, norm_ffn, w_in, mla_q_norm, mla_w_uq, mla_kv_norm,
           mla_w_ukv, ret_log_decay, ret_norm, w_br_mla, w_br_dil, w_br_ret, w_out, ffn_w1, ffn_w3, ffn_w2,
           moe_router, moe_w1, moe_w3, moe_w2, final_norm):
    batch, seq, d_model = x.shape
    depth = ada_w.shape[0]
    cos_t, sin_t = _rope_tables(positions)
    mod_all = _ada(c, ada_w, ada_b)
    xt = x.reshape(batch * seq, d_model)
    moe_w1_b = moe_w1.astype(BF16).reshape((-1,) + moe_w1.shape[2:])
    moe_w3_b = moe_w3.astype(BF16).reshape((-1,) + moe_w3.shape[2:])
    moe_w2_b = moe_w2.astype(BF16).reshape((-1,) + moe_w2.shape[2:])
    w_mla, w_dil, w_ret, w_gate = _split_w_in(w_in)
    w_uq = _split_w_uq(mla_w_uq)
    w_ukv = mla_w_ukv.astype(BF16)
    wbm, wbd, wbr, wout = (w.astype(BF16) for w in (w_br_mla, w_br_dil, w_br_ret, w_out))
    ffn_w1_b, ffn_w3_b, ffn_w2_b = (w.astype(BF16) for w in (ffn_w1, ffn_w3, ffn_w2))
    for layer in range(depth):
        mod = mod_all[layer]
        gmix = norm_mix[layer].reshape(1, d_model)

        q, k, v = _mla_prep(xt, mod, gmix, w_mla, mla_q_norm[layer].reshape(1, -1),
                            mla_kv_norm[layer].reshape(1, -1), w_uq, w_ukv, cos_t, sin_t, layer, batch, seq)
        y_mla = _mla_attn(q, k, v)

        dil_qkv = _dil_prep(xt, mod, gmix, w_dil, cos_t, sin_t, layer, batch, seq)
        dil_outs = [_dil_attn(*dil_qkv[3 * g:3 * g + 3], window, dil)
                    for g, (window, dil) in enumerate(DIL_PAIRS)]

        rq, rk, rv = _ret_prep(xt, mod, gmix, w_ret, cos_t, sin_t, layer, seq)
        y_ret = _ret_scan(ret_log_decay[layer].astype(F32), rq, rk, rv,
                          ret_norm[layer].reshape(1, -1).astype(F32), batch, seq)

        xt = _merge_out(xt, mod, gmix, w_gate, y_mla, y_ret, dil_outs, wbm, wbd, wbr, wout, layer, batch, seq)

        gffn = norm_ffn[layer].reshape(1, d_model)
        i = layer // 2
        if layer % 2 == 0:
            xt = _ffn_dense(xt, mod, gffn, ffn_w1_b, ffn_w3_b, ffn_w2_b, i, seq)
        else:
            xt = _moe(xt, mod, gffn, moe_router[i], moe_w1_b, moe_w3_b, moe_w2_b, i * N_EXPERTS, seq)
    return _final_norm(xt, final_norm.reshape(1, d_model)).reshape(batch, seq, d_model)
```

```python
import functools

import jax
import jax.numpy as jnp
from jax import lax
from jax.experimental import pallas as pl
from jax.experimental.pallas import tpu as pltpu

F32 = jnp.float32
BF16 = jnp.bfloat16

EPS = 1e-6
NEG_INF = -1e30
ROPE_THETA = 10000.0
ROPE_DIM = 64
ADA_CHUNKS = 6

MLA_HEADS = 8
MLA_Q_RANK = 768
MLA_KV_RANK = 512
MLA_NOPE = 128
MLA_ROPE = ROPE_DIM
MLA_V = 128
MLA_QK = MLA_NOPE + MLA_ROPE

DIL_PAIRS = ((128, 1), (512, 4), (2048, 16))
DIL_HEADS = 8
DIL_HEAD_DIM = ROPE_DIM
DIL_GROUP_W = DIL_HEADS * DIL_HEAD_DIM
DIL_QBLOCK = 128

RET_HEADS = 8
RET_QK_DIM = ROPE_DIM
RET_V_DIM = 2 * RET_QK_DIM
RET_CHUNK = 256
RET_UNROLL = 4
RET_QK_W = RET_HEADS * RET_QK_DIM
RET_V_W = RET_HEADS * RET_V_DIM

N_EXPERTS = 8
TOP_K = 2
MOE_BLOCK = 256

LANES = 128
V7X_VMEM_LIMIT_BYTES = 56 * 1024 * 1024
V7X_EXPERT_VMEM_LIMIT_BYTES = 60 * 1024 * 1024

ROW_TILE = 512
MERGE_TILE = 256
COMBINE_TILE = 256
MLA_Q_TILE = 512
MLA_HEADS_PER_STEP = 4
MLA_KEY_CHUNK = 1024


def _params(n_grid_dims, vmem_limit_bytes=V7X_VMEM_LIMIT_BYTES):
    return pltpu.CompilerParams(
        dimension_semantics=("arbitrary",) * n_grid_dims,
        vmem_limit_bytes=vmem_limit_bytes,
    )


def _resident(shape):
    zeros = (0,) * len(shape)
    return pl.BlockSpec(shape, lambda *_: zeros, pipeline_mode=pl.Buffered(1))


def _layer_resident(stacked, layer):
    idx = (layer,) + (0,) * (stacked.ndim - 1)
    return pl.BlockSpec((1,) + stacked.shape[1:], lambda *_: idx, pipeline_mode=pl.Buffered(1))


def _silu(v):
    return v * jax.nn.sigmoid(v)


def _norm_mod(x, gain, scale, shift):
    ms = jnp.mean(x * x, axis=-1, keepdims=True)
    y = x * lax.rsqrt(ms + EPS) * gain
    return y * (1.0 + scale) + shift


def _rms(v, gain):
    ms = jnp.mean(v * v, axis=-1, keepdims=True)
    return v * lax.rsqrt(ms + EPS) * gain


def _rope(t, cos_t, sin_t):
    lane = lax.broadcasted_iota(jnp.int32, (t.shape[0], LANES), 1)
    first_half = (lane % ROPE_DIM) < (ROPE_DIM // 2)
    out = []
    for j in range(t.shape[1] // LANES):
        c = t[:, j * LANES:(j + 1) * LANES]
        rot = jnp.where(first_half,
                        pltpu.roll(c, LANES - ROPE_DIM // 2, 1),
                        pltpu.roll(c, ROPE_DIM // 2, 1))
        out.append(c * cos_t + rot * sin_t)
    return out[0] if len(out) == 1 else jnp.concatenate(out, axis=1)


def _dot(a, b):
    return jnp.dot(a, b, preferred_element_type=F32)


def _dot_nt(a, b):
    return lax.dot_general(a, b, (((1,), (1,)), ((), ())), preferred_element_type=F32)


def _dot_tn(a, b):
    return lax.dot_general(a, b, (((0,), (0,)), ((), ())), preferred_element_type=F32)


def _ada_kernel(c_ref, w_ref, b_ref, o_ref):
    o_ref[0] = jnp.dot(_silu(c_ref[...]), w_ref[0], preferred_element_type=F32,
                       precision=lax.Precision.HIGHEST) + b_ref[0]


def _ada(c, ada_w, ada_b):
    depth, d_model, n = ada_w.shape
    batch = c.shape[0]
    tn = n // 4
    out = pl.pallas_call(
        _ada_kernel,
        grid=(depth, n // tn),
        in_specs=[
            pl.BlockSpec((batch, d_model), lambda l, j: (0, 0)),
            pl.BlockSpec((1, d_model, tn), lambda l, j: (l, 0, j)),
            pl.BlockSpec((1, 1, tn), lambda l, j: (l, 0, j)),
        ],
        out_specs=pl.BlockSpec((1, batch, tn), lambda l, j: (l, 0, j)),
        out_shape=jax.ShapeDtypeStruct((depth, batch, n), F32),
        compiler_params=_params(2),
        name="ada",
    )(c, ada_w, ada_b.reshape(depth, 1, n))
    return out.reshape(depth, batch, ADA_CHUNKS, d_model)


def _mla_prep_kernel(x_ref, mod_ref, g_ref, wa_ref, qn_ref, kvn_ref, wuq_ref, wukv_ref, cos_ref, sin_ref,
                     q_ref, k_ref, v_ref):
    mod = mod_ref[0]
    h = _norm_mod(x_ref[...], g_ref[...], mod[1:2], mod[0:1]).astype(BF16)
    p = _dot(h, wa_ref[0])
    cq = _rms(p[:, :MLA_Q_RANK], qn_ref[...]).astype(BF16)
    ckv = _rms(p[:, MLA_Q_RANK:MLA_Q_RANK + MLA_KV_RANK], kvn_ref[...]).astype(BF16)
    cos_t = cos_ref[0]
    sin_t = sin_ref[0]
    kr = _rope(p[:, MLA_Q_RANK + MLA_KV_RANK:], cos_t, sin_t)
    q = _dot(cq, wuq_ref[0])
    kv = _dot(ckv, wukv_ref[0])
    nope_w = MLA_HEADS * MLA_NOPE
    qr = _rope(q[:, nope_w:], cos_t, sin_t)
    scale = MLA_QK ** -0.5
    kr_b = kr[:, :MLA_ROPE].astype(BF16)
    for hd in range(MLA_HEADS):
        q_ref[0, hd, :, :MLA_NOPE] = (q[:, hd * MLA_NOPE:(hd + 1) * MLA_NOPE] * scale).astype(BF16)
        q_ref[0, hd, :, MLA_NOPE:] = (qr[:, hd * MLA_ROPE:(hd + 1) * MLA_ROPE] * scale).astype(BF16)
        base = hd * (MLA_NOPE + MLA_V)
        k_ref[0, hd, :, :MLA_NOPE] = kv[:, base:base + MLA_NOPE].astype(BF16)
        k_ref[0, hd, :, MLA_NOPE:] = kr_b
        v_ref[0, hd] = kv[:, base + MLA_NOPE:base + MLA_NOPE + MLA_V].astype(BF16)


def _mla_prep(x, mod, gain, wa, qn, kvn, wuq, wukv, cos_t, sin_t, layer, batch, seq):
    t, d_model = x.shape
    tm = ROW_TILE
    per_b = seq // tm
    row = lambda i: (i, 0)
    hs = lambda i: (i // per_b, 0, i % per_b, 0)
    return pl.pallas_call(
        _mla_prep_kernel,
        grid=(t // tm,),
        in_specs=[
            pl.BlockSpec((tm, d_model), row),
            pl.BlockSpec((1, ADA_CHUNKS, d_model), lambda i: (i // per_b, 0, 0)),
            _resident((1, d_model)),
            _layer_resident(wa, layer),
            _resident((1, MLA_Q_RANK)),
            _resident((1, MLA_KV_RANK)),
            _layer_resident(wuq, layer),
            _layer_resident(wukv, layer),
            pl.BlockSpec((1, tm, LANES), lambda i: (i // per_b, i % per_b, 0)),
            pl.BlockSpec((1, tm, LANES), lambda i: (i // per_b, i % per_b, 0)),
        ],
        out_specs=[
            pl.BlockSpec((1, MLA_HEADS, tm, MLA_QK), hs),
            pl.BlockSpec((1, MLA_HEADS, tm, MLA_QK), hs),
            pl.BlockSpec((1, MLA_HEADS, tm, MLA_V), hs),
        ],
        out_shape=[
            jax.ShapeDtypeStruct((batch, MLA_HEADS, seq, MLA_QK), BF16),
            jax.ShapeDtypeStruct((batch, MLA_HEADS, seq, MLA_QK), BF16),
            jax.ShapeDtypeStruct((batch, MLA_HEADS, seq, MLA_V), BF16),
        ],
        compiler_params=_params(1),
        name="mla_prep",
    )(x, mod, gain, wa, qn, kvn, wuq, wukv, cos_t, sin_t)


def _mla_attn_kernel(q_ref, k_ref, v_ref, o_ref, *, heads_per_step, key_chunk):
    seq = k_ref.shape[2]
    ones = jnp.ones((key_chunk, LANES), BF16)
    for hd in range(heads_per_step):
        q = q_ref[0, hd]
        m = acc = None
        for c in range(seq // key_chunk):
            rows = slice(c * key_chunk, (c + 1) * key_chunk)
            s = _dot_nt(q, k_ref[0, hd, rows, :])
            v_ext = jnp.concatenate([v_ref[0, hd, rows, :], ones], axis=1)
            m_c = jnp.max(s, axis=-1, keepdims=True)
            if c == 0:
                m = m_c
                acc = _dot(jnp.exp(s - m).astype(BF16), v_ext)
            else:
                m_new = jnp.maximum(m, m_c)
                acc = jnp.exp(m - m_new) * acc + _dot(jnp.exp(s - m_new).astype(BF16), v_ext)
                m = m_new
        o_ref[:, hd * MLA_V:(hd + 1) * MLA_V] = (acc[:, :MLA_V] / acc[:, MLA_V:MLA_V + 1]).astype(o_ref.dtype)


def _mla_attn(q, k, v):
    batch, heads, seq, _ = q.shape
    tq = MLA_Q_TILE
    nq = seq // tq
    hps = MLA_HEADS_PER_STEP
    return pl.pallas_call(
        functools.partial(_mla_attn_kernel, heads_per_step=hps, key_chunk=MLA_KEY_CHUNK),
        grid=(batch, heads // hps, nq),
        in_specs=[
            pl.BlockSpec((1, hps, tq, MLA_QK), lambda b, h, i: (b, h, i, 0)),
            pl.BlockSpec((1, hps, seq, MLA_QK), lambda b, h, i: (b, h, 0, 0)),
            pl.BlockSpec((1, hps, seq, MLA_V), lambda b, h, i: (b, h, 0, 0)),
        ],
        out_specs=pl.BlockSpec((tq, hps * MLA_V), lambda b, h, i: (b * nq + i, h)),
        out_shape=jax.ShapeDtypeStruct((batch * seq, heads * MLA_V), BF16),
        compiler_params=_params(3),
        name="mla_attn",
    )(q, k, v)


def _dil_prep_kernel(x_ref, mod_ref, g_ref, w_ref, cos_ref, sin_ref, *refs):
    out_refs = refs[:3 * len(DIL_PAIRS)]
    h_scr = refs[3 * len(DIL_PAIRS)]
    mod = mod_ref[0]
    h32 = _norm_mod(x_ref[...], g_ref[...], mod[1:2], mod[0:1])
    n_lane_blocks = h32.shape[1] // LANES
    for j in range(n_lane_blocks):
        h_scr[j] = h32[:, j * LANES:(j + 1) * LANES]
    tm = h32.shape[0]
    gw = DIL_GROUP_W
    for g, (_, dil) in enumerate(DIL_PAIRS):
        n = tm // dil
        if dil == 1:
            hp, cos_t, sin_t = h32, cos_ref[0], sin_ref[0]
        else:
            hp = jnp.concatenate(
                [jnp.concatenate([h_scr[j, pl.ds(r, n, stride=dil), :] for j in range(n_lane_blocks)], axis=1)
                 for r in range(dil)], axis=0)
            cos_t = jnp.concatenate([cos_ref[0, pl.ds(r, n, stride=dil), :] for r in range(dil)], axis=0)
            sin_t = jnp.concatenate([sin_ref[0, pl.ds(r, n, stride=dil), :] for r in range(dil)], axis=0)
        pg = _dot(hp.astype(BF16), w_ref[0, :, g * 3 * gw:(g + 1) * 3 * gw])
        qg = (_rope(pg[:, :gw], cos_t, sin_t) * (DIL_HEAD_DIM ** -0.5)).astype(BF16)
        kg = _rope(pg[:, gw:2 * gw], cos_t, sin_t).astype(BF16)
        vg = pg[:, 2 * gw:].astype(BF16)
        q_ref, k_ref, v_ref = out_refs[3 * g:3 * g + 3]
        for r in range(dil):
            q_ref[0, r] = qg[r * n:(r + 1) * n]
            k_ref[0, r] = kg[r * n:(r + 1) * n]
            v_ref[0, r] = vg[r * n:(r + 1) * n]


def _dil_prep(x, mod, gain, w, cos_t, sin_t, layer, batch, seq):
    t, d_model = x.shape
    tm = ROW_TILE
    per_b = seq // tm
    out_specs, out_shape = [], []
    for _, dil in DIL_PAIRS:
        for _ in range(3):
            out_specs.append(pl.BlockSpec((1, dil, tm // dil, DIL_GROUP_W),
                                          lambda i: (i // per_b, 0, i % per_b, 0)))
            out_shape.append(jax.ShapeDtypeStruct((batch, dil, seq // dil, DIL_GROUP_W), BF16))
    return pl.pallas_call(
        _dil_prep_kernel,
        grid=(t // tm,),
        in_specs=[
            pl.BlockSpec((tm, d_model), lambda i: (i, 0)),
            pl.BlockSpec((1, ADA_CHUNKS, d_model), lambda i: (i // per_b, 0, 0)),
            _resident((1, d_model)),
            _layer_resident(w, layer),
            pl.BlockSpec((1, tm, LANES), lambda i: (i // per_b, i % per_b, 0)),
            pl.BlockSpec((1, tm, LANES), lambda i: (i // per_b, i % per_b, 0)),
        ],
        out_specs=out_specs,
        out_shape=out_shape,
        scratch_shapes=[pltpu.VMEM((d_model // LANES, tm, LANES), F32)],
        compiler_params=_params(1),
        name="dil_prep",
    )(x, mod, gain, w, cos_t, sin_t)


def _dil_attn_kernel(q_ref, k_ref, v_ref, o_ref, lse_ref, *, n_rows, length, segment, band):
    qb_rows = DIL_QBLOCK
    kw = 2 * qb_rows
    nb = length // qb_rows
    low_k = lax.broadcasted_iota(jnp.int32, (kw, LANES), 1).astype(F32).astype(BF16) < DIL_HEAD_DIM
    low_q = lax.broadcasted_iota(jnp.int32, (qb_rows, LANES), 1) < DIL_HEAD_DIM
    row_i = lax.broadcasted_iota(jnp.int32, (qb_rows, kw), 0)
    col_i = lax.broadcasted_iota(jnp.int32, (qb_rows, kw), 1)
    zeros_k = jnp.zeros((kw, LANES), BF16)
    ones_k = jnp.ones((kw, LANES), BF16)
    denom_cols = jnp.concatenate([jnp.where(low_k, ones_k, zeros_k), jnp.where(low_k, zeros_k, ones_k)], axis=0)

    def tile(idx, carry):
        r = idx // nb
        q0 = pl.multiple_of((idx % nb) * qb_rows, qb_rows)
        ks = pl.multiple_of(jnp.clip(q0 - band, 0, length - kw), band)
        qt = q_ref[0, r, pl.ds(q0, qb_rows), :]
        kt = k_ref[0, r, pl.ds(ks, kw), :]
        vt = v_ref[0, r, pl.ds(ks, kw), :]
        jq = q0 + row_i
        jk = ks + col_i
        valid = jnp.logical_and(jnp.abs(jq - jk) <= band, jq // segment == jk // segment)
        for hp in range(DIL_GROUP_W // LANES):
            cols = slice(hp * LANES, (hp + 1) * LANES)
            qp, kp, vp = qt[:, cols], kt[:, cols], vt[:, cols]
            k2 = jnp.concatenate([jnp.where(low_k, kp, zeros_k), jnp.where(low_k, zeros_k, kp)], axis=0)
            v2 = jnp.concatenate([jnp.where(low_k, vp, zeros_k), jnp.where(low_k, zeros_k, vp)], axis=0)
            s = _dot_nt(qp, k2)
            s0 = jnp.where(valid, s[:, :kw], NEG_INF)
            s1 = jnp.where(valid, s[:, kw:], NEG_INF)
            m0 = jnp.max(s0, axis=-1, keepdims=True)
            m1 = jnp.max(s1, axis=-1, keepdims=True)
            p = jnp.concatenate([jnp.exp(s0 - m0), jnp.exp(s1 - m1)], axis=1).astype(BF16)
            pv = _dot(p, jnp.concatenate([v2, denom_cols], axis=1))
            den = pv[:, LANES:]
            o_ref[0, r, pl.ds(q0, qb_rows), cols] = (pv[:, :LANES] / den).astype(o_ref.dtype)
            lse_ref[0, r, pl.ds(q0, qb_rows), cols] = jnp.where(low_q, m0, m1) + jnp.log(den)
        return carry

    lax.fori_loop(0, n_rows * nb, tile, 0, unroll=4)


def _dil_attn(q, k, v, window, dil):
    batch, _, seg, gw = q.shape
    band = window // (2 * dil)
    assert band * 2 == DIL_QBLOCK and seg % DIL_QBLOCK == 0
    length = max(seg, 2 * DIL_QBLOCK)
    n_rows = dil * seg // length
    view = (batch, n_rows, length, gw)
    spec = pl.BlockSpec((1, n_rows, length, gw), lambda b: (b, 0, 0, 0))
    o, lse = pl.pallas_call(
        functools.partial(_dil_attn_kernel, n_rows=n_rows, length=length, segment=seg, band=band),
        grid=(batch,),
        in_specs=[spec, spec, spec],
        out_specs=[spec, spec],
        out_shape=[jax.ShapeDtypeStruct(view, BF16), jax.ShapeDtypeStruct(view, F32)],
        compiler_params=_params(1),
        name=f"dil_attn_d{dil}",
    )(q.reshape(view), k.reshape(view), v.reshape(view))
    return o.reshape(q.shape), lse.reshape(q.shape)


def _ret_prep_kernel(x_ref, mod_ref, g_ref, w_ref, cos_ref, sin_ref, q_ref, k_ref, v_ref):
    mod = mod_ref[0]
    h = _norm_mod(x_ref[...], g_ref[...], mod[1:2], mod[0:1]).astype(BF16)
    p = _dot(h, w_ref[0])
    cos_t, sin_t = cos_ref[0], sin_ref[0]
    q_ref[...] = _rope(p[:, :RET_QK_W], cos_t, sin_t).astype(BF16)
    k_ref[...] = (_rope(p[:, RET_QK_W:2 * RET_QK_W], cos_t, sin_t) * (RET_QK_DIM ** -0.5)).astype(BF16)
    v_ref[...] = p[:, 2 * RET_QK_W:].astype(BF16)


def _ret_prep(x, mod, gain, w, cos_t, sin_t, layer, seq):
    t, d_model = x.shape
    tm = ROW_TILE
    per_b = seq // tm
    row = lambda i: (i, 0)
    return pl.pallas_call(
        _ret_prep_kernel,
        grid=(t // tm,),
        in_specs=[
            pl.BlockSpec((tm, d_model), row),
            pl.BlockSpec((1, ADA_CHUNKS, d_model), lambda i: (i // per_b, 0, 0)),
            _resident((1, d_model)),
            _layer_resident(w, layer),
            pl.BlockSpec((1, tm, LANES), lambda i: (i // per_b, i % per_b, 0)),
            pl.BlockSpec((1, tm, LANES), lambda i: (i // per_b, i % per_b, 0)),
        ],
        out_specs=[pl.BlockSpec((tm, RET_QK_W), row), pl.BlockSpec((tm, RET_QK_W), row),
                   pl.BlockSpec((tm, RET_V_W), row)],
        out_shape=[jax.ShapeDtypeStruct((t, RET_QK_W), BF16), jax.ShapeDtypeStruct((t, RET_QK_W), BF16),
                   jax.ShapeDtypeStruct((t, RET_V_W), BF16)],
        compiler_params=_params(1),
        name="ret_prep",
    )(x, mod, gain, w, cos_t, sin_t)


def _ret_scan_kernel(ld_ref, q_ref, k_ref, v_ref, gn_ref, o_ref,
                     yf_ref, st_ref, intra_ref, qdec_ref, kdec_ref, cdec_ref):
    c = RET_CHUNK
    seq = q_ref.shape[1]
    nc = seq // c
    n_pairs = RET_HEADS // 2
    pair_v = 2 * RET_V_DIM
    ia = lax.broadcasted_iota(jnp.int32, (c, c), 0)
    ib = lax.broadcasted_iota(jnp.int32, (c, c), 1)
    diff = (ia - ib).astype(F32)
    idx_q = lax.broadcasted_iota(jnp.int32, (c, pair_v), 0).astype(F32)
    lane_q = lax.broadcasted_iota(jnp.int32, (c, pair_v), 1)
    idx_k = lax.broadcasted_iota(jnp.int32, (c, LANES), 0).astype(F32)
    lane_k = lax.broadcasted_iota(jnp.int32, (c, LANES), 1)
    low_k = lane_k < RET_QK_DIM
    low_k16 = lane_k.astype(F32).astype(BF16) < RET_QK_DIM
    low_v16 = lane_q.astype(F32).astype(BF16) < RET_V_DIM
    srow =lax.broadcasted_iota(jnp.int32, (LANES, pair_v), 0)
    scol = lax.broadcasted_iota(jnp.int32, (LANES, pair_v), 1)
    blk0 = jnp.logical_and(srow < RET_QK_DIM, scol < RET_V_DIM)
    blk1 = jnp.logical_and(srow >= RET_QK_DIM, scol >= RET_V_DIM)
    diag = jnp.where(jnp.logical_or(blk0, blk1), 1.0, 0.0).astype(F32)

    for dr in range(2):
        for hd in range(RET_HEADS):
            lg = ld_ref[dr, hd]
            if dr == 0:
                mask = diff >= 0
                dist = jnp.where(mask, diff, 0.0)
            else:
                mask = diff < 0
                dist = jnp.where(mask, -diff, 0.0)
            intra_ref[dr, hd] = jnp.where(mask, jnp.exp(lg * dist), 0.0)
        for hp in range(n_pairs):
            lg0 = ld_ref[dr, 2 * hp]
            lg1 = ld_ref[dr, 2 * hp + 1]
            q_exp = idx_q + 1.0 if dr == 0 else c - idx_q
            k_exp = c - 1.0 - idx_k if dr == 0 else idx_k
            qdec_ref[dr, hp] = jnp.exp(jnp.where(lane_q < RET_V_DIM, lg0, lg1) * q_exp)
            kdec_ref[dr, hp] = jnp.exp(jnp.where(low_k, lg0, lg1) * k_exp)
            cdec_ref[dr, hp] = jnp.where(blk0, jnp.exp(lg0 * c), jnp.where(blk1, jnp.exp(lg1 * c), 0.0))

    def chunk(dr, n):
        r0 = pl.multiple_of(n * c, c)
        qc = q_ref[0, pl.ds(r0, c), :]
        kc = k_ref[0, pl.ds(r0, c), :]
        vc = v_ref[0, pl.ds(r0, c), :]
        ys = []
        for hp in range(n_pairs):
            qp = qc[:, hp * LANES:(hp + 1) * LANES]
            kp = kc[:, hp * LANES:(hp + 1) * LANES]
            vp = vc[:, hp * pair_v:(hp + 1) * pair_v]
            zk = jnp.zeros_like(kp)
            zv = jnp.zeros_like(vp)
            k2 = jnp.concatenate([jnp.where(low_k16, kp, zk), jnp.where(low_k16, zk, kp)], axis=0)
            v2 = jnp.concatenate([jnp.where(low_v16, vp, zv), jnp.where(low_v16, zv, vp)], axis=0)
            s = _dot_nt(qp, k2) * jnp.concatenate([intra_ref[dr, 2 * hp], intra_ref[dr, 2 * hp + 1]], axis=1)
            y_intra = _dot(s.astype(BF16), v2)
            state = st_ref[hp]
            y_inter = _dot(qp, state.astype(BF16)) * qdec_ref[dr, hp]
            ys.append(y_intra + y_inter)
            kd = (kp.astype(F32) * kdec_ref[dr, hp]).astype(BF16)
            st_ref[hp] = cdec_ref[dr, hp] * state + diag * _dot_tn(kd, vp)
        return r0, ys

    st_ref[...] = jnp.zeros_like(st_ref)

    def fwd(n, carry):
        r0, ys = chunk(0, n)
        for hp in range(n_pairs):
            yf_ref[pl.ds(r0, c), hp * pair_v:(hp + 1) * pair_v] = ys[hp]
        return carry

    lax.fori_loop(0, nc, fwd, 0, unroll=RET_UNROLL)
    st_ref[...] = jnp.zeros_like(st_ref)

    def bwd(i, carry):
        r0, ys = chunk(1, nc - 1 - i)
        for hp in range(n_pairs):
            y = ys[hp] + yf_ref[pl.ds(r0, c), hp * pair_v:(hp + 1) * pair_v]
            for half in range(2):
                cols = slice(hp * pair_v + half * RET_V_DIM, hp * pair_v + (half + 1) * RET_V_DIM)
                yh = y[:, half * RET_V_DIM:(half + 1) * RET_V_DIM]
                mu = jnp.mean(yh, axis=-1, keepdims=True)
                dev = yh - mu
                var = jnp.mean(dev * dev, axis=-1, keepdims=True)
                o_ref[0, pl.ds(r0, c), cols] = (dev * lax.rsqrt(var + EPS) * gn_ref[:, cols]).astype(o_ref.dtype)
        return carry

    lax.fori_loop(0, nc, bwd, 0, unroll=RET_UNROLL)


def _ret_scan(log_decay, q, k, v, ret_norm, batch, seq):
    c = RET_CHUNK
    n_pairs = RET_HEADS // 2
    q3 = q.reshape(batch, seq, RET_QK_W)
    k3 = k.reshape(batch, seq, RET_QK_W)
    v3 = v.reshape(batch, seq, RET_V_W)
    out = pl.pallas_call(
        _ret_scan_kernel,
        grid=(batch,),
        in_specs=[
            pl.BlockSpec(memory_space=pltpu.SMEM),
            pl.BlockSpec((1, seq, RET_QK_W), lambda b: (b, 0, 0)),
            pl.BlockSpec((1, seq, RET_QK_W), lambda b: (b, 0, 0)),
            pl.BlockSpec((1, seq, RET_V_W), lambda b: (b, 0, 0)),
            _resident((1, RET_V_W)),
        ],
        out_specs=pl.BlockSpec((1, seq, RET_V_W), lambda b: (b, 0, 0)),
        out_shape=jax.ShapeDtypeStruct((batch, seq, RET_V_W), BF16),
        scratch_shapes=[
            pltpu.VMEM((seq, RET_V_W), F32),
            pltpu.VMEM((n_pairs, LANES, 2 * RET_V_DIM), F32),
            pltpu.VMEM((2, RET_HEADS, c, c), F32),
            pltpu.VMEM((2, n_pairs, c, 2 * RET_V_DIM), F32),
            pltpu.VMEM((2, n_pairs, c, LANES), F32),
            pltpu.VMEM((2, n_pairs, LANES, 2 * RET_V_DIM), F32),
        ],
        compiler_params=_params(1),
        name="ret_scan",
    )(log_decay, q3, k3, v3, ret_norm)
    return out.reshape(batch * seq, RET_V_W)


def _merge_kernel(x_ref, mod_ref, g_ref, wg_ref, ymla_ref, yret_ref,
                  o0_ref, l0_ref, o1_ref, l1_ref, o2_ref, l2_ref,
                  wbm_ref, wbd_ref, wbr_ref, wout_ref, out_ref, nat_o, nat_l):
    x = x_ref[...]
    mod = mod_ref[0]
    h = _norm_mod(x, g_ref[...], mod[1:2], mod[0:1]).astype(BF16)
    gates = _dot(h, wg_ref[0])
    d_model = x.shape[1]
    tm = x.shape[0]

    o_nat = [o0_ref[0, 0].astype(F32)]
    l_nat = [l0_ref[0, 0]]
    for g, (o_ref, l_ref) in enumerate(((o1_ref, l1_ref), (o2_ref, l2_ref))):
        dil = DIL_PAIRS[g + 1][1]
        n = tm // dil
        n_lane_blocks = DIL_GROUP_W // LANES
        for r in range(dil):
            o_r = o_ref[0, r].astype(F32)
            l_r = l_ref[0, r]
            for j in range(n_lane_blocks):
                nat_o[g, j, pl.ds(r, n, stride=dil), :] = o_r[:, j * LANES:(j + 1) * LANES]
                nat_l[g, j, pl.ds(r, n, stride=dil), :] = l_r[:, j * LANES:(j + 1) * LANES]
        o_nat.append(jnp.concatenate([nat_o[g, j] for j in range(n_lane_blocks)], axis=1))
        l_nat.append(jnp.concatenate([nat_l[g, j] for j in range(n_lane_blocks)], axis=1))
    m = jnp.maximum(jnp.maximum(l_nat[0], l_nat[1]), l_nat[2])
    ws = [jnp.exp(l - m) for l in l_nat]
    y_dil = (ws[0] * o_nat[0] + ws[1] * o_nat[1] + ws[2] * o_nat[2]) / (ws[0] + ws[1] + ws[2])

    y_ret = (_silu(gates[:, :RET_V_W]) * yret_ref[...].astype(F32)).astype(BF16)
    ga = gates[:, RET_V_W:RET_V_W + d_model]
    gb = gates[:, RET_V_W + d_model:RET_V_W + 2 * d_model]
    gc = gates[:, RET_V_W + 2 * d_model:]
    merged = (jax.nn.sigmoid(ga) * _dot(ymla_ref[...], wbm_ref[0])
              + jax.nn.sigmoid(gb) * _dot(y_dil.astype(BF16), wbd_ref[0])
              + jax.nn.sigmoid(gc) * _dot(y_ret, wbr_ref[0]))
    out_ref[...] = x + mod[2:3] * _dot(merged.astype(BF16), wout_ref[0])


def _merge_out(x, mod, gain, wg, y_mla, y_ret, dil_outs, wbm, wbd, wbr, wout, layer, batch, seq):
    t, d_model = x.shape
    tm = MERGE_TILE
    per_b = seq // tm
    row = lambda i: (i, 0)
    dil_specs, dil_args = [], []
    for (_, dil), (o, lse) in zip(DIL_PAIRS, dil_outs):
        spec = pl.BlockSpec((1, dil, tm // dil, DIL_GROUP_W), lambda i: (i // per_b, 0, i % per_b, 0))
        dil_specs += [spec, spec]
        dil_args += [o, lse]
    return pl.pallas_call(
        _merge_kernel,
        grid=(t // tm,),
        in_specs=[
            pl.BlockSpec((tm, d_model), row),
            pl.BlockSpec((1, ADA_CHUNKS, d_model), lambda i: (i // per_b, 0, 0)),
            _resident((1, d_model)),
            _layer_resident(wg, layer),
            pl.BlockSpec((tm, y_mla.shape[1]), row),
            pl.BlockSpec((tm, RET_V_W), row),
            *dil_specs,
            _layer_resident(wbm, layer), _layer_resident(wbd, layer), _layer_resident(wbr, layer),
            _layer_resident(wout, layer),
        ],
        out_specs=pl.BlockSpec((tm, d_model), row),
        out_shape=jax.ShapeDtypeStruct((t, d_model), F32),
        scratch_shapes=[pltpu.VMEM((2, DIL_GROUP_W // LANES, tm, LANES), F32),
                        pltpu.VMEM((2, DIL_GROUP_W // LANES, tm, LANES), F32)],
        compiler_params=_params(1),
        name="merge_out",
    )(x, mod, gain, wg, y_mla, y_ret, *dil_args, wbm, wbd, wbr, wout)


def _residual_out(y, fg_ref, final):
    return _rms(y, fg_ref[...]) if final else y


def _ffn_kernel(x_ref, mod_ref, g_ref, fg_ref, w1_ref, w3_ref, w2_ref, out_ref, *, n_chunks, final):
    x = x_ref[...]
    mod = mod_ref[0]
    h = _norm_mod(x, g_ref[...], mod[4:5], mod[3:4]).astype(BF16)
    d_ff = w1_ref.shape[2]
    fc = d_ff // n_chunks
    acc = jnp.zeros(x.shape, F32)
    for j in range(n_chunks):
        a = _dot(h, w1_ref[0, :, j * fc:(j + 1) * fc])
        b = _dot(h, w3_ref[0, :, j * fc:(j + 1) * fc])
        acc = acc + _dot((_silu(a) * b).astype(BF16), w2_ref[0, j * fc:(j + 1) * fc, :])
    out_ref[...] = _residual_out(x + mod[5:6] * acc, fg_ref, final)


def _ffn_dense(x, mod, gain, final_gain, w1, w3, w2, layer, seq, final):
    t, d_model = x.shape
    tm = ROW_TILE
    per_b = seq // tm
    row = lambda i: (i, 0)
    return pl.pallas_call(
        functools.partial(_ffn_kernel, n_chunks=2, final=final),
        grid=(t // tm,),
        in_specs=[
            pl.BlockSpec((tm, d_model), row),
            pl.BlockSpec((1, ADA_CHUNKS, d_model), lambda i: (i // per_b, 0, 0)),
            _resident((1, d_model)),
            _resident((1, d_model)),
            _layer_resident(w1, layer), _layer_resident(w3, layer), _layer_resident(w2, layer),
        ],
        out_specs=pl.BlockSpec((tm, d_model), row),
        out_shape=jax.ShapeDtypeStruct((t, d_model), F32),
        compiler_params=_params(1),
        name="ffn_dense",
    )(x, mod, gain, final_gain, w1, w3, w2)


def _route_kernel(x_ref, mod_ref, g_ref, wr_ref, h_ref, route_ref):
    mod = mod_ref[0]
    h = _norm_mod(x_ref[...], g_ref[...], mod[4:5], mod[3:4])
    h_ref[...] = h
    logits = jnp.dot(h, wr_ref[...], preferred_element_type=F32, precision=lax.Precision.HIGHEST)
    lane = lax.broadcasted_iota(jnp.int32, logits.shape, 1).astype(F32)
    lg = jnp.where(lane < N_EXPERTS, logits, -jnp.inf)
    m1 = jnp.max(lg, axis=-1, keepdims=True)
    i1 = jnp.min(jnp.where(lg == m1, lane, float(LANES)), axis=-1, keepdims=True)
    lg2 = jnp.where(lane == i1, -jnp.inf, lg)
    m2 = jnp.max(lg2, axis=-1, keepdims=True)
    i2 = jnp.min(jnp.where(lg2 == m2, lane, float(LANES)), axis=-1, keepdims=True)
    e = jnp.exp(m2 - m1)
    w1 = 1.0 / (1.0 + e)
    w2 = e / (1.0 + e)
    route_ref[...] = jnp.where(lane == 0, i1,
                               jnp.where(lane == 1, i2, jnp.where(lane == 2, w1, jnp.where(lane == 3, w2, 0.0))))


def _moe_route(x, mod, gain, w_router, seq):
    t, d_model = x.shape
    tm = ROW_TILE
    per_b = seq // tm
    row = lambda i: (i, 0)
    wr = jnp.zeros((d_model, LANES), F32).at[:, :N_EXPERTS].set(w_router)
    return pl.pallas_call(
        _route_kernel,
        grid=(t // tm,),
        in_specs=[
            pl.BlockSpec((tm, d_model), row),
            pl.BlockSpec((1, ADA_CHUNKS, d_model), lambda i: (i // per_b, 0, 0)),
            _resident((1, d_model)),
            _resident(wr.shape),
        ],
        out_specs=[pl.BlockSpec((tm, d_model), row), pl.BlockSpec((tm, LANES), row)],
        out_shape=[jax.ShapeDtypeStruct((t, d_model), F32), jax.ShapeDtypeStruct((t, LANES), F32)],
        compiler_params=_params(1),
        name="moe_route",
    )(x, mod, gain, wr)


def _row_copy(src_hbm, dst, src_row, dst_row, sem):
    return pltpu.make_async_copy(src_hbm.at[pl.ds(src_row, 1), :], dst.at[pl.ds(dst_row, 1), :], sem)


def _expert_kernel(be_ref, nact_ref, tok_ref, tok_next_ref, h_hbm, w1_ref, w3_ref, w2_ref, y_ref,
                   xbuf, sems, *, n_chunks):
    i = pl.program_id(0)
    nact = nact_ref[0]
    bm = y_ref.shape[0]
    slot = i % 2

    def wait_block(s):
        pltpu.make_async_copy(h_hbm.at[pl.ds(0, bm), :], xbuf.at[s], sems.at[s]).wait()

    @pl.when(i == 0)
    def _():
        def issue(r, carry):
            _row_copy(h_hbm, xbuf.at[0], tok_ref[0, 0, r], r, sems.at[0]).start()
            return carry

        lax.fori_loop(0, bm, issue, 0, unroll=8)

    @pl.when(i >= nact)
    def _():
        y_ref[...] = jnp.zeros_like(y_ref)

    @pl.when(i < nact)
    def _():
        wait_block(slot)
        xb = xbuf[slot].astype(BF16)
        for r in range(bm):
            _row_copy(h_hbm, xbuf.at[1 - slot], tok_next_ref[0, 0, r], r, sems.at[1 - slot]).start()
        d_ff = w1_ref.shape[2]
        fc = d_ff // n_chunks
        acc = jnp.zeros(y_ref.shape, F32)
        for j in range(n_chunks):
            a = _dot(xb, w1_ref[0, :, j * fc:(j + 1) * fc])
            b = _dot(xb, w3_ref[0, :, j * fc:(j + 1) * fc])
            acc = acc + _dot((_silu(a) * b).astype(BF16), w2_ref[0, j * fc:(j + 1) * fc, :])
        y_ref[...] = acc

    @pl.when(i == nact - 1)
    def _():
        wait_block(1 - slot)


def _moe_expert(block_expert, nact, row_tok, h, w1, w3, w2, expert_base):
    nb, _, bm = row_tok.shape
    d_model = h.shape[1]
    d_ff = w1.shape[2]
    wsel = lambda i, be, na: (expert_base + be[jnp.minimum(i, na[0] - 1)], 0, 0)
    return pl.pallas_call(
        functools.partial(_expert_kernel, n_chunks=2),
        grid_spec=pltpu.PrefetchScalarGridSpec(
            num_scalar_prefetch=2,
            grid=(nb,),
            in_specs=[
                pl.BlockSpec((1, 1, bm), lambda i, be, na: (i, 0, 0), memory_space=pltpu.SMEM),
                pl.BlockSpec((1, 1, bm), lambda i, be, na: (jnp.minimum(i + 1, nb - 1), 0, 0),
                             memory_space=pltpu.SMEM),
                pl.BlockSpec(memory_space=pl.ANY),
                pl.BlockSpec((1, d_model, d_ff), wsel),
                pl.BlockSpec((1, d_model, d_ff), wsel),
                pl.BlockSpec((1, d_ff, d_model), wsel),
            ],
            out_specs=pl.BlockSpec((bm, d_model), lambda i, be, na: (i, 0)),
            scratch_shapes=[pltpu.VMEM((2, bm, d_model), F32), pltpu.SemaphoreType.DMA((2,))],
        ),
        out_shape=jax.ShapeDtypeStruct((nb * bm, d_model), F32),
        compiler_params=_params(1, V7X_EXPERT_VMEM_LIMIT_BYTES),
        name="moe_expert",
    )(block_expert, nact, row_tok, row_tok, h, w1, w3, w2)


def _combine_kernel(dest_ref, dest_next_ref, x_ref, mod_ref, route_ref, fg_ref, yb_hbm, out_ref, buf, sems,
                    *, final):
    tm = x_ref.shape[0]
    n_rows = TOP_K * tm
    i = pl.program_id(0)
    slot = i % 2

    def issue(rows_ref, s):
        def body(r, carry):
            _row_copy(yb_hbm, buf.at[s], rows_ref[0, 0, r], r, sems.at[s]).start()
            return carry

        lax.fori_loop(0, n_rows, body, 0, unroll=8)

    @pl.when(i == 0)
    def _():
        issue(dest_ref, 0)

    @pl.when(i + 1 < pl.num_programs(0))
    def _():
        issue(dest_next_ref, 1 - slot)

    pltpu.make_async_copy(yb_hbm.at[pl.ds(0, n_rows), :], buf.at[slot], sems.at[slot]).wait()
    route = route_ref[...]
    y = route[:, 2:3] * buf[slot, :tm, :] + route[:, 3:4] * buf[slot, tm:, :]
    out_ref[...] = _residual_out(x_ref[...] + mod_ref[0][5:6] * y, fg_ref, final)


def _moe_combine(dest, x, mod, route, final_gain, yb, seq, final):
    t, d_model = x.shape
    tm = COMBINE_TILE
    per_b = seq // tm
    row = lambda i: (i, 0)
    return pl.pallas_call(
        functools.partial(_combine_kernel, final=final),
        grid=(t // tm,),
        in_specs=[
            pl.BlockSpec((1, 1, TOP_K * tm), lambda i: (i, 0, 0), memory_space=pltpu.SMEM),
            pl.BlockSpec((1, 1, TOP_K * tm), lambda i: (jnp.minimum(i + 1, t // tm - 1), 0, 0),
                         memory_space=pltpu.SMEM),
            pl.BlockSpec((tm, d_model), row),
            pl.BlockSpec((1, ADA_CHUNKS, d_model), lambda i: (i // per_b, 0, 0)),
            pl.BlockSpec((tm, LANES), row),
            _resident((1, d_model)),
            pl.BlockSpec(memory_space=pl.ANY),
        ],
        out_specs=pl.BlockSpec((tm, d_model), row),
        out_shape=jax.ShapeDtypeStruct((t, d_model), F32),
        scratch_shapes=[pltpu.VMEM((2, TOP_K * tm, d_model), F32), pltpu.SemaphoreType.DMA((2,))],
        compiler_params=_params(1),
        name="moe_combine",
    )(dest, dest, x, mod, route, final_gain, yb)


def _moe(x, mod, gain, final_gain, w_router, w1, w3, w2, expert_base, seq, final):
    t, d_model = x.shape
    bm = MOE_BLOCK
    h, route = _moe_route(x, mod, gain, w_router, seq)
    e_flat = route[:, :TOP_K].astype(jnp.int32).reshape(-1)
    n_assign = e_flat.shape[0]
    onehot = (e_flat[:, None] == jnp.arange(N_EXPERTS, dtype=jnp.int32)[None, :]).astype(jnp.int32)
    csum = jnp.cumsum(onehot, axis=0)
    rank = jnp.take_along_axis(csum, e_flat[:, None], axis=1)[:, 0] - 1
    counts = csum[-1]
    padded = (counts + bm - 1) // bm * bm
    pends = jnp.cumsum(padded)
    dest = (pends - padded)[e_flat] + rank
    nb = n_assign // bm + N_EXPERTS
    row_tok = jnp.zeros((nb * bm,), jnp.int32).at[dest].set(jnp.arange(n_assign, dtype=jnp.int32) // TOP_K)
    block_start = jnp.arange(nb, dtype=jnp.int32) * bm
    block_expert = jnp.minimum(
        jnp.sum((pends[None, :] <= block_start[:, None]).astype(jnp.int32), axis=1), N_EXPERTS - 1)
    nact = (pends[-1:] // bm).astype(jnp.int32)

    yb = _moe_expert(block_expert, nact, row_tok.reshape(nb, 1, bm), h, w1, w3, w2, expert_base)
    tmc = COMBINE_TILE
    dest_tiles = dest.reshape(t // tmc, tmc, TOP_K).transpose(0, 2, 1).reshape(t // tmc, 1, TOP_K * tmc)
    return _moe_combine(dest_tiles.astype(jnp.int32), x, mod, route, final_gain, yb, seq, final)


def _rope_tables(positions):
    inv_freq = ROPE_THETA ** (-jnp.arange(0, ROPE_DIM, 2, dtype=F32) / ROPE_DIM)
    ang = positions.astype(F32)[..., None] * inv_freq
    cos, sin = jnp.cos(ang), jnp.sin(ang)
    reps = LANES // ROPE_DIM
    cos_t = jnp.tile(jnp.concatenate([cos, cos], axis=-1), (1, 1, reps))
    sin_t = jnp.tile(jnp.concatenate([-sin, sin], axis=-1), (1, 1, reps))
    return cos_t, sin_t


def _split_w_in(w_in):
    d_model = w_in.shape[1]
    sizes = (MLA_Q_RANK, MLA_KV_RANK, MLA_ROPE,
             3 * DIL_GROUP_W, 3 * DIL_GROUP_W, 3 * DIL_GROUP_W,
             RET_QK_W, RET_QK_W, RET_V_W, RET_V_W, d_model, d_model, d_model)
    w_in = w_in.astype(BF16)
    cols, start = [], 0
    for n in sizes:
        cols.append(w_in[..., start:start + n])
        start += n
    cq, ckv, kr, dq, dk, dv, rq, rk, rv, rg, ga, gb, gc = cols
    w_mla = jnp.concatenate([cq, ckv, kr, kr], axis=-1)
    gw = DIL_GROUP_W
    w_dil = jnp.concatenate(
        [m[..., g * gw:(g + 1) * gw] for g in range(len(DIL_PAIRS)) for m in (dq, dk, dv)], axis=-1)
    w_ret = jnp.concatenate([rq, rk, rv], axis=-1)
    w_gate = jnp.concatenate([rg, ga, gb, gc], axis=-1)
    return w_mla, w_dil, w_ret, w_gate


def _split_w_uq(w_uq):
    depth = w_uq.shape[0]
    w = w_uq.astype(BF16).reshape(depth, MLA_Q_RANK, MLA_HEADS, MLA_QK)
    nope = w[..., :MLA_NOPE].reshape(depth, MLA_Q_RANK, MLA_HEADS * MLA_NOPE)
    rope = w[..., MLA_NOPE:].reshape(depth, MLA_Q_RANK, MLA_HEADS * MLA_ROPE)
    return jnp.concatenate([nope, rope], axis=-1)


def kernel(x, c, positions, ada_w, ada_b, norm_mix, norm_ffn, w_in, mla_q_norm, mla_w_uq, mla_kv_norm,
           mla_w_ukv, ret_log_decay, ret_norm, w_br_mla, w_br_dil, w_br_ret, w_out, ffn_w1, ffn_w3, ffn_w2,
           moe_router, moe_w1, moe_w3, moe_w2, final_norm):
    batch, seq, d_model = x.shape
    depth = ada_w.shape[0]
    cos_t, sin_t = _rope_tables(positions)
    mod_all = _ada(c, ada_w, ada_b)
    assert depth >= 1
    fgain = final_norm.reshape(1, d_model)
    xt = x.reshape(batch * seq, d_model)
    moe_w1_b = moe_w1.astype(BF16).reshape((-1,) + moe_w1.shape[2:])
    moe_w3_b = moe_w3.astype(BF16).reshape((-1,) + moe_w3.shape[2:])
    moe_w2_b = moe_w2.astype(BF16).reshape((-1,) + moe_w2.shape[2:])
    w_mla, w_dil, w_ret, w_gate = _split_w_in(w_in)
    w_uq = _split_w_uq(mla_w_uq)
    w_ukv = mla_w_ukv.astype(BF16)
    wbm, wbd, wbr, wout = (w.astype(BF16) for w in (w_br_mla, w_br_dil, w_br_ret, w_out))
    ffn_w1_b, ffn_w3_b, ffn_w2_b = (w.astype(BF16) for w in (ffn_w1, ffn_w3, ffn_w2))
    for layer in range(depth):
        mod = mod_all[layer]
        gmix = norm_mix[layer].reshape(1, d_model)

        q, k, v = _mla_prep(xt, mod, gmix, w_mla, mla_q_norm[layer].reshape(1, -1),
                            mla_kv_norm[layer].reshape(1, -1), w_uq, w_ukv, cos_t, sin_t, layer, batch, seq)
        y_mla = _mla_attn(q, k, v)

        dil_qkv = _dil_prep(xt, mod, gmix, w_dil, cos_t, sin_t, layer, batch, seq)
        dil_outs = [_dil_attn(*dil_qkv[3 * g:3 * g + 3], window, dil)
                    for g, (window, dil) in enumerate(DIL_PAIRS)]

        rq, rk, rv = _ret_prep(xt, mod, gmix, w_ret, cos_t, sin_t, layer, seq)
        y_ret = _ret_scan(ret_log_decay[layer].astype(F32), rq, rk, rv,
                          ret_norm[layer].reshape(1, -1).astype(F32), batch, seq)

        xt = _merge_out(xt, mod, gmix, w_gate, y_mla, y_ret, dil_outs, wbm, wbd, wbr, wout, layer, batch, seq)

        gffn = norm_ffn[layer].reshape(1, d_model)
        i = layer // 2
        last = layer == depth - 1
        if layer % 2 == 0:
            xt = _ffn_dense(xt, mod, gffn, fgain, ffn_w1_b, ffn_w3_b, ffn_w2_b, i, seq, last)
        else:
            xt = _moe(xt, mod, gffn, fgain, moe_router[i], moe_w1_b, moe_w3_b, moe_w2_b, i * N_EXPERTS, seq, last)
    return xt.reshape(batch, seq, d_model)
```

```python
import functools

import jax
import jax.numpy as jnp
from jax import lax
from jax.experimental import pallas as pl
from jax.experimental.pallas import tpu as pltpu

F32 = jnp.float32
BF16 = jnp.bfloat16

EPS = 1e-6
NEG_INF = -1e30
ROPE_THETA = 10000.0
ROPE_DIM = 64
ADA_CHUNKS = 6

MLA_HEADS = 8
MLA_Q_RANK = 768
MLA_KV_RANK = 512
MLA_NOPE = 128
MLA_ROPE = ROPE_DIM
MLA_V = 128
MLA_QK = MLA_NOPE + MLA_ROPE

DIL_PAIRS = ((128, 1), (512, 4), (2048, 16))
DIL_HEADS = 8
DIL_HEAD_DIM = ROPE_DIM
DIL_GROUP_W = DIL_HEADS * DIL_HEAD_DIM
DIL_QBLOCK = 128

RET_HEADS = 8
RET_QK_DIM = ROPE_DIM
RET_V_DIM = 2 * RET_QK_DIM
RET_CHUNK = 256
RET_UNROLL = 4
RET_QK_W = RET_HEADS * RET_QK_DIM
RET_V_W = RET_HEADS * RET_V_DIM

N_EXPERTS = 8
TOP_K = 2
MOE_BLOCK = 256

LANES = 128
V7X_VMEM_LIMIT_BYTES = 56 * 1024 * 1024

ROW_TILE = 512
MERGE_TILE = 256
COMBINE_TILE = 256
MLA_Q_TILE = 512
MLA_HEADS_PER_STEP = 4
MLA_KEY_CHUNK = 1024


def _params(n_grid_dims):
    return pltpu.CompilerParams(
        dimension_semantics=("arbitrary",) * n_grid_dims,
        vmem_limit_bytes=V7X_VMEM_LIMIT_BYTES,
    )


def _resident(shape):
    zeros = (0,) * len(shape)
    return pl.BlockSpec(shape, lambda *_: zeros, pipeline_mode=pl.Buffered(1))


def _layer_resident(stacked, layer):
    idx = (layer,) + (0,) * (stacked.ndim - 1)
    return pl.BlockSpec((1,) + stacked.shape[1:], lambda *_: idx, pipeline_mode=pl.Buffered(1))


def _silu(v):
    return v * jax.nn.sigmoid(v)


def _norm_mod(x, gain, scale, shift):
    ms = jnp.mean(x * x, axis=-1, keepdims=True)
    y = x * lax.rsqrt(ms + EPS) * gain
    return y * (1.0 + scale) + shift


def _rms(v, gain):
    ms = jnp.mean(v * v, axis=-1, keepdims=True)
    return v * lax.rsqrt(ms + EPS) * gain


def _rope(t, cos_t, sin_t):
    lane = lax.broadcasted_iota(jnp.int32, (t.shape[0], LANES), 1)
    first_half = (lane % ROPE_DIM) < (ROPE_DIM // 2)
    out = []
    for j in range(t.shape[1] // LANES):
        c = t[:, j * LANES:(j + 1) * LANES]
        rot = jnp.where(first_half,
                        pltpu.roll(c, LANES - ROPE_DIM // 2, 1),
                        pltpu.roll(c, ROPE_DIM // 2, 1))
        out.append(c * cos_t + rot * sin_t)
    return out[0] if len(out) == 1 else jnp.concatenate(out, axis=1)


def _dot(a, b):
    return jnp.dot(a, b, preferred_element_type=F32)


def _dot_nt(a, b):
    return lax.dot_general(a, b, (((1,), (1,)), ((), ())), preferred_element_type=F32)


def _dot_tn(a, b):
    return lax.dot_general(a, b, (((0,), (0,)), ((), ())), preferred_element_type=F32)


def _ada_kernel(c_ref, w_ref, b_ref, o_ref):
    o_ref[0] = jnp.dot(_silu(c_ref[...]), w_ref[0], preferred_element_type=F32,
                       precision=lax.Precision.HIGHEST) + b_ref[0]


def _ada(c, ada_w, ada_b):
    depth, d_model, n = ada_w.shape
    batch = c.shape[0]
    tn = n // 4
    out = pl.pallas_call(
        _ada_kernel,
        grid=(depth, n // tn),
        in_specs=[
            pl.BlockSpec((batch, d_model), lambda l, j: (0, 0)),
            pl.BlockSpec((1, d_model, tn), lambda l, j: (l, 0, j)),
            pl.BlockSpec((1, 1, tn), lambda l, j: (l, 0, j)),
        ],
        out_specs=pl.BlockSpec((1, batch, tn), lambda l, j: (l, 0, j)),
        out_shape=jax.ShapeDtypeStruct((depth, batch, n), F32),
        compiler_params=_params(2),
        name="ada",
    )(c, ada_w, ada_b.reshape(depth, 1, n))
    return out.reshape(depth, batch, ADA_CHUNKS, d_model)


def _mla_prep_kernel(x_ref, mod_ref, g_ref, wa_ref, qn_ref, kvn_ref, wuq_ref, wukv_ref, cos_ref, sin_ref,
                     q_ref, k_ref, v_ref):
    mod = mod_ref[0]
    h = _norm_mod(x_ref[...], g_ref[...], mod[1:2], mod[0:1]).astype(BF16)
    p = _dot(h, wa_ref[0])
    cq = _rms(p[:, :MLA_Q_RANK], qn_ref[...]).astype(BF16)
    ckv = _rms(p[:, MLA_Q_RANK:MLA_Q_RANK + MLA_KV_RANK], kvn_ref[...]).astype(BF16)
    cos_t = cos_ref[0]
    sin_t = sin_ref[0]
    kr = _rope(p[:, MLA_Q_RANK + MLA_KV_RANK:], cos_t, sin_t)
    q = _dot(cq, wuq_ref[0])
    kv = _dot(ckv, wukv_ref[0])
    nope_w = MLA_HEADS * MLA_NOPE
    qr = _rope(q[:, nope_w:], cos_t, sin_t)
    scale = MLA_QK ** -0.5
    kr_b = kr[:, :MLA_ROPE].astype(BF16)
    for hd in range(MLA_HEADS):
        q_ref[0, hd, :, :MLA_NOPE] = (q[:, hd * MLA_NOPE:(hd + 1) * MLA_NOPE] * scale).astype(BF16)
        q_ref[0, hd, :, MLA_NOPE:] = (qr[:, hd * MLA_ROPE:(hd + 1) * MLA_ROPE] * scale).astype(BF16)
        base = hd * (MLA_NOPE + MLA_V)
        k_ref[0, hd, :, :MLA_NOPE] = kv[:, base:base + MLA_NOPE].astype(BF16)
        k_ref[0, hd, :, MLA_NOPE:] = kr_b
        v_ref[0, hd] = kv[:, base + MLA_NOPE:base + MLA_NOPE + MLA_V].astype(BF16)


def _mla_prep(x, mod, gain, wa, qn, kvn, wuq, wukv, cos_t, sin_t, layer, batch, seq):
    t, d_model = x.shape
    tm = ROW_TILE
    per_b = seq // tm
    row = lambda i: (i, 0)
    hs = lambda i: (i // per_b, 0, i % per_b, 0)
    return pl.pallas_call(
        _mla_prep_kernel,
        grid=(t // tm,),
        in_specs=[
            pl.BlockSpec((tm, d_model), row),
            pl.BlockSpec((1, ADA_CHUNKS, d_model), lambda i: (i // per_b, 0, 0)),
            _resident((1, d_model)),
            _layer_resident(wa, layer),
            _resident((1, MLA_Q_RANK)),
            _resident((1, MLA_KV_RANK)),
            _layer_resident(wuq, layer),
            _layer_resident(wukv, layer),
            pl.BlockSpec((1, tm, LANES), lambda i: (i // per_b, i % per_b, 0)),
            pl.BlockSpec((1, tm, LANES), lambda i: (i // per_b, i % per_b, 0)),
        ],
        out_specs=[
            pl.BlockSpec((1, MLA_HEADS, tm, MLA_QK), hs),
            pl.BlockSpec((1, MLA_HEADS, tm, MLA_QK), hs),
            pl.BlockSpec((1, MLA_HEADS, tm, MLA_V), hs),
        ],
        out_shape=[
            jax.ShapeDtypeStruct((batch, MLA_HEADS, seq, MLA_QK), BF16),
            jax.ShapeDtypeStruct((batch, MLA_HEADS, seq, MLA_QK), BF16),
            jax.ShapeDtypeStruct((batch, MLA_HEADS, seq, MLA_V), BF16),
        ],
        compiler_params=_params(1),
        name="mla_prep",
    )(x, mod, gain, wa, qn, kvn, wuq, wukv, cos_t, sin_t)


def _mla_attn_kernel(q_ref, k_ref, v_ref, o_ref, *, heads_per_step, key_chunk):
    seq = k_ref.shape[2]
    ones = jnp.ones((key_chunk, LANES), BF16)
    for hd in range(heads_per_step):
        q = q_ref[0, hd]
        m = acc = None
        for c in range(seq // key_chunk):
            rows = slice(c * key_chunk, (c + 1) * key_chunk)
            s = _dot_nt(q, k_ref[0, hd, rows, :])
            v_ext = jnp.concatenate([v_ref[0, hd, rows, :], ones], axis=1)
            m_c = jnp.max(s, axis=-1, keepdims=True)
            if c == 0:
                m = m_c
                acc = _dot(jnp.exp(s - m).astype(BF16), v_ext)
            else:
                m_new = jnp.maximum(m, m_c)
                acc = jnp.exp(m - m_new) * acc + _dot(jnp.exp(s - m_new).astype(BF16), v_ext)
                m = m_new
        o_ref[:, hd * MLA_V:(hd + 1) * MLA_V] = (acc[:, :MLA_V] / acc[:, MLA_V:MLA_V + 1]).astype(o_ref.dtype)


def _mla_attn(q, k, v):
    batch, heads, seq, _ = q.shape
    tq = MLA_Q_TILE
    nq = seq // tq
    hps = MLA_HEADS_PER_STEP
    return pl.pallas_call(
        functools.partial(_mla_attn_kernel, heads_per_step=hps, key_chunk=MLA_KEY_CHUNK),
        grid=(batch, heads // hps, nq),
        in_specs=[
            pl.BlockSpec((1, hps, tq, MLA_QK), lambda b, h, i: (b, h, i, 0)),
            pl.BlockSpec((1, hps, seq, MLA_QK), lambda b, h, i: (b, h, 0, 0)),
            pl.BlockSpec((1, hps, seq, MLA_V), lambda b, h, i: (b, h, 0, 0)),
        ],
        out_specs=pl.BlockSpec((tq, hps * MLA_V), lambda b, h, i: (b * nq + i, h)),
        out_shape=jax.ShapeDtypeStruct((batch * seq, heads * MLA_V), BF16),
        compiler_params=_params(3),
        name="mla_attn",
    )(q, k, v)


def _dil_prep_kernel(x_ref, mod_ref, g_ref, w_ref, cos_ref, sin_ref, *refs):
    out_refs = refs[:3 * len(DIL_PAIRS)]
    h_scr = refs[3 * len(DIL_PAIRS)]
    mod = mod_ref[0]
    h32 = _norm_mod(x_ref[...], g_ref[...], mod[1:2], mod[0:1])
    n_lane_blocks = h32.shape[1] // LANES
    for j in range(n_lane_blocks):
        h_scr[j] = h32[:, j * LANES:(j + 1) * LANES]
    tm = h32.shape[0]
    gw = DIL_GROUP_W
    for g, (_, dil) in enumerate(DIL_PAIRS):
        n = tm // dil
        if dil == 1:
            hp, cos_t, sin_t = h32, cos_ref[0], sin_ref[0]
        else:
            hp = jnp.concatenate(
                [jnp.concatenate([h_scr[j, pl.ds(r, n, stride=dil), :] for j in range(n_lane_blocks)], axis=1)
                 for r in range(dil)], axis=0)
            cos_t = jnp.concatenate([cos_ref[0, pl.ds(r, n, stride=dil), :] for r in range(dil)], axis=0)
            sin_t = jnp.concatenate([sin_ref[0, pl.ds(r, n, stride=dil), :] for r in range(dil)], axis=0)
        pg = _dot(hp.astype(BF16), w_ref[0, :, g * 3 * gw:(g + 1) * 3 * gw])
        qg = (_rope(pg[:, :gw], cos_t, sin_t) * (DIL_HEAD_DIM ** -0.5)).astype(BF16)
        kg = _rope(pg[:, gw:2 * gw], cos_t, sin_t).astype(BF16)
        vg = pg[:, 2 * gw:].astype(BF16)
        q_ref, k_ref, v_ref = out_refs[3 * g:3 * g + 3]
        for r in range(dil):
            q_ref[0, r] = qg[r * n:(r + 1) * n]
            k_ref[0, r] = kg[r * n:(r + 1) * n]
            v_ref[0, r] = vg[r * n:(r + 1) * n]


def _dil_prep(x, mod, gain, w, cos_t, sin_t, layer, batch, seq):
    t, d_model = x.shape
    tm = ROW_TILE
    per_b = seq // tm
    out_specs, out_shape = [], []
    for _, dil in DIL_PAIRS:
        for _ in range(3):
            out_specs.append(pl.BlockSpec((1, dil, tm // dil, DIL_GROUP_W),
                                          lambda i: (i // per_b, 0, i % per_b, 0)))
            out_shape.append(jax.ShapeDtypeStruct((batch, dil, seq // dil, DIL_GROUP_W), BF16))
    return pl.pallas_call(
        _dil_prep_kernel,
        grid=(t // tm,),
        in_specs=[
            pl.BlockSpec((tm, d_model), lambda i: (i, 0)),
            pl.BlockSpec((1, ADA_CHUNKS, d_model), lambda i: (i // per_b, 0, 0)),
            _resident((1, d_model)),
            _layer_resident(w, layer),
            pl.BlockSpec((1, tm, LANES), lambda i: (i // per_b, i % per_b, 0)),
            pl.BlockSpec((1, tm, LANES), lambda i: (i // per_b, i % per_b, 0)),
        ],
        out_specs=out_specs,
        out_shape=out_shape,
        scratch_shapes=[pltpu.VMEM((d_model // LANES, tm, LANES), F32)],
        compiler_params=_params(1),
        name="dil_prep",
    )(x, mod, gain, w, cos_t, sin_t)


def _dil_attn_kernel(q_ref, k_ref, v_ref, o_ref, lse_ref, *, n_rows, length, segment, band):
    qb_rows = DIL_QBLOCK
    kw = 2 * qb_rows
    nb = length // qb_rows
    low_k = lax.broadcasted_iota(jnp.int32, (kw, LANES), 1).astype(F32).astype(BF16) < DIL_HEAD_DIM
    low_q = lax.broadcasted_iota(jnp.int32, (qb_rows, LANES), 1) < DIL_HEAD_DIM
    row_i = lax.broadcasted_iota(jnp.int32, (qb_rows, kw), 0)
    col_i = lax.broadcasted_iota(jnp.int32, (qb_rows, kw), 1)
    zeros_k = jnp.zeros((kw, LANES), BF16)
    ones_k = jnp.ones((kw, LANES), BF16)
    denom_cols = jnp.concatenate([jnp.where(low_k, ones_k, zeros_k), jnp.where(low_k, zeros_k, ones_k)], axis=0)

    def tile(idx, carry):
        r = idx // nb
        q0 = pl.multiple_of((idx % nb) * qb_rows, qb_rows)
        ks = pl.multiple_of(jnp.clip(q0 - band, 0, length - kw), band)
        qt = q_ref[0, r, pl.ds(q0, qb_rows), :]
        kt = k_ref[0, r, pl.ds(ks, kw), :]
        vt = v_ref[0, r, pl.ds(ks, kw), :]
        jq = q0 + row_i
        jk = ks + col_i
        valid = jnp.logical_and(jnp.abs(jq - jk) <= band, jq // segment == jk // segment)
        for hp in range(DIL_GROUP_W // LANES):
            cols = slice(hp * LANES, (hp + 1) * LANES)
            qp, kp, vp = qt[:, cols], kt[:, cols], vt[:, cols]
            k2 = jnp.concatenate([jnp.where(low_k, kp, zeros_k), jnp.where(low_k, zeros_k, kp)], axis=0)
            v2 = jnp.concatenate([jnp.where(low_k, vp, zeros_k), jnp.where(low_k, zeros_k, vp)], axis=0)
            s = _dot_nt(qp, k2)
            s0 = jnp.where(valid, s[:, :kw], NEG_INF)
            s1 = jnp.where(valid, s[:, kw:], NEG_INF)
            m0 = jnp.max(s0, axis=-1, keepdims=True)
            m1 = jnp.max(s1, axis=-1, keepdims=True)
            p = jnp.concatenate([jnp.exp(s0 - m0), jnp.exp(s1 - m1)], axis=1).astype(BF16)
            pv = _dot(p, jnp.concatenate([v2, denom_cols], axis=1))
            den = pv[:, LANES:]
            o_ref[0, r, pl.ds(q0, qb_rows), cols] = (pv[:, :LANES] / den).astype(o_ref.dtype)
            lse_ref[0, r, pl.ds(q0, qb_rows), cols] = jnp.where(low_q, m0, m1) + jnp.log(den)
        return carry

    lax.fori_loop(0, n_rows * nb, tile, 0, unroll=8)


def _dil_attn(q, k, v, window, dil):
    batch, _, seg, gw = q.shape
    band = window // (2 * dil)
    assert band * 2 == DIL_QBLOCK and seg % DIL_QBLOCK == 0
    length = max(seg, 2 * DIL_QBLOCK)
    n_rows = dil * seg // length
    view = (batch, n_rows, length, gw)
    spec = pl.BlockSpec((1, n_rows, length, gw), lambda b: (b, 0, 0, 0))
    o, lse = pl.pallas_call(
        functools.partial(_dil_attn_kernel, n_rows=n_rows, length=length, segment=seg, band=band),
        grid=(batch,),
        in_specs=[spec, spec, spec],
        out_specs=[spec, spec],
        out_shape=[jax.ShapeDtypeStruct(view, BF16), jax.ShapeDtypeStruct(view, F32)],
        compiler_params=_params(1),
        name=f"dil_attn_d{dil}",
    )(q.reshape(view), k.reshape(view), v.reshape(view))
    return o.reshape(q.shape), lse.reshape(q.shape)


def _ret_prep_kernel(x_ref, mod_ref, g_ref, w_ref, cos_ref, sin_ref, q_ref, k_ref, v_ref):
    mod = mod_ref[0]
    h = _norm_mod(x_ref[...], g_ref[...], mod[1:2], mod[0:1]).astype(BF16)
    p = _dot(h, w_ref[0])
    cos_t, sin_t = cos_ref[0], sin_ref[0]
    q_ref[...] = _rope(p[:, :RET_QK_W], cos_t, sin_t).astype(BF16)
    k_ref[...] = (_rope(p[:, RET_QK_W:2 * RET_QK_W], cos_t, sin_t) * (RET_QK_DIM ** -0.5)).astype(BF16)
    v_ref[...] = p[:, 2 * RET_QK_W:].astype(BF16)


def _ret_prep(x, mod, gain, w, cos_t, sin_t, layer, seq):
    t, d_model = x.shape
    tm = ROW_TILE
    per_b = seq // tm
    row = lambda i: (i, 0)
    return pl.pallas_call(
        _ret_prep_kernel,
        grid=(t // tm,),
        in_specs=[
            pl.BlockSpec((tm, d_model), row),
            pl.BlockSpec((1, ADA_CHUNKS, d_model), lambda i: (i // per_b, 0, 0)),
            _resident((1, d_model)),
            _layer_resident(w, layer),
            pl.BlockSpec((1, tm, LANES), lambda i: (i // per_b, i % per_b, 0)),
            pl.BlockSpec((1, tm, LANES), lambda i: (i // per_b, i % per_b, 0)),
        ],
        out_specs=[pl.BlockSpec((tm, RET_QK_W), row), pl.BlockSpec((tm, RET_QK_W), row),
                   pl.BlockSpec((tm, RET_V_W), row)],
        out_shape=[jax.ShapeDtypeStruct((t, RET_QK_W), BF16), jax.ShapeDtypeStruct((t, RET_QK_W), BF16),
                   jax.ShapeDtypeStruct((t, RET_V_W), BF16)],
        compiler_params=_params(1),
        name="ret_prep",
    )(x, mod, gain, w, cos_t, sin_t)


def _ret_scan_kernel(ld_ref, q_ref, k_ref, v_ref, gn_ref, o_ref,
                     yf_ref, st_ref, intra_ref, qdec_ref, kdec_ref, cdec_ref):
    c = RET_CHUNK
    seq = q_ref.shape[1]
    nc = seq // c
    n_pairs = RET_HEADS // 2
    pair_v = 2 * RET_V_DIM
    ia = lax.broadcasted_iota(jnp.int32, (c, c), 0)
    ib = lax.broadcasted_iota(jnp.int32, (c, c), 1)
    diff = (ia - ib).astype(F32)
    idx_q = lax.broadcasted_iota(jnp.int32, (c, pair_v), 0).astype(F32)
    lane_q = lax.broadcasted_iota(jnp.int32, (c, pair_v), 1)
    idx_k = lax.broadcasted_iota(jnp.int32, (c, LANES), 0).astype(F32)
    lane_k = lax.broadcasted_iota(jnp.int32, (c, LANES), 1)
    low_k = lane_k < RET_QK_DIM
    low_k16 = lane_k.astype(F32).astype(BF16) < RET_QK_DIM
    low_v16 = lane_q.astype(F32).astype(BF16) < RET_V_DIM
    srow =lax.broadcasted_iota(jnp.int32, (LANES, pair_v), 0)
    scol = lax.broadcasted_iota(jnp.int32, (LANES, pair_v), 1)
    blk0 = jnp.logical_and(srow < RET_QK_DIM, scol < RET_V_DIM)
    blk1 = jnp.logical_and(srow >= RET_QK_DIM, scol >= RET_V_DIM)
    diag = jnp.where(jnp.logical_or(blk0, blk1), 1.0, 0.0).astype(F32)

    for dr in range(2):
        for hd in range(RET_HEADS):
            lg = ld_ref[dr, hd]
            if dr == 0:
                mask = diff >= 0
                dist = jnp.where(mask, diff, 0.0)
            else:
                mask = diff < 0
                dist = jnp.where(mask, -diff, 0.0)
            intra_ref[dr, hd] = jnp.where(mask, jnp.exp(lg * dist), 0.0)
        for hp in range(n_pairs):
            lg0 = ld_ref[dr, 2 * hp]
            lg1 = ld_ref[dr, 2 * hp + 1]
            q_exp = idx_q + 1.0 if dr == 0 else c - idx_q
            k_exp = c - 1.0 - idx_k if dr == 0 else idx_k
            qdec_ref[dr, hp] = jnp.exp(jnp.where(lane_q < RET_V_DIM, lg0, lg1) * q_exp)
            kdec_ref[dr, hp] = jnp.exp(jnp.where(low_k, lg0, lg1) * k_exp)
            cdec_ref[dr, hp] = jnp.where(blk0, jnp.exp(lg0 * c), jnp.where(blk1, jnp.exp(lg1 * c), 0.0))

    def chunk(dr, n):
        r0 = pl.multiple_of(n * c, c)
        qc = q_ref[0, pl.ds(r0, c), :]
        kc = k_ref[0, pl.ds(r0, c), :]
        vc = v_ref[0, pl.ds(r0, c), :]
        ys = []
        for hp in range(n_pairs):
            qp = qc[:, hp * LANES:(hp + 1) * LANES]
            kp = kc[:, hp * LANES:(hp + 1) * LANES]
            vp = vc[:, hp * pair_v:(hp + 1) * pair_v]
            zk = jnp.zeros_like(kp)
            zv = jnp.zeros_like(vp)
            k2 = jnp.concatenate([jnp.where(low_k16, kp, zk), jnp.where(low_k16, zk, kp)], axis=0)
            v2 = jnp.concatenate([jnp.where(low_v16, vp, zv), jnp.where(low_v16, zv, vp)], axis=0)
            s = _dot_nt(qp, k2) * jnp.concatenate([intra_ref[dr, 2 * hp], intra_ref[dr, 2 * hp + 1]], axis=1)
            y_intra = _dot(s.astype(BF16), v2)
            state = st_ref[hp]
            y_inter = _dot(qp, state.astype(BF16)) * qdec_ref[dr, hp]
            ys.append(y_intra + y_inter)
            kd = (kp.astype(F32) * kdec_ref[dr, hp]).astype(BF16)
            st_ref[hp] = cdec_ref[dr, hp] * state + diag * _dot_tn(kd, vp)
        return r0, ys

    st_ref[...] = jnp.zeros_like(st_ref)

    def fwd(n, carry):
        r0, ys = chunk(0, n)
        for hp in range(n_pairs):
            yf_ref[pl.ds(r0, c), hp * pair_v:(hp + 1) * pair_v] = ys[hp]
        return carry

    lax.fori_loop(0, nc, fwd, 0, unroll=RET_UNROLL)
    st_ref[...] = jnp.zeros_like(st_ref)

    def bwd(i, carry):
        r0, ys = chunk(1, nc - 1 - i)
        for hp in range(n_pairs):
            y = ys[hp] + yf_ref[pl.ds(r0, c), hp * pair_v:(hp + 1) * pair_v]
            for half in range(2):
                cols = slice(hp * pair_v + half * RET_V_DIM, hp * pair_v + (half + 1) * RET_V_DIM)
                yh = y[:, half * RET_V_DIM:(half + 1) * RET_V_DIM]
                mu = jnp.mean(yh, axis=-1, keepdims=True)
                dev = yh - mu
                var = jnp.mean(dev * dev, axis=-1, keepdims=True)
                o_ref[0, pl.ds(r0, c), cols] = (dev * lax.rsqrt(var + EPS) * gn_ref[:, cols]).astype(o_ref.dtype)
        return carry

    lax.fori_loop(0, nc, bwd, 0, unroll=RET_UNROLL)


def _ret_scan(log_decay, q, k, v, ret_norm, batch, seq):
    c = RET_CHUNK
    n_pairs = RET_HEADS // 2
    q3 = q.reshape(batch, seq, RET_QK_W)
    k3 = k.reshape(batch, seq, RET_QK_W)
    v3 = v.reshape(batch, seq, RET_V_W)
    out = pl.pallas_call(
        _ret_scan_kernel,
        grid=(batch,),
        in_specs=[
            pl.BlockSpec(memory_space=pltpu.SMEM),
            pl.BlockSpec((1, seq, RET_QK_W), lambda b: (b, 0, 0)),
            pl.BlockSpec((1, seq, RET_QK_W), lambda b: (b, 0, 0)),
            pl.BlockSpec((1, seq, RET_V_W), lambda b: (b, 0, 0)),
            _resident((1, RET_V_W)),
        ],
        out_specs=pl.BlockSpec((1, seq, RET_V_W), lambda b: (b, 0, 0)),
        out_shape=jax.ShapeDtypeStruct((batch, seq, RET_V_W), BF16),
        scratch_shapes=[
            pltpu.VMEM((seq, RET_V_W), F32),
            pltpu.VMEM((n_pairs, LANES, 2 * RET_V_DIM), F32),
            pltpu.VMEM((2, RET_HEADS, c, c), F32),
            pltpu.VMEM((2, n_pairs, c, 2 * RET_V_DIM), F32),
            pltpu.VMEM((2, n_pairs, c, LANES), F32),
            pltpu.VMEM((2, n_pairs, LANES, 2 * RET_V_DIM), F32),
        ],
        compiler_params=_params(1),
        name="ret_scan",
    )(log_decay, q3, k3, v3, ret_norm)
    return out.reshape(batch * seq, RET_V_W)


def _top2_route(h, w_hi_lo):
    h_hi = h.astype(BF16)
    h_lo = (h - h_hi.astype(F32)).astype(BF16)
    hh = _dot(h_hi, w_hi_lo)
    logits = hh[:, :LANES] + (hh[:, LANES:] + _dot(h_lo, w_hi_lo[:, :LANES]))
    lane = lax.broadcasted_iota(jnp.int32, logits.shape, 1).astype(F32)
    lg = jnp.where(lane < N_EXPERTS, logits, -jnp.inf)
    m1 = jnp.max(lg, axis=-1, keepdims=True)
    i1 = jnp.min(jnp.where(lg == m1, lane, float(LANES)), axis=-1, keepdims=True)
    lg2 = jnp.where(lane == i1, -jnp.inf, lg)
    m2 = jnp.max(lg2, axis=-1, keepdims=True)
    i2 = jnp.min(jnp.where(lg2 == m2, lane, float(LANES)), axis=-1, keepdims=True)
    e = jnp.exp(m2 - m1)
    w1 = 1.0 / (1.0 + e)
    w2 = e / (1.0 + e)
    return jnp.where(lane == 0, i1, jnp.where(lane == 1, i2, jnp.where(lane == 2, w1, jnp.where(lane == 3, w2, 0.0))))


def _merge_kernel(x_ref, mod_ref, g_ref, wg_ref, ymla_ref, yret_ref,
                  o0_ref, l0_ref, o1_ref, l1_ref, o2_ref, l2_ref,
                  wbm_ref, wbd_ref, wbr_ref, wout_ref, gf_ref, wr_ref, out_ref, *rest, route):
    nat_o, nat_l = rest[-2:]
    x = x_ref[...]
    mod = mod_ref[0]
    h = _norm_mod(x, g_ref[...], mod[1:2], mod[0:1]).astype(BF16)
    gates = _dot(h, wg_ref[0])
    d_model = x.shape[1]
    tm = x.shape[0]

    o_nat = [o0_ref[0, 0].astype(F32)]
    l_nat = [l0_ref[0, 0]]
    for g, (o_ref, l_ref) in enumerate(((o1_ref, l1_ref), (o2_ref, l2_ref))):
        dil = DIL_PAIRS[g + 1][1]
        n = tm // dil
        n_lane_blocks = DIL_GROUP_W // LANES
        for r in range(dil):
            o_r = o_ref[0, r].astype(F32)
            l_r = l_ref[0, r]
            for j in range(n_lane_blocks):
                nat_o[g, j, pl.ds(r, n, stride=dil), :] = o_r[:, j * LANES:(j + 1) * LANES]
                nat_l[g, j, pl.ds(r, n, stride=dil), :] = l_r[:, j * LANES:(j + 1) * LANES]
        o_nat.append(jnp.concatenate([nat_o[g, j] for j in range(n_lane_blocks)], axis=1))
        l_nat.append(jnp.concatenate([nat_l[g, j] for j in range(n_lane_blocks)], axis=1))
    m = jnp.maximum(jnp.maximum(l_nat[0], l_nat[1]), l_nat[2])
    ws = [jnp.exp(l - m) for l in l_nat]
    y_dil = (ws[0] * o_nat[0] + ws[1] * o_nat[1] + ws[2] * o_nat[2]) / (ws[0] + ws[1] + ws[2])

    y_ret = (_silu(gates[:, :RET_V_W]) * yret_ref[...].astype(F32)).astype(BF16)
    ga = gates[:, RET_V_W:RET_V_W + d_model]
    gb = gates[:, RET_V_W + d_model:RET_V_W + 2 * d_model]
    gc = gates[:, RET_V_W + 2 * d_model:]
    merged = (jax.nn.sigmoid(ga) * _dot(ymla_ref[...], wbm_ref[0])
              + jax.nn.sigmoid(gb) * _dot(y_dil.astype(BF16), wbd_ref[0])
              + jax.nn.sigmoid(gc) * _dot(y_ret, wbr_ref[0]))
    x_new = x + mod[2:3] * _dot(merged.astype(BF16), wout_ref[0])
    out_ref[...] = x_new
    if route:
        h_ref, route_ref = rest[:2]
        h2 = _norm_mod(x_new, gf_ref[...], mod[4:5], mod[3:4])
        h_ref[...] = h2
        route_ref[...] = _top2_route(h2, wr_ref[...])


def _merge_out(x, mod, gain, wg, y_mla, y_ret, dil_outs, wbm, wbd, wbr, wout, gain_ffn, w_router,
               layer, batch, seq, route):
    t, d_model = x.shape
    tm = MERGE_TILE
    per_b = seq // tm
    row = lambda i: (i, 0)
    dil_specs, dil_args = [], []
    for (_, dil), (o, lse) in zip(DIL_PAIRS, dil_outs):
        spec = pl.BlockSpec((1, dil, tm // dil, DIL_GROUP_W), lambda i: (i // per_b, 0, i % per_b, 0))
        dil_specs += [spec, spec]
        dil_args += [o, lse]
    out_specs = [pl.BlockSpec((tm, d_model), row)]
    out_shape = [jax.ShapeDtypeStruct((t, d_model), F32)]
    if route:
        out_specs += [pl.BlockSpec((tm, d_model), row), pl.BlockSpec((tm, LANES), row)]
        out_shape += [jax.ShapeDtypeStruct((t, d_model), F32), jax.ShapeDtypeStruct((t, LANES), F32)]
    wr32 = jnp.zeros((d_model, LANES), F32).at[:, :N_EXPERTS].set(w_router)
    wr_hi = wr32.astype(BF16)
    wr = jnp.concatenate([wr_hi, (wr32 - wr_hi.astype(F32)).astype(BF16)], axis=1)
    return pl.pallas_call(
        functools.partial(_merge_kernel, route=route),
        grid=(t // tm,),
        in_specs=[
            pl.BlockSpec((tm, d_model), row),
            pl.BlockSpec((1, ADA_CHUNKS, d_model), lambda i: (i // per_b, 0, 0)),
            _resident((1, d_model)),
            _layer_resident(wg, layer),
            pl.BlockSpec((tm, y_mla.shape[1]), row),
            pl.BlockSpec((tm, RET_V_W), row),
            *dil_specs,
            _layer_resident(wbm, layer), _layer_resident(wbd, layer), _layer_resident(wbr, layer),
            _layer_resident(wout, layer),
            _resident((1, d_model)),
            _resident(wr.shape),
        ],
        out_specs=out_specs,
        out_shape=out_shape,
        scratch_shapes=[pltpu.VMEM((2, DIL_GROUP_W // LANES, tm, LANES), F32),
                        pltpu.VMEM((2, DIL_GROUP_W // LANES, tm, LANES), F32)],
        compiler_params=_params(1),
        name="merge_out",
    )(x, mod, gain, wg, y_mla, y_ret, *dil_args, wbm, wbd, wbr, wout, gain_ffn, wr)


def _residual_out(y, fg_ref, final):
    return _rms(y, fg_ref[...]) if final else y


def _ffn_kernel(x_ref, mod_ref, g_ref, fg_ref, w1_ref, w3_ref, w2_ref, out_ref, *, n_chunks, final):
    x = x_ref[...]
    mod = mod_ref[0]
    h = _norm_mod(x, g_ref[...], mod[4:5], mod[3:4]).astype(BF16)
    d_ff = w1_ref.shape[2]
    fc = d_ff // n_chunks
    acc = jnp.zeros(x.shape, F32)
    for j in range(n_chunks):
        a = _dot(h, w1_ref[0, :, j * fc:(j + 1) * fc])
        b = _dot(h, w3_ref[0, :, j * fc:(j + 1) * fc])
        acc = acc + _dot((_silu(a) * b).astype(BF16), w2_ref[0, j * fc:(j + 1) * fc, :])
    out_ref[...] = _residual_out(x + mod[5:6] * acc, fg_ref, final)


def _ffn_dense(x, mod, gain, final_gain, w1, w3, w2, layer, seq, final):
    t, d_model = x.shape
    tm = ROW_TILE
    per_b = seq // tm
    row = lambda i: (i, 0)
    return pl.pallas_call(
        functools.partial(_ffn_kernel, n_chunks=2, final=final),
        grid=(t // tm,),
        in_specs=[
            pl.BlockSpec((tm, d_model), row),
            pl.BlockSpec((1, ADA_CHUNKS, d_model), lambda i: (i // per_b, 0, 0)),
            _resident((1, d_model)),
            _resident((1, d_model)),
            _layer_resident(w1, layer), _layer_resident(w3, layer), _layer_resident(w2, layer),
        ],
        out_specs=pl.BlockSpec((tm, d_model), row),
        out_shape=jax.ShapeDtypeStruct((t, d_model), F32),
        compiler_params=_params(1),
        name="ffn_dense",
    )(x, mod, gain, final_gain, w1, w3, w2)


def _row_copy(src_hbm, dst, src_row, dst_row, sem):
    return pltpu.make_async_copy(src_hbm.at[pl.ds(src_row, 1), :], dst.at[pl.ds(dst_row, 1), :], sem)


def _expert_kernel(be_ref, nact_ref, tok_ref, tok_next_ref, h_hbm, w1_ref, w3_ref, w2_ref, y_ref,
                   xbuf, sems, *, n_chunks):
    i = pl.program_id(0)
    nact = nact_ref[0]
    bm = y_ref.shape[0]
    slot = i % 2

    def wait_block(s):
        pltpu.make_async_copy(h_hbm.at[pl.ds(0, bm), :], xbuf.at[s], sems.at[s]).wait()

    @pl.when(i == 0)
    def _():
        def issue(r, carry):
            _row_copy(h_hbm, xbuf.at[0], tok_ref[0, 0, r], r, sems.at[0]).start()
            return carry

        lax.fori_loop(0, bm, issue, 0, unroll=8)

    @pl.when(i >= nact)
    def _():
        y_ref[...] = jnp.zeros_like(y_ref)

    @pl.when(i < nact)
    def _():
        wait_block(slot)
        xb = xbuf[slot].astype(BF16)
        for r in range(bm):
            _row_copy(h_hbm, xbuf.at[1 - slot], tok_next_ref[0, 0, r], r, sems.at[1 - slot]).start()
        d_ff = w1_ref.shape[2]
        fc = d_ff // n_chunks
        acc = jnp.zeros(y_ref.shape, F32)
        for j in range(n_chunks):
            a = _dot(xb, w1_ref[0, :, j * fc:(j + 1) * fc])
            b = _dot(xb, w3_ref[0, :, j * fc:(j + 1) * fc])
            acc = acc + _dot((_silu(a) * b).astype(BF16), w2_ref[0, j * fc:(j + 1) * fc, :])
        y_ref[...] = acc

    @pl.when(i == nact - 1)
    def _():
        wait_block(1 - slot)


def _moe_expert(block_expert, nact, row_tok, h, w1, w3, w2, expert_base):
    nb, _, bm = row_tok.shape
    d_model = h.shape[1]
    d_ff = w1.shape[2]
    wsel = lambda i, be, na: (expert_base + be[jnp.minimum(i, na[0] - 1)], 0, 0)
    return pl.pallas_call(
        functools.partial(_expert_kernel, n_chunks=2),
        grid_spec=pltpu.PrefetchScalarGridSpec(
            num_scalar_prefetch=2,
            grid=(nb,),
            in_specs=[
                pl.BlockSpec((1, 1, bm), lambda i, be, na: (i, 0, 0), memory_space=pltpu.SMEM),
                pl.BlockSpec((1, 1, bm), lambda i, be, na: (jnp.minimum(i + 1, nb - 1), 0, 0),
                             memory_space=pltpu.SMEM),
                pl.BlockSpec(memory_space=pl.ANY),
                pl.BlockSpec((1, d_model, d_ff), wsel, pipeline_mode=pl.Buffered(1)),
                pl.BlockSpec((1, d_model, d_ff), wsel, pipeline_mode=pl.Buffered(1)),
                pl.BlockSpec((1, d_ff, d_model), wsel, pipeline_mode=pl.Buffered(1)),
            ],
            out_specs=pl.BlockSpec((bm, d_model), lambda i, be, na: (i, 0)),
            scratch_shapes=[pltpu.VMEM((2, bm, d_model), F32), pltpu.SemaphoreType.DMA((2,))],
        ),
        out_shape=jax.ShapeDtypeStruct((nb * bm, d_model), F32),
        compiler_params=_params(1),
        name="moe_expert",
    )(block_expert, nact, row_tok, row_tok, h, w1, w3, w2)


def _combine_kernel(dest_ref, dest_next_ref, x_ref, mod_ref, route_ref, fg_ref, yb_hbm, out_ref, buf, sems,
                    *, final):
    tm = x_ref.shape[0]
    n_rows = TOP_K * tm
    i = pl.program_id(0)
    slot = i % 2

    def issue(rows_ref, s):
        def body(r, carry):
            _row_copy(yb_hbm, buf.at[s], rows_ref[0, 0, r], r, sems.at[s]).start()
            return carry

        lax.fori_loop(0, n_rows, body, 0, unroll=8)

    @pl.when(i == 0)
    def _():
        issue(dest_ref, 0)

    @pl.when(i + 1 < pl.num_programs(0))
    def _():
        for r in range(n_rows):
            _row_copy(yb_hbm, buf.at[1 - slot], dest_next_ref[0, 0, r], r, sems.at[1 - slot]).start()

    pltpu.make_async_copy(yb_hbm.at[pl.ds(0, n_rows), :], buf.at[slot], sems.at[slot]).wait()
    route = route_ref[...]
    y = route[:, 2:3] * buf[slot, :tm, :] + route[:, 3:4] * buf[slot, tm:, :]
    out_ref[...] = _residual_out(x_ref[...] + mod_ref[0][5:6] * y, fg_ref, final)


def _moe_combine(dest, x, mod, route, final_gain, yb, seq, final):
    t, d_model = x.shape
    tm = COMBINE_TILE
    per_b = seq // tm
    row = lambda i: (i, 0)
    return pl.pallas_call(
        functools.partial(_combine_kernel, final=final),
        grid=(t // tm,),
        in_specs=[
            pl.BlockSpec((1, 1, TOP_K * tm), lambda i: (i, 0, 0), memory_space=pltpu.SMEM),
            pl.BlockSpec((1, 1, TOP_K * tm), lambda i: (jnp.minimum(i + 1, t // tm - 1), 0, 0),
                         memory_space=pltpu.SMEM),
            pl.BlockSpec((tm, d_model), row),
            pl.BlockSpec((1, ADA_CHUNKS, d_model), lambda i: (i // per_b, 0, 0)),
            pl.BlockSpec((tm, LANES), row),
            _resident((1, d_model)),
            pl.BlockSpec(memory_space=pl.ANY),
        ],
        out_specs=pl.BlockSpec((tm, d_model), row),
        out_shape=jax.ShapeDtypeStruct((t, d_model), F32),
        scratch_shapes=[pltpu.VMEM((2, TOP_K * tm, d_model), F32), pltpu.SemaphoreType.DMA((2,))],
        compiler_params=_params(1),
        name="moe_combine",
    )(dest, dest, x, mod, route, final_gain, yb)


def _moe(x, h, route, mod, final_gain, w1, w3, w2, expert_base, seq, final):
    t, d_model = x.shape
    bm = MOE_BLOCK
    e_flat = route[:, :TOP_K].astype(jnp.int32).reshape(-1)
    n_assign = e_flat.shape[0]
    onehot = (e_flat[:, None] == jnp.arange(N_EXPERTS, dtype=jnp.int32)[None, :]).astype(jnp.int32)
    csum = jnp.cumsum(onehot, axis=0)
    rank = jnp.take_along_axis(csum, e_flat[:, None], axis=1)[:, 0] - 1
    counts = csum[-1]
    padded = (counts + bm - 1) // bm * bm
    pends = jnp.cumsum(padded)
    dest = (pends - padded)[e_flat] + rank
    nb = n_assign // bm + N_EXPERTS
    row_tok = jnp.zeros((nb * bm,), jnp.int32).at[dest].set(jnp.arange(n_assign, dtype=jnp.int32) // TOP_K)
    block_start = jnp.arange(nb, dtype=jnp.int32) * bm
    block_expert = jnp.minimum(
        jnp.sum((pends[None, :] <= block_start[:, None]).astype(jnp.int32), axis=1), N_EXPERTS - 1)
    nact = (pends[-1:] // bm).astype(jnp.int32)

    yb = _moe_expert(block_expert, nact, row_tok.reshape(nb, 1, bm), h, w1, w3, w2, expert_base)
    tmc = COMBINE_TILE
    dest_tiles = dest.reshape(t // tmc, tmc, TOP_K).transpose(0, 2, 1).reshape(t // tmc, 1, TOP_K * tmc)
    return _moe_combine(dest_tiles.astype(jnp.int32), x, mod, route, final_gain, yb, seq, final)


def _rope_tables(positions):
    inv_freq = ROPE_THETA ** (-jnp.arange(0, ROPE_DIM, 2, dtype=F32) / ROPE_DIM)
    ang = positions.astype(F32)[..., None] * inv_freq
    cos, sin = jnp.cos(ang), jnp.sin(ang)
    reps = LANES // ROPE_DIM
    cos_t = jnp.tile(jnp.concatenate([cos, cos], axis=-1), (1, 1, reps))
    sin_t = jnp.tile(jnp.concatenate([-sin, sin], axis=-1), (1, 1, reps))
    return cos_t, sin_t


def _split_w_in(w_in):
    d_model = w_in.shape[1]
    sizes = (MLA_Q_RANK, MLA_KV_RANK, MLA_ROPE,
             3 * DIL_GROUP_W, 3 * DIL_GROUP_W, 3 * DIL_GROUP_W,
             RET_QK_W, RET_QK_W, RET_V_W, RET_V_W, d_model, d_model, d_model)
    w_in = w_in.astype(BF16)
    cols, start = [], 0
    for n in sizes:
        cols.append(w_in[..., start:start + n])
        start += n
    cq, ckv, kr, dq, dk, dv, rq, rk, rv, rg, ga, gb, gc = cols
    w_mla = jnp.concatenate([cq, ckv, kr, kr], axis=-1)
    gw = DIL_GROUP_W
    w_dil = jnp.concatenate(
        [m[..., g * gw:(g + 1) * gw] for g in range(len(DIL_PAIRS)) for m in (dq, dk, dv)], axis=-1)
    w_ret = jnp.concatenate([rq, rk, rv], axis=-1)
    w_gate = jnp.concatenate([rg, ga, gb, gc], axis=-1)
    return w_mla, w_dil, w_ret, w_gate


def _split_w_uq(w_uq):
    depth = w_uq.shape[0]
    w = w_uq.astype(BF16).reshape(depth, MLA_Q_RANK, MLA_HEADS, MLA_QK)
    nope = w[..., :MLA_NOPE].reshape(depth, MLA_Q_RANK, MLA_HEADS * MLA_NOPE)
    rope = w[..., MLA_NOPE:].reshape(depth, MLA_Q_RANK, MLA_HEADS * MLA_ROPE)
    return jnp.concatenate([nope, rope], axis=-1)


def kernel(x, c, positions, ada_w, ada_b, norm_mix, norm_ffn, w_in, mla_q_norm, mla_w_uq, mla_kv_norm,
           mla_w_ukv, ret_log_decay, ret_norm, w_br_mla, w_br_dil, w_br_ret, w_out, ffn_w1, ffn_w3, ffn_w2,
           moe_router, moe_w1, moe_w3, moe_w2, final_norm):
    batch, seq, d_model = x.shape
    depth = ada_w.shape[0]
    cos_t, sin_t = _rope_tables(positions)
    mod_all = _ada(c, ada_w, ada_b)
    assert depth >= 1
    fgain = final_norm.reshape(1, d_model)
    xt = x.reshape(batch * seq, d_model)
    moe_w1_b = moe_w1.astype(BF16).reshape((-1,) + moe_w1.shape[2:])
    moe_w3_b = moe_w3.astype(BF16).reshape((-1,) + moe_w3.shape[2:])
    moe_w2_b = moe_w2.astype(BF16).reshape((-1,) + moe_w2.shape[2:])
    w_mla, w_dil, w_ret, w_gate = _split_w_in(w_in)
    w_uq = _split_w_uq(mla_w_uq)
    w_ukv = mla_w_ukv.astype(BF16)
    wbm, wbd, wbr, wout = (w.astype(BF16) for w in (w_br_mla, w_br_dil, w_br_ret, w_out))
    ffn_w1_b, ffn_w3_b, ffn_w2_b = (w.astype(BF16) for w in (ffn_w1, ffn_w3, ffn_w2))
    for layer in range(depth):
        mod = mod_all[layer]
        gmix = norm_mix[layer].reshape(1, d_model)

        q, k, v = _mla_prep(xt, mod, gmix, w_mla, mla_q_norm[layer].reshape(1, -1),
                            mla_kv_norm[layer].reshape(1, -1), w_uq, w_ukv, cos_t, sin_t, layer, batch, seq)
        y_mla = _mla_attn(q, k, v)

        dil_qkv = _dil_prep(xt, mod, gmix, w_dil, cos_t, sin_t, layer, batch, seq)
        dil_outs = [_dil_attn(*dil_qkv[3 * g:3 * g + 3], window, dil)
                    for g, (window, dil) in enumerate(DIL_PAIRS)]

        rq, rk, rv = _ret_prep(xt, mod, gmix, w_ret, cos_t, sin_t, layer, seq)
        y_ret = _ret_scan(ret_log_decay[layer].astype(F32), rq, rk, rv,
                          ret_norm[layer].reshape(1, -1).astype(F32), batch, seq)

        gffn = norm_ffn[layer].reshape(1, d_model)
        i = layer // 2
        is_moe = layer % 2 == 1
        merged = _merge_out(xt, mod, gmix, w_gate, y_mla, y_ret, dil_outs, wbm, wbd, wbr, wout, gffn,
                            moe_router[i] if is_moe else jnp.zeros((d_model, N_EXPERTS), F32),
                            layer, batch, seq, is_moe)
        last = layer == depth - 1
        if is_moe:
            xt, h, route = merged
            xt = _moe(xt, h, route, mod, fgain, moe_w1_b, moe_w3_b, moe_w2_b, i * N_EXPERTS, seq, last)
        else:
            xt = _ffn_dense(merged[0], mod, gffn, fgain, ffn_w1_b, ffn_w3_b, ffn_w2_b, i, seq, last)
    return xt.reshape(batch, seq, d_model)
```

```python
import functools

import jax
import jax.numpy as jnp
from jax import lax
from jax.experimental import pallas as pl
from jax.experimental.pallas import tpu as pltpu

F32 = jnp.float32
BF16 = jnp.bfloat16

EPS = 1e-6
NEG_INF = -1e30
ROPE_THETA = 10000.0
ROPE_DIM = 64
ADA_CHUNKS = 6

MLA_HEADS = 8
MLA_Q_RANK = 768
MLA_KV_RANK = 512
MLA_NOPE = 128
MLA_ROPE = ROPE_DIM
MLA_V = 128
MLA_QK = MLA_NOPE + MLA_ROPE

DIL_PAIRS = ((128, 1), (512, 4), (2048, 16))
DIL_HEADS = 8
DIL_HEAD_DIM = ROPE_DIM
DIL_GROUP_W = DIL_HEADS * DIL_HEAD_DIM
DIL_QBLOCK = 128

RET_HEADS = 8
RET_QK_DIM = ROPE_DIM
RET_V_DIM = 2 * RET_QK_DIM
RET_CHUNK = 256
RET_UNROLL = 4
RET_QK_W = RET_HEADS * RET_QK_DIM
RET_V_W = RET_HEADS * RET_V_DIM

N_EXPERTS = 8
TOP_K = 2
MOE_BLOCK = 256

LANES = 128
V7X_VMEM_LIMIT_BYTES = 56 * 1024 * 1024

ROW_TILE = 512
MERGE_TILE = 512
COMBINE_TILE = 256
MLA_Q_TILE = 512
MLA_HEADS_PER_STEP = 4
MLA_KEY_CHUNK = 1024


def _params(n_grid_dims):
    return pltpu.CompilerParams(
        dimension_semantics=("arbitrary",) * n_grid_dims,
        vmem_limit_bytes=V7X_VMEM_LIMIT_BYTES,
    )


def _resident(shape):
    zeros = (0,) * len(shape)
    return pl.BlockSpec(shape, lambda *_: zeros, pipeline_mode=pl.Buffered(1))


def _layer_resident(stacked, layer):
    idx = (layer,) + (0,) * (stacked.ndim - 1)
    return pl.BlockSpec((1,) + stacked.shape[1:], lambda *_: idx, pipeline_mode=pl.Buffered(1))


def _silu(v):
    return v * jax.nn.sigmoid(v)


def _norm_mod(x, gain, scale, shift):
    ms = jnp.mean(x * x, axis=-1, keepdims=True)
    y = x * lax.rsqrt(ms + EPS) * gain
    return y * (1.0 + scale) + shift


def _rms(v, gain):
    ms = jnp.mean(v * v, axis=-1, keepdims=True)
    return v * lax.rsqrt(ms + EPS) * gain


def _rope(t, cos_t, sin_t):
    lane = lax.broadcasted_iota(jnp.int32, (t.shape[0], LANES), 1)
    first_half = (lane % ROPE_DIM) < (ROPE_DIM // 2)
    out = []
    for j in range(t.shape[1] // LANES):
        c = t[:, j * LANES:(j + 1) * LANES]
        rot = jnp.where(first_half,
                        pltpu.roll(c, LANES - ROPE_DIM // 2, 1),
                        pltpu.roll(c, ROPE_DIM // 2, 1))
        out.append(c * cos_t + rot * sin_t)
    return out[0] if len(out) == 1 else jnp.concatenate(out, axis=1)


def _dot(a, b):
    return jnp.dot(a, b, preferred_element_type=F32)


def _dot_nt(a, b):
    return lax.dot_general(a, b, (((1,), (1,)), ((), ())), preferred_element_type=F32)


def _dot_tn(a, b):
    return lax.dot_general(a, b, (((0,), (0,)), ((), ())), preferred_element_type=F32)


def _ada_kernel(c_ref, w_ref, b_ref, o_ref):
    o_ref[0] = jnp.dot(_silu(c_ref[...]), w_ref[0], preferred_element_type=F32,
                       precision=lax.Precision.HIGHEST) + b_ref[0]


def _ada(c, ada_w, ada_b):
    depth, d_model, n = ada_w.shape
    batch = c.shape[0]
    tn = n // 4
    out = pl.pallas_call(
        _ada_kernel,
        grid=(depth, n // tn),
        in_specs=[
            pl.BlockSpec((batch, d_model), lambda l, j: (0, 0)),
            pl.BlockSpec((1, d_model, tn), lambda l, j: (l, 0, j)),
            pl.BlockSpec((1, 1, tn), lambda l, j: (l, 0, j)),
        ],
        out_specs=pl.BlockSpec((1, batch, tn), lambda l, j: (l, 0, j)),
        out_shape=jax.ShapeDtypeStruct((depth, batch, n), F32),
        compiler_params=_params(2),
        name="ada",
    )(c, ada_w, ada_b.reshape(depth, 1, n))
    return out.reshape(depth, batch, ADA_CHUNKS, d_model)


def _mla_prep_kernel(x_ref, mod_ref, g_ref, wa_ref, qn_ref, kvn_ref, wuq_ref, wukv_ref, cos_ref, sin_ref,
                     q_ref, k_ref, v_ref):
    mod = mod_ref[0]
    h = _norm_mod(x_ref[...], g_ref[...], mod[1:2], mod[0:1]).astype(BF16)
    p = _dot(h, wa_ref[0])
    cq = _rms(p[:, :MLA_Q_RANK], qn_ref[...]).astype(BF16)
    ckv = _rms(p[:, MLA_Q_RANK:MLA_Q_RANK + MLA_KV_RANK], kvn_ref[...]).astype(BF16)
    cos_t = cos_ref[0]
    sin_t = sin_ref[0]
    kr = _rope(p[:, MLA_Q_RANK + MLA_KV_RANK:], cos_t, sin_t)
    q = _dot(cq, wuq_ref[0])
    kv = _dot(ckv, wukv_ref[0])
    nope_w = MLA_HEADS * MLA_NOPE
    qr = _rope(q[:, nope_w:], cos_t, sin_t)
    scale = MLA_QK ** -0.5
    kr_b = kr[:, :MLA_ROPE].astype(BF16)
    for hd in range(MLA_HEADS):
        q_ref[0, hd, :, :MLA_NOPE] = (q[:, hd * MLA_NOPE:(hd + 1) * MLA_NOPE] * scale).astype(BF16)
        q_ref[0, hd, :, MLA_NOPE:] = (qr[:, hd * MLA_ROPE:(hd + 1) * MLA_ROPE] * scale).astype(BF16)
        base = hd * (MLA_NOPE + MLA_V)
        k_ref[0, hd, :, :MLA_NOPE] = kv[:, base:base + MLA_NOPE].astype(BF16)
        k_ref[0, hd, :, MLA_NOPE:] = kr_b
        v_ref[0, hd] = kv[:, base + MLA_NOPE:base + MLA_NOPE + MLA_V].astype(BF16)


def _mla_prep(x, mod, gain, wa, qn, kvn, wuq, wukv, cos_t, sin_t, layer, batch, seq):
    t, d_model = x.shape
    tm = ROW_TILE
    per_b = seq // tm
    row = lambda i: (i, 0)
    hs = lambda i: (i // per_b, 0, i % per_b, 0)
    return pl.pallas_call(
        _mla_prep_kernel,
        grid=(t // tm,),
        in_specs=[
            pl.BlockSpec((tm, d_model), row),
            pl.BlockSpec((1, ADA_CHUNKS, d_model), lambda i: (i // per_b, 0, 0)),
            _resident((1, d_model)),
            _layer_resident(wa, layer),
            _resident((1, MLA_Q_RANK)),
            _resident((1, MLA_KV_RANK)),
            _layer_resident(wuq, layer),
            _layer_resident(wukv, layer),
            pl.BlockSpec((1, tm, LANES), lambda i: (i // per_b, i % per_b, 0)),
            pl.BlockSpec((1, tm, LANES), lambda i: (i // per_b, i % per_b, 0)),
        ],
        out_specs=[
            pl.BlockSpec((1, MLA_HEADS, tm, MLA_QK), hs),
            pl.BlockSpec((1, MLA_HEADS, tm, MLA_QK), hs),
            pl.BlockSpec((1, MLA_HEADS, tm, MLA_V), hs),
        ],
        out_shape=[
            jax.ShapeDtypeStruct((batch, MLA_HEADS, seq, MLA_QK), BF16),
            jax.ShapeDtypeStruct((batch, MLA_HEADS, seq, MLA_QK), BF16),
            jax.ShapeDtypeStruct((batch, MLA_HEADS, seq, MLA_V), BF16),
        ],
        compiler_params=_params(1),
        name="mla_prep",
    )(x, mod, gain, wa, qn, kvn, wuq, wukv, cos_t, sin_t)


def _mla_attn_kernel(q_ref, k_ref, v_ref, o_ref, *, heads_per_step, key_chunk):
    seq = k_ref.shape[2]
    ones = jnp.ones((key_chunk, LANES), BF16)
    for hd in range(heads_per_step):
        q = q_ref[0, hd]
        m = acc = None
        for c in range(seq // key_chunk):
            rows = slice(c * key_chunk, (c + 1) * key_chunk)
            s = _dot_nt(q, k_ref[0, hd, rows, :])
            v_ext = jnp.concatenate([v_ref[0, hd, rows, :], ones], axis=1)
            m_c = jnp.max(s, axis=-1, keepdims=True)
            if c == 0:
                m = m_c
                acc = _dot(jnp.exp(s - m).astype(BF16), v_ext)
            else:
                m_new = jnp.maximum(m, m_c)
                acc = jnp.exp(m - m_new) * acc + _dot(jnp.exp(s - m_new).astype(BF16), v_ext)
                m = m_new
        o_ref[:, hd * MLA_V:(hd + 1) * MLA_V] = (acc[:, :MLA_V] / acc[:, MLA_V:MLA_V + 1]).astype(o_ref.dtype)


def _mla_attn(q, k, v):
    batch, heads, seq, _ = q.shape
    tq = MLA_Q_TILE
    nq = seq // tq
    hps = MLA_HEADS_PER_STEP
    return pl.pallas_call(
        functools.partial(_mla_attn_kernel, heads_per_step=hps, key_chunk=MLA_KEY_CHUNK),
        grid=(batch, heads // hps, nq),
        in_specs=[
            pl.BlockSpec((1, hps, tq, MLA_QK), lambda b, h, i: (b, h, i, 0)),
            pl.BlockSpec((1, hps, seq, MLA_QK), lambda b, h, i: (b, h, 0, 0)),
            pl.BlockSpec((1, hps, seq, MLA_V), lambda b, h, i: (b, h, 0, 0)),
        ],
        out_specs=pl.BlockSpec((tq, hps * MLA_V), lambda b, h, i: (b * nq + i, h)),
        out_shape=jax.ShapeDtypeStruct((batch * seq, heads * MLA_V), BF16),
        compiler_params=_params(3),
        name="mla_attn",
    )(q, k, v)


def _dil_prep_kernel(x_ref, mod_ref, g_ref, w_ref, cos_ref, sin_ref, *refs):
    out_refs = refs[:3 * len(DIL_PAIRS)]
    h_scr = refs[3 * len(DIL_PAIRS)]
    mod = mod_ref[0]
    h32 = _norm_mod(x_ref[...], g_ref[...], mod[1:2], mod[0:1])
    n_lane_blocks = h32.shape[1] // LANES
    for j in range(n_lane_blocks):
        h_scr[j] = h32[:, j * LANES:(j + 1) * LANES]
    tm = h32.shape[0]
    gw = DIL_GROUP_W
    for g, (_, dil) in enumerate(DIL_PAIRS):
        n = tm // dil
        if dil == 1:
            hp, cos_t, sin_t = h32, cos_ref[0], sin_ref[0]
        else:
            hp = jnp.concatenate(
                [jnp.concatenate([h_scr[j, pl.ds(r, n, stride=dil), :] for j in range(n_lane_blocks)], axis=1)
                 for r in range(dil)], axis=0)
            cos_t = jnp.concatenate([cos_ref[0, pl.ds(r, n, stride=dil), :] for r in range(dil)], axis=0)
            sin_t = jnp.concatenate([sin_ref[0, pl.ds(r, n, stride=dil), :] for r in range(dil)], axis=0)
        pg = _dot(hp.astype(BF16), w_ref[0, :, g * 3 * gw:(g + 1) * 3 * gw])
        qg = (_rope(pg[:, :gw], cos_t, sin_t) * (DIL_HEAD_DIM ** -0.5)).astype(BF16)
        kg = _rope(pg[:, gw:2 * gw], cos_t, sin_t).astype(BF16)
        vg = pg[:, 2 * gw:].astype(BF16)
        q_ref, k_ref, v_ref = out_refs[3 * g:3 * g + 3]
        for r in range(dil):
            q_ref[0, r] = qg[r * n:(r + 1) * n]
            k_ref[0, r] = kg[r * n:(r + 1) * n]
            v_ref[0, r] = vg[r * n:(r + 1) * n]


def _dil_prep(x, mod, gain, w, cos_t, sin_t, layer, batch, seq):
    t, d_model = x.shape
    tm = ROW_TILE
    per_b = seq // tm
    out_specs, out_shape = [], []
    for _, dil in DIL_PAIRS:
        for _ in range(3):
            out_specs.append(pl.BlockSpec((1, dil, tm // dil, DIL_GROUP_W),
                                          lambda i: (i // per_b, 0, i % per_b, 0)))
            out_shape.append(jax.ShapeDtypeStruct((batch, dil, seq // dil, DIL_GROUP_W), BF16))
    return pl.pallas_call(
        _dil_prep_kernel,
        grid=(t // tm,),
        in_specs=[
            pl.BlockSpec((tm, d_model), lambda i: (i, 0)),
            pl.BlockSpec((1, ADA_CHUNKS, d_model), lambda i: (i // per_b, 0, 0)),
            _resident((1, d_model)),
            _layer_resident(w, layer),
            pl.BlockSpec((1, tm, LANES), lambda i: (i // per_b, i % per_b, 0)),
            pl.BlockSpec((1, tm, LANES), lambda i: (i // per_b, i % per_b, 0)),
        ],
        out_specs=out_specs,
        out_shape=out_shape,
        scratch_shapes=[pltpu.VMEM((d_model // LANES, tm, LANES), F32)],
        compiler_params=_params(1),
        name="dil_prep",
    )(x, mod, gain, w, cos_t, sin_t)


def _dil_attn_kernel(q_ref, k_ref, v_ref, o_ref, lse_ref, *, dil, segment, band):
    qb_rows = DIL_QBLOCK
    kw = 2 * qb_rows
    short = segment < kw
    per_window = kw // segment if short else 1
    nb = 1 if short else segment // qb_rows
    low_k = lax.broadcasted_iota(jnp.int32, (kw, LANES), 1).astype(F32).astype(BF16) < DIL_HEAD_DIM
    low_q = lax.broadcasted_iota(jnp.int32, (qb_rows, LANES), 1) < DIL_HEAD_DIM
    row_i = lax.broadcasted_iota(jnp.int32, (qb_rows, kw), 0)
    col_i = lax.broadcasted_iota(jnp.int32, (qb_rows, kw), 1)
    zeros_k = jnp.zeros((kw, LANES), BF16)
    ones_k = jnp.ones((kw, LANES), BF16)
    denom_cols = jnp.concatenate([jnp.where(low_k, ones_k, zeros_k), jnp.where(low_k, zeros_k, ones_k)], axis=0)

    def tile(idx, carry):
        r = idx // nb
        if short:
            q0 = 0
            r0 = (r // per_window) * per_window
            qt = q_ref[0, r]
            kt = jnp.concatenate([k_ref[0, r0 + j] for j in range(per_window)], axis=0)
            vt = jnp.concatenate([v_ref[0, r0 + j] for j in range(per_window)], axis=0)
            jq = (r - r0) * segment + row_i
            jk = col_i
        else:
            q0 = pl.multiple_of((idx % nb) * qb_rows, qb_rows)
            ks = pl.multiple_of(jnp.clip(q0 - band, 0, segment - kw), band)
            qt = q_ref[0, r, pl.ds(q0, qb_rows), :]
            kt = k_ref[0, r, pl.ds(ks, kw), :]
            vt = v_ref[0, r, pl.ds(ks, kw), :]
            jq = q0 + row_i
            jk = ks + col_i
        valid = jnp.logical_and(jnp.abs(jq - jk) <= band, jq // segment == jk // segment)
        for hp in range(DIL_GROUP_W // LANES):
            cols = slice(hp * LANES, (hp + 1) * LANES)
            qp, kp, vp = qt[:, cols], kt[:, cols], vt[:, cols]
            k2 = jnp.concatenate([jnp.where(low_k, kp, zeros_k), jnp.where(low_k, zeros_k, kp)], axis=0)
            v2 = jnp.concatenate([jnp.where(low_k, vp, zeros_k), jnp.where(low_k, zeros_k, vp)], axis=0)
            s = _dot_nt(qp, k2)
            s0 = jnp.where(valid, s[:, :kw], NEG_INF)
            s1 = jnp.where(valid, s[:, kw:], NEG_INF)
            m0 = jnp.max(s0, axis=-1, keepdims=True)
            m1 = jnp.max(s1, axis=-1, keepdims=True)
            p = jnp.concatenate([jnp.exp(s0 - m0), jnp.exp(s1 - m1)], axis=1).astype(BF16)
            pv = _dot(p, jnp.concatenate([v2, denom_cols], axis=1))
            den = pv[:, LANES:]
            o_ref[0, r, pl.ds(q0, qb_rows), cols] = (pv[:, :LANES] / den).astype(o_ref.dtype)
            lse_ref[0, r, pl.ds(q0, qb_rows), cols] = jnp.where(low_q, m0, m1) + jnp.log(den)
        return carry

    lax.fori_loop(0, dil * nb, tile, 0, unroll=8)


def _dil_attn(q, k, v, window, dil):
    batch, _, seg, gw = q.shape
    band = window // (2 * dil)
    assert band * 2 == DIL_QBLOCK and seg % DIL_QBLOCK == 0
    assert seg >= 2 * DIL_QBLOCK or (seg == DIL_QBLOCK and dil % 2 == 0)
    spec = pl.BlockSpec((1, dil, seg, gw), lambda b: (b, 0, 0, 0))
    return pl.pallas_call(
        functools.partial(_dil_attn_kernel, dil=dil, segment=seg, band=band),
        grid=(batch,),
        in_specs=[spec, spec, spec],
        out_specs=[spec, spec],
        out_shape=[jax.ShapeDtypeStruct(q.shape, BF16), jax.ShapeDtypeStruct(q.shape, F32)],
        compiler_params=_params(1),
        name=f"dil_attn_d{dil}",
    )(q, k, v)


def _ret_prep_kernel(x_ref, mod_ref, g_ref, w_ref, cos_ref, sin_ref, q_ref, k_ref, v_ref):
    mod = mod_ref[0]
    h = _norm_mod(x_ref[...], g_ref[...], mod[1:2], mod[0:1]).astype(BF16)
    p = _dot(h, w_ref[0])
    cos_t, sin_t = cos_ref[0], sin_ref[0]
    q_ref[...] = _rope(p[:, :RET_QK_W], cos_t, sin_t).astype(BF16)
    k_ref[...] = (_rope(p[:, RET_QK_W:2 * RET_QK_W], cos_t, sin_t) * (RET_QK_DIM ** -0.5)).astype(BF16)
    v_ref[...] = p[:, 2 * RET_QK_W:].astype(BF16)


def _ret_prep(x, mod, gain, w, cos_t, sin_t, layer, seq):
    t, d_model = x.shape
    tm = ROW_TILE
    per_b = seq // tm
    row = lambda i: (i, 0)
    return pl.pallas_call(
        _ret_prep_kernel,
        grid=(t // tm,),
        in_specs=[
            pl.BlockSpec((tm, d_model), row),
            pl.BlockSpec((1, ADA_CHUNKS, d_model), lambda i: (i // per_b, 0, 0)),
            _resident((1, d_model)),
            _layer_resident(w, layer),
            pl.BlockSpec((1, tm, LANES), lambda i: (i // per_b, i % per_b, 0)),
            pl.BlockSpec((1, tm, LANES), lambda i: (i // per_b, i % per_b, 0)),
        ],
        out_specs=[pl.BlockSpec((tm, RET_QK_W), row), pl.BlockSpec((tm, RET_QK_W), row),
                   pl.BlockSpec((tm, RET_V_W), row)],
        out_shape=[jax.ShapeDtypeStruct((t, RET_QK_W), BF16), jax.ShapeDtypeStruct((t, RET_QK_W), BF16),
                   jax.ShapeDtypeStruct((t, RET_V_W), BF16)],
        compiler_params=_params(1),
        name="ret_prep",
    )(x, mod, gain, w, cos_t, sin_t)


def _ret_scan_kernel(ld_ref, q_ref, k_ref, v_ref, gn_ref, o_ref,
                     yf_ref, st_ref, intra_ref, qdec_ref, kdec_ref, cdec_ref):
    c = RET_CHUNK
    seq = q_ref.shape[0]
    nc = seq // c
    n_pairs = RET_HEADS // 2
    pair_v = 2 * RET_V_DIM
    ia = lax.broadcasted_iota(jnp.int32, (c, c), 0)
    ib = lax.broadcasted_iota(jnp.int32, (c, c), 1)
    diff = (ia - ib).astype(F32)
    idx_q = lax.broadcasted_iota(jnp.int32, (c, pair_v), 0).astype(F32)
    lane_q = lax.broadcasted_iota(jnp.int32, (c, pair_v), 1)
    idx_k = lax.broadcasted_iota(jnp.int32, (c, LANES), 0).astype(F32)
    lane_k = lax.broadcasted_iota(jnp.int32, (c, LANES), 1)
    low_k = lane_k < RET_QK_DIM
    low_k16 = lane_k.astype(F32).astype(BF16) < RET_QK_DIM
    low_v16 = lane_q.astype(F32).astype(BF16) < RET_V_DIM
    srow =lax.broadcasted_iota(jnp.int32, (LANES, pair_v), 0)
    scol = lax.broadcasted_iota(jnp.int32, (LANES, pair_v), 1)
    blk0 = jnp.logical_and(srow < RET_QK_DIM, scol < RET_V_DIM)
    blk1 = jnp.logical_and(srow >= RET_QK_DIM, scol >= RET_V_DIM)
    diag = jnp.where(jnp.logical_or(blk0, blk1), 1.0, 0.0).astype(F32)

    for dr in range(2):
        for hd in range(RET_HEADS):
            lg = ld_ref[dr, hd]
            if dr == 0:
                mask = diff >= 0
                dist = jnp.where(mask, diff, 0.0)
            else:
                mask = diff < 0
                dist = jnp.where(mask, -diff, 0.0)
            intra_ref[dr, hd] = jnp.where(mask, jnp.exp(lg * dist), 0.0)
        for hp in range(n_pairs):
            lg0 = ld_ref[dr, 2 * hp]
            lg1 = ld_ref[dr, 2 * hp + 1]
            q_exp = idx_q + 1.0 if dr == 0 else c - idx_q
            k_exp = c - 1.0 - idx_k if dr == 0 else idx_k
            qdec_ref[dr, hp] = jnp.exp(jnp.where(lane_q < RET_V_DIM, lg0, lg1) * q_exp)
            kdec_ref[dr, hp] = jnp.exp(jnp.where(low_k, lg0, lg1) * k_exp)
            cdec_ref[dr, hp] = jnp.where(blk0, jnp.exp(lg0 * c), jnp.where(blk1, jnp.exp(lg1 * c), 0.0))

    def chunk(dr, n):
        r0 = pl.multiple_of(n * c, c)
        qc = q_ref[pl.ds(r0, c), :]
        kc = k_ref[pl.ds(r0, c), :]
        vc = v_ref[pl.ds(r0, c), :]
        ys = []
        for hp in range(n_pairs):
            qp = qc[:, hp * LANES:(hp + 1) * LANES]
            kp = kc[:, hp * LANES:(hp + 1) * LANES]
            vp = vc[:, hp * pair_v:(hp + 1) * pair_v]
            zk = jnp.zeros_like(kp)
            zv = jnp.zeros_like(vp)
            k2 = jnp.concatenate([jnp.where(low_k16, kp, zk), jnp.where(low_k16, zk, kp)], axis=0)
            v2 = jnp.concatenate([jnp.where(low_v16, vp, zv), jnp.where(low_v16, zv, vp)], axis=0)
            s = _dot_nt(qp, k2) * jnp.concatenate([intra_ref[dr, 2 * hp], intra_ref[dr, 2 * hp + 1]], axis=1)
            y_intra = _dot(s.astype(BF16), v2)
            state = st_ref[hp]
            y_inter = _dot(qp, state.astype(BF16)) * qdec_ref[dr, hp]
            ys.append(y_intra + y_inter)
            kd = (kp.astype(F32) * kdec_ref[dr, hp]).astype(BF16)
            st_ref[hp] = cdec_ref[dr, hp] * state + diag * _dot_tn(kd, vp)
        return r0, ys

    st_ref[...] = jnp.zeros_like(st_ref)

    def fwd(n, carry):
        r0, ys = chunk(0, n)
        for hp in range(n_pairs):
            yf_ref[pl.ds(r0, c), hp * pair_v:(hp + 1) * pair_v] = ys[hp]
        return carry

    lax.fori_loop(0, nc, fwd, 0, unroll=RET_UNROLL)
    st_ref[...] = jnp.zeros_like(st_ref)

    def bwd(i, carry):
        r0, ys = chunk(1, nc - 1 - i)
        for hp in range(n_pairs):
            y = ys[hp] + yf_ref[pl.ds(r0, c), hp * pair_v:(hp + 1) * pair_v]
            for half in range(2):
                cols = slice(hp * pair_v + half * RET_V_DIM, hp * pair_v + (half + 1) * RET_V_DIM)
                yh = y[:, half * RET_V_DIM:(half + 1) * RET_V_DIM]
                mu = jnp.mean(yh, axis=-1, keepdims=True)
                dev = yh - mu
                var = jnp.mean(dev * dev, axis=-1, keepdims=True)
                o_ref[pl.ds(r0, c), cols] = (dev * lax.rsqrt(var + EPS) * gn_ref[:, cols]).astype(o_ref.dtype)
        return carry

    lax.fori_loop(0, nc, bwd, 0, unroll=RET_UNROLL)


def _ret_scan(log_decay, q, k, v, ret_norm, batch, seq):
    c = RET_CHUNK
    n_pairs = RET_HEADS // 2
    return pl.pallas_call(
        _ret_scan_kernel,
        grid=(batch,),
        in_specs=[
            pl.BlockSpec(memory_space=pltpu.SMEM),
            pl.BlockSpec((seq, RET_QK_W), lambda b: (b, 0)),
            pl.BlockSpec((seq, RET_QK_W), lambda b: (b, 0)),
            pl.BlockSpec((seq, RET_V_W), lambda b: (b, 0)),
            _resident((1, RET_V_W)),
        ],
        out_specs=pl.BlockSpec((seq, RET_V_W), lambda b: (b, 0)),
        out_shape=jax.ShapeDtypeStruct((batch * seq, RET_V_W), BF16),
        scratch_shapes=[
            pltpu.VMEM((seq, RET_V_W), F32),
            pltpu.VMEM((n_pairs, LANES, 2 * RET_V_DIM), F32),
            pltpu.VMEM((2, RET_HEADS, c, c), F32),
            pltpu.VMEM((2, n_pairs, c, 2 * RET_V_DIM), F32),
            pltpu.VMEM((2, n_pairs, c, LANES), F32),
            pltpu.VMEM((2, n_pairs, LANES, 2 * RET_V_DIM), F32),
        ],
        compiler_params=_params(1),
        name="ret_scan",
    )(log_decay, q, k, v, ret_norm)


def _top2_route(h, w_hi_lo):
    h_hi = h.astype(BF16)
    h_lo = (h - h_hi.astype(F32)).astype(BF16)
    hh = _dot(h_hi, w_hi_lo)
    logits = hh[:, :LANES] + (hh[:, LANES:] + _dot(h_lo, w_hi_lo[:, :LANES]))
    lane = lax.broadcasted_iota(jnp.int32, logits.shape, 1).astype(F32)
    lg = jnp.where(lane < N_EXPERTS, logits, -jnp.inf)
    m1 = jnp.max(lg, axis=-1, keepdims=True)
    i1 = jnp.min(jnp.where(lg == m1, lane, float(LANES)), axis=-1, keepdims=True)
    lg2 = jnp.where(lane == i1, -jnp.inf, lg)
    m2 = jnp.max(lg2, axis=-1, keepdims=True)
    i2 = jnp.min(jnp.where(lg2 == m2, lane, float(LANES)), axis=-1, keepdims=True)
    e = jnp.exp(m2 - m1)
    w1 = 1.0 / (1.0 + e)
    w2 = e / (1.0 + e)
    return jnp.where(lane == 0, i1, jnp.where(lane == 1, i2, jnp.where(lane == 2, w1, jnp.where(lane == 3, w2, 0.0))))


def _merge_kernel(x_ref, mod_ref, g_ref, wg_ref, ymla_ref, yret_ref,
                  o0_ref, l0_ref, o1_ref, l1_ref, o2_ref, l2_ref,
                  wbm_ref, wbd_ref, wbr_ref, wout_ref, gf_ref, wr_ref, out_ref, *rest, route):
    nat_o, nat_l = rest[-2:]
    x = x_ref[...]
    mod = mod_ref[0]
    h = _norm_mod(x, g_ref[...], mod[1:2], mod[0:1]).astype(BF16)
    gates = _dot(h, wg_ref[0])
    d_model = x.shape[1]
    tm = x.shape[0]

    o_nat = [o0_ref[0, 0].astype(F32)]
    l_nat = [l0_ref[0, 0]]
    for g, (o_ref, l_ref) in enumerate(((o1_ref, l1_ref), (o2_ref, l2_ref))):
        dil = DIL_PAIRS[g + 1][1]
        n = tm // dil
        n_lane_blocks = DIL_GROUP_W // LANES
        for r in range(dil):
            o_r = o_ref[0, r].astype(F32)
            l_r = l_ref[0, r]
            for j in range(n_lane_blocks):
                nat_o[g, j, pl.ds(r, n, stride=dil), :] = o_r[:, j * LANES:(j + 1) * LANES]
                nat_l[g, j, pl.ds(r, n, stride=dil), :] = l_r[:, j * LANES:(j + 1) * LANES]
        o_nat.append(jnp.concatenate([nat_o[g, j] for j in range(n_lane_blocks)], axis=1))
        l_nat.append(jnp.concatenate([nat_l[g, j] for j in range(n_lane_blocks)], axis=1))
    m = jnp.maximum(jnp.maximum(l_nat[0], l_nat[1]), l_nat[2])
    ws = [jnp.exp(l - m) for l in l_nat]
    y_dil = (ws[0] * o_nat[0] + ws[1] * o_nat[1] + ws[2] * o_nat[2]) / (ws[0] + ws[1] + ws[2])

    y_ret = (_silu(gates[:, :RET_V_W]) * yret_ref[...].astype(F32)).astype(BF16)
    ga = gates[:, RET_V_W:RET_V_W + d_model]
    gb = gates[:, RET_V_W + d_model:RET_V_W + 2 * d_model]
    gc = gates[:, RET_V_W + 2 * d_model:]
    merged = (jax.nn.sigmoid(ga) * _dot(ymla_ref[...], wbm_ref[0])
              + jax.nn.sigmoid(gb) * _dot(y_dil.astype(BF16), wbd_ref[0])
              + jax.nn.sigmoid(gc) * _dot(y_ret, wbr_ref[0]))
    x_new = x + mod[2:3] * _dot(merged.astype(BF16), wout_ref[0])
    out_ref[...] = x_new
    if route:
        h_ref, route_ref = rest[:2]
        h2 = _norm_mod(x_new, gf_ref[...], mod[4:5], mod[3:4])
        h_ref[...] = h2
        route_ref[...] = _top2_route(h2, wr_ref[...])


def _merge_out(x, mod, gain, wg, y_mla, y_ret, dil_outs, wbm, wbd, wbr, wout, gain_ffn, w_router,
               layer, batch, seq, route):
    t, d_model = x.shape
    tm = MERGE_TILE
    per_b = seq // tm
    row = lambda i: (i, 0)
    dil_specs, dil_args = [], []
    for (_, dil), (o, lse) in zip(DIL_PAIRS, dil_outs):
        spec = pl.BlockSpec((1, dil, tm // dil, DIL_GROUP_W), lambda i: (i // per_b, 0, i % per_b, 0))
        dil_specs += [spec, spec]
        dil_args += [o, lse]
    out_specs = [pl.BlockSpec((tm, d_model), row)]
    out_shape = [jax.ShapeDtypeStruct((t, d_model), F32)]
    if route:
        out_specs += [pl.BlockSpec((tm, d_model), row), pl.BlockSpec((tm, LANES), row)]
        out_shape += [jax.ShapeDtypeStruct((t, d_model), F32), jax.ShapeDtypeStruct((t, LANES), F32)]
    wr32 = jnp.zeros((d_model, LANES), F32).at[:, :N_EXPERTS].set(w_router)
    wr_hi = wr32.astype(BF16)
    wr = jnp.concatenate([wr_hi, (wr32 - wr_hi.astype(F32)).astype(BF16)], axis=1)
    return pl.pallas_call(
        functools.partial(_merge_kernel, route=route),
        grid=(t // tm,),
        in_specs=[
            pl.BlockSpec((tm, d_model), row),
            pl.BlockSpec((1, ADA_CHUNKS, d_model), lambda i: (i // per_b, 0, 0)),
            _resident((1, d_model)),
            _layer_resident(wg, layer),
            pl.BlockSpec((tm, y_mla.shape[1]), row),
            pl.BlockSpec((tm, RET_V_W), row),
            *dil_specs,
            _layer_resident(wbm, layer), _layer_resident(wbd, layer), _layer_resident(wbr, layer),
            _layer_resident(wout, layer),
            _resident((1, d_model)),
            _resident(wr.shape),
        ],
        out_specs=out_specs,
        out_shape=out_shape,
        scratch_shapes=[pltpu.VMEM((2, DIL_GROUP_W // LANES, tm, LANES), F32),
                        pltpu.VMEM((2, DIL_GROUP_W // LANES, tm, LANES), F32)],
        compiler_params=_params(1),
        name="merge_out",
    )(x, mod, gain, wg, y_mla, y_ret, *dil_args, wbm, wbd, wbr, wout, gain_ffn, wr)


def _residual_out(y, fg_ref, final):
    return _rms(y, fg_ref[...]) if final else y


def _ffn_kernel(x_ref, mod_ref, g_ref, fg_ref, w1_ref, w3_ref, w2_ref, out_ref, *, n_chunks, final):
    x = x_ref[...]
    mod = mod_ref[0]
    h = _norm_mod(x, g_ref[...], mod[4:5], mod[3:4]).astype(BF16)
    d_ff = w1_ref.shape[2]
    fc = d_ff // n_chunks
    acc = jnp.zeros(x.shape, F32)
    for j in range(n_chunks):
        a = _dot(h, w1_ref[0, :, j * fc:(j + 1) * fc])
        b = _dot(h, w3_ref[0, :, j * fc:(j + 1) * fc])
        acc = acc + _dot((_silu(a) * b).astype(BF16), w2_ref[0, j * fc:(j + 1) * fc, :])
    out_ref[...] = _residual_out(x + mod[5:6] * acc, fg_ref, final)


def _ffn_dense(x, mod, gain, final_gain, w1, w3, w2, layer, seq, final):
    t, d_model = x.shape
    tm = ROW_TILE
    per_b = seq // tm
    row = lambda i: (i, 0)
    return pl.pallas_call(
        functools.partial(_ffn_kernel, n_chunks=2, final=final),
        grid=(t // tm,),
        in_specs=[
            pl.BlockSpec((tm, d_model), row),
            pl.BlockSpec((1, ADA_CHUNKS, d_model), lambda i: (i // per_b, 0, 0)),
            _resident((1, d_model)),
            _resident((1, d_model)),
            _layer_resident(w1, layer), _layer_resident(w3, layer), _layer_resident(w2, layer),
        ],
        out_specs=pl.BlockSpec((tm, d_model), row),
        out_shape=jax.ShapeDtypeStruct((t, d_model), F32),
        compiler_params=_params(1),
        name="ffn_dense",
    )(x, mod, gain, final_gain, w1, w3, w2)


def _row_copy(src_hbm, dst, src_row, dst_row, sem):
    return pltpu.make_async_copy(src_hbm.at[pl.ds(src_row, 1), :], dst.at[pl.ds(dst_row, 1), :], sem)


def _expert_kernel(be_ref, nact_ref, tok_ref, tok_next_ref, h_hbm, w1_ref, w3_ref, w2_ref, y_ref,
                   xbuf, sems, *, n_chunks):
    i = pl.program_id(0)
    nact = nact_ref[0]
    bm = y_ref.shape[0]
    slot = i % 2

    def wait_block(s):
        pltpu.make_async_copy(h_hbm.at[pl.ds(0, bm), :], xbuf.at[s], sems.at[s]).wait()

    @pl.when(i == 0)
    def _():
        def issue(r, carry):
            _row_copy(h_hbm, xbuf.at[0], tok_ref[0, 0, r], r, sems.at[0]).start()
            return carry

        lax.fori_loop(0, bm, issue, 0, unroll=8)

    @pl.when(i >= nact)
    def _():
        y_ref[...] = jnp.zeros_like(y_ref)

    @pl.when(i < nact)
    def _():
        wait_block(slot)
        xb = xbuf[slot].astype(BF16)
        for r in range(bm):
            _row_copy(h_hbm, xbuf.at[1 - slot], tok_next_ref[0, 0, r], r, sems.at[1 - slot]).start()
        d_ff = w1_ref.shape[2]
        fc = d_ff // n_chunks
        acc = jnp.zeros(y_ref.shape, F32)
        for j in range(n_chunks):
            a = _dot(xb, w1_ref[0, :, j * fc:(j + 1) * fc])
            b = _dot(xb, w3_ref[0, :, j * fc:(j + 1) * fc])
            acc = acc + _dot((_silu(a) * b).astype(BF16), w2_ref[0, j * fc:(j + 1) * fc, :])
        y_ref[...] = acc

    @pl.when(i == nact - 1)
    def _():
        wait_block(1 - slot)


def _moe_expert(block_expert, nact, row_tok, h, w1, w3, w2, expert_base):
    nb, _, bm = row_tok.shape
    d_model = h.shape[1]
    d_ff = w1.shape[2]
    wsel = lambda i, be, na: (expert_base + be[jnp.minimum(i, na[0] - 1)], 0, 0)
    return pl.pallas_call(
        functools.partial(_expert_kernel, n_chunks=2),
        grid_spec=pltpu.PrefetchScalarGridSpec(
            num_scalar_prefetch=2,
            grid=(nb,),
            in_specs=[
                pl.BlockSpec((1, 1, bm), lambda i, be, na: (i, 0, 0), memory_space=pltpu.SMEM),
                pl.BlockSpec((1, 1, bm), lambda i, be, na: (jnp.minimum(i + 1, nb - 1), 0, 0),
                             memory_space=pltpu.SMEM),
                pl.BlockSpec(memory_space=pl.ANY),
                pl.BlockSpec((1, d_model, d_ff), wsel, pipeline_mode=pl.Buffered(1)),
                pl.BlockSpec((1, d_model, d_ff), wsel, pipeline_mode=pl.Buffered(1)),
                pl.BlockSpec((1, d_ff, d_model), wsel, pipeline_mode=pl.Buffered(1)),
            ],
            out_specs=pl.BlockSpec((bm, d_model), lambda i, be, na: (i, 0)),
            scratch_shapes=[pltpu.VMEM((2, bm, d_model), F32), pltpu.SemaphoreType.DMA((2,))],
        ),
        out_shape=jax.ShapeDtypeStruct((nb * bm, d_model), F32),
        compiler_params=_params(1),
        name="moe_expert",
    )(block_expert, nact, row_tok, row_tok, h, w1, w3, w2)


def _combine_kernel(dest_ref, dest_next_ref, x_ref, mod_ref, route_ref, fg_ref, yb_hbm, out_ref, buf, sems,
                    *, final):
    tm = x_ref.shape[0]
    n_rows = TOP_K * tm
    i = pl.program_id(0)
    slot = i % 2

    def issue(rows_ref, s):
        def body(r, carry):
            _row_copy(yb_hbm, buf.at[s], rows_ref[0, 0, r], r, sems.at[s]).start()
            return carry

        lax.fori_loop(0, n_rows, body, 0, unroll=8)

    @pl.when(i == 0)
    def _():
        issue(dest_ref, 0)

    @pl.when(i + 1 < pl.num_programs(0))
    def _():
        for r in range(n_rows):
            _row_copy(yb_hbm, buf.at[1 - slot], dest_next_ref[0, 0, r], r, sems.at[1 - slot]).start()

    pltpu.make_async_copy(yb_hbm.at[pl.ds(0, n_rows), :], buf.at[slot], sems.at[slot]).wait()
    route = route_ref[...]
    y = route[:, 2:3] * buf[slot, :tm, :] + route[:, 3:4] * buf[slot, tm:, :]
    out_ref[...] = _residual_out(x_ref[...] + mod_ref[0][5:6] * y, fg_ref, final)


def _moe_combine(dest, x, mod, route, final_gain, yb, seq, final):
    t, d_model = x.shape
    tm = COMBINE_TILE
    per_b = seq // tm
    row = lambda i: (i, 0)
    return pl.pallas_call(
        functools.partial(_combine_kernel, final=final),
        grid=(t // tm,),
        in_specs=[
            pl.BlockSpec((1, 1, TOP_K * tm), lambda i: (i, 0, 0), memory_space=pltpu.SMEM),
            pl.BlockSpec((1, 1, TOP_K * tm), lambda i: (jnp.minimum(i + 1, t // tm - 1), 0, 0),
                         memory_space=pltpu.SMEM),
            pl.BlockSpec((tm, d_model), row),
            pl.BlockSpec((1, ADA_CHUNKS, d_model), lambda i: (i // per_b, 0, 0)),
            pl.BlockSpec((tm, LANES), row),
            _resident((1, d_model)),
            pl.BlockSpec(memory_space=pl.ANY),
        ],
        out_specs=pl.BlockSpec((tm, d_model), row),
        out_shape=jax.ShapeDtypeStruct((t, d_model), F32),
        scratch_shapes=[pltpu.VMEM((2, TOP_K * tm, d_model), F32), pltpu.SemaphoreType.DMA((2,))],
        compiler_params=_params(1),
        name="moe_combine",
    )(dest, dest, x, mod, route, final_gain, yb)


def _moe(x, h, route, mod, final_gain, w1, w3, w2, expert_base, seq, final):
    t, d_model = x.shape
    bm = MOE_BLOCK
    e_flat = route[:, :TOP_K].astype(jnp.int32).reshape(-1)
    n_assign = e_flat.shape[0]
    onehot = (e_flat[:, None] == jnp.arange(N_EXPERTS, dtype=jnp.int32)[None, :]).astype(jnp.int32)
    csum = jnp.cumsum(onehot, axis=0)
    rank = jnp.take_along_axis(csum, e_flat[:, None], axis=1)[:, 0] - 1
    counts = csum[-1]
    padded = (counts + bm - 1) // bm * bm
    pends = jnp.cumsum(padded)
    dest = (pends - padded)[e_flat] + rank
    nb = n_assign // bm + N_EXPERTS
    row_tok = jnp.zeros((nb * bm,), jnp.int32).at[dest].set(jnp.arange(n_assign, dtype=jnp.int32) // TOP_K)
    block_start = jnp.arange(nb, dtype=jnp.int32) * bm
    block_expert = jnp.minimum(
        jnp.sum((pends[None, :] <= block_start[:, None]).astype(jnp.int32), axis=1), N_EXPERTS - 1)
    nact = (pends[-1:] // bm).astype(jnp.int32)

    yb = _moe_expert(block_expert, nact, row_tok.reshape(nb, 1, bm), h, w1, w3, w2, expert_base)
    tmc = COMBINE_TILE
    dest_tiles = dest.reshape(t // tmc, tmc, TOP_K).transpose(0, 2, 1).reshape(t // tmc, 1, TOP_K * tmc)
    return _moe_combine(dest_tiles.astype(jnp.int32), x, mod, route, final_gain, yb, seq, final)


def _rope_tables(positions):
    inv_freq = ROPE_THETA ** (-jnp.arange(0, ROPE_DIM, 2, dtype=F32) / ROPE_DIM)
    ang = positions.astype(F32)[..., None] * inv_freq
    cos, sin = jnp.cos(ang), jnp.sin(ang)
    reps = LANES // ROPE_DIM
    cos_t = jnp.tile(jnp.concatenate([cos, cos], axis=-1), (1, 1, reps))
    sin_t = jnp.tile(jnp.concatenate([-sin, sin], axis=-1), (1, 1, reps))
    return cos_t, sin_t


def _split_w_in(w_in):
    d_model = w_in.shape[1]
    sizes = (MLA_Q_RANK, MLA_KV_RANK, MLA_ROPE,
             3 * DIL_GROUP_W, 3 * DIL_GROUP_W, 3 * DIL_GROUP_W,
             RET_QK_W, RET_QK_W, RET_V_W, RET_V_W, d_model, d_model, d_model)
    w_in = w_in.astype(BF16)
    cols, start = [], 0
    for n in sizes:
        cols.append(w_in[..., start:start + n])
        start += n
    cq, ckv, kr, dq, dk, dv, rq, rk, rv, rg, ga, gb, gc = cols
    w_mla = jnp.concatenate([cq, ckv, kr, kr], axis=-1)
    gw = DIL_GROUP_W
    w_dil = jnp.concatenate(
        [m[..., g * gw:(g + 1) * gw] for g in range(len(DIL_PAIRS)) for m in (dq, dk, dv)], axis=-1)
    w_ret = jnp.concatenate([rq, rk, rv], axis=-1)
    w_gate = jnp.concatenate([rg, ga, gb, gc], axis=-1)
    return w_mla, w_dil, w_ret, w_gate


def _split_w_uq(w_uq):
    depth = w_uq.shape[0]
    w = w_uq.astype(BF16).reshape(depth, MLA_Q_RANK, MLA_HEADS, MLA_QK)
    nope = w[..., :MLA_NOPE].reshape(depth, MLA_Q_RANK, MLA_HEADS * MLA_NOPE)
    rope = w[..., MLA_NOPE:].reshape(depth, MLA_Q_RANK, MLA_HEADS * MLA_ROPE)
    return jnp.concatenate([nope, rope], axis=-1)


def kernel(x, c, positions, ada_w, ada_b, norm_mix, norm_ffn, w_in, mla_q_norm, mla_w_uq, mla_kv_norm,
           mla_w_ukv, ret_log_decay, ret_norm, w_br_mla, w_br_dil, w_br_ret, w_out, ffn_w1, ffn_w3, ffn_w2,
           moe_router, moe_w1, moe_w3, moe_w2, final_norm):
    batch, seq, d_model = x.shape
    depth = ada_w.shape[0]
    cos_t, sin_t = _rope_tables(positions)
    mod_all = _ada(c, ada_w, ada_b)
    assert depth >= 1
    fgain = final_norm.reshape(1, d_model)
    xt = x.reshape(batch * seq, d_model)
    moe_w1_b = moe_w1.astype(BF16).reshape((-1,) + moe_w1.shape[2:])
    moe_w3_b = moe_w3.astype(BF16).reshape((-1,) + moe_w3.shape[2:])
    moe_w2_b = moe_w2.astype(BF16).reshape((-1,) + moe_w2.shape[2:])
    w_mla, w_dil, w_ret, w_gate = _split_w_in(w_in)
    w_uq = _split_w_uq(mla_w_uq)
    w_ukv = mla_w_ukv.astype(BF16)
    wbm, wbd, wbr, wout = (w.astype(BF16) for w in (w_br_mla, w_br_dil, w_br_ret, w_out))
    ffn_w1_b, ffn_w3_b, ffn_w2_b = (w.astype(BF16) for w in (ffn_w1, ffn_w3, ffn_w2))
    for layer in range(depth):
        mod = mod_all[layer]
        gmix = norm_mix[layer].reshape(1, d_model)

        q, k, v = _mla_prep(xt, mod, gmix, w_mla, mla_q_norm[layer].reshape(1, -1),
                            mla_kv_norm[layer].reshape(1, -1), w_uq, w_ukv, cos_t, sin_t, layer, batch, seq)
        y_mla = _mla_attn(q, k, v)

        dil_qkv = _dil_prep(xt, mod, gmix, w_dil, cos_t, sin_t, layer, batch, seq)
        dil_outs = [_dil_attn(*dil_qkv[3 * g:3 * g + 3], window, dil)
                    for g, (window, dil) in enumerate(DIL_PAIRS)]

        rq, rk, rv = _ret_prep(xt, mod, gmix, w_ret, cos_t, sin_t, layer, seq)
        y_ret = _ret_scan(ret_log_decay[layer].astype(F32), rq, rk, rv,
                          ret_norm[layer].reshape(1, -1).astype(F32), batch, seq)

        gffn = norm_ffn[layer].reshape(1, d_model)
        i = layer // 2
        is_moe = layer % 2 == 1
        merged = _merge_out(xt, mod, gmix, w_gate, y_mla, y_ret, dil_outs, wbm, wbd, wbr, wout, gffn,
                            moe_router[i] if is_moe else jnp.zeros((d_model, N_EXPERTS), F32),
                            layer, batch, seq, is_moe)
        last = layer == depth - 1
        if is_moe:
            xt, h, route = merged
            xt = _moe(xt, h, route, mod, fgain, moe_w1_b, moe_w3_b, moe_w2_b, i * N_EXPERTS, seq, last)
        else:
            xt = _ffn_dense(merged[0], mod, gffn, fgain, ffn_w1_b, ffn_w3_b, ffn_w2_b, i, seq, last)
    return xt.reshape(batch, seq, d_model)
```

```python
import functools

import jax
import jax.numpy as jnp
from jax import lax
from jax.experimental import pallas as pl
from jax.experimental.pallas import tpu as pltpu

F32 = jnp.float32
BF16 = jnp.bfloat16

EPS = 1e-6
NEG_INF = -1e30
ROPE_THETA = 10000.0
ROPE_DIM = 64
ADA_CHUNKS = 6

MLA_HEADS = 8
MLA_Q_RANK = 768
MLA_KV_RANK = 512
MLA_NOPE = 128
MLA_ROPE = ROPE_DIM
MLA_V = 128
MLA_QK = MLA_NOPE + MLA_ROPE

DIL_PAIRS = ((128, 1), (512, 4), (2048, 16))
DIL_HEADS = 8
DIL_HEAD_DIM = ROPE_DIM
DIL_GROUP_W = DIL_HEADS * DIL_HEAD_DIM
DIL_QBLOCK = 128

RET_HEADS = 8
RET_QK_DIM = ROPE_DIM
RET_V_DIM = 2 * RET_QK_DIM
RET_CHUNK = 256
RET_UNROLL = 4
RET_QK_W = RET_HEADS * RET_QK_DIM
RET_V_W = RET_HEADS * RET_V_DIM

N_EXPERTS = 8
TOP_K = 2
MOE_BLOCK = 256

LANES = 128
SUBLANES = 8
V7X_VMEM_LIMIT_BYTES = 56 * 1024 * 1024

ROW_TILE = 512
MERGE_TILE = 512
COMBINE_TILE = 256
MLA_Q_TILE = 512
MLA_HEADS_PER_STEP = 4
MLA_KEY_CHUNK = 1024


def _params(n_grid_dims):
    return pltpu.CompilerParams(
        dimension_semantics=("arbitrary",) * n_grid_dims,
        vmem_limit_bytes=V7X_VMEM_LIMIT_BYTES,
    )


def _resident(shape):
    zeros = (0,) * len(shape)
    return pl.BlockSpec(shape, lambda *_: zeros, pipeline_mode=pl.Buffered(1))


def _layer_resident(stacked, layer):
    idx = (layer,) + (0,) * (stacked.ndim - 1)
    return pl.BlockSpec((1,) + stacked.shape[1:], lambda *_: idx, pipeline_mode=pl.Buffered(1))


def _silu(v):
    return v * jax.nn.sigmoid(v)


def _norm_mod(x, gain, scale, shift):
    ms = jnp.mean(x * x, axis=-1, keepdims=True)
    y = x * lax.rsqrt(ms + EPS) * gain
    return y * (1.0 + scale) + shift


def _rms(v, gain):
    ms = jnp.mean(v * v, axis=-1, keepdims=True)
    return v * lax.rsqrt(ms + EPS) * gain


def _rope(t, cos_t, sin_t):
    lane = lax.broadcasted_iota(jnp.int32, (t.shape[0], LANES), 1)
    first_half = (lane % ROPE_DIM) < (ROPE_DIM // 2)
    out = []
    for j in range(t.shape[1] // LANES):
        c = t[:, j * LANES:(j + 1) * LANES]
        rot = jnp.where(first_half,
                        pltpu.roll(c, LANES - ROPE_DIM // 2, 1),
                        pltpu.roll(c, ROPE_DIM // 2, 1))
        out.append(c * cos_t + rot * sin_t)
    return out[0] if len(out) == 1 else jnp.concatenate(out, axis=1)


def _dot(a, b):
    return jnp.dot(a, b, preferred_element_type=F32)


def _dot_nt(a, b):
    return lax.dot_general(a, b, (((1,), (1,)), ((), ())), preferred_element_type=F32)


def _dot_tn(a, b):
    return lax.dot_general(a, b, (((0,), (0,)), ((), ())), preferred_element_type=F32)


def _ada_kernel(c_ref, w_ref, b_ref, o_ref):
    o_ref[0] = jnp.dot(_silu(c_ref[...]), w_ref[0], preferred_element_type=F32,
                       precision=lax.Precision.HIGHEST) + b_ref[0]


def _ada(c, ada_w, ada_b):
    depth, d_model, n = ada_w.shape
    batch = c.shape[0]
    tn = n // 4
    out = pl.pallas_call(
        _ada_kernel,
        grid=(depth, n // tn),
        in_specs=[
            pl.BlockSpec((batch, d_model), lambda l, j: (0, 0)),
            pl.BlockSpec((1, d_model, tn), lambda l, j: (l, 0, j)),
            pl.BlockSpec((1, 1, tn), lambda l, j: (l, 0, j)),
        ],
        out_specs=pl.BlockSpec((1, batch, tn), lambda l, j: (l, 0, j)),
        out_shape=jax.ShapeDtypeStruct((depth, batch, n), F32),
        compiler_params=_params(2),
        name="ada",
    )(c, ada_w, ada_b.reshape(depth, 1, n))
    return out.reshape(depth, batch, ADA_CHUNKS, d_model)


def _mla_prep_kernel(x_ref, mod_ref, g_ref, wa_ref, qn_ref, kvn_ref, wuq_ref, wukv_ref, cos_ref, sin_ref,
                     q_ref, k_ref, v_ref):
    mod = mod_ref[0]
    h = _norm_mod(x_ref[...], g_ref[...], mod[1:2], mod[0:1]).astype(BF16)
    p = _dot(h, wa_ref[0])
    cq = _rms(p[:, :MLA_Q_RANK], qn_ref[...]).astype(BF16)
    ckv = _rms(p[:, MLA_Q_RANK:MLA_Q_RANK + MLA_KV_RANK], kvn_ref[...]).astype(BF16)
    cos_t = cos_ref[0]
    sin_t = sin_ref[0]
    kr = _rope(p[:, MLA_Q_RANK + MLA_KV_RANK:], cos_t, sin_t)
    q = _dot(cq, wuq_ref[0])
    kv = _dot(ckv, wukv_ref[0])
    nope_w = MLA_HEADS * MLA_NOPE
    qr = _rope(q[:, nope_w:], cos_t, sin_t)
    scale = MLA_QK ** -0.5
    kr_b = kr[:, :MLA_ROPE].astype(BF16)
    for hd in range(MLA_HEADS):
        q_ref[0, hd, :, :MLA_NOPE] = (q[:, hd * MLA_NOPE:(hd + 1) * MLA_NOPE] * scale).astype(BF16)
        q_ref[0, hd, :, MLA_NOPE:] = (qr[:, hd * MLA_ROPE:(hd + 1) * MLA_ROPE] * scale).astype(BF16)
        base = hd * (MLA_NOPE + MLA_V)
        k_ref[0, hd, :, :MLA_NOPE] = kv[:, base:base + MLA_NOPE].astype(BF16)
        k_ref[0, hd, :, MLA_NOPE:] = kr_b
        v_ref[0, hd] = kv[:, base + MLA_NOPE:base + MLA_NOPE + MLA_V].astype(BF16)


def _mla_prep(x, mod, gain, wa, qn, kvn, wuq, wukv, cos_t, sin_t, layer, batch, seq):
    t, d_model = x.shape
    tm = ROW_TILE
    per_b = seq // tm
    row = lambda i: (i, 0)
    hs = lambda i: (i // per_b, 0, i % per_b, 0)
    return pl.pallas_call(
        _mla_prep_kernel,
        grid=(t // tm,),
        in_specs=[
            pl.BlockSpec((tm, d_model), row),
            pl.BlockSpec((1, ADA_CHUNKS, d_model), lambda i: (i // per_b, 0, 0)),
            _resident((1, d_model)),
            _layer_resident(wa, layer),
            _resident((1, MLA_Q_RANK)),
            _resident((1, MLA_KV_RANK)),
            _layer_resident(wuq, layer),
            _layer_resident(wukv, layer),
            pl.BlockSpec((1, tm, LANES), lambda i: (i // per_b, i % per_b, 0)),
            pl.BlockSpec((1, tm, LANES), lambda i: (i // per_b, i % per_b, 0)),
        ],
        out_specs=[
            pl.BlockSpec((1, MLA_HEADS, tm, MLA_QK), hs),
            pl.BlockSpec((1, MLA_HEADS, tm, MLA_QK), hs),
            pl.BlockSpec((1, MLA_HEADS, tm, MLA_V), hs),
        ],
        out_shape=[
            jax.ShapeDtypeStruct((batch, MLA_HEADS, seq, MLA_QK), BF16),
            jax.ShapeDtypeStruct((batch, MLA_HEADS, seq, MLA_QK), BF16),
            jax.ShapeDtypeStruct((batch, MLA_HEADS, seq, MLA_V), BF16),
        ],
        compiler_params=_params(1),
        name="mla_prep",
    )(x, mod, gain, wa, qn, kvn, wuq, wukv, cos_t, sin_t)


def _mla_attn_kernel(q_ref, k_ref, v_ref, o_ref, *, heads_per_step, key_chunk):
    seq = k_ref.shape[2]
    ones = jnp.ones((key_chunk, LANES), BF16)
    for hd in range(heads_per_step):
        q = q_ref[0, hd]
        m = acc = None
        for c in range(seq // key_chunk):
            rows = slice(c * key_chunk, (c + 1) * key_chunk)
            s = _dot_nt(q, k_ref[0, hd, rows, :])
            v_ext = jnp.concatenate([v_ref[0, hd, rows, :], ones], axis=1)
            m_c = jnp.max(s, axis=-1, keepdims=True)
            if c == 0:
                m = m_c
                acc = _dot(jnp.exp(s - m).astype(BF16), v_ext)
            else:
                m_new = jnp.maximum(m, m_c)
                acc = jnp.exp(m - m_new) * acc + _dot(jnp.exp(s - m_new).astype(BF16), v_ext)
                m = m_new
        o_ref[:, hd * MLA_V:(hd + 1) * MLA_V] = (acc[:, :MLA_V] / acc[:, MLA_V:MLA_V + 1]).astype(o_ref.dtype)


def _mla_attn(q, k, v):
    batch, heads, seq, _ = q.shape
    tq = MLA_Q_TILE
    nq = seq // tq
    hps = MLA_HEADS_PER_STEP
    return pl.pallas_call(
        functools.partial(_mla_attn_kernel, heads_per_step=hps, key_chunk=MLA_KEY_CHUNK),
        grid=(batch, heads // hps, nq),
        in_specs=[
            pl.BlockSpec((1, hps, tq, MLA_QK), lambda b, h, i: (b, h, i, 0)),
            pl.BlockSpec((1, hps, seq, MLA_QK), lambda b, h, i: (b, h, 0, 0)),
            pl.BlockSpec((1, hps, seq, MLA_V), lambda b, h, i: (b, h, 0, 0)),
        ],
        out_specs=pl.BlockSpec((tq, hps * MLA_V), lambda b, h, i: (b * nq + i, h)),
        out_shape=jax.ShapeDtypeStruct((batch * seq, heads * MLA_V), BF16),
        compiler_params=_params(3),
        name="mla_attn",
    )(q, k, v)


def _dil_prep_kernel(x_ref, mod_ref, g_ref, w_ref, cos_ref, sin_ref, *refs):
    out_refs = refs[:3 * len(DIL_PAIRS)]
    h_scr = refs[3 * len(DIL_PAIRS)]
    mod = mod_ref[0]
    h32 = _norm_mod(x_ref[...], g_ref[...], mod[1:2], mod[0:1])
    n_lane_blocks = h32.shape[1] // LANES
    for j in range(n_lane_blocks):
        h_scr[j] = h32[:, j * LANES:(j + 1) * LANES]
    tm = h32.shape[0]
    gw = DIL_GROUP_W
    for g, (_, dil) in enumerate(DIL_PAIRS):
        n = tm // dil
        if dil == 1:
            hp, cos_t, sin_t = h32, cos_ref[0], sin_ref[0]
        else:
            hp = jnp.concatenate(
                [jnp.concatenate([h_scr[j, pl.ds(r, n, stride=dil), :] for j in range(n_lane_blocks)], axis=1)
                 for r in range(dil)], axis=0)
            cos_t = jnp.concatenate([cos_ref[0, pl.ds(r, n, stride=dil), :] for r in range(dil)], axis=0)
            sin_t = jnp.concatenate([sin_ref[0, pl.ds(r, n, stride=dil), :] for r in range(dil)], axis=0)
        pg = _dot(hp.astype(BF16), w_ref[0, :, g * 3 * gw:(g + 1) * 3 * gw])
        qg = (_rope(pg[:, :gw], cos_t, sin_t) * (DIL_HEAD_DIM ** -0.5)).astype(BF16)
        kg = _rope(pg[:, gw:2 * gw], cos_t, sin_t).astype(BF16)
        vg = pg[:, 2 * gw:].astype(BF16)
        q_ref, k_ref, v_ref = out_refs[3 * g:3 * g + 3]
        for r in range(dil):
            q_ref[0, r] = qg[r * n:(r + 1) * n]
            k_ref[0, r] = kg[r * n:(r + 1) * n]
            v_ref[0, r] = vg[r * n:(r + 1) * n]


def _dil_prep(x, mod, gain, w, cos_t, sin_t, layer, batch, seq):
    t, d_model = x.shape
    tm = ROW_TILE
    per_b = seq // tm
    out_specs, out_shape = [], []
    for _, dil in DIL_PAIRS:
        for _ in range(3):
            out_specs.append(pl.BlockSpec((1, dil, tm // dil, DIL_GROUP_W),
                                          lambda i: (i // per_b, 0, i % per_b, 0)))
            out_shape.append(jax.ShapeDtypeStruct((batch, dil, seq // dil, DIL_GROUP_W), BF16))
    return pl.pallas_call(
        _dil_prep_kernel,
        grid=(t // tm,),
        in_specs=[
            pl.BlockSpec((tm, d_model), lambda i: (i, 0)),
            pl.BlockSpec((1, ADA_CHUNKS, d_model), lambda i: (i // per_b, 0, 0)),
            _resident((1, d_model)),
            _layer_resident(w, layer),
            pl.BlockSpec((1, tm, LANES), lambda i: (i // per_b, i % per_b, 0)),
            pl.BlockSpec((1, tm, LANES), lambda i: (i // per_b, i % per_b, 0)),
        ],
        out_specs=out_specs,
        out_shape=out_shape,
        scratch_shapes=[pltpu.VMEM((d_model // LANES, tm, LANES), F32)],
        compiler_params=_params(1),
        name="dil_prep",
    )(x, mod, gain, w, cos_t, sin_t)


def _dil_attn_kernel(q_ref, k_ref, v_ref, o_ref, lse_ref, *, dil, segment, band):
    qb_rows = DIL_QBLOCK
    kw = 2 * qb_rows
    short = segment < kw
    per_window = kw // segment if short else 1
    nb = 1 if short else segment // qb_rows
    low_k = lax.broadcasted_iota(jnp.int32, (kw, LANES), 1).astype(F32).astype(BF16) < DIL_HEAD_DIM
    low_q = lax.broadcasted_iota(jnp.int32, (qb_rows, LANES), 1) < DIL_HEAD_DIM
    row_i = lax.broadcasted_iota(jnp.int32, (qb_rows, kw), 0)
    col_i = lax.broadcasted_iota(jnp.int32, (qb_rows, kw), 1)
    zeros_k = jnp.zeros((kw, LANES), BF16)
    ones_k = jnp.ones((kw, LANES), BF16)
    denom_cols = jnp.concatenate([jnp.where(low_k, ones_k, zeros_k), jnp.where(low_k, zeros_k, ones_k)], axis=0)

    def tile(idx, carry):
        r = idx // nb
        if short:
            q0 = 0
            r0 = (r // per_window) * per_window
            qt = q_ref[0, r]
            kt = jnp.concatenate([k_ref[0, r0 + j] for j in range(per_window)], axis=0)
            vt = jnp.concatenate([v_ref[0, r0 + j] for j in range(per_window)], axis=0)
            jq = (r - r0) * segment + row_i
            jk = col_i
        else:
            q0 = pl.multiple_of((idx % nb) * qb_rows, qb_rows)
            ks = pl.multiple_of(jnp.clip(q0 - band, 0, segment - kw), band)
            qt = q_ref[0, r, pl.ds(q0, qb_rows), :]
            kt = k_ref[0, r, pl.ds(ks, kw), :]
            vt = v_ref[0, r, pl.ds(ks, kw), :]
            jq = q0 + row_i
            jk = ks + col_i
        valid = jnp.logical_and(jnp.abs(jq - jk) <= band, jq // segment == jk // segment)
        for hp in range(DIL_GROUP_W // LANES):
            cols = slice(hp * LANES, (hp + 1) * LANES)
            qp, kp, vp = qt[:, cols], kt[:, cols], vt[:, cols]
            k2 = jnp.concatenate([jnp.where(low_k, kp, zeros_k), jnp.where(low_k, zeros_k, kp)], axis=0)
            v2 = jnp.concatenate([jnp.where(low_k, vp, zeros_k), jnp.where(low_k, zeros_k, vp)], axis=0)
            s = _dot_nt(qp, k2)
            s0 = jnp.where(valid, s[:, :kw], NEG_INF)
            s1 = jnp.where(valid, s[:, kw:], NEG_INF)
            m0 = jnp.max(s0, axis=-1, keepdims=True)
            m1 = jnp.max(s1, axis=-1, keepdims=True)
            p = jnp.concatenate([jnp.exp(s0 - m0), jnp.exp(s1 - m1)], axis=1).astype(BF16)
            pv = _dot(p, jnp.concatenate([v2, denom_cols], axis=1))
            den = pv[:, LANES:]
            o_ref[0, r, pl.ds(q0, qb_rows), cols] = (pv[:, :LANES] / den).astype(o_ref.dtype)
            lse_ref[0, r, pl.ds(q0, qb_rows), cols] = jnp.where(low_q, m0, m1) + jnp.log(den)
        return carry

    lax.fori_loop(0, dil * nb, tile, 0, unroll=8)


def _dil_attn(q, k, v, window, dil):
    batch, _, seg, gw = q.shape
    band = window // (2 * dil)
    assert band * 2 == DIL_QBLOCK and seg % DIL_QBLOCK == 0
    assert seg >= 2 * DIL_QBLOCK or (seg == DIL_QBLOCK and dil % 2 == 0)
    spec = pl.BlockSpec((1, dil, seg, gw), lambda b: (b, 0, 0, 0))
    return pl.pallas_call(
        functools.partial(_dil_attn_kernel, dil=dil, segment=seg, band=band),
        grid=(batch,),
        in_specs=[spec, spec, spec],
        out_specs=[spec, spec],
        out_shape=[jax.ShapeDtypeStruct(q.shape, BF16), jax.ShapeDtypeStruct(q.shape, F32)],
        compiler_params=_params(1),
        name=f"dil_attn_d{dil}",
    )(q, k, v)


def _ret_prep_kernel(x_ref, mod_ref, g_ref, w_ref, cos_ref, sin_ref, q_ref, k_ref, v_ref):
    mod = mod_ref[0]
    h = _norm_mod(x_ref[...], g_ref[...], mod[1:2], mod[0:1]).astype(BF16)
    p = _dot(h, w_ref[0])
    cos_t, sin_t = cos_ref[0], sin_ref[0]
    q_ref[...] = _rope(p[:, :RET_QK_W], cos_t, sin_t).astype(BF16)
    k_ref[...] = (_rope(p[:, RET_QK_W:2 * RET_QK_W], cos_t, sin_t) * (RET_QK_DIM ** -0.5)).astype(BF16)
    v_ref[...] = p[:, 2 * RET_QK_W:].astype(BF16)


def _ret_prep(x, mod, gain, w, cos_t, sin_t, layer, seq):
    t, d_model = x.shape
    tm = ROW_TILE
    per_b = seq // tm
    row = lambda i: (i, 0)
    return pl.pallas_call(
        _ret_prep_kernel,
        grid=(t // tm,),
        in_specs=[
            pl.BlockSpec((tm, d_model), row),
            pl.BlockSpec((1, ADA_CHUNKS, d_model), lambda i: (i // per_b, 0, 0)),
            _resident((1, d_model)),
            _layer_resident(w, layer),
            pl.BlockSpec((1, tm, LANES), lambda i: (i // per_b, i % per_b, 0)),
            pl.BlockSpec((1, tm, LANES), lambda i: (i // per_b, i % per_b, 0)),
        ],
        out_specs=[pl.BlockSpec((tm, RET_QK_W), row), pl.BlockSpec((tm, RET_QK_W), row),
                   pl.BlockSpec((tm, RET_V_W), row)],
        out_shape=[jax.ShapeDtypeStruct((t, RET_QK_W), BF16), jax.ShapeDtypeStruct((t, RET_QK_W), BF16),
                   jax.ShapeDtypeStruct((t, RET_V_W), BF16)],
        compiler_params=_params(1),
        name="ret_prep",
    )(x, mod, gain, w, cos_t, sin_t)


def _ret_scan_kernel(ld_ref, q_ref, k_ref, v_ref, gn_ref, o_ref,
                     yf_ref, st_ref, intra_ref, qdec_ref, kdec_ref, cdec_ref):
    c = RET_CHUNK
    seq = q_ref.shape[0]
    nc = seq // c
    n_pairs = RET_HEADS // 2
    pair_v = 2 * RET_V_DIM
    ia = lax.broadcasted_iota(jnp.int32, (c, c), 0)
    ib = lax.broadcasted_iota(jnp.int32, (c, c), 1)
    diff = (ia - ib).astype(F32)
    idx_q = lax.broadcasted_iota(jnp.int32, (c, pair_v), 0).astype(F32)
    lane_q = lax.broadcasted_iota(jnp.int32, (c, pair_v), 1)
    idx_k = lax.broadcasted_iota(jnp.int32, (c, LANES), 0).astype(F32)
    lane_k = lax.broadcasted_iota(jnp.int32, (c, LANES), 1)
    low_k = lane_k < RET_QK_DIM
    low_k16 = lane_k.astype(F32).astype(BF16) < RET_QK_DIM
    low_v16 = lane_q.astype(F32).astype(BF16) < RET_V_DIM
    srow =lax.broadcasted_iota(jnp.int32, (LANES, pair_v), 0)
    scol = lax.broadcasted_iota(jnp.int32, (LANES, pair_v), 1)
    blk0 = jnp.logical_and(srow < RET_QK_DIM, scol < RET_V_DIM)
    blk1 = jnp.logical_and(srow >= RET_QK_DIM, scol >= RET_V_DIM)
    diag = jnp.where(jnp.logical_or(blk0, blk1), 1.0, 0.0).astype(F32)

    for dr in range(2):
        for hd in range(RET_HEADS):
            lg = ld_ref[dr, hd]
            if dr == 0:
                mask = diff >= 0
                dist = jnp.where(mask, diff, 0.0)
            else:
                mask = diff < 0
                dist = jnp.where(mask, -diff, 0.0)
            intra_ref[dr, hd] = jnp.where(mask, jnp.exp(lg * dist), 0.0)
        for hp in range(n_pairs):
            lg0 = ld_ref[dr, 2 * hp]
            lg1 = ld_ref[dr, 2 * hp + 1]
            q_exp = idx_q + 1.0 if dr == 0 else c - idx_q
            k_exp = c - 1.0 - idx_k if dr == 0 else idx_k
            qdec_ref[dr, hp] = jnp.exp(jnp.where(lane_q < RET_V_DIM, lg0, lg1) * q_exp)
            kdec_ref[dr, hp] = jnp.exp(jnp.where(low_k, lg0, lg1) * k_exp)
            cdec_ref[dr, hp] = jnp.where(blk0, jnp.exp(lg0 * c), jnp.where(blk1, jnp.exp(lg1 * c), 0.0))

    def chunk(dr, n):
        r0 = pl.multiple_of(n * c, c)
        qc = q_ref[pl.ds(r0, c), :]
        kc = k_ref[pl.ds(r0, c), :]
        vc = v_ref[pl.ds(r0, c), :]
        ys = []
        for hp in range(n_pairs):
            qp = qc[:, hp * LANES:(hp + 1) * LANES]
            kp = kc[:, hp * LANES:(hp + 1) * LANES]
            vp = vc[:, hp * pair_v:(hp + 1) * pair_v]
            zk = jnp.zeros_like(kp)
            zv = jnp.zeros_like(vp)
            k2 = jnp.concatenate([jnp.where(low_k16, kp, zk), jnp.where(low_k16, zk, kp)], axis=0)
            v2 = jnp.concatenate([jnp.where(low_v16, vp, zv), jnp.where(low_v16, zv, vp)], axis=0)
            s = _dot_nt(qp, k2) * jnp.concatenate([intra_ref[dr, 2 * hp], intra_ref[dr, 2 * hp + 1]], axis=1)
            y_intra = _dot(s.astype(BF16), v2)
            state = st_ref[hp]
            y_inter = _dot(qp, state.astype(BF16)) * qdec_ref[dr, hp]
            ys.append(y_intra + y_inter)
            kd = (kp.astype(F32) * kdec_ref[dr, hp]).astype(BF16)
            st_ref[hp] = cdec_ref[dr, hp] * state + diag * _dot_tn(kd, vp)
        return r0, ys

    st_ref[...] = jnp.zeros_like(st_ref)

    def fwd(n, carry):
        r0, ys = chunk(0, n)
        for hp in range(n_pairs):
            yf_ref[pl.ds(r0, c), hp * pair_v:(hp + 1) * pair_v] = ys[hp]
        return carry

    lax.fori_loop(0, nc, fwd, 0, unroll=RET_UNROLL)
    st_ref[...] = jnp.zeros_like(st_ref)

    def bwd(i, carry):
        r0, ys = chunk(1, nc - 1 - i)
        for hp in range(n_pairs):
            y = ys[hp] + yf_ref[pl.ds(r0, c), hp * pair_v:(hp + 1) * pair_v]
            for half in range(2):
                cols = slice(hp * pair_v + half * RET_V_DIM, hp * pair_v + (half + 1) * RET_V_DIM)
                yh = y[:, half * RET_V_DIM:(half + 1) * RET_V_DIM]
                mu = jnp.mean(yh, axis=-1, keepdims=True)
                dev = yh - mu
                var = jnp.mean(dev * dev, axis=-1, keepdims=True)
                o_ref[pl.ds(r0, c), cols] = (dev * lax.rsqrt(var + EPS) * gn_ref[:, cols]).astype(o_ref.dtype)
        return carry

    lax.fori_loop(0, nc, bwd, 0, unroll=RET_UNROLL)


def _ret_scan(log_decay, q, k, v, ret_norm, batch, seq):
    c = RET_CHUNK
    n_pairs = RET_HEADS // 2
    return pl.pallas_call(
        _ret_scan_kernel,
        grid=(batch,),
        in_specs=[
            pl.BlockSpec(memory_space=pltpu.SMEM),
            pl.BlockSpec((seq, RET_QK_W), lambda b: (b, 0)),
            pl.BlockSpec((seq, RET_QK_W), lambda b: (b, 0)),
            pl.BlockSpec((seq, RET_V_W), lambda b: (b, 0)),
            _resident((1, RET_V_W)),
        ],
        out_specs=pl.BlockSpec((seq, RET_V_W), lambda b: (b, 0)),
        out_shape=jax.ShapeDtypeStruct((batch * seq, RET_V_W), BF16),
        scratch_shapes=[
            pltpu.VMEM((seq, RET_V_W), F32),
            pltpu.VMEM((n_pairs, LANES, 2 * RET_V_DIM), F32),
            pltpu.VMEM((2, RET_HEADS, c, c), F32),
            pltpu.VMEM((2, n_pairs, c, 2 * RET_V_DIM), F32),
            pltpu.VMEM((2, n_pairs, c, LANES), F32),
            pltpu.VMEM((2, n_pairs, LANES, 2 * RET_V_DIM), F32),
        ],
        compiler_params=_params(1),
        name="ret_scan",
    )(log_decay, q, k, v, ret_norm)


def _top2_route(h, w_hi_lo):
    h_hi = h.astype(BF16)
    h_lo = (h - h_hi.astype(F32)).astype(BF16)
    hh = _dot(h_hi, w_hi_lo)
    logits = hh[:, :LANES] + (hh[:, LANES:] + _dot(h_lo, w_hi_lo[:, :LANES]))
    lane = lax.broadcasted_iota(jnp.int32, logits.shape, 1).astype(F32)
    lg = jnp.where(lane < N_EXPERTS, logits, -jnp.inf)
    m1 = jnp.max(lg, axis=-1, keepdims=True)
    i1 = jnp.min(jnp.where(lg == m1, lane, float(LANES)), axis=-1, keepdims=True)
    lg2 = jnp.where(lane == i1, -jnp.inf, lg)
    m2 = jnp.max(lg2, axis=-1, keepdims=True)
    i2 = jnp.min(jnp.where(lg2 == m2, lane, float(LANES)), axis=-1, keepdims=True)
    e = jnp.exp(m2 - m1)
    w1 = 1.0 / (1.0 + e)
    w2 = e / (1.0 + e)
    return jnp.where(lane == 0, i1, jnp.where(lane == 1, i2, jnp.where(lane == 2, w1, jnp.where(lane == 3, w2, 0.0))))


def _merge_kernel(x_ref, mod_ref, g_ref, wg_ref, ymla_ref, yret_ref,
                  o0_ref, l0_ref, o1_ref, l1_ref, o2_ref, l2_ref,
                  wbm_ref, wbd_ref, wbr_ref, wout_ref, gf_ref, wr_ref, out_ref, *rest, route):
    nat_o, nat_l = rest[-2:]
    x = x_ref[...]
    mod = mod_ref[0]
    h = _norm_mod(x, g_ref[...], mod[1:2], mod[0:1]).astype(BF16)
    gates = _dot(h, wg_ref[0])
    d_model = x.shape[1]
    tm = x.shape[0]

    o_nat = [o0_ref[0, 0].astype(F32)]
    l_nat = [l0_ref[0, 0]]
    for g, (o_ref, l_ref) in enumerate(((o1_ref, l1_ref), (o2_ref, l2_ref))):
        dil = DIL_PAIRS[g + 1][1]
        n = tm // dil
        n_lane_blocks = DIL_GROUP_W // LANES
        for r in range(dil):
            o_r = o_ref[0, r].astype(F32)
            l_r = l_ref[0, r]
            for j in range(n_lane_blocks):
                nat_o[g, j, pl.ds(r, n, stride=dil), :] = o_r[:, j * LANES:(j + 1) * LANES]
                nat_l[g, j, pl.ds(r, n, stride=dil), :] = l_r[:, j * LANES:(j + 1) * LANES]
        o_nat.append(jnp.concatenate([nat_o[g, j] for j in range(n_lane_blocks)], axis=1))
        l_nat.append(jnp.concatenate([nat_l[g, j] for j in range(n_lane_blocks)], axis=1))
    m = jnp.maximum(jnp.maximum(l_nat[0], l_nat[1]), l_nat[2])
    ws = [jnp.exp(l - m) for l in l_nat]
    y_dil = (ws[0] * o_nat[0] + ws[1] * o_nat[1] + ws[2] * o_nat[2]) / (ws[0] + ws[1] + ws[2])

    y_ret = (_silu(gates[:, :RET_V_W]) * yret_ref[...].astype(F32)).astype(BF16)
    ga = gates[:, RET_V_W:RET_V_W + d_model]
    gb = gates[:, RET_V_W + d_model:RET_V_W + 2 * d_model]
    gc = gates[:, RET_V_W + 2 * d_model:]
    merged = (jax.nn.sigmoid(ga) * _dot(ymla_ref[...], wbm_ref[0])
              + jax.nn.sigmoid(gb) * _dot(y_dil.astype(BF16), wbd_ref[0])
              + jax.nn.sigmoid(gc) * _dot(y_ret, wbr_ref[0]))
    x_new = x + mod[2:3] * _dot(merged.astype(BF16), wout_ref[0])
    out_ref[...] = x_new
    if route:
        h_ref, route_ref, ids_ref = rest[:3]
        h2 = _norm_mod(x_new, gf_ref[...], mod[4:5], mod[3:4])
        h_ref[...] = h2
        r = _top2_route(h2, wr_ref[...])
        route_ref[...] = r
        ids_ref[...] = r.T[:ids_ref.shape[0]]


def _merge_out(x, mod, gain, wg, y_mla, y_ret, dil_outs, wbm, wbd, wbr, wout, gain_ffn, w_router,
               layer, batch, seq, route):
    t, d_model = x.shape
    tm = MERGE_TILE
    per_b = seq // tm
    row = lambda i: (i, 0)
    dil_specs, dil_args = [], []
    for (_, dil), (o, lse) in zip(DIL_PAIRS, dil_outs):
        spec = pl.BlockSpec((1, dil, tm // dil, DIL_GROUP_W), lambda i: (i // per_b, 0, i % per_b, 0))
        dil_specs += [spec, spec]
        dil_args += [o, lse]
    out_specs = [pl.BlockSpec((tm, d_model), row)]
    out_shape = [jax.ShapeDtypeStruct((t, d_model), F32)]
    if route:
        out_specs += [pl.BlockSpec((tm, d_model), row), pl.BlockSpec((tm, LANES), row),
                      pl.BlockSpec((SUBLANES, tm), lambda i: (0, i))]
        out_shape += [jax.ShapeDtypeStruct((t, d_model), F32), jax.ShapeDtypeStruct((t, LANES), F32),
                      jax.ShapeDtypeStruct((SUBLANES, t), F32)]
    wr32 = jnp.zeros((d_model, LANES), F32).at[:, :N_EXPERTS].set(w_router)
    wr_hi = wr32.astype(BF16)
    wr = jnp.concatenate([wr_hi, (wr32 - wr_hi.astype(F32)).astype(BF16)], axis=1)
    return pl.pallas_call(
        functools.partial(_merge_kernel, route=route),
        grid=(t // tm,),
        in_specs=[
            pl.BlockSpec((tm, d_model), row),
            pl.BlockSpec((1, ADA_CHUNKS, d_model), lambda i: (i // per_b, 0, 0)),
            _resident((1, d_model)),
            _layer_resident(wg, layer),
            pl.BlockSpec((tm, y_mla.shape[1]), row),
            pl.BlockSpec((tm, RET_V_W), row),
            *dil_specs,
            _layer_resident(wbm, layer), _layer_resident(wbd, layer), _layer_resident(wbr, layer),
            _layer_resident(wout, layer),
            _resident((1, d_model)),
            _resident(wr.shape),
        ],
        out_specs=out_specs,
        out_shape=out_shape,
        scratch_shapes=[pltpu.VMEM((2, DIL_GROUP_W // LANES, tm, LANES), F32),
                        pltpu.VMEM((2, DIL_GROUP_W // LANES, tm, LANES), F32)],
        compiler_params=_params(1),
        name="merge_out",
    )(x, mod, gain, wg, y_mla, y_ret, *dil_args, wbm, wbd, wbr, wout, gain_ffn, wr)


def _residual_out(y, fg_ref, final):
    return _rms(y, fg_ref[...]) if final else y


def _ffn_kernel(x_ref, mod_ref, g_ref, fg_ref, w1_ref, w3_ref, w2_ref, out_ref, *, n_chunks, final):
    x = x_ref[...]
    mod = mod_ref[0]
    h = _norm_mod(x, g_ref[...], mod[4:5], mod[3:4]).astype(BF16)
    d_ff = w1_ref.shape[2]
    fc = d_ff // n_chunks
    acc = jnp.zeros(x.shape, F32)
    for j in range(n_chunks):
        a = _dot(h, w1_ref[0, :, j * fc:(j + 1) * fc])
        b = _dot(h, w3_ref[0, :, j * fc:(j + 1) * fc])
        acc = acc + _dot((_silu(a) * b).astype(BF16), w2_ref[0, j * fc:(j + 1) * fc, :])
    out_ref[...] = _residual_out(x + mod[5:6] * acc, fg_ref, final)


def _ffn_dense(x, mod, gain, final_gain, w1, w3, w2, layer, seq, final):
    t, d_model = x.shape
    tm = ROW_TILE
    per_b = seq // tm
    row = lambda i: (i, 0)
    return pl.pallas_call(
        functools.partial(_ffn_kernel, n_chunks=1, final=final),
        grid=(t // tm,),
        in_specs=[
            pl.BlockSpec((tm, d_model), row),
            pl.BlockSpec((1, ADA_CHUNKS, d_model), lambda i: (i // per_b, 0, 0)),
            _resident((1, d_model)),
            _resident((1, d_model)),
            _layer_resident(w1, layer), _layer_resident(w3, layer), _layer_resident(w2, layer),
        ],
        out_specs=pl.BlockSpec((tm, d_model), row),
        out_shape=jax.ShapeDtypeStruct((t, d_model), F32),
        compiler_params=_params(1),
        name="ffn_dense",
    )(x, mod, gain, final_gain, w1, w3, w2)


def _row_copy(src_hbm, dst, src_row, dst_row, sem):
    return pltpu.make_async_copy(src_hbm.at[pl.ds(src_row, 1), :], dst.at[pl.ds(dst_row, 1), :], sem)


def _expert_kernel(be_ref, nact_ref, tok_ref, tok_next_ref, h_hbm, w1_ref, w3_ref, w2_ref, y_ref,
                   xbuf, sems, *, n_chunks):
    i = pl.program_id(0)
    nact = nact_ref[0]
    bm = y_ref.shape[0]
    slot = i % 2

    def wait_block(s):
        pltpu.make_async_copy(h_hbm.at[pl.ds(0, bm), :], xbuf.at[s], sems.at[s]).wait()

    @pl.when(i == 0)
    def _():
        def issue(r, carry):
            _row_copy(h_hbm, xbuf.at[0], tok_ref[0, 0, r], r, sems.at[0]).start()
            return carry

        lax.fori_loop(0, bm, issue, 0, unroll=8)

    @pl.when(i >= nact)
    def _():
        y_ref[...] = jnp.zeros_like(y_ref)

    @pl.when(i < nact)
    def _():
        wait_block(slot)
        xb = xbuf[slot].astype(BF16)
        for r in range(bm):
            _row_copy(h_hbm, xbuf.at[1 - slot], tok_next_ref[0, 0, r], r, sems.at[1 - slot]).start()
        d_ff = w1_ref.shape[2]
        fc = d_ff // n_chunks
        acc = jnp.zeros(y_ref.shape, F32)
        for j in range(n_chunks):
            a = _dot(xb, w1_ref[0, :, j * fc:(j + 1) * fc])
            b = _dot(xb, w3_ref[0, :, j * fc:(j + 1) * fc])
            acc = acc + _dot((_silu(a) * b).astype(BF16), w2_ref[0, j * fc:(j + 1) * fc, :])
        y_ref[...] = acc

    @pl.when(i == nact - 1)
    def _():
        wait_block(1 - slot)


def _moe_expert(block_expert, nact, row_tok, h, w1, w3, w2, expert_base):
    nb, _, bm = row_tok.shape
    d_model = h.shape[1]
    d_ff = w1.shape[2]
    wsel = lambda i, be, na: (expert_base + be[jnp.minimum(i, na[0] - 1)], 0, 0)
    return pl.pallas_call(
        functools.partial(_expert_kernel, n_chunks=2),
        grid_spec=pltpu.PrefetchScalarGridSpec(
            num_scalar_prefetch=2,
            grid=(nb,),
            in_specs=[
                pl.BlockSpec((1, 1, bm), lambda i, be, na: (i, 0, 0), memory_space=pltpu.SMEM),
                pl.BlockSpec((1, 1, bm), lambda i, be, na: (jnp.minimum(i + 1, nb - 1), 0, 0),
                             memory_space=pltpu.SMEM),
                pl.BlockSpec(memory_space=pl.ANY),
                pl.BlockSpec((1, d_model, d_ff), wsel, pipeline_mode=pl.Buffered(1)),
                pl.BlockSpec((1, d_model, d_ff), wsel, pipeline_mode=pl.Buffered(1)),
                pl.BlockSpec((1, d_ff, d_model), wsel, pipeline_mode=pl.Buffered(1)),
            ],
            out_specs=pl.BlockSpec((bm, d_model), lambda i, be, na: (i, 0)),
            scratch_shapes=[pltpu.VMEM((2, bm, d_model), F32), pltpu.SemaphoreType.DMA((2,))],
        ),
        out_shape=jax.ShapeDtypeStruct((nb * bm, d_model), F32),
        compiler_params=_params(1),
        name="moe_expert",
    )(block_expert, nact, row_tok, row_tok, h, w1, w3, w2)


def _combine_kernel(dest_ref, dest_next_ref, x_ref, mod_ref, route_ref, fg_ref, yb_hbm, out_ref, buf, sems,
                    *, final):
    tm = x_ref.shape[0]
    n_rows = TOP_K * tm
    i = pl.program_id(0)
    slot = i % 2

    def issue(rows_ref, s):
        def body(r, carry):
            _row_copy(yb_hbm, buf.at[s], rows_ref[0, 0, r], r, sems.at[s]).start()
            return carry

        lax.fori_loop(0, n_rows, body, 0, unroll=8)

    @pl.when(i == 0)
    def _():
        issue(dest_ref, 0)

    @pl.when(i + 1 < pl.num_programs(0))
    def _():
        for r in range(n_rows):
            _row_copy(yb_hbm, buf.at[1 - slot], dest_next_ref[0, 0, r], r, sems.at[1 - slot]).start()

    pltpu.make_async_copy(yb_hbm.at[pl.ds(0, n_rows), :], buf.at[slot], sems.at[slot]).wait()
    route = route_ref[...]
    y = route[:, 2:3] * buf[slot, :tm, :] + route[:, 3:4] * buf[slot, tm:, :]
    out_ref[...] = _residual_out(x_ref[...] + mod_ref[0][5:6] * y, fg_ref, final)


def _moe_combine(dest, x, mod, route, final_gain, yb, seq, final):
    t, d_model = x.shape
    tm = COMBINE_TILE
    per_b = seq // tm
    row = lambda i: (i, 0)
    return pl.pallas_call(
        functools.partial(_combine_kernel, final=final),
        grid=(t // tm,),
        in_specs=[
            pl.BlockSpec((1, 1, TOP_K * tm), lambda i: (i, 0, 0), memory_space=pltpu.SMEM),
            pl.BlockSpec((1, 1, TOP_K * tm), lambda i: (jnp.minimum(i + 1, t // tm - 1), 0, 0),
                         memory_space=pltpu.SMEM),
            pl.BlockSpec((tm, d_model), row),
            pl.BlockSpec((1, ADA_CHUNKS, d_model), lambda i: (i // per_b, 0, 0)),
            pl.BlockSpec((tm, LANES), row),
            _resident((1, d_model)),
            pl.BlockSpec(memory_space=pl.ANY),
        ],
        out_specs=pl.BlockSpec((tm, d_model), row),
        out_shape=jax.ShapeDtypeStruct((t, d_model), F32),
        scratch_shapes=[pltpu.VMEM((2, TOP_K * tm, d_model), F32), pltpu.SemaphoreType.DMA((2,))],
        compiler_params=_params(1),
        name="moe_combine",
    )(dest, dest, x, mod, route, final_gain, yb)


def _moe(x, h, route, expert_ids, mod, final_gain, w1, w3, w2, expert_base, seq, final):
    t, d_model = x.shape
    bm = MOE_BLOCK
    e_flat = expert_ids[:TOP_K].astype(jnp.int32).reshape(1, -1)
    n_assign = e_flat.shape[1]
    onehot = e_flat == jnp.arange(N_EXPERTS, dtype=jnp.int32)[:, None]
    csum = jnp.cumsum(onehot.astype(jnp.int32), axis=1)
    counts = csum[:, -1]
    padded = (counts + bm - 1) // bm * bm
    pends = jnp.cumsum(padded)
    row_of = (pends - padded)[:, None] + csum - 1
    dest = jnp.sum(jnp.where(onehot, row_of, 0), axis=0)
    nb = n_assign // bm + N_EXPERTS
    row_tok = jnp.zeros((nb * bm,), jnp.int32).at[dest].set(jnp.arange(n_assign, dtype=jnp.int32) % t)
    block_start = jnp.arange(nb, dtype=jnp.int32) * bm
    block_expert = jnp.minimum(
        jnp.sum((pends[None, :] <= block_start[:, None]).astype(jnp.int32), axis=1), N_EXPERTS - 1)
    nact = (pends[-1:] // bm).astype(jnp.int32)

    yb = _moe_expert(block_expert, nact, row_tok.reshape(nb, 1, bm), h, w1, w3, w2, expert_base)
    tmc = COMBINE_TILE
    dest_tiles = dest.reshape(TOP_K, t // tmc, tmc).transpose(1, 0, 2).reshape(t // tmc, 1, TOP_K * tmc)
    return _moe_combine(dest_tiles, x, mod, route, final_gain, yb, seq, final)


def _rope_tables(positions):
    inv_freq = ROPE_THETA ** (-jnp.arange(0, ROPE_DIM, 2, dtype=F32) / ROPE_DIM)
    ang = positions.astype(F32)[..., None] * inv_freq
    cos, sin = jnp.cos(ang), jnp.sin(ang)
    reps = LANES // ROPE_DIM
    cos_t = jnp.tile(jnp.concatenate([cos, cos], axis=-1), (1, 1, reps))
    sin_t = jnp.tile(jnp.concatenate([-sin, sin], axis=-1), (1, 1, reps))
    return cos_t, sin_t


def _split_w_in(w_in):
    d_model = w_in.shape[1]
    sizes = (MLA_Q_RANK, MLA_KV_RANK, MLA_ROPE,
             3 * DIL_GROUP_W, 3 * DIL_GROUP_W, 3 * DIL_GROUP_W,
             RET_QK_W, RET_QK_W, RET_V_W, RET_V_W, d_model, d_model, d_model)
    w_in = w_in.astype(BF16)
    cols, start = [], 0
    for n in sizes:
        cols.append(w_in[..., start:start + n])
        start += n
    cq, ckv, kr, dq, dk, dv, rq, rk, rv, rg, ga, gb, gc = cols
    w_mla = jnp.concatenate([cq, ckv, kr, kr], axis=-1)
    lead = dq.shape[:-1]
    per_group = [m.reshape(lead + (len(DIL_PAIRS), DIL_GROUP_W)) for m in (dq, dk, dv)]
    w_dil = jnp.stack(per_group, axis=-2).reshape(lead + (3 * len(DIL_PAIRS) * DIL_GROUP_W,))
    w_ret = jnp.concatenate([rq, rk, rv], axis=-1)
    w_gate = jnp.concatenate([rg, ga, gb, gc], axis=-1)
    return w_mla, w_dil, w_ret, w_gate


def _split_w_uq(w_uq):
    depth = w_uq.shape[0]
    w = w_uq.astype(BF16).reshape(depth, MLA_Q_RANK, MLA_HEADS, MLA_QK)
    nope = w[..., :MLA_NOPE].reshape(depth, MLA_Q_RANK, MLA_HEADS * MLA_NOPE)
    rope = w[..., MLA_NOPE:].reshape(depth, MLA_Q_RANK, MLA_HEADS * MLA_ROPE)
    return jnp.concatenate([nope, rope], axis=-1)


def kernel(x, c, positions, ada_w, ada_b, norm_mix, norm_ffn, w_in, mla_q_norm, mla_w_uq, mla_kv_norm,
           mla_w_ukv, ret_log_decay, ret_norm, w_br_mla, w_br_dil, w_br_ret, w_out, ffn_w1, ffn_w3, ffn_w2,
           moe_router, moe_w1, moe_w3, moe_w2, final_norm):
    batch, seq, d_model = x.shape
    depth = ada_w.shape[0]
    cos_t, sin_t = _rope_tables(positions)
    mod_all = _ada(c, ada_w, ada_b)
    assert depth >= 1
    fgain = final_norm.reshape(1, d_model)
    xt = x.reshape(batch * seq, d_model)
    moe_w1_b = moe_w1.astype(BF16).reshape((-1,) + moe_w1.shape[2:])
    moe_w3_b = moe_w3.astype(BF16).reshape((-1,) + moe_w3.shape[2:])
    moe_w2_b = moe_w2.astype(BF16).reshape((-1,) + moe_w2.shape[2:])
    w_mla, w_dil, w_ret, w_gate = _split_w_in(w_in)
    w_uq = _split_w_uq(mla_w_uq)
    w_ukv = mla_w_ukv.astype(BF16)
    wbm, wbd, wbr, wout = (w.astype(BF16) for w in (w_br_mla, w_br_dil, w_br_ret, w_out))
    ffn_w1_b, ffn_w3_b, ffn_w2_b = (w.astype(BF16) for w in (ffn_w1, ffn_w3, ffn_w2))
    for layer in range(depth):
        mod = mod_all[layer]
        gmix = norm_mix[layer].reshape(1, d_model)

        q, k, v = _mla_prep(xt, mod, gmix, w_mla, mla_q_norm[layer].reshape(1, -1),
                            mla_kv_norm[layer].reshape(1, -1), w_uq, w_ukv, cos_t, sin_t, layer, batch, seq)
        y_mla = _mla_attn(q, k, v)

        dil_qkv = _dil_prep(xt, mod, gmix, w_dil, cos_t, sin_t, layer, batch, seq)
        dil_outs = [_dil_attn(*dil_qkv[3 * g:3 * g + 3], window, dil)
                    for g, (window, dil) in enumerate(DIL_PAIRS)]

        rq, rk, rv = _ret_prep(xt, mod, gmix, w_ret, cos_t, sin_t, layer, seq)
        y_ret = _ret_scan(ret_log_decay[layer].astype(F32), rq, rk, rv,
                          ret_norm[layer].reshape(1, -1).astype(F32), batch, seq)

        gffn = norm_ffn[layer].reshape(1, d_model)
        i = layer // 2
        is_moe = layer % 2 == 1
        merged = _merge_out(xt, mod, gmix, w_gate, y_mla, y_ret, dil_outs, wbm, wbd, wbr, wout, gffn,
                            moe_router[i] if is_moe else jnp.zeros((d_model, N_EXPERTS), F32),
                            layer, batch, seq, is_moe)
        last = layer == depth - 1
        if is_moe:
            xt, h, route, expert_ids = merged
            xt = _moe(xt, h, route, expert_ids, mod, fgain, moe_w1_b, moe_w3_b, moe_w2_b, i * N_EXPERTS, seq,
                      last)
        else:
            xt = _ffn_dense(merged[0], mod, gffn, fgain, ffn_w1_b, ffn_w3_b, ffn_w2_b, i, seq, last)
    return xt.reshape(batch, seq, d_model)
```

```python
import functools

import jax
import jax.numpy as jnp
from jax import lax
from jax.experimental import pallas as pl
from jax.experimental.pallas import tpu as pltpu

F32 = jnp.float32
BF16 = jnp.bfloat16

EPS = 1e-6
NEG_INF = -1e30
ROPE_THETA = 10000.0
ROPE_DIM = 64
ADA_CHUNKS = 6

MLA_HEADS = 8
MLA_Q_RANK = 768
MLA_KV_RANK = 512
MLA_NOPE = 128
MLA_ROPE = ROPE_DIM
MLA_V = 128
MLA_QK = MLA_NOPE + MLA_ROPE

DIL_PAIRS = ((128, 1), (512, 4), (2048, 16))
DIL_HEADS = 8
DIL_HEAD_DIM = ROPE_DIM
DIL_GROUP_W = DIL_HEADS * DIL_HEAD_DIM
DIL_QBLOCK = 128

RET_HEADS = 8
RET_QK_DIM = ROPE_DIM
RET_V_DIM = 2 * RET_QK_DIM
RET_CHUNK = 256
RET_UNROLL = 4
RET_QK_W = RET_HEADS * RET_QK_DIM
RET_V_W = RET_HEADS * RET_V_DIM

N_EXPERTS = 8
TOP_K = 2
MOE_BLOCK = 256

LANES = 128
SUBLANES = 8
V7X_VMEM_LIMIT_BYTES = 56 * 1024 * 1024

ROW_TILE = 512
MERGE_TILE = 512
COMBINE_TILE = 256
MLA_Q_TILE = 512
MLA_HEADS_PER_STEP = 4
MLA_KEY_CHUNK = 1024


def _params(n_grid_dims):
    return pltpu.CompilerParams(
        dimension_semantics=("arbitrary",) * n_grid_dims,
        vmem_limit_bytes=V7X_VMEM_LIMIT_BYTES,
    )


def _resident(shape):
    zeros = (0,) * len(shape)
    return pl.BlockSpec(shape, lambda *_: zeros, pipeline_mode=pl.Buffered(1))


def _layer_resident(stacked, layer):
    idx = (layer,) + (0,) * (stacked.ndim - 1)
    return pl.BlockSpec((1,) + stacked.shape[1:], lambda *_: idx, pipeline_mode=pl.Buffered(1))


def _silu(v):
    return v * jax.nn.sigmoid(v)


def _norm_mod(x, gain, scale, shift):
    ms = jnp.mean(x * x, axis=-1, keepdims=True)
    y = x * lax.rsqrt(ms + EPS) * gain
    return y * (1.0 + scale) + shift


def _rms(v, gain):
    ms = jnp.mean(v * v, axis=-1, keepdims=True)
    return v * lax.rsqrt(ms + EPS) * gain


def _rope(t, cos_t, sin_t):
    lane = lax.broadcasted_iota(jnp.int32, (t.shape[0], LANES), 1)
    first_half = (lane % ROPE_DIM) < (ROPE_DIM // 2)
    out = []
    for j in range(t.shape[1] // LANES):
        c = t[:, j * LANES:(j + 1) * LANES]
        rot = jnp.where(first_half,
                        pltpu.roll(c, LANES - ROPE_DIM // 2, 1),
                        pltpu.roll(c, ROPE_DIM // 2, 1))
        out.append(c * cos_t + rot * sin_t)
    return out[0] if len(out) == 1 else jnp.concatenate(out, axis=1)


def _dot(a, b):
    return jnp.dot(a, b, preferred_element_type=F32)


def _dot_nt(a, b):
    return lax.dot_general(a, b, (((1,), (1,)), ((), ())), preferred_element_type=F32)


def _dot_tn(a, b):
    return lax.dot_general(a, b, (((0,), (0,)), ((), ())), preferred_element_type=F32)


def _ada_kernel(c_ref, w_ref, b_ref, o_ref):
    o_ref[0] = jnp.dot(_silu(c_ref[...]), w_ref[0], preferred_element_type=F32,
                       precision=lax.Precision.HIGHEST) + b_ref[0]


def _ada(c, ada_w, ada_b):
    depth, d_model, n = ada_w.shape
    batch = c.shape[0]
    tn = n // 4
    out = pl.pallas_call(
        _ada_kernel,
        grid=(depth, n // tn),
        in_specs=[
            pl.BlockSpec((batch, d_model), lambda l, j: (0, 0)),
            pl.BlockSpec((1, d_model, tn), lambda l, j: (l, 0, j)),
            pl.BlockSpec((1, 1, tn), lambda l, j: (l, 0, j)),
        ],
        out_specs=pl.BlockSpec((1, batch, tn), lambda l, j: (l, 0, j)),
        out_shape=jax.ShapeDtypeStruct((depth, batch, n), F32),
        compiler_params=_params(2),
        name="ada",
    )(c, ada_w, ada_b.reshape(depth, 1, n))
    return out.reshape(depth, batch, ADA_CHUNKS, d_model)


def _mla_prep_kernel(x_ref, mod_ref, g_ref, wa_ref, qn_ref, kvn_ref, wuq_ref, wukv_ref, cos_ref, sin_ref,
                     q_ref, k_ref, v_ref):
    mod = mod_ref[0]
    h = _norm_mod(x_ref[...], g_ref[...], mod[1:2], mod[0:1]).astype(BF16)
    p = _dot(h, wa_ref[0])
    cq = _rms(p[:, :MLA_Q_RANK], qn_ref[...]).astype(BF16)
    ckv = _rms(p[:, MLA_Q_RANK:MLA_Q_RANK + MLA_KV_RANK], kvn_ref[...]).astype(BF16)
    cos_t = cos_ref[0]
    sin_t = sin_ref[0]
    kr = _rope(p[:, MLA_Q_RANK + MLA_KV_RANK:], cos_t, sin_t)
    q = _dot(cq, wuq_ref[0])
    kv = _dot(ckv, wukv_ref[0])
    nope_w = MLA_HEADS * MLA_NOPE
    qr = _rope(q[:, nope_w:], cos_t, sin_t)
    scale = MLA_QK ** -0.5
    kr_b = kr[:, :MLA_ROPE].astype(BF16)
    for hd in range(MLA_HEADS):
        q_ref[0, hd, :, :MLA_NOPE] = (q[:, hd * MLA_NOPE:(hd + 1) * MLA_NOPE] * scale).astype(BF16)
        q_ref[0, hd, :, MLA_NOPE:] = (qr[:, hd * MLA_ROPE:(hd + 1) * MLA_ROPE] * scale).astype(BF16)
        base = hd * (MLA_NOPE + MLA_V)
        k_ref[0, hd, :, :MLA_NOPE] = kv[:, base:base + MLA_NOPE].astype(BF16)
        k_ref[0, hd, :, MLA_NOPE:] = kr_b
        v_ref[0, hd] = kv[:, base + MLA_NOPE:base + MLA_NOPE + MLA_V].astype(BF16)


def _mla_prep(x, mod, gain, wa, qn, kvn, wuq, wukv, cos_t, sin_t, layer, batch, seq):
    t, d_model = x.shape
    tm = ROW_TILE
    per_b = seq // tm
    row = lambda i: (i, 0)
    hs = lambda i: (i // per_b, 0, i % per_b, 0)
    return pl.pallas_call(
        _mla_prep_kernel,
        grid=(t // tm,),
        in_specs=[
            pl.BlockSpec((tm, d_model), row),
            pl.BlockSpec((1, ADA_CHUNKS, d_model), lambda i: (i // per_b, 0, 0)),
            _resident((1, d_model)),
            _layer_resident(wa, layer),
            _resident((1, MLA_Q_RANK)),
            _resident((1, MLA_KV_RANK)),
            _layer_resident(wuq, layer),
            _layer_resident(wukv, layer),
            pl.BlockSpec((1, tm, LANES), lambda i: (i // per_b, i % per_b, 0)),
            pl.BlockSpec((1, tm, LANES), lambda i: (i // per_b, i % per_b, 0)),
        ],
        out_specs=[
            pl.BlockSpec((1, MLA_HEADS, tm, MLA_QK), hs),
            pl.BlockSpec((1, MLA_HEADS, tm, MLA_QK), hs),
            pl.BlockSpec((1, MLA_HEADS, tm, MLA_V), hs),
        ],
        out_shape=[
            jax.ShapeDtypeStruct((batch, MLA_HEADS, seq, MLA_QK), BF16),
            jax.ShapeDtypeStruct((batch, MLA_HEADS, seq, MLA_QK), BF16),
            jax.ShapeDtypeStruct((batch, MLA_HEADS, seq, MLA_V), BF16),
        ],
        compiler_params=_params(1),
        name="mla_prep",
    )(x, mod, gain, wa, qn, kvn, wuq, wukv, cos_t, sin_t)


def _mla_attn_kernel(q_ref, k_ref, v_ref, o_ref, *, heads_per_step, key_chunk):
    seq = k_ref.shape[2]
    ones = jnp.ones((key_chunk, LANES), BF16)
    for hd in range(heads_per_step):
        q = q_ref[0, hd]
        m = acc = None
        for c in range(seq // key_chunk):
            rows = slice(c * key_chunk, (c + 1) * key_chunk)
            s = _dot_nt(q, k_ref[0, hd, rows, :])
            v_ext = jnp.concatenate([v_ref[0, hd, rows, :], ones], axis=1)
            m_c = jnp.max(s, axis=-1, keepdims=True)
            if c == 0:
                m = m_c
                acc = _dot(jnp.exp(s - m).astype(BF16), v_ext)
            else:
                m_new = jnp.maximum(m, m_c)
                acc = jnp.exp(m - m_new) * acc + _dot(jnp.exp(s - m_new).astype(BF16), v_ext)
                m = m_new
        o_ref[:, hd * MLA_V:(hd + 1) * MLA_V] = (acc[:, :MLA_V] / acc[:, MLA_V:MLA_V + 1]).astype(o_ref.dtype)


def _mla_attn(q, k, v):
    batch, heads, seq, _ = q.shape
    tq = MLA_Q_TILE
    nq = seq // tq
    hps = MLA_HEADS_PER_STEP
    return pl.pallas_call(
        functools.partial(_mla_attn_kernel, heads_per_step=hps, key_chunk=MLA_KEY_CHUNK),
        grid=(batch, heads // hps, nq),
        in_specs=[
            pl.BlockSpec((1, hps, tq, MLA_QK), lambda b, h, i: (b, h, i, 0)),
            pl.BlockSpec((1, hps, seq, MLA_QK), lambda b, h, i: (b, h, 0, 0)),
            pl.BlockSpec((1, hps, seq, MLA_V), lambda b, h, i: (b, h, 0, 0)),
        ],
        out_specs=pl.BlockSpec((tq, hps * MLA_V), lambda b, h, i: (b * nq + i, h)),
        out_shape=jax.ShapeDtypeStruct((batch * seq, heads * MLA_V), BF16),
        compiler_params=_params(3),
        name="mla_attn",
    )(q, k, v)


def _dil_prep_kernel(x_ref, mod_ref, g_ref, w_ref, cos_ref, sin_ref, *refs):
    out_refs = refs[:3 * len(DIL_PAIRS)]
    h_scr = refs[3 * len(DIL_PAIRS)]
    mod = mod_ref[0]
    h32 = _norm_mod(x_ref[...], g_ref[...], mod[1:2], mod[0:1])
    n_lane_blocks = h32.shape[1] // LANES
    for j in range(n_lane_blocks):
        h_scr[j] = h32[:, j * LANES:(j + 1) * LANES]
    tm = h32.shape[0]
    gw = DIL_GROUP_W
    for g, (_, dil) in enumerate(DIL_PAIRS):
        n = tm // dil
        if dil == 1:
            hp, cos_t, sin_t = h32, cos_ref[0], sin_ref[0]
        else:
            hp = jnp.concatenate(
                [jnp.concatenate([h_scr[j, pl.ds(r, n, stride=dil), :] for j in range(n_lane_blocks)], axis=1)
                 for r in range(dil)], axis=0)
            cos_t = jnp.concatenate([cos_ref[0, pl.ds(r, n, stride=dil), :] for r in range(dil)], axis=0)
            sin_t = jnp.concatenate([sin_ref[0, pl.ds(r, n, stride=dil), :] for r in range(dil)], axis=0)
        pg = _dot(hp.astype(BF16), w_ref[0, :, g * 3 * gw:(g + 1) * 3 * gw])
        qg = (_rope(pg[:, :gw], cos_t, sin_t) * (DIL_HEAD_DIM ** -0.5)).astype(BF16)
        kg = _rope(pg[:, gw:2 * gw], cos_t, sin_t).astype(BF16)
        vg = pg[:, 2 * gw:].astype(BF16)
        q_ref, k_ref, v_ref = out_refs[3 * g:3 * g + 3]
        for r in range(dil):
            q_ref[0, r] = qg[r * n:(r + 1) * n]
            k_ref[0, r] = kg[r * n:(r + 1) * n]
            v_ref[0, r] = vg[r * n:(r + 1) * n]


def _dil_prep(x, mod, gain, w, cos_t, sin_t, layer, batch, seq):
    t, d_model = x.shape
    tm = ROW_TILE
    per_b = seq // tm
    out_specs, out_shape = [], []
    for _, dil in DIL_PAIRS:
        for _ in range(3):
            out_specs.append(pl.BlockSpec((1, dil, tm // dil, DIL_GROUP_W),
                                          lambda i: (i // per_b, 0, i % per_b, 0)))
            out_shape.append(jax.ShapeDtypeStruct((batch, dil, seq // dil, DIL_GROUP_W), BF16))
    return pl.pallas_call(
        _dil_prep_kernel,
        grid=(t // tm,),
        in_specs=[
            pl.BlockSpec((tm, d_model), lambda i: (i, 0)),
            pl.BlockSpec((1, ADA_CHUNKS, d_model), lambda i: (i // per_b, 0, 0)),
            _resident((1, d_model)),
            _layer_resident(w, layer),
            pl.BlockSpec((1, tm, LANES), lambda i: (i // per_b, i % per_b, 0)),
            pl.BlockSpec((1, tm, LANES), lambda i: (i // per_b, i % per_b, 0)),
        ],
        out_specs=out_specs,
        out_shape=out_shape,
        scratch_shapes=[pltpu.VMEM((d_model // LANES, tm, LANES), F32)],
        compiler_params=_params(1),
        name="dil_prep",
    )(x, mod, gain, w, cos_t, sin_t)


def _dil_attn_kernel(q_ref, k_ref, v_ref, o_ref, lse_ref, *, dil, segment, band):
    qb_rows = DIL_QBLOCK
    kw = 2 * qb_rows
    short = segment < kw
    per_window = kw // segment if short else 1
    nb = 1 if short else segment // qb_rows
    low_k = lax.broadcasted_iota(jnp.int32, (kw, LANES), 1).astype(F32).astype(BF16) < DIL_HEAD_DIM
    low_q = lax.broadcasted_iota(jnp.int32, (qb_rows, LANES), 1) < DIL_HEAD_DIM
    row_i = lax.broadcasted_iota(jnp.int32, (qb_rows, kw), 0)
    col_i = lax.broadcasted_iota(jnp.int32, (qb_rows, kw), 1)
    zeros_k = jnp.zeros((kw, LANES), BF16)
    ones_k = jnp.ones((kw, LANES), BF16)
    denom_cols = jnp.concatenate([jnp.where(low_k, ones_k, zeros_k), jnp.where(low_k, zeros_k, ones_k)], axis=0)

    def tile(idx, carry):
        r = idx // nb
        if short:
            q0 = 0
            r0 = (r // per_window) * per_window
            qt = q_ref[0, r]
            kt = jnp.concatenate([k_ref[0, r0 + j] for j in range(per_window)], axis=0)
            vt = jnp.concatenate([v_ref[0, r0 + j] for j in range(per_window)], axis=0)
            jq = (r - r0) * segment + row_i
            jk = col_i
        else:
            q0 = pl.multiple_of((idx % nb) * qb_rows, qb_rows)
            ks = pl.multiple_of(jnp.clip(q0 - band, 0, segment - kw), band)
            qt = q_ref[0, r, pl.ds(q0, qb_rows), :]
            kt = k_ref[0, r, pl.ds(ks, kw), :]
            vt = v_ref[0, r, pl.ds(ks, kw), :]
            jq = q0 + row_i
            jk = ks + col_i
        valid = jnp.logical_and(jnp.abs(jq - jk) <= band, jq // segment == jk // segment)
        for hp in range(DIL_GROUP_W // LANES):
            cols = slice(hp * LANES, (hp + 1) * LANES)
            qp, kp, vp = qt[:, cols], kt[:, cols], vt[:, cols]
            k2 = jnp.concatenate([jnp.where(low_k, kp, zeros_k), jnp.where(low_k, zeros_k, kp)], axis=0)
            v2 = jnp.concatenate([jnp.where(low_k, vp, zeros_k), jnp.where(low_k, zeros_k, vp)], axis=0)
            s = _dot_nt(qp, k2)
            s0 = jnp.where(valid, s[:, :kw], NEG_INF)
            s1 = jnp.where(valid, s[:, kw:], NEG_INF)
            m0 = jnp.max(s0, axis=-1, keepdims=True)
            m1 = jnp.max(s1, axis=-1, keepdims=True)
            p = jnp.concatenate([jnp.exp(s0 - m0), jnp.exp(s1 - m1)], axis=1).astype(BF16)
            pv = _dot(p, jnp.concatenate([v2, denom_cols], axis=1))
            den = pv[:, LANES:]
            o_ref[0, r, pl.ds(q0, qb_rows), cols] = (pv[:, :LANES] / den).astype(o_ref.dtype)
            lse_ref[0, r, pl.ds(q0, qb_rows), cols] = jnp.where(low_q, m0, m1) + jnp.log(den)
        return carry

    lax.fori_loop(0, dil * nb, tile, 0, unroll=8)


def _dil_attn(q, k, v, window, dil):
    batch, _, seg, gw = q.shape
    band = window // (2 * dil)
    assert band * 2 == DIL_QBLOCK and seg % DIL_QBLOCK == 0
    assert seg >= 2 * DIL_QBLOCK or (seg == DIL_QBLOCK and dil % 2 == 0)
    spec = pl.BlockSpec((1, dil, seg, gw), lambda b: (b, 0, 0, 0))
    return pl.pallas_call(
        functools.partial(_dil_attn_kernel, dil=dil, segment=seg, band=band),
        grid=(batch,),
        in_specs=[spec, spec, spec],
        out_specs=[spec, spec],
        out_shape=[jax.ShapeDtypeStruct(q.shape, BF16), jax.ShapeDtypeStruct(q.shape, F32)],
        compiler_params=_params(1),
        name=f"dil_attn_d{dil}",
    )(q, k, v)


def _ret_prep_kernel(x_ref, mod_ref, g_ref, w_ref, cos_ref, sin_ref, q_ref, k_ref, v_ref):
    mod = mod_ref[0]
    h = _norm_mod(x_ref[...], g_ref[...], mod[1:2], mod[0:1]).astype(BF16)
    p = _dot(h, w_ref[0])
    cos_t, sin_t = cos_ref[0], sin_ref[0]
    q_ref[...] = _rope(p[:, :RET_QK_W], cos_t, sin_t).astype(BF16)
    k_ref[...] = (_rope(p[:, RET_QK_W:2 * RET_QK_W], cos_t, sin_t) * (RET_QK_DIM ** -0.5)).astype(BF16)
    v_ref[...] = p[:, 2 * RET_QK_W:].astype(BF16)


def _ret_prep(x, mod, gain, w, cos_t, sin_t, layer, seq):
    t, d_model = x.shape
    tm = ROW_TILE
    per_b = seq // tm
    row = lambda i: (i, 0)
    return pl.pallas_call(
        _ret_prep_kernel,
        grid=(t // tm,),
        in_specs=[
            pl.BlockSpec((tm, d_model), row),
            pl.BlockSpec((1, ADA_CHUNKS, d_model), lambda i: (i // per_b, 0, 0)),
            _resident((1, d_model)),
            _layer_resident(w, layer),
            pl.BlockSpec((1, tm, LANES), lambda i: (i // per_b, i % per_b, 0)),
            pl.BlockSpec((1, tm, LANES), lambda i: (i // per_b, i % per_b, 0)),
        ],
        out_specs=[pl.BlockSpec((tm, RET_QK_W), row), pl.BlockSpec((tm, RET_QK_W), row),
                   pl.BlockSpec((tm, RET_V_W), row)],
        out_shape=[jax.ShapeDtypeStruct((t, RET_QK_W), BF16), jax.ShapeDtypeStruct((t, RET_QK_W), BF16),
                   jax.ShapeDtypeStruct((t, RET_V_W), BF16)],
        compiler_params=_params(1),
        name="ret_prep",
    )(x, mod, gain, w, cos_t, sin_t)


def _ret_scan_kernel(ld_ref, q_ref, k_ref, v_ref, gn_ref, o_ref,
                     yf_ref, st_ref, intra_ref, qdec_ref, kdec_ref, cdec_ref):
    c = RET_CHUNK
    seq = q_ref.shape[0]
    nc = seq // c
    n_pairs = RET_HEADS // 2
    pair_v = 2 * RET_V_DIM
    ia = lax.broadcasted_iota(jnp.int32, (c, c), 0)
    ib = lax.broadcasted_iota(jnp.int32, (c, c), 1)
    diff = (ia - ib).astype(F32)
    idx_q = lax.broadcasted_iota(jnp.int32, (c, pair_v), 0).astype(F32)
    lane_q = lax.broadcasted_iota(jnp.int32, (c, pair_v), 1)
    idx_k = lax.broadcasted_iota(jnp.int32, (c, LANES), 0).astype(F32)
    lane_k = lax.broadcasted_iota(jnp.int32, (c, LANES), 1)
    low_k = lane_k < RET_QK_DIM
    low_k16 = lane_k.astype(F32).astype(BF16) < RET_QK_DIM
    low_v16 = lane_q.astype(F32).astype(BF16) < RET_V_DIM
    srow =lax.broadcasted_iota(jnp.int32, (LANES, pair_v), 0)
    scol = lax.broadcasted_iota(jnp.int32, (LANES, pair_v), 1)
    blk0 = jnp.logical_and(srow < RET_QK_DIM, scol < RET_V_DIM)
    blk1 = jnp.logical_and(srow >= RET_QK_DIM, scol >= RET_V_DIM)
    diag = jnp.where(jnp.logical_or(blk0, blk1), 1.0, 0.0).astype(F32)

    for dr in range(2):
        for hd in range(RET_HEADS):
            lg = ld_ref[dr, hd]
            if dr == 0:
                mask = diff >= 0
                dist = jnp.where(mask, diff, 0.0)
            else:
                mask = diff < 0
                dist = jnp.where(mask, -diff, 0.0)
            intra_ref[dr, hd] = jnp.where(mask, jnp.exp(lg * dist), 0.0)
        for hp in range(n_pairs):
            lg0 = ld_ref[dr, 2 * hp]
            lg1 = ld_ref[dr, 2 * hp + 1]
            q_exp = idx_q + 1.0 if dr == 0 else c - idx_q
            k_exp = c - 1.0 - idx_k if dr == 0 else idx_k
            qdec_ref[dr, hp] = jnp.exp(jnp.where(lane_q < RET_V_DIM, lg0, lg1) * q_exp)
            kdec_ref[dr, hp] = jnp.exp(jnp.where(low_k, lg0, lg1) * k_exp)
            cdec_ref[dr, hp] = jnp.where(blk0, jnp.exp(lg0 * c), jnp.where(blk1, jnp.exp(lg1 * c), 0.0))

    def chunk(dr, n):
        r0 = pl.multiple_of(n * c, c)
        qc = q_ref[pl.ds(r0, c), :]
        kc = k_ref[pl.ds(r0, c), :]
        vc = v_ref[pl.ds(r0, c), :]
        ys = []
        for hp in range(n_pairs):
            qp = qc[:, hp * LANES:(hp + 1) * LANES]
            kp = kc[:, hp * LANES:(hp + 1) * LANES]
            vp = vc[:, hp * pair_v:(hp + 1) * pair_v]
            zk = jnp.zeros_like(kp)
            zv = jnp.zeros_like(vp)
            k2 = jnp.concatenate([jnp.where(low_k16, kp, zk), jnp.where(low_k16, zk, kp)], axis=0)
            v2 = jnp.concatenate([jnp.where(low_v16, vp, zv), jnp.where(low_v16, zv, vp)], axis=0)
            s = _dot_nt(qp, k2) * jnp.concatenate([intra_ref[dr, 2 * hp], intra_ref[dr, 2 * hp + 1]], axis=1)
            y_intra = _dot(s.astype(BF16), v2)
            state = st_ref[hp]
            y_inter = _dot(qp, state.astype(BF16)) * qdec_ref[dr, hp]
            ys.append(y_intra + y_inter)
            kd = (kp.astype(F32) * kdec_ref[dr, hp]).astype(BF16)
            st_ref[hp] = cdec_ref[dr, hp] * state + diag * _dot_tn(kd, vp)
        return r0, ys

    st_ref[...] = jnp.zeros_like(st_ref)

    def fwd(n, carry):
        r0, ys = chunk(0, n)
        for hp in range(n_pairs):
            yf_ref[pl.ds(r0, c), hp * pair_v:(hp + 1) * pair_v] = ys[hp]
        return carry

    lax.fori_loop(0, nc, fwd, 0, unroll=RET_UNROLL)
    st_ref[...] = jnp.zeros_like(st_ref)

    def bwd(i, carry):
        r0, ys = chunk(1, nc - 1 - i)
        for hp in range(n_pairs):
            y = ys[hp] + yf_ref[pl.ds(r0, c), hp * pair_v:(hp + 1) * pair_v]
            for half in range(2):
                cols = slice(hp * pair_v + half * RET_V_DIM, hp * pair_v + (half + 1) * RET_V_DIM)
                yh = y[:, half * RET_V_DIM:(half + 1) * RET_V_DIM]
                mu = jnp.mean(yh, axis=-1, keepdims=True)
                dev = yh - mu
                var = jnp.mean(dev * dev, axis=-1, keepdims=True)
                o_ref[pl.ds(r0, c), cols] = (dev * lax.rsqrt(var + EPS) * gn_ref[:, cols]).astype(o_ref.dtype)
        return carry

    lax.fori_loop(0, nc, bwd, 0, unroll=RET_UNROLL)


def _ret_scan(log_decay, q, k, v, ret_norm, batch, seq):
    c = RET_CHUNK
    n_pairs = RET_HEADS // 2
    return pl.pallas_call(
        _ret_scan_kernel,
        grid=(batch,),
        in_specs=[
            pl.BlockSpec(memory_space=pltpu.SMEM),
            pl.BlockSpec((seq, RET_QK_W), lambda b: (b, 0)),
            pl.BlockSpec((seq, RET_QK_W), lambda b: (b, 0)),
            pl.BlockSpec((seq, RET_V_W), lambda b: (b, 0)),
            _resident((1, RET_V_W)),
        ],
        out_specs=pl.BlockSpec((seq, RET_V_W), lambda b: (b, 0)),
        out_shape=jax.ShapeDtypeStruct((batch * seq, RET_V_W), BF16),
        scratch_shapes=[
            pltpu.VMEM((seq, RET_V_W), F32),
            pltpu.VMEM((n_pairs, LANES, 2 * RET_V_DIM), F32),
            pltpu.VMEM((2, RET_HEADS, c, c), F32),
            pltpu.VMEM((2, n_pairs, c, 2 * RET_V_DIM), F32),
            pltpu.VMEM((2, n_pairs, c, LANES), F32),
            pltpu.VMEM((2, n_pairs, LANES, 2 * RET_V_DIM), F32),
        ],
        compiler_params=_params(1),
        name="ret_scan",
    )(log_decay, q, k, v, ret_norm)


def _top2_route(h, w_hi_lo):
    h_hi = h.astype(BF16)
    h_lo = (h - h_hi.astype(F32)).astype(BF16)
    hh = _dot(h_hi, w_hi_lo)
    logits = hh[:, :LANES] + (hh[:, LANES:] + _dot(h_lo, w_hi_lo[:, :LANES]))
    lane = lax.broadcasted_iota(jnp.int32, logits.shape, 1).astype(F32)
    lg = jnp.where(lane < N_EXPERTS, logits, -jnp.inf)
    m1 = jnp.max(lg, axis=-1, keepdims=True)
    i1 = jnp.min(jnp.where(lg == m1, lane, float(LANES)), axis=-1, keepdims=True)
    lg2 = jnp.where(lane == i1, -jnp.inf, lg)
    m2 = jnp.max(lg2, axis=-1, keepdims=True)
    i2 = jnp.min(jnp.where(lg2 == m2, lane, float(LANES)), axis=-1, keepdims=True)
    e = jnp.exp(m2 - m1)
    w1 = 1.0 / (1.0 + e)
    w2 = e / (1.0 + e)
    return jnp.where(lane == 0, i1, jnp.where(lane == 1, i2, jnp.where(lane == 2, w1, jnp.where(lane == 3, w2, 0.0))))


def _merge_kernel(x_ref, mod_ref, g_ref, wg_ref, ymla_ref, yret_ref,
                  o0_ref, l0_ref, o1_ref, l1_ref, o2_ref, l2_ref,
                  wbm_ref, wbd_ref, wbr_ref, wout_ref, gf_ref, wr_ref, out_ref, *rest, route):
    nat_o, nat_l = rest[-2:]
    x = x_ref[...]
    mod = mod_ref[0]
    h = _norm_mod(x, g_ref[...], mod[1:2], mod[0:1]).astype(BF16)
    gates = _dot(h, wg_ref[0])
    d_model = x.shape[1]
    tm = x.shape[0]

    o_nat = [o0_ref[0, 0].astype(F32)]
    l_nat = [l0_ref[0, 0]]
    for g, (o_ref, l_ref) in enumerate(((o1_ref, l1_ref), (o2_ref, l2_ref))):
        dil = DIL_PAIRS[g + 1][1]
        n = tm // dil
        n_lane_blocks = DIL_GROUP_W // LANES
        for r in range(dil):
            o_r = o_ref[0, r].astype(F32)
            l_r = l_ref[0, r]
            for j in range(n_lane_blocks):
                nat_o[g, j, pl.ds(r, n, stride=dil), :] = o_r[:, j * LANES:(j + 1) * LANES]
                nat_l[g, j, pl.ds(r, n, stride=dil), :] = l_r[:, j * LANES:(j + 1) * LANES]
        o_nat.append(jnp.concatenate([nat_o[g, j] for j in range(n_lane_blocks)], axis=1))
        l_nat.append(jnp.concatenate([nat_l[g, j] for j in range(n_lane_blocks)], axis=1))
    m = jnp.maximum(jnp.maximum(l_nat[0], l_nat[1]), l_nat[2])
    ws = [jnp.exp(l - m) for l in l_nat]
    y_dil = (ws[0] * o_nat[0] + ws[1] * o_nat[1] + ws[2] * o_nat[2]) / (ws[0] + ws[1] + ws[2])

    y_ret = (_silu(gates[:, :RET_V_W]) * yret_ref[...].astype(F32)).astype(BF16)
    ga = gates[:, RET_V_W:RET_V_W + d_model]
    gb = gates[:, RET_V_W + d_model:RET_V_W + 2 * d_model]
    gc = gates[:, RET_V_W + 2 * d_model:]
    merged = (jax.nn.sigmoid(ga) * _dot(ymla_ref[...], wbm_ref[0])
              + jax.nn.sigmoid(gb) * _dot(y_dil.astype(BF16), wbd_ref[0])
              + jax.nn.sigmoid(gc) * _dot(y_ret, wbr_ref[0]))
    x_new = x + mod[2:3] * _dot(merged.astype(BF16), wout_ref[0])
    out_ref[...] = x_new
    if route:
        h_ref, route_ref, ids_ref = rest[:3]
        h2 = _norm_mod(x_new, gf_ref[...], mod[4:5], mod[3:4])
        h_ref[...] = h2
        r = _top2_route(h2, wr_ref[...])
        route_ref[...] = r
        ids_ref[...] = r.T[:ids_ref.shape[0]]


def _merge_out(x, mod, gain, wg, y_mla, y_ret, dil_outs, wbm, wbd, wbr, wout, gain_ffn, w_router,
               layer, batch, seq, route):
    t, d_model = x.shape
    tm = MERGE_TILE
    per_b = seq // tm
    row = lambda i: (i, 0)
    dil_specs, dil_args = [], []
    for (_, dil), (o, lse) in zip(DIL_PAIRS, dil_outs):
        spec = pl.BlockSpec((1, dil, tm // dil, DIL_GROUP_W), lambda i: (i // per_b, 0, i % per_b, 0))
        dil_specs += [spec, spec]
        dil_args += [o, lse]
    out_specs = [pl.BlockSpec((tm, d_model), row)]
    out_shape = [jax.ShapeDtypeStruct((t, d_model), F32)]
    if route:
        out_specs += [pl.BlockSpec((tm, d_model), row), pl.BlockSpec((tm, LANES), row),
                      pl.BlockSpec((SUBLANES, tm), lambda i: (0, i))]
        out_shape += [jax.ShapeDtypeStruct((t, d_model), F32), jax.ShapeDtypeStruct((t, LANES), F32),
                      jax.ShapeDtypeStruct((SUBLANES, t), F32)]
    wr32 = jnp.zeros((d_model, LANES), F32).at[:, :N_EXPERTS].set(w_router)
    wr_hi = wr32.astype(BF16)
    wr = jnp.concatenate([wr_hi, (wr32 - wr_hi.astype(F32)).astype(BF16)], axis=1)
    return pl.pallas_call(
        functools.partial(_merge_kernel, route=route),
        grid=(t // tm,),
        in_specs=[
            pl.BlockSpec((tm, d_model), row),
            pl.BlockSpec((1, ADA_CHUNKS, d_model), lambda i: (i // per_b, 0, 0)),
            _resident((1, d_model)),
            _layer_resident(wg, layer),
            pl.BlockSpec((tm, y_mla.shape[1]), row),
            pl.BlockSpec((tm, RET_V_W), row),
            *dil_specs,
            _layer_resident(wbm, layer), _layer_resident(wbd, layer), _layer_resident(wbr, layer),
            _layer_resident(wout, layer),
            _resident((1, d_model)),
            _resident(wr.shape),
        ],
        out_specs=out_specs,
        out_shape=out_shape,
        scratch_shapes=[pltpu.VMEM((2, DIL_GROUP_W // LANES, tm, LANES), F32),
                        pltpu.VMEM((2, DIL_GROUP_W // LANES, tm, LANES), F32)],
        compiler_params=_params(1),
        name="merge_out",
    )(x, mod, gain, wg, y_mla, y_ret, *dil_args, wbm, wbd, wbr, wout, gain_ffn, wr)


def _residual_out(y, fg_ref, final):
    return _rms(y, fg_ref[...]) if final else y


def _ffn_kernel(x_ref, mod_ref, g_ref, fg_ref, w1_ref, w3_ref, w2_ref, out_ref, *, n_chunks, final):
    x = x_ref[...]
    mod = mod_ref[0]
    h = _norm_mod(x, g_ref[...], mod[4:5], mod[3:4]).astype(BF16)
    d_ff = w1_ref.shape[2]
    fc = d_ff // n_chunks
    acc = jnp.zeros(x.shape, F32)
    for j in range(n_chunks):
        a = _dot(h, w1_ref[0, :, j * fc:(j + 1) * fc])
        b = _dot(h, w3_ref[0, :, j * fc:(j + 1) * fc])
        acc = acc + _dot((_silu(a) * b).astype(BF16), w2_ref[0, j * fc:(j + 1) * fc, :])
    out_ref[...] = _residual_out(x + mod[5:6] * acc, fg_ref, final)


def _ffn_dense(x, mod, gain, final_gain, w1, w3, w2, layer, seq, final):
    t, d_model = x.shape
    tm = ROW_TILE
    per_b = seq // tm
    row = lambda i: (i, 0)
    return pl.pallas_call(
        functools.partial(_ffn_kernel, n_chunks=1, final=final),
        grid=(t // tm,),
        in_specs=[
            pl.BlockSpec((tm, d_model), row),
            pl.BlockSpec((1, ADA_CHUNKS, d_model), lambda i: (i // per_b, 0, 0)),
            _resident((1, d_model)),
            _resident((1, d_model)),
            _layer_resident(w1, layer), _layer_resident(w3, layer), _layer_resident(w2, layer),
        ],
        out_specs=pl.BlockSpec((tm, d_model), row),
        out_shape=jax.ShapeDtypeStruct((t, d_model), F32),
        compiler_params=_params(1),
        name="ffn_dense",
    )(x, mod, gain, final_gain, w1, w3, w2)


def _row_copy(src_hbm, dst, src_row, dst_row, sem):
    return pltpu.make_async_copy(src_hbm.at[pl.ds(src_row, 1), :], dst.at[pl.ds(dst_row, 1), :], sem)


def _expert_kernel(be_ref, nact_ref, tok_ref, tok_next_ref, h_hbm, w1_ref, w3_ref, w2_ref, y_ref,
                   xbuf, sems, *, n_chunks):
    i = pl.program_id(0)
    nact = nact_ref[0]
    bm = y_ref.shape[0]
    slot = i % 2

    def wait_block(s):
        pltpu.make_async_copy(h_hbm.at[pl.ds(0, bm), :], xbuf.at[s], sems.at[s]).wait()

    @pl.when(i == 0)
    def _():
        def issue(r, carry):
            _row_copy(h_hbm, xbuf.at[0], tok_ref[0, 0, r], r, sems.at[0]).start()
            return carry

        lax.fori_loop(0, bm, issue, 0, unroll=8)

    @pl.when(i >= nact)
    def _():
        y_ref[...] = jnp.zeros_like(y_ref)

    @pl.when(i < nact)
    def _():
        wait_block(slot)
        xb = xbuf[slot].astype(BF16)
        for r in range(bm):
            _row_copy(h_hbm, xbuf.at[1 - slot], tok_next_ref[0, 0, r], r, sems.at[1 - slot]).start()
        d_ff = w1_ref.shape[2]
        fc = d_ff // n_chunks
        acc = jnp.zeros(y_ref.shape, F32)
        for j in range(n_chunks):
            a = _dot(xb, w1_ref[0, :, j * fc:(j + 1) * fc])
            b = _dot(xb, w3_ref[0, :, j * fc:(j + 1) * fc])
            acc = acc + _dot((_silu(a) * b).astype(BF16), w2_ref[0, j * fc:(j + 1) * fc, :])
        y_ref[...] = acc

    @pl.when(i == nact - 1)
    def _():
        wait_block(1 - slot)


def _moe_expert(block_expert, nact, row_tok, h, w1, w3, w2, expert_base):
    nb, _, bm = row_tok.shape
    d_model = h.shape[1]
    d_ff = w1.shape[2]
    wsel = lambda i, be, na: (expert_base + be[jnp.minimum(i, na[0] - 1)], 0, 0)
    return pl.pallas_call(
        functools.partial(_expert_kernel, n_chunks=2),
        grid_spec=pltpu.PrefetchScalarGridSpec(
            num_scalar_prefetch=2,
            grid=(nb,),
            in_specs=[
                pl.BlockSpec((1, 1, bm), lambda i, be, na: (i, 0, 0), memory_space=pltpu.SMEM),
                pl.BlockSpec((1, 1, bm), lambda i, be, na: (jnp.minimum(i + 1, nb - 1), 0, 0),
                             memory_space=pltpu.SMEM),
                pl.BlockSpec(memory_space=pl.ANY),
                pl.BlockSpec((1, d_model, d_ff), wsel, pipeline_mode=pl.Buffered(1)),
                pl.BlockSpec((1, d_model, d_ff), wsel, pipeline_mode=pl.Buffered(1)),
                pl.BlockSpec((1, d_ff, d_model), wsel, pipeline_mode=pl.Buffered(1)),
            ],
            out_specs=pl.BlockSpec((bm, d_model), lambda i, be, na: (i, 0)),
            scratch_shapes=[pltpu.VMEM((2, bm, d_model), F32), pltpu.SemaphoreType.DMA((2,))],
        ),
        out_shape=jax.ShapeDtypeStruct((nb * bm, d_model), F32),
        compiler_params=_params(1),
        name="moe_expert",
    )(block_expert, nact, row_tok, row_tok, h, w1, w3, w2)


def _combine_kernel(dest_ref, dest_next_ref, x_ref, mod_ref, route_ref, fg_ref, yb_hbm, out_ref, buf, sems,
                    *, final):
    tm = x_ref.shape[0]
    n_rows = TOP_K * tm
    i = pl.program_id(0)
    slot = i % 2

    def issue(rows_ref, s):
        def body(r, carry):
            _row_copy(yb_hbm, buf.at[s], rows_ref[0, 0, r], r, sems.at[s]).start()
            return carry

        lax.fori_loop(0, n_rows, body, 0, unroll=8)

    @pl.when(i == 0)
    def _():
        issue(dest_ref, 0)

    @pl.when(i + 1 < pl.num_programs(0))
    def _():
        for r in range(n_rows):
            _row_copy(yb_hbm, buf.at[1 - slot], dest_next_ref[0, 0, r], r, sems.at[1 - slot]).start()

    pltpu.make_async_copy(yb_hbm.at[pl.ds(0, n_rows), :], buf.at[slot], sems.at[slot]).wait()
    route = route_ref[...]
    y = route[:, 2:3] * buf[slot, :tm, :] + route[:, 3:4] * buf[slot, tm:, :]
    out_ref[...] = _residual_out(x_ref[...] + mod_ref[0][5:6] * y, fg_ref, final)


def _moe_combine(dest, x, mod, route, final_gain, yb, seq, final):
    t, d_model = x.shape
    tm = COMBINE_TILE
    per_b = seq // tm
    row = lambda i: (i, 0)
    return pl.pallas_call(
        functools.partial(_combine_kernel, final=final),
        grid=(t // tm,),
        in_specs=[
            pl.BlockSpec((1, 1, TOP_K * tm), lambda i: (i, 0, 0), memory_space=pltpu.SMEM),
            pl.BlockSpec((1, 1, TOP_K * tm), lambda i: (jnp.minimum(i + 1, t // tm - 1), 0, 0),
                         memory_space=pltpu.SMEM),
            pl.BlockSpec((tm, d_model), row),
            pl.BlockSpec((1, ADA_CHUNKS, d_model), lambda i: (i // per_b, 0, 0)),
            pl.BlockSpec((tm, LANES), row),
            _resident((1, d_model)),
            pl.BlockSpec(memory_space=pl.ANY),
        ],
        out_specs=pl.BlockSpec((tm, d_model), row),
        out_shape=jax.ShapeDtypeStruct((t, d_model), F32),
        scratch_shapes=[pltpu.VMEM((2, TOP_K * tm, d_model), F32), pltpu.SemaphoreType.DMA((2,))],
        compiler_params=_params(1),
        name="moe_combine",
    )(dest, dest, x, mod, route, final_gain, yb)


def _moe(x, h, route, expert_ids, mod, final_gain, w1, w3, w2, expert_base, seq, final):
    t, d_model = x.shape
    bm = MOE_BLOCK
    e_flat = expert_ids[:TOP_K].astype(jnp.int32).reshape(1, -1)
    n_assign = e_flat.shape[1]
    onehot = e_flat == jnp.arange(N_EXPERTS, dtype=jnp.int32)[:, None]
    csum = jnp.cumsum(onehot.astype(jnp.int32), axis=1)
    counts = csum[:, -1]
    padded = (counts + bm - 1) // bm * bm
    pends = jnp.cumsum(padded)
    row_of = (pends - padded)[:, None] + csum - 1
    dest = jnp.sum(jnp.where(onehot, row_of, 0), axis=0)
    nb = n_assign // bm + N_EXPERTS
    row_tok = jnp.zeros((nb * bm,), jnp.int32).at[dest].set(jnp.arange(n_assign, dtype=jnp.int32) % t)
    block_start = jnp.arange(nb, dtype=jnp.int32) * bm
    block_expert = jnp.minimum(
        jnp.sum((pends[None, :] <= block_start[:, None]).astype(jnp.int32), axis=1), N_EXPERTS - 1)
    nact = (pends[-1:] // bm).astype(jnp.int32)

    yb = _moe_expert(block_expert, nact, row_tok.reshape(nb, 1, bm), h, w1, w3, w2, expert_base)
    tmc = COMBINE_TILE
    dest_tiles = dest.reshape(TOP_K, t // tmc, tmc).transpose(1, 0, 2).reshape(t // tmc, 1, TOP_K * tmc)
    return _moe_combine(dest_tiles, x, mod, route, final_gain, yb, seq, final)


def _rope_tables(positions):
    inv_freq = ROPE_THETA ** (-jnp.arange(0, ROPE_DIM, 2, dtype=F32) / ROPE_DIM)
    ang = positions.astype(F32)[..., None] * inv_freq
    cos, sin = jnp.cos(ang), jnp.sin(ang)
    reps = LANES // ROPE_DIM
    cos_t = jnp.tile(jnp.concatenate([cos, cos], axis=-1), (1, 1, reps))
    sin_t = jnp.tile(jnp.concatenate([-sin, sin], axis=-1), (1, 1, reps))
    return cos_t, sin_t


def _split_w_in_kernel(w_ref, mla_ref, dil_ref, ret_ref, gate_ref):
    w = w_ref[0]
    d_model = gate_ref.shape[2] // 4
    n_mla = MLA_Q_RANK + MLA_KV_RANK
    kr = w[:, n_mla:n_mla + MLA_ROPE]
    mla_ref[0] = jnp.concatenate([w[:, :n_mla], kr, kr], axis=1).astype(BF16)
    tail = w[:, n_mla + MLA_ROPE:]
    dil_w = 3 * len(DIL_PAIRS) * DIL_GROUP_W
    gw = DIL_GROUP_W
    n_g = len(DIL_PAIRS)
    dil_ref[0] = jnp.concatenate(
        [tail[:, (m * n_g + g) * gw:(m * n_g + g + 1) * gw] for g in range(n_g) for m in range(3)],
        axis=1).astype(BF16)
    ret_w = 2 * RET_QK_W + RET_V_W
    ret_ref[0] = tail[:, dil_w:dil_w + ret_w].astype(BF16)
    gate_ref[0] = tail[:, dil_w + ret_w:dil_w + ret_w + RET_V_W + 3 * d_model].astype(BF16)


def _split_w_in(w_in):
    depth, d_model, n_cols = w_in.shape
    rb = 256
    widths = (MLA_Q_RANK + MLA_KV_RANK + 2 * MLA_ROPE, 3 * len(DIL_PAIRS) * DIL_GROUP_W,
              2 * RET_QK_W + RET_V_W, RET_V_W + 3 * d_model)
    assert sum(widths) - MLA_ROPE == n_cols
    return pl.pallas_call(
        _split_w_in_kernel,
        grid=(depth, d_model // rb),
        in_specs=[pl.BlockSpec((1, rb, n_cols), lambda l, i: (l, i, 0))],
        out_specs=[pl.BlockSpec((1, rb, n), lambda l, i: (l, i, 0)) for n in widths],
        out_shape=[jax.ShapeDtypeStruct((depth, d_model, n), BF16) for n in widths],
        compiler_params=_params(2),
        name="split_w_in",
    )(w_in)


def _split_w_uq(w_uq):
    depth = w_uq.shape[0]
    w = w_uq.astype(BF16).reshape(depth, MLA_Q_RANK, MLA_HEADS, MLA_QK)
    nope = w[..., :MLA_NOPE].reshape(depth, MLA_Q_RANK, MLA_HEADS * MLA_NOPE)
    rope = w[..., MLA_NOPE:].reshape(depth, MLA_Q_RANK, MLA_HEADS * MLA_ROPE)
    return jnp.concatenate([nope, rope], axis=-1)


def kernel(x, c, positions, ada_w, ada_b, norm_mix, norm_ffn, w_in, mla_q_norm, mla_w_uq, mla_kv_norm,
           mla_w_ukv, ret_log_decay, ret_norm, w_br_mla, w_br_dil, w_br_ret, w_out, ffn_w1, ffn_w3, ffn_w2,
           moe_router, moe_w1, moe_w3, moe_w2, final_norm):
    batch, seq, d_model = x.shape
    depth = ada_w.shape[0]
    cos_t, sin_t = _rope_tables(positions)
    mod_all = _ada(c, ada_w, ada_b)
    assert depth >= 1
    fgain = final_norm.reshape(1, d_model)
    xt = x.reshape(batch * seq, d_model)
    moe_w1_b = moe_w1.astype(BF16).reshape((-1,) + moe_w1.shape[2:])
    moe_w3_b = moe_w3.astype(BF16).reshape((-1,) + moe_w3.shape[2:])
    moe_w2_b = moe_w2.astype(BF16).reshape((-1,) + moe_w2.shape[2:])
    w_mla, w_dil, w_ret, w_gate = _split_w_in(w_in)
    w_uq = _split_w_uq(mla_w_uq)
    w_ukv = mla_w_ukv.astype(BF16)
    wbm, wbd, wbr, wout = (w.astype(BF16) for w in (w_br_mla, w_br_dil, w_br_ret, w_out))
    ffn_w1_b, ffn_w3_b, ffn_w2_b = (w.astype(BF16) for w in (ffn_w1, ffn_w3, ffn_w2))
    for layer in range(depth):
        mod = mod_all[layer]
        gmix = norm_mix[layer].reshape(1, d_model)

        q, k, v = _mla_prep(xt, mod, gmix, w_mla, mla_q_norm[layer].reshape(1, -1),
                            mla_kv_norm[layer].reshape(1, -1), w_uq, w_ukv, cos_t, sin_t, layer, batch, seq)
        y_mla = _mla_attn(q, k, v)

        dil_qkv = _dil_prep(xt, mod, gmix, w_dil, cos_t, sin_t, layer, batch, seq)
        dil_outs = [_dil_attn(*dil_qkv[3 * g:3 * g + 3], window, dil)
                    for g, (window, dil) in enumerate(DIL_PAIRS)]

        rq, rk, rv = _ret_prep(xt, mod, gmix, w_ret, cos_t, sin_t, layer, seq)
        y_ret = _ret_scan(ret_log_decay[layer].astype(F32), rq, rk, rv,
                          ret_norm[layer].reshape(1, -1).astype(F32), batch, seq)

        gffn = norm_ffn[layer].reshape(1, d_model)
        i = layer // 2
        is_moe = layer % 2 == 1
        merged = _merge_out(xt, mod, gmix, w_gate, y_mla, y_ret, dil_outs, wbm, wbd, wbr, wout, gffn,
                            moe_router[i] if is_moe else jnp.zeros((d_model, N_EXPERTS), F32),
                            layer, batch, seq, is_moe)
        last = layer == depth - 1
        if is_moe:
            xt, h, route, expert_ids = merged
            xt = _moe(xt, h, route, expert_ids, mod, fgain, moe_w1_b, moe_w3_b, moe_w2_b, i * N_EXPERTS, seq,
                      last)
        else:
            xt = _ffn_dense(merged[0], mod, gffn, fgain, ffn_w1_b, ffn_w3_b, ffn_w2_b, i, seq, last)
    return xt.reshape(batch, seq, d_model)
```

```python
import functools

import jax
import jax.numpy as jnp
from jax import lax
from jax.experimental import pallas as pl
from jax.experimental.pallas import tpu as pltpu

F32 = jnp.float32
BF16 = jnp.bfloat16

EPS = 1e-6
NEG_INF = -1e30
ROPE_THETA = 10000.0
ROPE_DIM = 64
ADA_CHUNKS = 6

MLA_HEADS = 8
MLA_Q_RANK = 768
MLA_KV_RANK = 512
MLA_NOPE = 128
MLA_ROPE = ROPE_DIM
MLA_V = 128
MLA_QK = MLA_NOPE + MLA_ROPE

DIL_PAIRS = ((128, 1), (512, 4), (2048, 16))
DIL_HEADS = 8
DIL_HEAD_DIM = ROPE_DIM
DIL_GROUP_W = DIL_HEADS * DIL_HEAD_DIM
DIL_QBLOCK = 128

RET_HEADS = 8
RET_QK_DIM = ROPE_DIM
RET_V_DIM = 2 * RET_QK_DIM
RET_CHUNK = 256
RET_UNROLL = 4
RET_QK_W = RET_HEADS * RET_QK_DIM
RET_V_W = RET_HEADS * RET_V_DIM

N_EXPERTS = 8
TOP_K = 2
MOE_BLOCK = 256

LANES = 128
SUBLANES = 8
V7X_VMEM_LIMIT_BYTES = 56 * 1024 * 1024

ROW_TILE = 512
MERGE_TILE = 512
COMBINE_TILE = 256
MLA_Q_TILE = 512
MLA_HEADS_PER_STEP = 4
MLA_KEY_CHUNK = 1024


def _params(n_grid_dims):
    return pltpu.CompilerParams(
        dimension_semantics=("arbitrary",) * n_grid_dims,
        vmem_limit_bytes=V7X_VMEM_LIMIT_BYTES,
    )


def _resident(shape):
    zeros = (0,) * len(shape)
    return pl.BlockSpec(shape, lambda *_: zeros, pipeline_mode=pl.Buffered(1))


def _layer_resident(stacked, layer):
    idx = (layer,) + (0,) * (stacked.ndim - 1)
    return pl.BlockSpec((1,) + stacked.shape[1:], lambda *_: idx, pipeline_mode=pl.Buffered(1))


def _silu(v):
    return v * jax.nn.sigmoid(v)


def _norm_mod(x, gain, scale, shift):
    ms = jnp.mean(x * x, axis=-1, keepdims=True)
    y = x * lax.rsqrt(ms + EPS) * gain
    return y * (1.0 + scale) + shift


def _rms(v, gain):
    ms = jnp.mean(v * v, axis=-1, keepdims=True)
    return v * lax.rsqrt(ms + EPS) * gain


def _rope(t, cos_t, sin_t):
    lane = lax.broadcasted_iota(jnp.int32, (t.shape[0], LANES), 1)
    first_half = (lane % ROPE_DIM) < (ROPE_DIM // 2)
    out = []
    for j in range(t.shape[1] // LANES):
        c = t[:, j * LANES:(j + 1) * LANES]
        rot = jnp.where(first_half,
                        pltpu.roll(c, LANES - ROPE_DIM // 2, 1),
                        pltpu.roll(c, ROPE_DIM // 2, 1))
        out.append(c * cos_t + rot * sin_t)
    return out[0] if len(out) == 1 else jnp.concatenate(out, axis=1)


def _dot(a, b):
    return jnp.dot(a, b, preferred_element_type=F32)


def _dot_nt(a, b):
    return lax.dot_general(a, b, (((1,), (1,)), ((), ())), preferred_element_type=F32)


def _dot_tn(a, b):
    return lax.dot_general(a, b, (((0,), (0,)), ((), ())), preferred_element_type=F32)


def _ada_kernel(c_ref, w_ref, b_ref, o_ref):
    o_ref[0] = jnp.dot(_silu(c_ref[...]), w_ref[0], preferred_element_type=F32,
                       precision=lax.Precision.HIGHEST) + b_ref[0]


def _ada(c, ada_w, ada_b):
    depth, d_model, n = ada_w.shape
    batch = c.shape[0]
    tn = n // 4
    out = pl.pallas_call(
        _ada_kernel,
        grid=(depth, n // tn),
        in_specs=[
            pl.BlockSpec((batch, d_model), lambda l, j: (0, 0)),
            pl.BlockSpec((1, d_model, tn), lambda l, j: (l, 0, j)),
            pl.BlockSpec((1, 1, tn), lambda l, j: (l, 0, j)),
        ],
        out_specs=pl.BlockSpec((1, batch, tn), lambda l, j: (l, 0, j)),
        out_shape=jax.ShapeDtypeStruct((depth, batch, n), F32),
        compiler_params=_params(2),
        name="ada",
    )(c, ada_w, ada_b.reshape(depth, 1, n))
    return out.reshape(depth, batch, ADA_CHUNKS, d_model)


def _mla_prep_kernel(x_ref, mod_ref, g_ref, wa_ref, qn_ref, kvn_ref, wuq_ref, wukv_ref, cos_ref, sin_ref,
                     q_ref, k_ref, v_ref):
    mod = mod_ref[0]
    h = _norm_mod(x_ref[...], g_ref[...], mod[1:2], mod[0:1]).astype(BF16)
    p = _dot(h, wa_ref[0])
    cq = _rms(p[:, :MLA_Q_RANK], qn_ref[...]).astype(BF16)
    ckv = _rms(p[:, MLA_Q_RANK:MLA_Q_RANK + MLA_KV_RANK], kvn_ref[...]).astype(BF16)
    cos_t = cos_ref[0]
    sin_t = sin_ref[0]
    kr = _rope(p[:, MLA_Q_RANK + MLA_KV_RANK:], cos_t, sin_t)
    q = _dot(cq, wuq_ref[0])
    kv = _dot(ckv, wukv_ref[0])
    nope_w = MLA_HEADS * MLA_NOPE
    qr = _rope(q[:, nope_w:], cos_t, sin_t)
    scale = MLA_QK ** -0.5
    kr_b = kr[:, :MLA_ROPE].astype(BF16)
    for hd in range(MLA_HEADS):
        q_ref[0, hd, :, :MLA_NOPE] = (q[:, hd * MLA_NOPE:(hd + 1) * MLA_NOPE] * scale).astype(BF16)
        q_ref[0, hd, :, MLA_NOPE:] = (qr[:, hd * MLA_ROPE:(hd + 1) * MLA_ROPE] * scale).astype(BF16)
        base = hd * (MLA_NOPE + MLA_V)
        k_ref[0, hd, :, :MLA_NOPE] = kv[:, base:base + MLA_NOPE].astype(BF16)
        k_ref[0, hd, :, MLA_NOPE:] = kr_b
        v_ref[0, hd] = kv[:, base + MLA_NOPE:base + MLA_NOPE + MLA_V].astype(BF16)


def _mla_prep(x, mod, gain, wa, qn, kvn, wuq, wukv, cos_t, sin_t, layer, batch, seq):
    t, d_model = x.shape
    tm = ROW_TILE
    per_b = seq // tm
    row = lambda i: (i, 0)
    hs = lambda i: (i // per_b, 0, i % per_b, 0)
    return pl.pallas_call(
        _mla_prep_kernel,
        grid=(t // tm,),
        in_specs=[
            pl.BlockSpec((tm, d_model), row),
            pl.BlockSpec((1, ADA_CHUNKS, d_model), lambda i: (i // per_b, 0, 0)),
            _resident((1, d_model)),
            _layer_resident(wa, layer),
            _resident((1, MLA_Q_RANK)),
            _resident((1, MLA_KV_RANK)),
            _layer_resident(wuq, layer),
            _layer_resident(wukv, layer),
            pl.BlockSpec((1, tm, LANES), lambda i: (i // per_b, i % per_b, 0)),
            pl.BlockSpec((1, tm, LANES), lambda i: (i // per_b, i % per_b, 0)),
        ],
        out_specs=[
            pl.BlockSpec((1, MLA_HEADS, tm, MLA_QK), hs),
            pl.BlockSpec((1, MLA_HEADS, tm, MLA_QK), hs),
            pl.BlockSpec((1, MLA_HEADS, tm, MLA_V), hs),
        ],
        out_shape=[
            jax.ShapeDtypeStruct((batch, MLA_HEADS, seq, MLA_QK), BF16),
            jax.ShapeDtypeStruct((batch, MLA_HEADS, seq, MLA_QK), BF16),
            jax.ShapeDtypeStruct((batch, MLA_HEADS, seq, MLA_V), BF16),
        ],
        compiler_params=_params(1),
        name="mla_prep",
    )(x, mod, gain, wa, qn, kvn, wuq, wukv, cos_t, sin_t)


def _mla_attn_kernel(q_ref, k_ref, v_ref, o_ref, *, heads_per_step, key_chunk):
    seq = k_ref.shape[2]
    ones = jnp.ones((key_chunk, LANES), BF16)
    for hd in range(heads_per_step):
        q = q_ref[0, hd]
        m = acc = None
        for c in range(seq // key_chunk):
            rows = slice(c * key_chunk, (c + 1) * key_chunk)
            s = _dot_nt(q, k_ref[0, hd, rows, :])
            v_ext = jnp.concatenate([v_ref[0, hd, rows, :], ones], axis=1)
            m_c = jnp.max(s, axis=-1, keepdims=True)
            if c == 0:
                m = m_c
                acc = _dot(jnp.exp(s - m).astype(BF16), v_ext)
            else:
                m_new = jnp.maximum(m, m_c)
                acc = jnp.exp(m - m_new) * acc + _dot(jnp.exp(s - m_new).astype(BF16), v_ext)
                m = m_new
        o_ref[:, hd * MLA_V:(hd + 1) * MLA_V] = (acc[:, :MLA_V] / acc[:, MLA_V:MLA_V + 1]).astype(o_ref.dtype)


def _mla_attn(q, k, v):
    batch, heads, seq, _ = q.shape
    tq = MLA_Q_TILE
    nq = seq // tq
    hps = MLA_HEADS_PER_STEP
    return pl.pallas_call(
        functools.partial(_mla_attn_kernel, heads_per_step=hps, key_chunk=MLA_KEY_CHUNK),
        grid=(batch, heads // hps, nq),
        in_specs=[
            pl.BlockSpec((1, hps, tq, MLA_QK), lambda b, h, i: (b, h, i, 0)),
            pl.BlockSpec((1, hps, seq, MLA_QK), lambda b, h, i: (b, h, 0, 0)),
            pl.BlockSpec((1, hps, seq, MLA_V), lambda b, h, i: (b, h, 0, 0)),
        ],
        out_specs=pl.BlockSpec((tq, hps * MLA_V), lambda b, h, i: (b * nq + i, h)),
        out_shape=jax.ShapeDtypeStruct((batch * seq, heads * MLA_V), BF16),
        compiler_params=_params(3),
        name="mla_attn",
    )(q, k, v)


def _dil_prep_kernel(x_ref, mod_ref, g_ref, w_ref, cos_ref, sin_ref, *refs):
    out_refs = refs[:3 * len(DIL_PAIRS)]
    h_scr = refs[3 * len(DIL_PAIRS)]
    mod = mod_ref[0]
    h32 = _norm_mod(x_ref[...], g_ref[...], mod[1:2], mod[0:1])
    n_lane_blocks = h32.shape[1] // LANES
    for j in range(n_lane_blocks):
        h_scr[j] = h32[:, j * LANES:(j + 1) * LANES]
    tm = h32.shape[0]
    gw = DIL_GROUP_W
    for g, (_, dil) in enumerate(DIL_PAIRS):
        n = tm // dil
        if dil == 1:
            hp, cos_t, sin_t = h32, cos_ref[0], sin_ref[0]
        else:
            hp = jnp.concatenate(
                [jnp.concatenate([h_scr[j, pl.ds(r, n, stride=dil), :] for j in range(n_lane_blocks)], axis=1)
                 for r in range(dil)], axis=0)
            cos_t = jnp.concatenate([cos_ref[0, pl.ds(r, n, stride=dil), :] for r in range(dil)], axis=0)
            sin_t = jnp.concatenate([sin_ref[0, pl.ds(r, n, stride=dil), :] for r in range(dil)], axis=0)
        pg = _dot(hp.astype(BF16), w_ref[0, :, g * 3 * gw:(g + 1) * 3 * gw])
        qg = (_rope(pg[:, :gw], cos_t, sin_t) * (DIL_HEAD_DIM ** -0.5)).astype(BF16)
        kg = _rope(pg[:, gw:2 * gw], cos_t, sin_t).astype(BF16)
        vg = pg[:, 2 * gw:].astype(BF16)
        q_ref, k_ref, v_ref = out_refs[3 * g:3 * g + 3]
        for r in range(dil):
            q_ref[0, r] = qg[r * n:(r + 1) * n]
            k_ref[0, r] = kg[r * n:(r + 1) * n]
            v_ref[0, r] = vg[r * n:(r + 1) * n]


def _dil_prep(x, mod, gain, w, cos_t, sin_t, layer, batch, seq):
    t, d_model = x.shape
    tm = ROW_TILE
    per_b = seq // tm
    out_specs, out_shape = [], []
    for _, dil in DIL_PAIRS:
        for _ in range(3):
            out_specs.append(pl.BlockSpec((1, dil, tm // dil, DIL_GROUP_W),
                                          lambda i: (i // per_b, 0, i % per_b, 0)))
            out_shape.append(jax.ShapeDtypeStruct((batch, dil, seq // dil, DIL_GROUP_W), BF16))
    return pl.pallas_call(
        _dil_prep_kernel,
        grid=(t // tm,),
        in_specs=[
            pl.BlockSpec((tm, d_model), lambda i: (i, 0)),
            pl.BlockSpec((1, ADA_CHUNKS, d_model), lambda i: (i // per_b, 0, 0)),
            _resident((1, d_model)),
            _layer_resident(w, layer),
            pl.BlockSpec((1, tm, LANES), lambda i: (i // per_b, i % per_b, 0)),
            pl.BlockSpec((1, tm, LANES), lambda i: (i // per_b, i % per_b, 0)),
        ],
        out_specs=out_specs,
        out_shape=out_shape,
        scratch_shapes=[pltpu.VMEM((d_model // LANES, tm, LANES), F32)],
        compiler_params=_params(1),
        name="dil_prep",
    )(x, mod, gain, w, cos_t, sin_t)


def _dil_attn_kernel(q_ref, k_ref, v_ref, o_ref, lse_ref, *, dil, segment, band):
    qb_rows = DIL_QBLOCK
    kw = 2 * qb_rows
    short = segment < kw
    per_window = kw // segment if short else 1
    nb = 1 if short else segment // qb_rows
    low_k = lax.broadcasted_iota(jnp.int32, (kw, LANES), 1).astype(F32).astype(BF16) < DIL_HEAD_DIM
    low_q = lax.broadcasted_iota(jnp.int32, (qb_rows, LANES), 1) < DIL_HEAD_DIM
    row_i = lax.broadcasted_iota(jnp.int32, (qb_rows, kw), 0)
    col_i = lax.broadcasted_iota(jnp.int32, (qb_rows, kw), 1)
    zeros_k = jnp.zeros((kw, LANES), BF16)
    ones_k = jnp.ones((kw, LANES), BF16)
    denom_cols = jnp.concatenate([jnp.where(low_k, ones_k, zeros_k), jnp.where(low_k, zeros_k, ones_k)], axis=0)

    def tile(idx, carry):
        r = idx // nb
        if short:
            q0 = 0
            r0 = (r // per_window) * per_window
            qt = q_ref[0, r]
            kt = jnp.concatenate([k_ref[0, r0 + j] for j in range(per_window)], axis=0)
            vt = jnp.concatenate([v_ref[0, r0 + j] for j in range(per_window)], axis=0)
            jq = (r - r0) * segment + row_i
            jk = col_i
        else:
            q0 = pl.multiple_of((idx % nb) * qb_rows, qb_rows)
            ks = pl.multiple_of(jnp.clip(q0 - band, 0, segment - kw), band)
            qt = q_ref[0, r, pl.ds(q0, qb_rows), :]
            kt = k_ref[0, r, pl.ds(ks, kw), :]
            vt = v_ref[0, r, pl.ds(ks, kw), :]
            jq = q0 + row_i
            jk = ks + col_i
        valid = jnp.logical_and(jnp.abs(jq - jk) <= band, jq // segment == jk // segment)
        for hp in range(DIL_GROUP_W // LANES):
            cols = slice(hp * LANES, (hp + 1) * LANES)
            qp, kp, vp = qt[:, cols], kt[:, cols], vt[:, cols]
            k2 = jnp.concatenate([jnp.where(low_k, kp, zeros_k), jnp.where(low_k, zeros_k, kp)], axis=0)
            v2 = jnp.concatenate([jnp.where(low_k, vp, zeros_k), jnp.where(low_k, zeros_k, vp)], axis=0)
            s = _dot_nt(qp, k2)
            s0 = jnp.where(valid, s[:, :kw], NEG_INF)
            s1 = jnp.where(valid, s[:, kw:], NEG_INF)
            m0 = jnp.max(s0, axis=-1, keepdims=True)
            m1 = jnp.max(s1, axis=-1, keepdims=True)
            p = jnp.concatenate([jnp.exp(s0 - m0), jnp.exp(s1 - m1)], axis=1).astype(BF16)
            pv = _dot(p, jnp.concatenate([v2, denom_cols], axis=1))
            den = pv[:, LANES:]
            o_ref[0, r, pl.ds(q0, qb_rows), cols] = (pv[:, :LANES] / den).astype(o_ref.dtype)
            lse_ref[0, r, pl.ds(q0, qb_rows), cols] = jnp.where(low_q, m0, m1) + jnp.log(den)
        return carry

    lax.fori_loop(0, dil * nb, tile, 0, unroll=8)


def _dil_attn(q, k, v, window, dil):
    batch, _, seg, gw = q.shape
    band = window // (2 * dil)
    assert band * 2 == DIL_QBLOCK and seg % DIL_QBLOCK == 0
    assert seg >= 2 * DIL_QBLOCK or (seg == DIL_QBLOCK and dil % 2 == 0)
    spec = pl.BlockSpec((1, dil, seg, gw), lambda b: (b, 0, 0, 0))
    return pl.pallas_call(
        functools.partial(_dil_attn_kernel, dil=dil, segment=seg, band=band),
        grid=(batch,),
        in_specs=[spec, spec, spec],
        out_specs=[spec, spec],
        out_shape=[jax.ShapeDtypeStruct(q.shape, BF16), jax.ShapeDtypeStruct(q.shape, F32)],
        compiler_params=_params(1),
        name=f"dil_attn_d{dil}",
    )(q, k, v)


def _ret_prep_kernel(x_ref, mod_ref, g_ref, w_ref, cos_ref, sin_ref, q_ref, k_ref, v_ref):
    mod = mod_ref[0]
    h = _norm_mod(x_ref[...], g_ref[...], mod[1:2], mod[0:1]).astype(BF16)
    p = _dot(h, w_ref[0])
    cos_t, sin_t = cos_ref[0], sin_ref[0]
    q_ref[...] = _rope(p[:, :RET_QK_W], cos_t, sin_t).astype(BF16)
    k_ref[...] = (_rope(p[:, RET_QK_W:2 * RET_QK_W], cos_t, sin_t) * (RET_QK_DIM ** -0.5)).astype(BF16)
    v_ref[...] = p[:, 2 * RET_QK_W:].astype(BF16)


def _ret_prep(x, mod, gain, w, cos_t, sin_t, layer, seq):
    t, d_model = x.shape
    tm = ROW_TILE
    per_b = seq // tm
    row = lambda i: (i, 0)
    return pl.pallas_call(
        _ret_prep_kernel,
        grid=(t // tm,),
        in_specs=[
            pl.BlockSpec((tm, d_model), row),
            pl.BlockSpec((1, ADA_CHUNKS, d_model), lambda i: (i // per_b, 0, 0)),
            _resident((1, d_model)),
            _layer_resident(w, layer),
            pl.BlockSpec((1, tm, LANES), lambda i: (i // per_b, i % per_b, 0)),
            pl.BlockSpec((1, tm, LANES), lambda i: (i // per_b, i % per_b, 0)),
        ],
        out_specs=[pl.BlockSpec((tm, RET_QK_W), row), pl.BlockSpec((tm, RET_QK_W), row),
                   pl.BlockSpec((tm, RET_V_W), row)],
        out_shape=[jax.ShapeDtypeStruct((t, RET_QK_W), BF16), jax.ShapeDtypeStruct((t, RET_QK_W), BF16),
                   jax.ShapeDtypeStruct((t, RET_V_W), BF16)],
        compiler_params=_params(1),
        name="ret_prep",
    )(x, mod, gain, w, cos_t, sin_t)


def _ret_scan_kernel(ld_ref, q_ref, k_ref, v_ref, gn_ref, o_ref,
                     yf_ref, st_ref, intra_ref, qdec_ref, kdec_ref, cdec_ref):
    c = RET_CHUNK
    seq = q_ref.shape[0]
    nc = seq // c
    n_pairs = RET_HEADS // 2
    pair_v = 2 * RET_V_DIM
    ia = lax.broadcasted_iota(jnp.int32, (c, c), 0)
    ib = lax.broadcasted_iota(jnp.int32, (c, c), 1)
    diff = (ia - ib).astype(F32)
    idx_q = lax.broadcasted_iota(jnp.int32, (c, pair_v), 0).astype(F32)
    lane_q = lax.broadcasted_iota(jnp.int32, (c, pair_v), 1)
    idx_k = lax.broadcasted_iota(jnp.int32, (c, LANES), 0).astype(F32)
    lane_k = lax.broadcasted_iota(jnp.int32, (c, LANES), 1)
    low_k = lane_k < RET_QK_DIM
    low_k16 = lane_k.astype(F32).astype(BF16) < RET_QK_DIM
    low_v16 = lane_q.astype(F32).astype(BF16) < RET_V_DIM
    srow =lax.broadcasted_iota(jnp.int32, (LANES, pair_v), 0)
    scol = lax.broadcasted_iota(jnp.int32, (LANES, pair_v), 1)
    blk0 = jnp.logical_and(srow < RET_QK_DIM, scol < RET_V_DIM)
    blk1 = jnp.logical_and(srow >= RET_QK_DIM, scol >= RET_V_DIM)
    diag = jnp.where(jnp.logical_or(blk0, blk1), 1.0, 0.0).astype(F32)

    for dr in range(2):
        for hd in range(RET_HEADS):
            lg = ld_ref[dr, hd]
            if dr == 0:
                mask = diff >= 0
                dist = jnp.where(mask, diff, 0.0)
            else:
                mask = diff < 0
                dist = jnp.where(mask, -diff, 0.0)
            intra_ref[dr, hd] = jnp.where(mask, jnp.exp(lg * dist), 0.0)
        for hp in range(n_pairs):
            lg0 = ld_ref[dr, 2 * hp]
            lg1 = ld_ref[dr, 2 * hp + 1]
            q_exp = idx_q + 1.0 if dr == 0 else c - idx_q
            k_exp = c - 1.0 - idx_k if dr == 0 else idx_k
            qdec_ref[dr, hp] = jnp.exp(jnp.where(lane_q < RET_V_DIM, lg0, lg1) * q_exp)
            kdec_ref[dr, hp] = jnp.exp(jnp.where(low_k, lg0, lg1) * k_exp)
            cdec_ref[dr, hp] = jnp.where(blk0, jnp.exp(lg0 * c), jnp.where(blk1, jnp.exp(lg1 * c), 0.0))

    def chunk(dr, n):
        r0 = pl.multiple_of(n * c, c)
        qc = q_ref[pl.ds(r0, c), :]
        kc = k_ref[pl.ds(r0, c), :]
        vc = v_ref[pl.ds(r0, c), :]
        ys = []
        for hp in range(n_pairs):
            qp = qc[:, hp * LANES:(hp + 1) * LANES]
            kp = kc[:, hp * LANES:(hp + 1) * LANES]
            vp = vc[:, hp * pair_v:(hp + 1) * pair_v]
            zk = jnp.zeros_like(kp)
            zv = jnp.zeros_like(vp)
            k2 = jnp.concatenate([jnp.where(low_k16, kp, zk), jnp.where(low_k16, zk, kp)], axis=0)
            v2 = jnp.concatenate([jnp.where(low_v16, vp, zv), jnp.where(low_v16, zv, vp)], axis=0)
            s = _dot_nt(qp, k2) * jnp.concatenate([intra_ref[dr, 2 * hp], intra_ref[dr, 2 * hp + 1]], axis=1)
            y_intra = _dot(s.astype(BF16), v2)
            state = st_ref[hp]
            y_inter = _dot(qp, state.astype(BF16)) * qdec_ref[dr, hp]
            ys.append(y_intra + y_inter)
            kd = (kp.astype(F32) * kdec_ref[dr, hp]).astype(BF16)
            st_ref[hp] = cdec_ref[dr, hp] * state + diag * _dot_tn(kd, vp)
        return r0, ys

    st_ref[...] = jnp.zeros_like(st_ref)

    def fwd(n, carry):
        r0, ys = chunk(0, n)
        for hp in range(n_pairs):
            yf_ref[pl.ds(r0, c), hp * pair_v:(hp + 1) * pair_v] = ys[hp]
        return carry

    lax.fori_loop(0, nc, fwd, 0, unroll=RET_UNROLL)
    st_ref[...] = jnp.zeros_like(st_ref)

    def bwd(i, carry):
        r0, ys = chunk(1, nc - 1 - i)
        for hp in range(n_pairs):
            y = ys[hp] + yf_ref[pl.ds(r0, c), hp * pair_v:(hp + 1) * pair_v]
            for half in range(2):
                cols = slice(hp * pair_v + half * RET_V_DIM, hp * pair_v + (half + 1) * RET_V_DIM)
                yh = y[:, half * RET_V_DIM:(half + 1) * RET_V_DIM]
                mu = jnp.mean(yh, axis=-1, keepdims=True)
                dev = yh - mu
                var = jnp.mean(dev * dev, axis=-1, keepdims=True)
                o_ref[pl.ds(r0, c), cols] = (dev * lax.rsqrt(var + EPS) * gn_ref[:, cols]).astype(o_ref.dtype)
        return carry

    lax.fori_loop(0, nc, bwd, 0, unroll=RET_UNROLL)


def _ret_scan(log_decay, q, k, v, ret_norm, batch, seq):
    c = RET_CHUNK
    n_pairs = RET_HEADS // 2
    return pl.pallas_call(
        _ret_scan_kernel,
        grid=(batch,),
        in_specs=[
            pl.BlockSpec(memory_space=pltpu.SMEM),
            pl.BlockSpec((seq, RET_QK_W), lambda b: (b, 0)),
            pl.BlockSpec((seq, RET_QK_W), lambda b: (b, 0)),
            pl.BlockSpec((seq, RET_V_W), lambda b: (b, 0)),
            _resident((1, RET_V_W)),
        ],
        out_specs=pl.BlockSpec((seq, RET_V_W), lambda b: (b, 0)),
        out_shape=jax.ShapeDtypeStruct((batch * seq, RET_V_W), BF16),
        scratch_shapes=[
            pltpu.VMEM((seq, RET_V_W), F32),
            pltpu.VMEM((n_pairs, LANES, 2 * RET_V_DIM), F32),
            pltpu.VMEM((2, RET_HEADS, c, c), F32),
            pltpu.VMEM((2, n_pairs, c, 2 * RET_V_DIM), F32),
            pltpu.VMEM((2, n_pairs, c, LANES), F32),
            pltpu.VMEM((2, n_pairs, LANES, 2 * RET_V_DIM), F32),
        ],
        compiler_params=_params(1),
        name="ret_scan",
    )(log_decay, q, k, v, ret_norm)


def _top2_route(h, w_hi_lo):
    h_hi = h.astype(BF16)
    h_lo = (h - h_hi.astype(F32)).astype(BF16)
    hh = _dot(h_hi, w_hi_lo)
    logits = hh[:, :LANES] + (hh[:, LANES:] + _dot(h_lo, w_hi_lo[:, :LANES]))
    lane = lax.broadcasted_iota(jnp.int32, logits.shape, 1).astype(F32)
    lg = jnp.where(lane < N_EXPERTS, logits, -jnp.inf)
    m1 = jnp.max(lg, axis=-1, keepdims=True)
    i1 = jnp.min(jnp.where(lg == m1, lane, float(LANES)), axis=-1, keepdims=True)
    lg2 = jnp.where(lane == i1, -jnp.inf, lg)
    m2 = jnp.max(lg2, axis=-1, keepdims=True)
    i2 = jnp.min(jnp.where(lg2 == m2, lane, float(LANES)), axis=-1, keepdims=True)
    e = jnp.exp(m2 - m1)
    w1 = 1.0 / (1.0 + e)
    w2 = e / (1.0 + e)
    return jnp.where(lane == 0, i1, jnp.where(lane == 1, i2, jnp.where(lane == 2, w1, jnp.where(lane == 3, w2, 0.0))))


def _merge_kernel(x_ref, mod_ref, g_ref, wg_ref, ymla_ref, yret_ref,
                  o0_ref, l0_ref, o1_ref, l1_ref, o2_ref, l2_ref,
                  wbm_ref, wbd_ref, wbr_ref, wout_ref, gf_ref, wr_ref, out_ref, *rest, route):
    nat_o, nat_l = rest[-2:]
    x = x_ref[...]
    mod = mod_ref[0]
    h = _norm_mod(x, g_ref[...], mod[1:2], mod[0:1]).astype(BF16)
    gates = _dot(h, wg_ref[0])
    d_model = x.shape[1]
    tm = x.shape[0]

    o_nat = [o0_ref[0, 0].astype(F32)]
    l_nat = [l0_ref[0, 0]]
    for g, (o_ref, l_ref) in enumerate(((o1_ref, l1_ref), (o2_ref, l2_ref))):
        dil = DIL_PAIRS[g + 1][1]
        n = tm // dil
        n_lane_blocks = DIL_GROUP_W // LANES
        for r in range(dil):
            o_r = o_ref[0, r].astype(F32)
            l_r = l_ref[0, r]
            for j in range(n_lane_blocks):
                nat_o[g, j, pl.ds(r, n, stride=dil), :] = o_r[:, j * LANES:(j + 1) * LANES]
                nat_l[g, j, pl.ds(r, n, stride=dil), :] = l_r[:, j * LANES:(j + 1) * LANES]
        o_nat.append(jnp.concatenate([nat_o[g, j] for j in range(n_lane_blocks)], axis=1))
        l_nat.append(jnp.concatenate([nat_l[g, j] for j in range(n_lane_blocks)], axis=1))
    m = jnp.maximum(jnp.maximum(l_nat[0], l_nat[1]), l_nat[2])
    ws = [jnp.exp(l - m) for l in l_nat]
    y_dil = (ws[0] * o_nat[0] + ws[1] * o_nat[1] + ws[2] * o_nat[2]) / (ws[0] + ws[1] + ws[2])

    y_ret = (_silu(gates[:, :RET_V_W]) * yret_ref[...].astype(F32)).astype(BF16)
    ga = gates[:, RET_V_W:RET_V_W + d_model]
    gb = gates[:, RET_V_W + d_model:RET_V_W + 2 * d_model]
    gc = gates[:, RET_V_W + 2 * d_model:]
    merged = (jax.nn.sigmoid(ga) * _dot(ymla_ref[...], wbm_ref[0])
              + jax.nn.sigmoid(gb) * _dot(y_dil.astype(BF16), wbd_ref[0])
              + jax.nn.sigmoid(gc) * _dot(y_ret, wbr_ref[0]))
    x_new = x + mod[2:3] * _dot(merged.astype(BF16), wout_ref[0])
    out_ref[...] = x_new
    if route:
        h_ref, route_ref, ids_ref = rest[:3]
        h2 = _norm_mod(x_new, gf_ref[...], mod[4:5], mod[3:4])
        h_ref[...] = h2
        r = _top2_route(h2, wr_ref[...])
        route_ref[...] = r
        ids_ref[...] = r.T[:ids_ref.shape[0]]


def _merge_out(x, mod, gain, wg, y_mla, y_ret, dil_outs, wbm, wbd, wbr, wout, gain_ffn, w_router,
               layer, batch, seq, route):
    t, d_model = x.shape
    tm = MERGE_TILE
    per_b = seq // tm
    row = lambda i: (i, 0)
    dil_specs, dil_args = [], []
    for (_, dil), (o, lse) in zip(DIL_PAIRS, dil_outs):
        spec = pl.BlockSpec((1, dil, tm // dil, DIL_GROUP_W), lambda i: (i // per_b, 0, i % per_b, 0))
        dil_specs += [spec, spec]
        dil_args += [o, lse]
    out_specs = [pl.BlockSpec((tm, d_model), row)]
    out_shape = [jax.ShapeDtypeStruct((t, d_model), F32)]
    if route:
        out_specs += [pl.BlockSpec((tm, d_model), row), pl.BlockSpec((tm, LANES), row),
                      pl.BlockSpec((SUBLANES, tm), lambda i: (0, i))]
        out_shape += [jax.ShapeDtypeStruct((t, d_model), F32), jax.ShapeDtypeStruct((t, LANES), F32),
                      jax.ShapeDtypeStruct((SUBLANES, t), F32)]
    wr32 = jnp.zeros((d_model, LANES), F32).at[:, :N_EXPERTS].set(w_router)
    wr_hi = wr32.astype(BF16)
    wr = jnp.concatenate([wr_hi, (wr32 - wr_hi.astype(F32)).astype(BF16)], axis=1)
    return pl.pallas_call(
        functools.partial(_merge_kernel, route=route),
        grid=(t // tm,),
        in_specs=[
            pl.BlockSpec((tm, d_model), row),
            pl.BlockSpec((1, ADA_CHUNKS, d_model), lambda i: (i // per_b, 0, 0)),
            _resident((1, d_model)),
            _layer_resident(wg, layer),
            pl.BlockSpec((tm, y_mla.shape[1]), row),
            pl.BlockSpec((tm, RET_V_W), row),
            *dil_specs,
            _layer_resident(wbm, layer), _layer_resident(wbd, layer), _layer_resident(wbr, layer),
            _layer_resident(wout, layer),
            _resident((1, d_model)),
            _resident(wr.shape),
        ],
        out_specs=out_specs,
        out_shape=out_shape,
        scratch_shapes=[pltpu.VMEM((2, DIL_GROUP_W // LANES, tm, LANES), F32),
                        pltpu.VMEM((2, DIL_GROUP_W // LANES, tm, LANES), F32)],
        compiler_params=_params(1),
        name="merge_out",
    )(x, mod, gain, wg, y_mla, y_ret, *dil_args, wbm, wbd, wbr, wout, gain_ffn, wr)


def _residual_out(y, fg_ref, final):
    return _rms(y, fg_ref[...]) if final else y


def _ffn_kernel(x_ref, mod_ref, g_ref, fg_ref, w1_ref, w3_ref, w2_ref, out_ref, *, n_chunks, final):
    x = x_ref[...]
    mod = mod_ref[0]
    h = _norm_mod(x, g_ref[...], mod[4:5], mod[3:4]).astype(BF16)
    d_ff = w1_ref.shape[2]
    fc = d_ff // n_chunks
    acc = jnp.zeros(x.shape, F32)
    for j in range(n_chunks):
        a = _dot(h, w1_ref[0, :, j * fc:(j + 1) * fc])
        b = _dot(h, w3_ref[0, :, j * fc:(j + 1) * fc])
        acc = acc + _dot((_silu(a) * b).astype(BF16), w2_ref[0, j * fc:(j + 1) * fc, :])
    out_ref[...] = _residual_out(x + mod[5:6] * acc, fg_ref, final)


def _ffn_dense(x, mod, gain, final_gain, w1, w3, w2, layer, seq, final):
    t, d_model = x.shape
    tm = ROW_TILE
    per_b = seq // tm
    row = lambda i: (i, 0)
    return pl.pallas_call(
        functools.partial(_ffn_kernel, n_chunks=1, final=final),
        grid=(t // tm,),
        in_specs=[
            pl.BlockSpec((tm, d_model), row),
            pl.BlockSpec((1, ADA_CHUNKS, d_model), lambda i: (i // per_b, 0, 0)),
            _resident((1, d_model)),
            _resident((1, d_model)),
            _layer_resident(w1, layer), _layer_resident(w3, layer), _layer_resident(w2, layer),
        ],
        out_specs=pl.BlockSpec((tm, d_model), row),
        out_shape=jax.ShapeDtypeStruct((t, d_model), F32),
        compiler_params=_params(1),
        name="ffn_dense",
    )(x, mod, gain, final_gain, w1, w3, w2)


def _row_copy(src_hbm, dst, src_row, dst_row, sem):
    return pltpu.make_async_copy(src_hbm.at[pl.ds(src_row, 1), :], dst.at[pl.ds(dst_row, 1), :], sem)


def _expert_kernel(be_ref, nact_ref, tok_ref, tok_next_ref, h_hbm, w1_ref, w3_ref, w2_ref, y_ref,
                   xbuf, sems, *, n_chunks):
    i = pl.program_id(0)
    nact = nact_ref[0]
    bm = y_ref.shape[0]
    slot = i % 2

    def wait_block(s):
        pltpu.make_async_copy(h_hbm.at[pl.ds(0, bm), :], xbuf.at[s], sems.at[s]).wait()

    @pl.when(i == 0)
    def _():
        def issue(r, carry):
            _row_copy(h_hbm, xbuf.at[0], tok_ref[0, 0, r], r, sems.at[0]).start()
            return carry

        lax.fori_loop(0, bm, issue, 0, unroll=8)

    @pl.when(i >= nact)
    def _():
        y_ref[...] = jnp.zeros_like(y_ref)

    @pl.when(i < nact)
    def _():
        wait_block(slot)
        xb = xbuf[slot].astype(BF16)
        for r in range(bm):
            _row_copy(h_hbm, xbuf.at[1 - slot], tok_next_ref[0, 0, r], r, sems.at[1 - slot]).start()
        d_ff = w1_ref.shape[2]
        fc = d_ff // n_chunks
        acc = jnp.zeros(y_ref.shape, F32)
        for j in range(n_chunks):
            a = _dot(xb, w1_ref[0, :, j * fc:(j + 1) * fc])
            b = _dot(xb, w3_ref[0, :, j * fc:(j + 1) * fc])
            acc = acc + _dot((_silu(a) * b).astype(BF16), w2_ref[0, j * fc:(j + 1) * fc, :])
        y_ref[...] = acc

    @pl.when(i == nact - 1)
    def _():
        wait_block(1 - slot)


def _moe_expert(block_expert, nact, row_tok, h, w1, w3, w2, expert_base):
    nb, _, bm = row_tok.shape
    d_model = h.shape[1]
    d_ff = w1.shape[2]
    wsel = lambda i, be, na: (expert_base + be[jnp.minimum(i, na[0] - 1)], 0, 0)
    return pl.pallas_call(
        functools.partial(_expert_kernel, n_chunks=2),
        grid_spec=pltpu.PrefetchScalarGridSpec(
            num_scalar_prefetch=2,
            grid=(nb,),
            in_specs=[
                pl.BlockSpec((1, 1, bm), lambda i, be, na: (i, 0, 0), memory_space=pltpu.SMEM),
                pl.BlockSpec((1, 1, bm), lambda i, be, na: (jnp.minimum(i + 1, nb - 1), 0, 0),
                             memory_space=pltpu.SMEM),
                pl.BlockSpec(memory_space=pl.ANY),
                pl.BlockSpec((1, d_model, d_ff), wsel, pipeline_mode=pl.Buffered(1)),
                pl.BlockSpec((1, d_model, d_ff), wsel, pipeline_mode=pl.Buffered(1)),
                pl.BlockSpec((1, d_ff, d_model), wsel, pipeline_mode=pl.Buffered(1)),
            ],
            out_specs=pl.BlockSpec((bm, d_model), lambda i, be, na: (i, 0)),
            scratch_shapes=[pltpu.VMEM((2, bm, d_model), F32), pltpu.SemaphoreType.DMA((2,))],
        ),
        out_shape=jax.ShapeDtypeStruct((nb * bm, d_model), F32),
        compiler_params=_params(1),
        name="moe_expert",
    )(block_expert, nact, row_tok, row_tok, h, w1, w3, w2)


def _combine_kernel(dest_ref, dest_next_ref, x_ref, mod_ref, route_ref, fg_ref, yb_hbm, out_ref, buf, sems,
                    *, final):
    tm = x_ref.shape[0]
    n_rows = TOP_K * tm
    i = pl.program_id(0)
    slot = i % 2

    def issue(rows_ref, s):
        def body(r, carry):
            _row_copy(yb_hbm, buf.at[s], rows_ref[0, 0, r], r, sems.at[s]).start()
            return carry

        lax.fori_loop(0, n_rows, body, 0, unroll=8)

    @pl.when(i == 0)
    def _():
        issue(dest_ref, 0)

    @pl.when(i + 1 < pl.num_programs(0))
    def _():
        for r in range(n_rows):
            _row_copy(yb_hbm, buf.at[1 - slot], dest_next_ref[0, 0, r], r, sems.at[1 - slot]).start()

    pltpu.make_async_copy(yb_hbm.at[pl.ds(0, n_rows), :], buf.at[slot], sems.at[slot]).wait()
    route = route_ref[...]
    y = route[:, 2:3] * buf[slot, :tm, :] + route[:, 3:4] * buf[slot, tm:, :]
    out_ref[...] = _residual_out(x_ref[...] + mod_ref[0][5:6] * y, fg_ref, final)


def _moe_combine(dest, x, mod, route, final_gain, yb, seq, final):
    t, d_model = x.shape
    tm = COMBINE_TILE
    per_b = seq // tm
    row = lambda i: (i, 0)
    return pl.pallas_call(
        functools.partial(_combine_kernel, final=final),
        grid=(t // tm,),
        in_specs=[
            pl.BlockSpec((1, 1, TOP_K * tm), lambda i: (i, 0, 0), memory_space=pltpu.SMEM),
            pl.BlockSpec((1, 1, TOP_K * tm), lambda i: (jnp.minimum(i + 1, t // tm - 1), 0, 0),
                         memory_space=pltpu.SMEM),
            pl.BlockSpec((tm, d_model), row),
            pl.BlockSpec((1, ADA_CHUNKS, d_model), lambda i: (i // per_b, 0, 0)),
            pl.BlockSpec((tm, LANES), row),
            _resident((1, d_model)),
            pl.BlockSpec(memory_space=pl.ANY),
        ],
        out_specs=pl.BlockSpec((tm, d_model), row),
        out_shape=jax.ShapeDtypeStruct((t, d_model), F32),
        scratch_shapes=[pltpu.VMEM((2, TOP_K * tm, d_model), F32), pltpu.SemaphoreType.DMA((2,))],
        compiler_params=_params(1),
        name="moe_combine",
    )(dest, dest, x, mod, route, final_gain, yb)


def _moe(x, h, route, expert_ids, mod, final_gain, w1, w3, w2, expert_base, seq, final):
    t, d_model = x.shape
    bm = MOE_BLOCK
    e_flat = expert_ids[:TOP_K].astype(jnp.int32).reshape(1, -1)
    n_assign = e_flat.shape[1]
    onehot = e_flat == jnp.arange(N_EXPERTS, dtype=jnp.int32)[:, None]
    csum = jnp.cumsum(onehot.astype(jnp.int32), axis=1)
    counts = csum[:, -1]
    padded = (counts + bm - 1) // bm * bm
    pends = jnp.cumsum(padded)
    row_of = (pends - padded)[:, None] + csum - 1
    dest = jnp.sum(jnp.where(onehot, row_of, 0), axis=0)
    nb = n_assign // bm + N_EXPERTS
    row_tok = jnp.zeros((nb * bm,), jnp.int32).at[dest].set(jnp.arange(n_assign, dtype=jnp.int32) % t)
    block_start = jnp.arange(nb, dtype=jnp.int32) * bm
    block_expert = jnp.minimum(
        jnp.sum((pends[None, :] <= block_start[:, None]).astype(jnp.int32), axis=1), N_EXPERTS - 1)
    nact = (pends[-1:] // bm).astype(jnp.int32)

    yb = _moe_expert(block_expert, nact, row_tok.reshape(nb, 1, bm), h, w1, w3, w2, expert_base)
    tmc = COMBINE_TILE
    dest_tiles = dest.reshape(TOP_K, t // tmc, tmc).transpose(1, 0, 2).reshape(t // tmc, 1, TOP_K * tmc)
    return _moe_combine(dest_tiles, x, mod, route, final_gain, yb, seq, final)


def _rope_tables(positions):
    inv_freq = ROPE_THETA ** (-jnp.arange(0, ROPE_DIM, 2, dtype=F32) / ROPE_DIM)
    ang = positions.astype(F32)[..., None] * inv_freq
    cos, sin = jnp.cos(ang), jnp.sin(ang)
    reps = LANES // ROPE_DIM
    cos_t = jnp.tile(jnp.concatenate([cos, cos], axis=-1), (1, 1, reps))
    sin_t = jnp.tile(jnp.concatenate([-sin, sin], axis=-1), (1, 1, reps))
    return cos_t, sin_t


def _split_w_in_kernel(wt_ref, mla_ref, dil_ref, ret_ref, gate_ref):
    n_cols = wt_ref.shape[1]
    main = (n_cols // LANES) * LANES
    w = wt_ref[0, :main, :].T
    if main < n_cols:
        last = wt_ref[0, n_cols - LANES:, :].T
        w = jnp.concatenate([w, last[:, LANES - (n_cols - main):]], axis=1)
    d_model = gate_ref.shape[2] // 4
    n_mla = MLA_Q_RANK + MLA_KV_RANK
    kr = w[:, n_mla:n_mla + MLA_ROPE]
    mla_ref[0] = jnp.concatenate([w[:, :n_mla], kr, kr], axis=1).astype(BF16)
    tail = w[:, n_mla + MLA_ROPE:]
    dil_w = 3 * len(DIL_PAIRS) * DIL_GROUP_W
    gw = DIL_GROUP_W
    n_g = len(DIL_PAIRS)
    dil_ref[0] = jnp.concatenate(
        [tail[:, (m * n_g + g) * gw:(m * n_g + g + 1) * gw] for g in range(n_g) for m in range(3)],
        axis=1).astype(BF16)
    ret_w = 2 * RET_QK_W + RET_V_W
    ret_ref[0] = tail[:, dil_w:dil_w + ret_w].astype(BF16)
    gate_ref[0] = tail[:, dil_w + ret_w:dil_w + ret_w + RET_V_W + 3 * d_model].astype(BF16)


def _split_w_in(w_in):
    depth, d_model, n_cols = w_in.shape
    rb = LANES
    widths = (MLA_Q_RANK + MLA_KV_RANK + 2 * MLA_ROPE, 3 * len(DIL_PAIRS) * DIL_GROUP_W,
              2 * RET_QK_W + RET_V_W, RET_V_W + 3 * d_model)
    assert sum(widths) - MLA_ROPE == n_cols
    return pl.pallas_call(
        _split_w_in_kernel,
        grid=(depth, d_model // rb),
        in_specs=[pl.BlockSpec((1, n_cols, rb), lambda l, i: (l, 0, i))],
        out_specs=[pl.BlockSpec((1, rb, n), lambda l, i: (l, i, 0)) for n in widths],
        out_shape=[jax.ShapeDtypeStruct((depth, d_model, n), BF16) for n in widths],
        compiler_params=_params(2),
        name="split_w_in",
    )(jnp.swapaxes(w_in, 1, 2))


def _split_w_uq(w_uq):
    depth = w_uq.shape[0]
    w = w_uq.astype(BF16).reshape(depth, MLA_Q_RANK, MLA_HEADS, MLA_QK)
    nope = w[..., :MLA_NOPE].reshape(depth, MLA_Q_RANK, MLA_HEADS * MLA_NOPE)
    rope = w[..., MLA_NOPE:].reshape(depth, MLA_Q_RANK, MLA_HEADS * MLA_ROPE)
    return jnp.concatenate([nope, rope], axis=-1)


def kernel(x, c, positions, ada_w, ada_b, norm_mix, norm_ffn, w_in, mla_q_norm, mla_w_uq, mla_kv_norm,
           mla_w_ukv, ret_log_decay, ret_norm, w_br_mla, w_br_dil, w_br_ret, w_out, ffn_w1, ffn_w3, ffn_w2,
           moe_router, moe_w1, moe_w3, moe_w2, final_norm):
    batch, seq, d_model = x.shape
    depth = ada_w.shape[0]
    cos_t, sin_t = _rope_tables(positions)
    mod_all = _ada(c, ada_w, ada_b)
    assert depth >= 1
    fgain = final_norm.reshape(1, d_model)
    xt = x.reshape(batch * seq, d_model)
    moe_w1_b = moe_w1.astype(BF16).reshape((-1,) + moe_w1.shape[2:])
    moe_w3_b = moe_w3.astype(BF16).reshape((-1,) + moe_w3.shape[2:])
    moe_w2_b = moe_w2.astype(BF16).reshape((-1,) + moe_w2.shape[2:])
    w_mla, w_dil, w_ret, w_gate = _split_w_in(w_in)
    w_uq = _split_w_uq(mla_w_uq)
    w_ukv = mla_w_ukv.astype(BF16)
    wbm, wbd, wbr, wout = (w.astype(BF16) for w in (w_br_mla, w_br_dil, w_br_ret, w_out))
    ffn_w1_b, ffn_w3_b, ffn_w2_b = (w.astype(BF16) for w in (ffn_w1, ffn_w3, ffn_w2))
    for layer in range(depth):
        mod = mod_all[layer]
        gmix = norm_mix[layer].reshape(1, d_model)

        q, k, v = _mla_prep(xt, mod, gmix, w_mla, mla_q_norm[layer].reshape(1, -1),
                            mla_kv_norm[layer].reshape(1, -1), w_uq, w_ukv, cos_t, sin_t, layer, batch, seq)
        y_mla = _mla_attn(q, k, v)

        dil_qkv = _dil_prep(xt, mod, gmix, w_dil, cos_t, sin_t, layer, batch, seq)
        dil_outs = [_dil_attn(*dil_qkv[3 * g:3 * g + 3], window, dil)
                    for g, (window, dil) in enumerate(DIL_PAIRS)]

        rq, rk, rv = _ret_prep(xt, mod, gmix, w_ret, cos_t, sin_t, layer, seq)
        y_ret = _ret_scan(ret_log_decay[layer].astype(F32), rq, rk, rv,
                          ret_norm[layer].reshape(1, -1).astype(F32), batch, seq)

        gffn = norm_ffn[layer].reshape(1, d_model)
        i = layer // 2
        is_moe = layer % 2 == 1
        merged = _merge_out(xt, mod, gmix, w_gate, y_mla, y_ret, dil_outs, wbm, wbd, wbr, wout, gffn,
                            moe_router[i] if is_moe else jnp.zeros((d_model, N_EXPERTS), F32),
                            layer, batch, seq, is_moe)
        last = layer == depth - 1
        if is_moe:
            xt, h, route, expert_ids = merged
            xt = _moe(xt, h, route, expert_ids, mod, fgain, moe_w1_b, moe_w3_b, moe_w2_b, i * N_EXPERTS, seq,
                      last)
        else:
            xt = _ffn_dense(merged[0], mod, gffn, fgain, ffn_w1_b, ffn_w3_b, ffn_w2_b, i, seq, last)
    return xt.reshape(batch, seq, d_model)
```

```python
import functools

import jax
import jax.numpy as jnp
from jax import lax
from jax.experimental import pallas as pl
from jax.experimental.pallas import tpu as pltpu

F32 = jnp.float32
BF16 = jnp.bfloat16

EPS = 1e-6
NEG_INF = -1e30
ROPE_THETA = 10000.0
ROPE_DIM = 64
ADA_CHUNKS = 6

MLA_HEADS = 8
MLA_Q_RANK = 768
MLA_KV_RANK = 512
MLA_NOPE = 128
MLA_ROPE = ROPE_DIM
MLA_V = 128
MLA_QK = MLA_NOPE + MLA_ROPE

DIL_PAIRS = ((128, 1), (512, 4), (2048, 16))
DIL_HEADS = 8
DIL_HEAD_DIM = ROPE_DIM
DIL_GROUP_W = DIL_HEADS * DIL_HEAD_DIM
DIL_QBLOCK = 128

RET_HEADS = 8
RET_QK_DIM = ROPE_DIM
RET_V_DIM = 2 * RET_QK_DIM
RET_CHUNK = 256
RET_UNROLL = 4
RET_QK_W = RET_HEADS * RET_QK_DIM
RET_V_W = RET_HEADS * RET_V_DIM

N_EXPERTS = 8
TOP_K = 2
MOE_BLOCK = 256

LANES = 128
SUBLANES = 8
V7X_VMEM_LIMIT_BYTES = 56 * 1024 * 1024

ROW_TILE = 512
MERGE_TILE = 512
COMBINE_TILE = 256
MLA_Q_TILE = 1024
MLA_HEADS_PER_STEP = 8
MLA_KEY_CHUNK = 1024


def _params(n_grid_dims):
    return pltpu.CompilerParams(
        dimension_semantics=("arbitrary",) * n_grid_dims,
        vmem_limit_bytes=V7X_VMEM_LIMIT_BYTES,
    )


def _resident(shape):
    zeros = (0,) * len(shape)
    return pl.BlockSpec(shape, lambda *_: zeros, pipeline_mode=pl.Buffered(1))


def _layer_resident(stacked, layer):
    idx = (layer,) + (0,) * (stacked.ndim - 1)
    return pl.BlockSpec((1,) + stacked.shape[1:], lambda *_: idx, pipeline_mode=pl.Buffered(1))


def _silu(v):
    return v * jax.nn.sigmoid(v)


def _norm_mod(x, gain, scale, shift):
    ms = jnp.mean(x * x, axis=-1, keepdims=True)
    y = x * lax.rsqrt(ms + EPS) * gain
    return y * (1.0 + scale) + shift


def _rms(v, gain):
    ms = jnp.mean(v * v, axis=-1, keepdims=True)
    return v * lax.rsqrt(ms + EPS) * gain


def _rope(t, cos_t, sin_t):
    lane = lax.broadcasted_iota(jnp.int32, (t.shape[0], LANES), 1)
    first_half = (lane % ROPE_DIM) < (ROPE_DIM // 2)
    out = []
    for j in range(t.shape[1] // LANES):
        c = t[:, j * LANES:(j + 1) * LANES]
        rot = jnp.where(first_half,
                        pltpu.roll(c, LANES - ROPE_DIM // 2, 1),
                        pltpu.roll(c, ROPE_DIM // 2, 1))
        out.append(c * cos_t + rot * sin_t)
    return out[0] if len(out) == 1 else jnp.concatenate(out, axis=1)


def _dot(a, b):
    return jnp.dot(a, b, preferred_element_type=F32)


def _dot_nt(a, b):
    return lax.dot_general(a, b, (((1,), (1,)), ((), ())), preferred_element_type=F32)


def _dot_tn(a, b):
    return lax.dot_general(a, b, (((0,), (0,)), ((), ())), preferred_element_type=F32)


def _ada_kernel(c_ref, w_ref, b_ref, o_ref):
    o_ref[0] = jnp.dot(_silu(c_ref[...]), w_ref[0], preferred_element_type=F32,
                       precision=lax.Precision.HIGHEST) + b_ref[0]


def _ada(c, ada_w, ada_b):
    depth, d_model, n = ada_w.shape
    batch = c.shape[0]
    tn = n // 4
    out = pl.pallas_call(
        _ada_kernel,
        grid=(depth, n // tn),
        in_specs=[
            pl.BlockSpec((batch, d_model), lambda l, j: (0, 0)),
            pl.BlockSpec((1, d_model, tn), lambda l, j: (l, 0, j)),
            pl.BlockSpec((1, 1, tn), lambda l, j: (l, 0, j)),
        ],
        out_specs=pl.BlockSpec((1, batch, tn), lambda l, j: (l, 0, j)),
        out_shape=jax.ShapeDtypeStruct((depth, batch, n), F32),
        compiler_params=_params(2),
        name="ada",
    )(c, ada_w, ada_b.reshape(depth, 1, n))
    return out.reshape(depth, batch, ADA_CHUNKS, d_model)


def _mla_prep_kernel(x_ref, mod_ref, g_ref, wa_ref, qn_ref, kvn_ref, wuq_ref, wukv_ref, cos_ref, sin_ref,
                     q_ref, k_ref, v_ref):
    mod = mod_ref[0]
    h = _norm_mod(x_ref[...], g_ref[...], mod[1:2], mod[0:1]).astype(BF16)
    p = _dot(h, wa_ref[0])
    cq = _rms(p[:, :MLA_Q_RANK], qn_ref[...]).astype(BF16)
    ckv = _rms(p[:, MLA_Q_RANK:MLA_Q_RANK + MLA_KV_RANK], kvn_ref[...]).astype(BF16)
    cos_t = cos_ref[0]
    sin_t = sin_ref[0]
    kr = _rope(p[:, MLA_Q_RANK + MLA_KV_RANK:], cos_t, sin_t)
    q = _dot(cq, wuq_ref[0])
    kv = _dot(ckv, wukv_ref[0])
    nope_w = MLA_HEADS * MLA_NOPE
    qr = _rope(q[:, nope_w:], cos_t, sin_t)
    scale = MLA_QK ** -0.5
    kr_b = kr[:, :MLA_ROPE].astype(BF16)
    for hd in range(MLA_HEADS):
        q_ref[0, hd, :, :MLA_NOPE] = (q[:, hd * MLA_NOPE:(hd + 1) * MLA_NOPE] * scale).astype(BF16)
        q_ref[0, hd, :, MLA_NOPE:] = (qr[:, hd * MLA_ROPE:(hd + 1) * MLA_ROPE] * scale).astype(BF16)
        base = hd * (MLA_NOPE + MLA_V)
        k_ref[0, hd, :, :MLA_NOPE] = kv[:, base:base + MLA_NOPE].astype(BF16)
        k_ref[0, hd, :, MLA_NOPE:] = kr_b
        v_ref[0, hd] = kv[:, base + MLA_NOPE:base + MLA_NOPE + MLA_V].astype(BF16)


def _mla_prep(x, mod, gain, wa, qn, kvn, wuq, wukv, cos_t, sin_t, layer, batch, seq):
    t, d_model = x.shape
    tm = ROW_TILE
    per_b = seq // tm
    row = lambda i: (i, 0)
    hs = lambda i: (i // per_b, 0, i % per_b, 0)
    return pl.pallas_call(
        _mla_prep_kernel,
        grid=(t // tm,),
        in_specs=[
            pl.BlockSpec((tm, d_model), row),
            pl.BlockSpec((1, ADA_CHUNKS, d_model), lambda i: (i // per_b, 0, 0)),
            _resident((1, d_model)),
            _layer_resident(wa, layer),
            _resident((1, MLA_Q_RANK)),
            _resident((1, MLA_KV_RANK)),
            _layer_resident(wuq, layer),
            _layer_resident(wukv, layer),
            pl.BlockSpec((1, tm, LANES), lambda i: (i // per_b, i % per_b, 0)),
            pl.BlockSpec((1, tm, LANES), lambda i: (i // per_b, i % per_b, 0)),
        ],
        out_specs=[
            pl.BlockSpec((1, MLA_HEADS, tm, MLA_QK), hs),
            pl.BlockSpec((1, MLA_HEADS, tm, MLA_QK), hs),
            pl.BlockSpec((1, MLA_HEADS, tm, MLA_V), hs),
        ],
        out_shape=[
            jax.ShapeDtypeStruct((batch, MLA_HEADS, seq, MLA_QK), BF16),
            jax.ShapeDtypeStruct((batch, MLA_HEADS, seq, MLA_QK), BF16),
            jax.ShapeDtypeStruct((batch, MLA_HEADS, seq, MLA_V), BF16),
        ],
        compiler_params=_params(1),
        name="mla_prep",
    )(x, mod, gain, wa, qn, kvn, wuq, wukv, cos_t, sin_t)


def _mla_attn_kernel(q_ref, k_ref, v_ref, o_ref, *, heads_per_step, key_chunk):
    seq = k_ref.shape[2]
    ones = jnp.ones((key_chunk, LANES), BF16)
    for hd in range(heads_per_step):
        q = q_ref[0, hd]
        m = acc = None
        for c in range(seq // key_chunk):
            rows = slice(c * key_chunk, (c + 1) * key_chunk)
            s = _dot_nt(q, k_ref[0, hd, rows, :])
            v_ext = jnp.concatenate([v_ref[0, hd, rows, :], ones], axis=1)
            m_c = jnp.max(s, axis=-1, keepdims=True)
            if c == 0:
                m = m_c
                acc = _dot(jnp.exp(s - m).astype(BF16), v_ext)
            else:
                m_new = jnp.maximum(m, m_c)
                acc = jnp.exp(m - m_new) * acc + _dot(jnp.exp(s - m_new).astype(BF16), v_ext)
                m = m_new
        o_ref[:, hd * MLA_V:(hd + 1) * MLA_V] = (acc[:, :MLA_V] / acc[:, MLA_V:MLA_V + 1]).astype(o_ref.dtype)


def _mla_attn(q, k, v):
    batch, heads, seq, _ = q.shape
    tq = MLA_Q_TILE
    nq = seq // tq
    hps = MLA_HEADS_PER_STEP
    return pl.pallas_call(
        functools.partial(_mla_attn_kernel, heads_per_step=hps, key_chunk=MLA_KEY_CHUNK),
        grid=(batch, heads // hps, nq),
        in_specs=[
            pl.BlockSpec((1, hps, tq, MLA_QK), lambda b, h, i: (b, h, i, 0)),
            pl.BlockSpec((1, hps, seq, MLA_QK), lambda b, h, i: (b, h, 0, 0)),
            pl.BlockSpec((1, hps, seq, MLA_V), lambda b, h, i: (b, h, 0, 0)),
        ],
        out_specs=pl.BlockSpec((tq, hps * MLA_V), lambda b, h, i: (b * nq + i, h)),
        out_shape=jax.ShapeDtypeStruct((batch * seq, heads * MLA_V), BF16),
        compiler_params=_params(3),
        name="mla_attn",
    )(q, k, v)


def _dil_prep_kernel(x_ref, mod_ref, g_ref, w_ref, cos_ref, sin_ref, *refs):
    out_refs = refs[:3 * len(DIL_PAIRS)]
    h_scr = refs[3 * len(DIL_PAIRS)]
    mod = mod_ref[0]
    h32 = _norm_mod(x_ref[...], g_ref[...], mod[1:2], mod[0:1])
    n_lane_blocks = h32.shape[1] // LANES
    for j in range(n_lane_blocks):
        h_scr[j] = h32[:, j * LANES:(j + 1) * LANES]
    tm = h32.shape[0]
    gw = DIL_GROUP_W
    for g, (_, dil) in enumerate(DIL_PAIRS):
        n = tm // dil
        if dil == 1:
            hp, cos_t, sin_t = h32, cos_ref[0], sin_ref[0]
        else:
            hp = jnp.concatenate(
                [jnp.concatenate([h_scr[j, pl.ds(r, n, stride=dil), :] for j in range(n_lane_blocks)], axis=1)
                 for r in range(dil)], axis=0)
            cos_t = jnp.concatenate([cos_ref[0, pl.ds(r, n, stride=dil), :] for r in range(dil)], axis=0)
            sin_t = jnp.concatenate([sin_ref[0, pl.ds(r, n, stride=dil), :] for r in range(dil)], axis=0)
        pg = _dot(hp.astype(BF16), w_ref[0, :, g * 3 * gw:(g + 1) * 3 * gw])
        qg = (_rope(pg[:, :gw], cos_t, sin_t) * (DIL_HEAD_DIM ** -0.5)).astype(BF16)
        kg = _rope(pg[:, gw:2 * gw], cos_t, sin_t).astype(BF16)
        vg = pg[:, 2 * gw:].astype(BF16)
        q_ref, k_ref, v_ref = out_refs[3 * g:3 * g + 3]
        for r in range(dil):
            q_ref[0, r] = qg[r * n:(r + 1) * n]
            k_ref[0, r] = kg[r * n:(r + 1) * n]
            v_ref[0, r] = vg[r * n:(r + 1) * n]


def _dil_prep(x, mod, gain, w, cos_t, sin_t, layer, batch, seq):
    t, d_model = x.shape
    tm = ROW_TILE
    per_b = seq // tm
    out_specs, out_shape = [], []
    for _, dil in DIL_PAIRS:
        for _ in range(3):
            out_specs.append(pl.BlockSpec((1, dil, tm // dil, DIL_GROUP_W),
                                          lambda i: (i // per_b, 0, i % per_b, 0)))
            out_shape.append(jax.ShapeDtypeStruct((batch, dil, seq // dil, DIL_GROUP_W), BF16))
    return pl.pallas_call(
        _dil_prep_kernel,
        grid=(t // tm,),
        in_specs=[
            pl.BlockSpec((tm, d_model), lambda i: (i, 0)),
            pl.BlockSpec((1, ADA_CHUNKS, d_model), lambda i: (i // per_b, 0, 0)),
            _resident((1, d_model)),
            _layer_resident(w, layer),
            pl.BlockSpec((1, tm, LANES), lambda i: (i // per_b, i % per_b, 0)),
            pl.BlockSpec((1, tm, LANES), lambda i: (i // per_b, i % per_b, 0)),
        ],
        out_specs=out_specs,
        out_shape=out_shape,
        scratch_shapes=[pltpu.VMEM((d_model // LANES, tm, LANES), F32)],
        compiler_params=_params(1),
        name="dil_prep",
    )(x, mod, gain, w, cos_t, sin_t)


def _dil_attn_kernel(q_ref, k_ref, v_ref, o_ref, lse_ref, *, dil, segment, band):
    qb_rows = DIL_QBLOCK
    kw = 2 * qb_rows
    short = segment < kw
    per_window = kw // segment if short else 1
    nb = 1 if short else segment // qb_rows
    low_k = lax.broadcasted_iota(jnp.int32, (kw, LANES), 1).astype(F32).astype(BF16) < DIL_HEAD_DIM
    low_q = lax.broadcasted_iota(jnp.int32, (qb_rows, LANES), 1) < DIL_HEAD_DIM
    row_i = lax.broadcasted_iota(jnp.int32, (qb_rows, kw), 0)
    col_i = lax.broadcasted_iota(jnp.int32, (qb_rows, kw), 1)
    zeros_k = jnp.zeros((kw, LANES), BF16)
    ones_k = jnp.ones((kw, LANES), BF16)
    denom_cols = jnp.concatenate([jnp.where(low_k, ones_k, zeros_k), jnp.where(low_k, zeros_k, ones_k)], axis=0)

    def tile(idx, carry):
        r = idx // nb
        if short:
            q0 = 0
            r0 = (r // per_window) * per_window
            qt = q_ref[0, r]
            kt = jnp.concatenate([k_ref[0, r0 + j] for j in range(per_window)], axis=0)
            vt = jnp.concatenate([v_ref[0, r0 + j] for j in range(per_window)], axis=0)
            jq = (r - r0) * segment + row_i
            jk = col_i
        else:
            q0 = pl.multiple_of((idx % nb) * qb_rows, qb_rows)
            ks = pl.multiple_of(jnp.clip(q0 - band, 0, segment - kw), band)
            qt = q_ref[0, r, pl.ds(q0, qb_rows), :]
            kt = k_ref[0, r, pl.ds(ks, kw), :]
            vt = v_ref[0, r, pl.ds(ks, kw), :]
            jq = q0 + row_i
            jk = ks + col_i
        valid = jnp.logical_and(jnp.abs(jq - jk) <= band, jq // segment == jk // segment)
        for hp in range(DIL_GROUP_W // LANES):
            cols = slice(hp * LANES, (hp + 1) * LANES)
            qp, kp, vp = qt[:, cols], kt[:, cols], vt[:, cols]
            k2 = jnp.concatenate([jnp.where(low_k, kp, zeros_k), jnp.where(low_k, zeros_k, kp)], axis=0)
            v2 = jnp.concatenate([jnp.where(low_k, vp, zeros_k), jnp.where(low_k, zeros_k, vp)], axis=0)
            s = _dot_nt(qp, k2)
            s0 = jnp.where(valid, s[:, :kw], NEG_INF)
            s1 = jnp.where(valid, s[:, kw:], NEG_INF)
            m0 = jnp.max(s0, axis=-1, keepdims=True)
            m1 = jnp.max(s1, axis=-1, keepdims=True)
            p = jnp.concatenate([jnp.exp(s0 - m0), jnp.exp(s1 - m1)], axis=1).astype(BF16)
            pv = _dot(p, jnp.concatenate([v2, denom_cols], axis=1))
            den = pv[:, LANES:]
            o_ref[0, r, pl.ds(q0, qb_rows), cols] = (pv[:, :LANES] / den).astype(o_ref.dtype)
            lse_ref[0, r, pl.ds(q0, qb_rows), cols] = jnp.where(low_q, m0, m1) + jnp.log(den)
        return carry

    lax.fori_loop(0, dil * nb, tile, 0, unroll=8)


def _dil_attn(q, k, v, window, dil):
    batch, _, seg, gw = q.shape
    band = window // (2 * dil)
    assert band * 2 == DIL_QBLOCK and seg % DIL_QBLOCK == 0
    assert seg >= 2 * DIL_QBLOCK or (seg == DIL_QBLOCK and dil % 2 == 0)
    spec = pl.BlockSpec((1, dil, seg, gw), lambda b: (b, 0, 0, 0))
    return pl.pallas_call(
        functools.partial(_dil_attn_kernel, dil=dil, segment=seg, band=band),
        grid=(batch,),
        in_specs=[spec, spec, spec],
        out_specs=[spec, spec],
        out_shape=[jax.ShapeDtypeStruct(q.shape, BF16), jax.ShapeDtypeStruct(q.shape, F32)],
        compiler_params=_params(1),
        name=f"dil_attn_d{dil}",
    )(q, k, v)


def _ret_prep_kernel(x_ref, mod_ref, g_ref, w_ref, cos_ref, sin_ref, q_ref, k_ref, v_ref):
    mod = mod_ref[0]
    h = _norm_mod(x_ref[...], g_ref[...], mod[1:2], mod[0:1]).astype(BF16)
    p = _dot(h, w_ref[0])
    cos_t, sin_t = cos_ref[0], sin_ref[0]
    q_ref[...] = _rope(p[:, :RET_QK_W], cos_t, sin_t).astype(BF16)
    k_ref[...] = (_rope(p[:, RET_QK_W:2 * RET_QK_W], cos_t, sin_t) * (RET_QK_DIM ** -0.5)).astype(BF16)
    v_ref[...] = p[:, 2 * RET_QK_W:].astype(BF16)


def _ret_prep(x, mod, gain, w, cos_t, sin_t, layer, seq):
    t, d_model = x.shape
    tm = ROW_TILE
    per_b = seq // tm
    row = lambda i: (i, 0)
    return pl.pallas_call(
        _ret_prep_kernel,
        grid=(t // tm,),
        in_specs=[
            pl.BlockSpec((tm, d_model), row),
            pl.BlockSpec((1, ADA_CHUNKS, d_model), lambda i: (i // per_b, 0, 0)),
            _resident((1, d_model)),
            _layer_resident(w, layer),
            pl.BlockSpec((1, tm, LANES), lambda i: (i // per_b, i % per_b, 0)),
            pl.BlockSpec((1, tm, LANES), lambda i: (i // per_b, i % per_b, 0)),
        ],
        out_specs=[pl.BlockSpec((tm, RET_QK_W), row), pl.BlockSpec((tm, RET_QK_W), row),
                   pl.BlockSpec((tm, RET_V_W), row)],
        out_shape=[jax.ShapeDtypeStruct((t, RET_QK_W), BF16), jax.ShapeDtypeStruct((t, RET_QK_W), BF16),
                   jax.ShapeDtypeStruct((t, RET_V_W), BF16)],
        compiler_params=_params(1),
        name="ret_prep",
    )(x, mod, gain, w, cos_t, sin_t)


def _ret_scan_kernel(ld_ref, q_ref, k_ref, v_ref, gn_ref, o_ref,
                     yf_ref, st_ref, intra_ref, qdec_ref, kdec_ref, cdec_ref):
    c = RET_CHUNK
    seq = q_ref.shape[0]
    nc = seq // c
    n_pairs = RET_HEADS // 2
    pair_v = 2 * RET_V_DIM
    ia = lax.broadcasted_iota(jnp.int32, (c, c), 0)
    ib = lax.broadcasted_iota(jnp.int32, (c, c), 1)
    diff = (ia - ib).astype(F32)
    idx_q = lax.broadcasted_iota(jnp.int32, (c, pair_v), 0).astype(F32)
    lane_q = lax.broadcasted_iota(jnp.int32, (c, pair_v), 1)
    idx_k = lax.broadcasted_iota(jnp.int32, (c, LANES), 0).astype(F32)
    lane_k = lax.broadcasted_iota(jnp.int32, (c, LANES), 1)
    low_k = lane_k < RET_QK_DIM
    low_k16 = lane_k.astype(F32).astype(BF16) < RET_QK_DIM
    low_v16 = lane_q.astype(F32).astype(BF16) < RET_V_DIM
    srow =lax.broadcasted_iota(jnp.int32, (LANES, pair_v), 0)
    scol = lax.broadcasted_iota(jnp.int32, (LANES, pair_v), 1)
    blk0 = jnp.logical_and(srow < RET_QK_DIM, scol < RET_V_DIM)
    blk1 = jnp.logical_and(srow >= RET_QK_DIM, scol >= RET_V_DIM)
    diag = jnp.where(jnp.logical_or(blk0, blk1), 1.0, 0.0).astype(F32)

    for dr in range(2):
        for hd in range(RET_HEADS):
            lg = ld_ref[dr, hd]
            if dr == 0:
                mask = diff >= 0
                dist = jnp.where(mask, diff, 0.0)
            else:
                mask = diff < 0
                dist = jnp.where(mask, -diff, 0.0)
            intra_ref[dr, hd] = jnp.where(mask, jnp.exp(lg * dist), 0.0)
        for hp in range(n_pairs):
            lg0 = ld_ref[dr, 2 * hp]
            lg1 = ld_ref[dr, 2 * hp + 1]
            q_exp = idx_q + 1.0 if dr == 0 else c - idx_q
            k_exp = c - 1.0 - idx_k if dr == 0 else idx_k
            qdec_ref[dr, hp] = jnp.exp(jnp.where(lane_q < RET_V_DIM, lg0, lg1) * q_exp)
            kdec_ref[dr, hp] = jnp.exp(jnp.where(low_k, lg0, lg1) * k_exp)
            cdec_ref[dr, hp] = jnp.where(blk0, jnp.exp(lg0 * c), jnp.where(blk1, jnp.exp(lg1 * c), 0.0))

    def chunk(dr, n):
        r0 = pl.multiple_of(n * c, c)
        qc = q_ref[pl.ds(r0, c), :]
        kc = k_ref[pl.ds(r0, c), :]
        vc = v_ref[pl.ds(r0, c), :]
        ys = []
        for hp in range(n_pairs):
            qp = qc[:, hp * LANES:(hp + 1) * LANES]
            kp = kc[:, hp * LANES:(hp + 1) * LANES]
            vp = vc[:, hp * pair_v:(hp + 1) * pair_v]
            zk = jnp.zeros_like(kp)
            zv = jnp.zeros_like(vp)
            k2 = jnp.concatenate([jnp.where(low_k16, kp, zk), jnp.where(low_k16, zk, kp)], axis=0)
            v2 = jnp.concatenate([jnp.where(low_v16, vp, zv), jnp.where(low_v16, zv, vp)], axis=0)
            s = _dot_nt(qp, k2) * jnp.concatenate([intra_ref[dr, 2 * hp], intra_ref[dr, 2 * hp + 1]], axis=1)
            y_intra = _dot(s.astype(BF16), v2)
            state = st_ref[hp]
            y_inter = _dot(qp, state.astype(BF16)) * qdec_ref[dr, hp]
            ys.append(y_intra + y_inter)
            kd = (kp.astype(F32) * kdec_ref[dr, hp]).astype(BF16)
            st_ref[hp] = cdec_ref[dr, hp] * state + diag * _dot_tn(kd, vp)
        return r0, ys

    st_ref[...] = jnp.zeros_like(st_ref)

    def fwd(n, carry):
        r0, ys = chunk(0, n)
        for hp in range(n_pairs):
            yf_ref[pl.ds(r0, c), hp * pair_v:(hp + 1) * pair_v] = ys[hp]
        return carry

    lax.fori_loop(0, nc, fwd, 0, unroll=RET_UNROLL)
    st_ref[...] = jnp.zeros_like(st_ref)

    def bwd(i, carry):
        r0, ys = chunk(1, nc - 1 - i)
        for hp in range(n_pairs):
            y = ys[hp] + yf_ref[pl.ds(r0, c), hp * pair_v:(hp + 1) * pair_v]
            for half in range(2):
                cols = slice(hp * pair_v + half * RET_V_DIM, hp * pair_v + (half + 1) * RET_V_DIM)
                yh = y[:, half * RET_V_DIM:(half + 1) * RET_V_DIM]
                mu = jnp.mean(yh, axis=-1, keepdims=True)
                dev = yh - mu
                var = jnp.mean(dev * dev, axis=-1, keepdims=True)
                o_ref[pl.ds(r0, c), cols] = (dev * lax.rsqrt(var + EPS) * gn_ref[:, cols]).astype(o_ref.dtype)
        return carry

    lax.fori_loop(0, nc, bwd, 0, unroll=RET_UNROLL)


def _ret_scan(log_decay, q, k, v, ret_norm, batch, seq):
    c = RET_CHUNK
    n_pairs = RET_HEADS // 2
    return pl.pallas_call(
        _ret_scan_kernel,
        grid=(batch,),
        in_specs=[
            pl.BlockSpec(memory_space=pltpu.SMEM),
            pl.BlockSpec((seq, RET_QK_W), lambda b: (b, 0)),
            pl.BlockSpec((seq, RET_QK_W), lambda b: (b, 0)),
            pl.BlockSpec((seq, RET_V_W), lambda b: (b, 0)),
            _resident((1, RET_V_W)),
        ],
        out_specs=pl.BlockSpec((seq, RET_V_W), lambda b: (b, 0)),
        out_shape=jax.ShapeDtypeStruct((batch * seq, RET_V_W), BF16),
        scratch_shapes=[
            pltpu.VMEM((seq, RET_V_W), F32),
            pltpu.VMEM((n_pairs, LANES, 2 * RET_V_DIM), F32),
            pltpu.VMEM((2, RET_HEADS, c, c), F32),
            pltpu.VMEM((2, n_pairs, c, 2 * RET_V_DIM), F32),
            pltpu.VMEM((2, n_pairs, c, LANES), F32),
            pltpu.VMEM((2, n_pairs, LANES, 2 * RET_V_DIM), F32),
        ],
        compiler_params=_params(1),
        name="ret_scan",
    )(log_decay, q, k, v, ret_norm)


def _top2_route(h, w_hi_lo):
    h_hi = h.astype(BF16)
    h_lo = (h - h_hi.astype(F32)).astype(BF16)
    hh = _dot(h_hi, w_hi_lo)
    logits = hh[:, :LANES] + (hh[:, LANES:] + _dot(h_lo, w_hi_lo[:, :LANES]))
    lane = lax.broadcasted_iota(jnp.int32, logits.shape, 1).astype(F32)
    lg = jnp.where(lane < N_EXPERTS, logits, -jnp.inf)
    m1 = jnp.max(lg, axis=-1, keepdims=True)
    i1 = jnp.min(jnp.where(lg == m1, lane, float(LANES)), axis=-1, keepdims=True)
    lg2 = jnp.where(lane == i1, -jnp.inf, lg)
    m2 = jnp.max(lg2, axis=-1, keepdims=True)
    i2 = jnp.min(jnp.where(lg2 == m2, lane, float(LANES)), axis=-1, keepdims=True)
    e = jnp.exp(m2 - m1)
    w1 = 1.0 / (1.0 + e)
    w2 = e / (1.0 + e)
    return jnp.where(lane == 0, i1, jnp.where(lane == 1, i2, jnp.where(lane == 2, w1, jnp.where(lane == 3, w2, 0.0))))


def _merge_kernel(x_ref, mod_ref, g_ref, wg_ref, ymla_ref, yret_ref,
                  o0_ref, l0_ref, o1_ref, l1_ref, o2_ref, l2_ref,
                  wbm_ref, wbd_ref, wbr_ref, wout_ref, gf_ref, wr_ref, out_ref, *rest, route):
    nat_o, nat_l = rest[-2:]
    x = x_ref[...]
    mod = mod_ref[0]
    h = _norm_mod(x, g_ref[...], mod[1:2], mod[0:1]).astype(BF16)
    gates = _dot(h, wg_ref[0])
    d_model = x.shape[1]
    tm = x.shape[0]

    o_nat = [o0_ref[0, 0].astype(F32)]
    l_nat = [l0_ref[0, 0]]
    for g, (o_ref, l_ref) in enumerate(((o1_ref, l1_ref), (o2_ref, l2_ref))):
        dil = DIL_PAIRS[g + 1][1]
        n = tm // dil
        n_lane_blocks = DIL_GROUP_W // LANES
        for r in range(dil):
            o_r = o_ref[0, r].astype(F32)
            l_r = l_ref[0, r]
            for j in range(n_lane_blocks):
                nat_o[g, j, pl.ds(r, n, stride=dil), :] = o_r[:, j * LANES:(j + 1) * LANES]
                nat_l[g, j, pl.ds(r, n, stride=dil), :] = l_r[:, j * LANES:(j + 1) * LANES]
        o_nat.append(jnp.concatenate([nat_o[g, j] for j in range(n_lane_blocks)], axis=1))
        l_nat.append(jnp.concatenate([nat_l[g, j] for j in range(n_lane_blocks)], axis=1))
    m = jnp.maximum(jnp.maximum(l_nat[0], l_nat[1]), l_nat[2])
    ws = [jnp.exp(l - m) for l in l_nat]
    y_dil = (ws[0] * o_nat[0] + ws[1] * o_nat[1] + ws[2] * o_nat[2]) / (ws[0] + ws[1] + ws[2])

    y_ret = (_silu(gates[:, :RET_V_W]) * yret_ref[...].astype(F32)).astype(BF16)
    ga = gates[:, RET_V_W:RET_V_W + d_model]
    gb = gates[:, RET_V_W + d_model:RET_V_W + 2 * d_model]
    gc = gates[:, RET_V_W + 2 * d_model:]
    merged = (jax.nn.sigmoid(ga) * _dot(ymla_ref[...], wbm_ref[0])
              + jax.nn.sigmoid(gb) * _dot(y_dil.astype(BF16), wbd_ref[0])
              + jax.nn.sigmoid(gc) * _dot(y_ret, wbr_ref[0]))
    x_new = x + mod[2:3] * _dot(merged.astype(BF16), wout_ref[0])
    out_ref[...] = x_new
    if route:
        h_ref, route_ref, ids_ref = rest[:3]
        h2 = _norm_mod(x_new, gf_ref[...], mod[4:5], mod[3:4])
        h_ref[...] = h2
        r = _top2_route(h2, wr_ref[...])
        route_ref[...] = r
        ids_ref[...] = r.T[:ids_ref.shape[0]]


def _merge_out(x, mod, gain, wg, y_mla, y_ret, dil_outs, wbm, wbd, wbr, wout, gain_ffn, w_router,
               layer, batch, seq, route):
    t, d_model = x.shape
    tm = MERGE_TILE
    per_b = seq // tm
    row = lambda i: (i, 0)
    dil_specs, dil_args = [], []
    for (_, dil), (o, lse) in zip(DIL_PAIRS, dil_outs):
        spec = pl.BlockSpec((1, dil, tm // dil, DIL_GROUP_W), lambda i: (i // per_b, 0, i % per_b, 0))
        dil_specs += [spec, spec]
        dil_args += [o, lse]
    out_specs = [pl.BlockSpec((tm, d_model), row)]
    out_shape = [jax.ShapeDtypeStruct((t, d_model), F32)]
    if route:
        out_specs += [pl.BlockSpec((tm, d_model), row), pl.BlockSpec((tm, LANES), row),
                      pl.BlockSpec((SUBLANES, tm), lambda i: (0, i))]
        out_shape += [jax.ShapeDtypeStruct((t, d_model), F32), jax.ShapeDtypeStruct((t, LANES), F32),
                      jax.ShapeDtypeStruct((SUBLANES, t), F32)]
    wr32 = jnp.zeros((d_model, LANES), F32).at[:, :N_EXPERTS].set(w_router)
    wr_hi = wr32.astype(BF16)
    wr = jnp.concatenate([wr_hi, (wr32 - wr_hi.astype(F32)).astype(BF16)], axis=1)
    return pl.pallas_call(
        functools.partial(_merge_kernel, route=route),
        grid=(t // tm,),
        in_specs=[
            pl.BlockSpec((tm, d_model), row),
            pl.BlockSpec((1, ADA_CHUNKS, d_model), lambda i: (i // per_b, 0, 0)),
            _resident((1, d_model)),
            _layer_resident(wg, layer),
            pl.BlockSpec((tm, y_mla.shape[1]), row),
            pl.BlockSpec((tm, RET_V_W), row),
            *dil_specs,
            _layer_resident(wbm, layer), _layer_resident(wbd, layer), _layer_resident(wbr, layer),
            _layer_resident(wout, layer),
            _resident((1, d_model)),
            _resident(wr.shape),
        ],
        out_specs=out_specs,
        out_shape=out_shape,
        scratch_shapes=[pltpu.VMEM((2, DIL_GROUP_W // LANES, tm, LANES), F32),
                        pltpu.VMEM((2, DIL_GROUP_W // LANES, tm, LANES), F32)],
        compiler_params=_params(1),
        name="merge_out",
    )(x, mod, gain, wg, y_mla, y_ret, *dil_args, wbm, wbd, wbr, wout, gain_ffn, wr)


def _residual_out(y, fg_ref, final):
    return _rms(y, fg_ref[...]) if final else y


def _ffn_kernel(x_ref, mod_ref, g_ref, fg_ref, w1_ref, w3_ref, w2_ref, out_ref, *, n_chunks, final):
    x = x_ref[...]
    mod = mod_ref[0]
    h = _norm_mod(x, g_ref[...], mod[4:5], mod[3:4]).astype(BF16)
    d_ff = w1_ref.shape[2]
    fc = d_ff // n_chunks
    acc = jnp.zeros(x.shape, F32)
    for j in range(n_chunks):
        a = _dot(h, w1_ref[0, :, j * fc:(j + 1) * fc])
        b = _dot(h, w3_ref[0, :, j * fc:(j + 1) * fc])
        acc = acc + _dot((_silu(a) * b).astype(BF16), w2_ref[0, j * fc:(j + 1) * fc, :])
    out_ref[...] = _residual_out(x + mod[5:6] * acc, fg_ref, final)


def _ffn_dense(x, mod, gain, final_gain, w1, w3, w2, layer, seq, final):
    t, d_model = x.shape
    tm = ROW_TILE
    per_b = seq // tm
    row = lambda i: (i, 0)
    return pl.pallas_call(
        functools.partial(_ffn_kernel, n_chunks=1, final=final),
        grid=(t // tm,),
        in_specs=[
            pl.BlockSpec((tm, d_model), row),
            pl.BlockSpec((1, ADA_CHUNKS, d_model), lambda i: (i // per_b, 0, 0)),
            _resident((1, d_model)),
            _resident((1, d_model)),
            _layer_resident(w1, layer), _layer_resident(w3, layer), _layer_resident(w2, layer),
        ],
        out_specs=pl.BlockSpec((tm, d_model), row),
        out_shape=jax.ShapeDtypeStruct((t, d_model), F32),
        compiler_params=_params(1),
        name="ffn_dense",
    )(x, mod, gain, final_gain, w1, w3, w2)


def _row_copy(src_hbm, dst, src_row, dst_row, sem):
    return pltpu.make_async_copy(src_hbm.at[pl.ds(src_row, 1), :], dst.at[pl.ds(dst_row, 1), :], sem)


def _expert_kernel(be_ref, nact_ref, tok_ref, tok_next_ref, h_hbm, w1_ref, w3_ref, w2_ref, y_ref,
                   xbuf, sems, *, n_chunks):
    i = pl.program_id(0)
    nact = nact_ref[0]
    bm = y_ref.shape[0]
    slot = i % 2

    def wait_block(s):
        pltpu.make_async_copy(h_hbm.at[pl.ds(0, bm), :], xbuf.at[s], sems.at[s]).wait()

    @pl.when(i == 0)
    def _():
        def issue(r, carry):
            _row_copy(h_hbm, xbuf.at[0], tok_ref[0, 0, r], r, sems.at[0]).start()
            return carry

        lax.fori_loop(0, bm, issue, 0, unroll=8)

    @pl.when(i >= nact)
    def _():
        y_ref[...] = jnp.zeros_like(y_ref)

    @pl.when(i < nact)
    def _():
        wait_block(slot)
        xb = xbuf[slot].astype(BF16)
        for r in range(bm):
            _row_copy(h_hbm, xbuf.at[1 - slot], tok_next_ref[0, 0, r], r, sems.at[1 - slot]).start()
        d_ff = w1_ref.shape[2]
        fc = d_ff // n_chunks
        acc = jnp.zeros(y_ref.shape, F32)
        for j in range(n_chunks):
            a = _dot(xb, w1_ref[0, :, j * fc:(j + 1) * fc])
            b = _dot(xb, w3_ref[0, :, j * fc:(j + 1) * fc])
            acc = acc + _dot((_silu(a) * b).astype(BF16), w2_ref[0, j * fc:(j + 1) * fc, :])
        y_ref[...] = acc

    @pl.when(i == nact - 1)
    def _():
        wait_block(1 - slot)


def _moe_expert(block_expert, nact, row_tok, h, w1, w3, w2, expert_base):
    nb, _, bm = row_tok.shape
    d_model = h.shape[1]
    d_ff = w1.shape[2]
    wsel = lambda i, be, na: (expert_base + be[jnp.minimum(i, na[0] - 1)], 0, 0)
    return pl.pallas_call(
        functools.partial(_expert_kernel, n_chunks=2),
        grid_spec=pltpu.PrefetchScalarGridSpec(
            num_scalar_prefetch=2,
            grid=(nb,),
            in_specs=[
                pl.BlockSpec((1, 1, bm), lambda i, be, na: (i, 0, 0), memory_space=pltpu.SMEM),
                pl.BlockSpec((1, 1, bm), lambda i, be, na: (jnp.minimum(i + 1, nb - 1), 0, 0),
                             memory_space=pltpu.SMEM),
                pl.BlockSpec(memory_space=pl.ANY),
                pl.BlockSpec((1, d_model, d_ff), wsel, pipeline_mode=pl.Buffered(1)),
                pl.BlockSpec((1, d_model, d_ff), wsel, pipeline_mode=pl.Buffered(1)),
                pl.BlockSpec((1, d_ff, d_model), wsel, pipeline_mode=pl.Buffered(1)),
            ],
            out_specs=pl.BlockSpec((bm, d_model), lambda i, be, na: (i, 0)),
            scratch_shapes=[pltpu.VMEM((2, bm, d_model), F32), pltpu.SemaphoreType.DMA((2,))],
        ),
        out_shape=jax.ShapeDtypeStruct((nb * bm, d_model), F32),
        compiler_params=_params(1),
        name="moe_expert",
    )(block_expert, nact, row_tok, row_tok, h, w1, w3, w2)


def _combine_kernel(dest_ref, dest_next_ref, x_ref, mod_ref, route_ref, fg_ref, yb_hbm, out_ref, buf, sems,
                    *, final):
    tm = x_ref.shape[0]
    n_rows = TOP_K * tm
    i = pl.program_id(0)
    slot = i % 2

    def issue(rows_ref, s):
        def body(r, carry):
            _row_copy(yb_hbm, buf.at[s], rows_ref[0, 0, r], r, sems.at[s]).start()
            return carry

        lax.fori_loop(0, n_rows, body, 0, unroll=8)

    @pl.when(i == 0)
    def _():
        issue(dest_ref, 0)

    @pl.when(i + 1 < pl.num_programs(0))
    def _():
        for r in range(n_rows):
            _row_copy(yb_hbm, buf.at[1 - slot], dest_next_ref[0, 0, r], r, sems.at[1 - slot]).start()

    pltpu.make_async_copy(yb_hbm.at[pl.ds(0, n_rows), :], buf.at[slot], sems.at[slot]).wait()
    route = route_ref[...]
    y = route[:, 2:3] * buf[slot, :tm, :] + route[:, 3:4] * buf[slot, tm:, :]
    out_ref[...] = _residual_out(x_ref[...] + mod_ref[0][5:6] * y, fg_ref, final)


def _moe_combine(dest, x, mod, route, final_gain, yb, seq, final):
    t, d_model = x.shape
    tm = COMBINE_TILE
    per_b = seq // tm
    row = lambda i: (i, 0)
    return pl.pallas_call(
        functools.partial(_combine_kernel, final=final),
        grid=(t // tm,),
        in_specs=[
            pl.BlockSpec((1, 1, TOP_K * tm), lambda i: (i, 0, 0), memory_space=pltpu.SMEM),
            pl.BlockSpec((1, 1, TOP_K * tm), lambda i: (jnp.minimum(i + 1, t // tm - 1), 0, 0),
                         memory_space=pltpu.SMEM),
            pl.BlockSpec((tm, d_model), row),
            pl.BlockSpec((1, ADA_CHUNKS, d_model), lambda i: (i // per_b, 0, 0)),
            pl.BlockSpec((tm, LANES), row),
            _resident((1, d_model)),
            pl.BlockSpec(memory_space=pl.ANY),
        ],
        out_specs=pl.BlockSpec((tm, d_model), row),
        out_shape=jax.ShapeDtypeStruct((t, d_model), F32),
        scratch_shapes=[pltpu.VMEM((2, TOP_K * tm, d_model), F32), pltpu.SemaphoreType.DMA((2,))],
        compiler_params=_params(1),
        name="moe_combine",
    )(dest, dest, x, mod, route, final_gain, yb)


def _moe(x, h, route, expert_ids, mod, final_gain, w1, w3, w2, expert_base, seq, final):
    t, d_model = x.shape
    bm = MOE_BLOCK
    e_flat = expert_ids[:TOP_K].astype(jnp.int32).reshape(1, -1)
    n_assign = e_flat.shape[1]
    onehot = e_flat == jnp.arange(N_EXPERTS, dtype=jnp.int32)[:, None]
    csum = jnp.cumsum(onehot.astype(jnp.int32), axis=1)
    counts = csum[:, -1]
    padded = (counts + bm - 1) // bm * bm
    pends = jnp.cumsum(padded)
    row_of = (pends - padded)[:, None] + csum - 1
    dest = jnp.sum(jnp.where(onehot, row_of, 0), axis=0)
    nb = n_assign // bm + N_EXPERTS
    row_tok = jnp.zeros((nb * bm,), jnp.int32).at[dest].set(jnp.arange(n_assign, dtype=jnp.int32) % t)
    block_start = jnp.arange(nb, dtype=jnp.int32) * bm
    block_expert = jnp.minimum(
        jnp.sum((pends[None, :] <= block_start[:, None]).astype(jnp.int32), axis=1), N_EXPERTS - 1)
    nact = (pends[-1:] // bm).astype(jnp.int32)

    yb = _moe_expert(block_expert, nact, row_tok.reshape(nb, 1, bm), h, w1, w3, w2, expert_base)
    tmc = COMBINE_TILE
    dest_tiles = dest.reshape(TOP_K, t // tmc, tmc).transpose(1, 0, 2).reshape(t // tmc, 1, TOP_K * tmc)
    return _moe_combine(dest_tiles, x, mod, route, final_gain, yb, seq, final)


def _rope_tables(positions):
    inv_freq = ROPE_THETA ** (-jnp.arange(0, ROPE_DIM, 2, dtype=F32) / ROPE_DIM)
    ang = positions.astype(F32)[..., None] * inv_freq
    cos, sin = jnp.cos(ang), jnp.sin(ang)
    reps = LANES // ROPE_DIM
    cos_t = jnp.tile(jnp.concatenate([cos, cos], axis=-1), (1, 1, reps))
    sin_t = jnp.tile(jnp.concatenate([-sin, sin], axis=-1), (1, 1, reps))
    return cos_t, sin_t


def _split_w_in_kernel(wt_ref, mla_ref, dil_ref, ret_ref, gate_ref):
    n_cols = wt_ref.shape[1]
    main = (n_cols // LANES) * LANES
    w = wt_ref[0, :main, :].T
    if main < n_cols:
        last = wt_ref[0, n_cols - LANES:, :].T
        w = jnp.concatenate([w, last[:, LANES - (n_cols - main):]], axis=1)
    d_model = gate_ref.shape[2] // 4
    n_mla = MLA_Q_RANK + MLA_KV_RANK
    kr = w[:, n_mla:n_mla + MLA_ROPE]
    mla_ref[0] = jnp.concatenate([w[:, :n_mla], kr, kr], axis=1).astype(BF16)
    tail = w[:, n_mla + MLA_ROPE:]
    dil_w = 3 * len(DIL_PAIRS) * DIL_GROUP_W
    gw = DIL_GROUP_W
    n_g = len(DIL_PAIRS)
    dil_ref[0] = jnp.concatenate(
        [tail[:, (m * n_g + g) * gw:(m * n_g + g + 1) * gw] for g in range(n_g) for m in range(3)],
        axis=1).astype(BF16)
    ret_w = 2 * RET_QK_W + RET_V_W
    ret_ref[0] = tail[:, dil_w:dil_w + ret_w].astype(BF16)
    gate_ref[0] = tail[:, dil_w + ret_w:dil_w + ret_w + RET_V_W + 3 * d_model].astype(BF16)


def _split_w_in(w_in):
    depth, d_model, n_cols = w_in.shape
    rb = LANES
    widths = (MLA_Q_RANK + MLA_KV_RANK + 2 * MLA_ROPE, 3 * len(DIL_PAIRS) * DIL_GROUP_W,
              2 * RET_QK_W + RET_V_W, RET_V_W + 3 * d_model)
    assert sum(widths) - MLA_ROPE == n_cols
    return pl.pallas_call(
        _split_w_in_kernel,
        grid=(depth, d_model // rb),
        in_specs=[pl.BlockSpec((1, n_cols, rb), lambda l, i: (l, 0, i))],
        out_specs=[pl.BlockSpec((1, rb, n), lambda l, i: (l, i, 0)) for n in widths],
        out_shape=[jax.ShapeDtypeStruct((depth, d_model, n), BF16) for n in widths],
        compiler_params=_params(2),
        name="split_w_in",
    )(jnp.swapaxes(w_in, 1, 2))


def _split_w_uq(w_uq):
    depth = w_uq.shape[0]
    w = w_uq.astype(BF16).reshape(depth, MLA_Q_RANK, MLA_HEADS, MLA_QK)
    nope = w[..., :MLA_NOPE].reshape(depth, MLA_Q_RANK, MLA_HEADS * MLA_NOPE)
    rope = w[..., MLA_NOPE:].reshape(depth, MLA_Q_RANK, MLA_HEADS * MLA_ROPE)
    return jnp.concatenate([nope, rope], axis=-1)


def kernel(x, c, positions, ada_w, ada_b, norm_mix, norm_ffn, w_in, mla_q_norm, mla_w_uq, mla_kv_norm,
           mla_w_ukv, ret_log_decay, ret_norm, w_br_mla, w_br_dil, w_br_ret, w_out, ffn_w1, ffn_w3, ffn_w2,
           moe_router, moe_w1, moe_w3, moe_w2, final_norm):
    batch, seq, d_model = x.shape
    depth = ada_w.shape[0]
    cos_t, sin_t = _rope_tables(positions)
    mod_all = _ada(c, ada_w, ada_b)
    assert depth >= 1
    fgain = final_norm.reshape(1, d_model)
    xt = x.reshape(batch * seq, d_model)
    moe_w1_b = moe_w1.astype(BF16).reshape((-1,) + moe_w1.shape[2:])
    moe_w3_b = moe_w3.astype(BF16).reshape((-1,) + moe_w3.shape[2:])
    moe_w2_b = moe_w2.astype(BF16).reshape((-1,) + moe_w2.shape[2:])
    w_mla, w_dil, w_ret, w_gate = _split_w_in(w_in)
    w_uq = _split_w_uq(mla_w_uq)
    w_ukv = mla_w_ukv.astype(BF16)
    wbm, wbd, wbr, wout = (w.astype(BF16) for w in (w_br_mla, w_br_dil, w_br_ret, w_out))
    ffn_w1_b, ffn_w3_b, ffn_w2_b = (w.astype(BF16) for w in (ffn_w1, ffn_w3, ffn_w2))
    for layer in range(depth):
        mod = mod_all[layer]
        gmix = norm_mix[layer].reshape(1, d_model)

        q, k, v = _mla_prep(xt, mod, gmix, w_mla, mla_q_norm[layer].reshape(1, -1),
                            mla_kv_norm[layer].reshape(1, -1), w_uq, w_ukv, cos_t, sin_t, layer, batch, seq)
        y_mla = _mla_attn(q, k, v)

        dil_qkv = _dil_prep(xt, mod, gmix, w_dil, cos_t, sin_t, layer, batch, seq)
        dil_outs = [_dil_attn(*dil_qkv[3 * g:3 * g + 3], window, dil)
                    for g, (window, dil) in enumerate(DIL_PAIRS)]

        rq, rk, rv = _ret_prep(xt, mod, gmix, w_ret, cos_t, sin_t, layer, seq)
        y_ret = _ret_scan(ret_log_decay[layer].astype(F32), rq, rk, rv,
                          ret_norm[layer].reshape(1, -1).astype(F32), batch, seq)

        gffn = norm_ffn[layer].reshape(1, d_model)
        i = layer // 2
        is_moe = layer % 2 == 1
        merged = _merge_out(xt, mod, gmix, w_gate, y_mla, y_ret, dil_outs, wbm, wbd, wbr, wout, gffn,
                            moe_router[i] if is_moe else jnp.zeros((d_model, N_EXPERTS), F32),
                            layer, batch, seq, is_moe)
        last = layer == depth - 1
        if is_moe:
            xt, h, route, expert_ids = merged
            xt = _moe(xt, h, route, expert_ids, mod, fgain, moe_w1_b, moe_w3_b, moe_w2_b, i * N_EXPERTS, seq,
                      last)
        else:
            xt = _ffn_dense(merged[0], mod, gffn, fgain, ffn_w1_b, ffn_w3_b, ffn_w2_b, i, seq, last)
    return xt.reshape(batch, seq, d_model)
```

```python
import functools

import jax
import jax.numpy as jnp
from jax import lax
from jax.experimental import pallas as pl
from jax.experimental.pallas import tpu as pltpu

F32 = jnp.float32
BF16 = jnp.bfloat16

EPS = 1e-6
NEG_INF = -1e30
ROPE_THETA = 10000.0
ROPE_DIM = 64
ADA_CHUNKS = 6

MLA_HEADS = 8
MLA_Q_RANK = 768
MLA_KV_RANK = 512
MLA_NOPE = 128
MLA_ROPE = ROPE_DIM
MLA_V = 128
MLA_QK = MLA_NOPE + MLA_ROPE

DIL_PAIRS = ((128, 1), (512, 4), (2048, 16))
DIL_HEADS = 8
DIL_HEAD_DIM = ROPE_DIM
DIL_GROUP_W = DIL_HEADS * DIL_HEAD_DIM
DIL_QBLOCK = 128

RET_HEADS = 8
RET_QK_DIM = ROPE_DIM
RET_V_DIM = 2 * RET_QK_DIM
RET_CHUNK = 256
RET_UNROLL = 4
RET_QK_W = RET_HEADS * RET_QK_DIM
RET_V_W = RET_HEADS * RET_V_DIM

N_EXPERTS = 8
TOP_K = 2
MOE_BLOCK = 256

LANES = 128
SUBLANES = 8
V7X_VMEM_LIMIT_BYTES = 56 * 1024 * 1024

ROW_TILE = 1024
MERGE_TILE = 512
COMBINE_TILE = 256
MLA_Q_TILE = 1024
MLA_HEADS_PER_STEP = 8
MLA_KEY_CHUNK = 1024


def _params(n_grid_dims):
    return pltpu.CompilerParams(
        dimension_semantics=("arbitrary",) * n_grid_dims,
        vmem_limit_bytes=V7X_VMEM_LIMIT_BYTES,
    )


def _resident(shape):
    zeros = (0,) * len(shape)
    return pl.BlockSpec(shape, lambda *_: zeros, pipeline_mode=pl.Buffered(1))


def _layer_resident(stacked, layer):
    idx = (layer,) + (0,) * (stacked.ndim - 1)
    return pl.BlockSpec((1,) + stacked.shape[1:], lambda *_: idx, pipeline_mode=pl.Buffered(1))


def _silu(v):
    return v * jax.nn.sigmoid(v)


def _norm_mod(x, gain, scale, shift):
    ms = jnp.mean(x * x, axis=-1, keepdims=True)
    y = x * lax.rsqrt(ms + EPS) * gain
    return y * (1.0 + scale) + shift


def _rms(v, gain):
    ms = jnp.mean(v * v, axis=-1, keepdims=True)
    return v * lax.rsqrt(ms + EPS) * gain


def _rope(t, cos_t, sin_t):
    lane = lax.broadcasted_iota(jnp.int32, (t.shape[0], LANES), 1)
    first_half = (lane % ROPE_DIM) < (ROPE_DIM // 2)
    out = []
    for j in range(t.shape[1] // LANES):
        c = t[:, j * LANES:(j + 1) * LANES]
        rot = jnp.where(first_half,
                        pltpu.roll(c, LANES - ROPE_DIM // 2, 1),
                        pltpu.roll(c, ROPE_DIM // 2, 1))
        out.append(c * cos_t + rot * sin_t)
    return out[0] if len(out) == 1 else jnp.concatenate(out, axis=1)


def _dot(a, b):
    return jnp.dot(a, b, preferred_element_type=F32)


def _dot_nt(a, b):
    return lax.dot_general(a, b, (((1,), (1,)), ((), ())), preferred_element_type=F32)


def _dot_tn(a, b):
    return lax.dot_general(a, b, (((0,), (0,)), ((), ())), preferred_element_type=F32)


def _ada_kernel(c_ref, w_ref, b_ref, o_ref):
    o_ref[0] = jnp.dot(_silu(c_ref[...]), w_ref[0], preferred_element_type=F32,
                       precision=lax.Precision.HIGHEST) + b_ref[0]


def _ada(c, ada_w, ada_b):
    depth, d_model, n = ada_w.shape
    batch = c.shape[0]
    tn = n // 4
    out = pl.pallas_call(
        _ada_kernel,
        grid=(depth, n // tn),
        in_specs=[
            pl.BlockSpec((batch, d_model), lambda l, j: (0, 0)),
            pl.BlockSpec((1, d_model, tn), lambda l, j: (l, 0, j)),
            pl.BlockSpec((1, 1, tn), lambda l, j: (l, 0, j)),
        ],
        out_specs=pl.BlockSpec((1, batch, tn), lambda l, j: (l, 0, j)),
        out_shape=jax.ShapeDtypeStruct((depth, batch, n), F32),
        compiler_params=_params(2),
        name="ada",
    )(c, ada_w, ada_b.reshape(depth, 1, n))
    return out.reshape(depth, batch, ADA_CHUNKS, d_model)


def _mla_prep_kernel(x_ref, mod_ref, g_ref, wa_ref, qn_ref, kvn_ref, wuq_ref, wukv_ref, cos_ref, sin_ref,
                     q_ref, k_ref, v_ref):
    mod = mod_ref[0]
    h = _norm_mod(x_ref[...], g_ref[...], mod[1:2], mod[0:1]).astype(BF16)
    p = _dot(h, wa_ref[0])
    cq = _rms(p[:, :MLA_Q_RANK], qn_ref[...]).astype(BF16)
    ckv = _rms(p[:, MLA_Q_RANK:MLA_Q_RANK + MLA_KV_RANK], kvn_ref[...]).astype(BF16)
    cos_t = cos_ref[0]
    sin_t = sin_ref[0]
    kr = _rope(p[:, MLA_Q_RANK + MLA_KV_RANK:], cos_t, sin_t)
    q = _dot(cq, wuq_ref[0])
    kv = _dot(ckv, wukv_ref[0])
    nope_w = MLA_HEADS * MLA_NOPE
    qr = _rope(q[:, nope_w:], cos_t, sin_t)
    scale = MLA_QK ** -0.5
    kr_b = kr[:, :MLA_ROPE].astype(BF16)
    for hd in range(MLA_HEADS):
        q_ref[0, hd, :, :MLA_NOPE] = (q[:, hd * MLA_NOPE:(hd + 1) * MLA_NOPE] * scale).astype(BF16)
        q_ref[0, hd, :, MLA_NOPE:] = (qr[:, hd * MLA_ROPE:(hd + 1) * MLA_ROPE] * scale).astype(BF16)
        base = hd * (MLA_NOPE + MLA_V)
        k_ref[0, hd, :, :MLA_NOPE] = kv[:, base:base + MLA_NOPE].astype(BF16)
        k_ref[0, hd, :, MLA_NOPE:] = kr_b
        v_ref[0, hd] = kv[:, base + MLA_NOPE:base + MLA_NOPE + MLA_V].astype(BF16)


def _mla_prep(x, mod, gain, wa, qn, kvn, wuq, wukv, cos_t, sin_t, layer, batch, seq):
    t, d_model = x.shape
    tm = ROW_TILE
    per_b = seq // tm
    row = lambda i: (i, 0)
    hs = lambda i: (i // per_b, 0, i % per_b, 0)
    return pl.pallas_call(
        _mla_prep_kernel,
        grid=(t // tm,),
        in_specs=[
            pl.BlockSpec((tm, d_model), row),
            pl.BlockSpec((1, ADA_CHUNKS, d_model), lambda i: (i // per_b, 0, 0)),
            _resident((1, d_model)),
            _layer_resident(wa, layer),
            _resident((1, MLA_Q_RANK)),
            _resident((1, MLA_KV_RANK)),
            _layer_resident(wuq, layer),
            _layer_resident(wukv, layer),
            pl.BlockSpec((1, tm, LANES), lambda i: (i // per_b, i % per_b, 0)),
            pl.BlockSpec((1, tm, LANES), lambda i: (i // per_b, i % per_b, 0)),
        ],
        out_specs=[
            pl.BlockSpec((1, MLA_HEADS, tm, MLA_QK), hs),
            pl.BlockSpec((1, MLA_HEADS, tm, MLA_QK), hs),
            pl.BlockSpec((1, MLA_HEADS, tm, MLA_V), hs),
        ],
        out_shape=[
            jax.ShapeDtypeStruct((batch, MLA_HEADS, seq, MLA_QK), BF16),
            jax.ShapeDtypeStruct((batch, MLA_HEADS, seq, MLA_QK), BF16),
            jax.ShapeDtypeStruct((batch, MLA_HEADS, seq, MLA_V), BF16),
        ],
        compiler_params=_params(1),
        name="mla_prep",
    )(x, mod, gain, wa, qn, kvn, wuq, wukv, cos_t, sin_t)


def _mla_attn_kernel(q_ref, k_ref, v_ref, o_ref, *, heads_per_step, key_chunk):
    seq = k_ref.shape[2]
    ones = jnp.ones((key_chunk, LANES), BF16)
    for hd in range(heads_per_step):
        q = q_ref[0, hd]
        m = acc = None
        for c in range(seq // key_chunk):
            rows = slice(c * key_chunk, (c + 1) * key_chunk)
            s = _dot_nt(q, k_ref[0, hd, rows, :])
            v_ext = jnp.concatenate([v_ref[0, hd, rows, :], ones], axis=1)
            m_c = jnp.max(s, axis=-1, keepdims=True)
            if c == 0:
                m = m_c
                acc = _dot(jnp.exp(s - m).astype(BF16), v_ext)
            else:
                m_new = jnp.maximum(m, m_c)
                acc = jnp.exp(m - m_new) * acc + _dot(jnp.exp(s - m_new).astype(BF16), v_ext)
                m = m_new
        o_ref[:, hd * MLA_V:(hd + 1) * MLA_V] = (acc[:, :MLA_V] / acc[:, MLA_V:MLA_V + 1]).astype(o_ref.dtype)


def _mla_attn(q, k, v):
    batch, heads, seq, _ = q.shape
    tq = MLA_Q_TILE
    nq = seq // tq
    hps = MLA_HEADS_PER_STEP
    return pl.pallas_call(
        functools.partial(_mla_attn_kernel, heads_per_step=hps, key_chunk=MLA_KEY_CHUNK),
        grid=(batch, heads // hps, nq),
        in_specs=[
            pl.BlockSpec((1, hps, tq, MLA_QK), lambda b, h, i: (b, h, i, 0)),
            pl.BlockSpec((1, hps, seq, MLA_QK), lambda b, h, i: (b, h, 0, 0)),
            pl.BlockSpec((1, hps, seq, MLA_V), lambda b, h, i: (b, h, 0, 0)),
        ],
        out_specs=pl.BlockSpec((tq, hps * MLA_V), lambda b, h, i: (b * nq + i, h)),
        out_shape=jax.ShapeDtypeStruct((batch * seq, heads * MLA_V), BF16),
        compiler_params=_params(3),
        name="mla_attn",
    )(q, k, v)


def _dil_prep_kernel(x_ref, mod_ref, g_ref, w_ref, cos_ref, sin_ref, *refs):
    out_refs = refs[:3 * len(DIL_PAIRS)]
    h_scr = refs[3 * len(DIL_PAIRS)]
    mod = mod_ref[0]
    h32 = _norm_mod(x_ref[...], g_ref[...], mod[1:2], mod[0:1])
    n_lane_blocks = h32.shape[1] // LANES
    for j in range(n_lane_blocks):
        h_scr[j] = h32[:, j * LANES:(j + 1) * LANES]
    tm = h32.shape[0]
    gw = DIL_GROUP_W
    for g, (_, dil) in enumerate(DIL_PAIRS):
        n = tm // dil
        if dil == 1:
            hp, cos_t, sin_t = h32, cos_ref[0], sin_ref[0]
        else:
            hp = jnp.concatenate(
                [jnp.concatenate([h_scr[j, pl.ds(r, n, stride=dil), :] for j in range(n_lane_blocks)], axis=1)
                 for r in range(dil)], axis=0)
            cos_t = jnp.concatenate([cos_ref[0, pl.ds(r, n, stride=dil), :] for r in range(dil)], axis=0)
            sin_t = jnp.concatenate([sin_ref[0, pl.ds(r, n, stride=dil), :] for r in range(dil)], axis=0)
        pg = _dot(hp.astype(BF16), w_ref[0, :, g * 3 * gw:(g + 1) * 3 * gw])
        qg = (_rope(pg[:, :gw], cos_t, sin_t) * (DIL_HEAD_DIM ** -0.5)).astype(BF16)
        kg = _rope(pg[:, gw:2 * gw], cos_t, sin_t).astype(BF16)
        vg = pg[:, 2 * gw:].astype(BF16)
        q_ref, k_ref, v_ref = out_refs[3 * g:3 * g + 3]
        for r in range(dil):
            q_ref[0, r] = qg[r * n:(r + 1) * n]
            k_ref[0, r] = kg[r * n:(r + 1) * n]
            v_ref[0, r] = vg[r * n:(r + 1) * n]


def _dil_prep(x, mod, gain, w, cos_t, sin_t, layer, batch, seq):
    t, d_model = x.shape
    tm = ROW_TILE
    per_b = seq // tm
    out_specs, out_shape = [], []
    for _, dil in DIL_PAIRS:
        for _ in range(3):
            out_specs.append(pl.BlockSpec((1, dil, tm // dil, DIL_GROUP_W),
                                          lambda i: (i // per_b, 0, i % per_b, 0)))
            out_shape.append(jax.ShapeDtypeStruct((batch, dil, seq // dil, DIL_GROUP_W), BF16))
    return pl.pallas_call(
        _dil_prep_kernel,
        grid=(t // tm,),
        in_specs=[
            pl.BlockSpec((tm, d_model), lambda i: (i, 0)),
            pl.BlockSpec((1, ADA_CHUNKS, d_model), lambda i: (i // per_b, 0, 0)),
            _resident((1, d_model)),
            _layer_resident(w, layer),
            pl.BlockSpec((1, tm, LANES), lambda i: (i // per_b, i % per_b, 0)),
            pl.BlockSpec((1, tm, LANES), lambda i: (i // per_b, i % per_b, 0)),
        ],
        out_specs=out_specs,
        out_shape=out_shape,
        scratch_shapes=[pltpu.VMEM((d_model // LANES, tm, LANES), F32)],
        compiler_params=_params(1),
        name="dil_prep",
    )(x, mod, gain, w, cos_t, sin_t)


def _dil_attn_kernel(q_ref, k_ref, v_ref, o_ref, lse_ref, *, dil, segment, band):
    qb_rows = DIL_QBLOCK
    kw = 2 * qb_rows
    short = segment < kw
    per_window = kw // segment if short else 1
    nb = 1 if short else segment // qb_rows
    low_k = lax.broadcasted_iota(jnp.int32, (kw, LANES), 1).astype(F32).astype(BF16) < DIL_HEAD_DIM
    low_q = lax.broadcasted_iota(jnp.int32, (qb_rows, LANES), 1) < DIL_HEAD_DIM
    row_i = lax.broadcasted_iota(jnp.int32, (qb_rows, kw), 0)
    col_i = lax.broadcasted_iota(jnp.int32, (qb_rows, kw), 1)
    zeros_k = jnp.zeros((kw, LANES), BF16)
    ones_k = jnp.ones((kw, LANES), BF16)
    denom_cols = jnp.concatenate([jnp.where(low_k, ones_k, zeros_k), jnp.where(low_k, zeros_k, ones_k)], axis=0)

    def tile(idx, carry):
        r = idx // nb
        if short:
            q0 = 0
            r0 = (r // per_window) * per_window
            qt = q_ref[0, r]
            kt = jnp.concatenate([k_ref[0, r0 + j] for j in range(per_window)], axis=0)
            vt = jnp.concatenate([v_ref[0, r0 + j] for j in range(per_window)], axis=0)
            jq = (r - r0) * segment + row_i
            jk = col_i
        else:
            q0 = pl.multiple_of((idx % nb) * qb_rows, qb_rows)
            ks = pl.multiple_of(jnp.clip(q0 - band, 0, segment - kw), band)
            qt = q_ref[0, r, pl.ds(q0, qb_rows), :]
            kt = k_ref[0, r, pl.ds(ks, kw), :]
            vt = v_ref[0, r, pl.ds(ks, kw), :]
            jq = q0 + row_i
            jk = ks + col_i
        valid = jnp.logical_and(jnp.abs(jq - jk) <= band, jq // segment == jk // segment)
        for hp in range(DIL_GROUP_W // LANES):
            cols = slice(hp * LANES, (hp + 1) * LANES)
            qp, kp, vp = qt[:, cols], kt[:, cols], vt[:, cols]
            k2 = jnp.concatenate([jnp.where(low_k, kp, zeros_k), jnp.where(low_k, zeros_k, kp)], axis=0)
            v2 = jnp.concatenate([jnp.where(low_k, vp, zeros_k), jnp.where(low_k, zeros_k, vp)], axis=0)
            s = _dot_nt(qp, k2)
            s0 = jnp.where(valid, s[:, :kw], NEG_INF)
            s1 = jnp.where(valid, s[:, kw:], NEG_INF)
            m0 = jnp.max(s0, axis=-1, keepdims=True)
            m1 = jnp.max(s1, axis=-1, keepdims=True)
            p = jnp.concatenate([jnp.exp(s0 - m0), jnp.exp(s1 - m1)], axis=1).astype(BF16)
            pv = _dot(p, jnp.concatenate([v2, denom_cols], axis=1))
            den = pv[:, LANES:]
            o_ref[0, r, pl.ds(q0, qb_rows), cols] = (pv[:, :LANES] / den).astype(o_ref.dtype)
            lse_ref[0, r, pl.ds(q0, qb_rows), cols] = jnp.where(low_q, m0, m1) + jnp.log(den)
        return carry

    lax.fori_loop(0, dil * nb, tile, 0, unroll=8)


def _dil_attn(q, k, v, window, dil):
    batch, _, seg, gw = q.shape
    band = window // (2 * dil)
    assert band * 2 == DIL_QBLOCK and seg % DIL_QBLOCK == 0
    assert seg >= 2 * DIL_QBLOCK or (seg == DIL_QBLOCK and dil % 2 == 0)
    spec = pl.BlockSpec((1, dil, seg, gw), lambda b: (b, 0, 0, 0))
    return pl.pallas_call(
        functools.partial(_dil_attn_kernel, dil=dil, segment=seg, band=band),
        grid=(batch,),
        in_specs=[spec, spec, spec],
        out_specs=[spec, spec],
        out_shape=[jax.ShapeDtypeStruct(q.shape, BF16), jax.ShapeDtypeStruct(q.shape, F32)],
        compiler_params=_params(1),
        name=f"dil_attn_d{dil}",
    )(q, k, v)


def _ret_prep_kernel(x_ref, mod_ref, g_ref, w_ref, cos_ref, sin_ref, q_ref, k_ref, v_ref):
    mod = mod_ref[0]
    h = _norm_mod(x_ref[...], g_ref[...], mod[1:2], mod[0:1]).astype(BF16)
    p = _dot(h, w_ref[0])
    cos_t, sin_t = cos_ref[0], sin_ref[0]
    q_ref[...] = _rope(p[:, :RET_QK_W], cos_t, sin_t).astype(BF16)
    k_ref[...] = (_rope(p[:, RET_QK_W:2 * RET_QK_W], cos_t, sin_t) * (RET_QK_DIM ** -0.5)).astype(BF16)
    v_ref[...] = p[:, 2 * RET_QK_W:].astype(BF16)


def _ret_prep(x, mod, gain, w, cos_t, sin_t, layer, seq):
    t, d_model = x.shape
    tm = ROW_TILE
    per_b = seq // tm
    row = lambda i: (i, 0)
    return pl.pallas_call(
        _ret_prep_kernel,
        grid=(t // tm,),
        in_specs=[
            pl.BlockSpec((tm, d_model), row),
            pl.BlockSpec((1, ADA_CHUNKS, d_model), lambda i: (i // per_b, 0, 0)),
            _resident((1, d_model)),
            _layer_resident(w, layer),
            pl.BlockSpec((1, tm, LANES), lambda i: (i // per_b, i % per_b, 0)),
            pl.BlockSpec((1, tm, LANES), lambda i: (i // per_b, i % per_b, 0)),
        ],
        out_specs=[pl.BlockSpec((tm, RET_QK_W), row), pl.BlockSpec((tm, RET_QK_W), row),
                   pl.BlockSpec((tm, RET_V_W), row)],
        out_shape=[jax.ShapeDtypeStruct((t, RET_QK_W), BF16), jax.ShapeDtypeStruct((t, RET_QK_W), BF16),
                   jax.ShapeDtypeStruct((t, RET_V_W), BF16)],
        compiler_params=_params(1),
        name="ret_prep",
    )(x, mod, gain, w, cos_t, sin_t)


def _ret_scan_kernel(ld_ref, q_ref, k_ref, v_ref, gn_ref, o_ref,
                     yf_ref, st_ref, intra_ref, qdec_ref, kdec_ref, cdec_ref):
    c = RET_CHUNK
    seq = q_ref.shape[0]
    nc = seq // c
    n_pairs = RET_HEADS // 2
    pair_v = 2 * RET_V_DIM
    ia = lax.broadcasted_iota(jnp.int32, (c, c), 0)
    ib = lax.broadcasted_iota(jnp.int32, (c, c), 1)
    diff = (ia - ib).astype(F32)
    idx_q = lax.broadcasted_iota(jnp.int32, (c, pair_v), 0).astype(F32)
    lane_q = lax.broadcasted_iota(jnp.int32, (c, pair_v), 1)
    idx_k = lax.broadcasted_iota(jnp.int32, (c, LANES), 0).astype(F32)
    lane_k = lax.broadcasted_iota(jnp.int32, (c, LANES), 1)
    low_k = lane_k < RET_QK_DIM
    low_k16 = lane_k.astype(F32).astype(BF16) < RET_QK_DIM
    low_v16 = lane_q.astype(F32).astype(BF16) < RET_V_DIM
    srow =lax.broadcasted_iota(jnp.int32, (LANES, pair_v), 0)
    scol = lax.broadcasted_iota(jnp.int32, (LANES, pair_v), 1)
    blk0 = jnp.logical_and(srow < RET_QK_DIM, scol < RET_V_DIM)
    blk1 = jnp.logical_and(srow >= RET_QK_DIM, scol >= RET_V_DIM)
    diag = jnp.where(jnp.logical_or(blk0, blk1), 1.0, 0.0).astype(F32)

    for dr in range(2):
        for hd in range(RET_HEADS):
            lg = ld_ref[dr, hd]
            if dr == 0:
                mask = diff >= 0
                dist = jnp.where(mask, diff, 0.0)
            else:
                mask = diff < 0
                dist = jnp.where(mask, -diff, 0.0)
            intra_ref[dr, hd] = jnp.where(mask, jnp.exp(lg * dist), 0.0)
        for hp in range(n_pairs):
            lg0 = ld_ref[dr, 2 * hp]
            lg1 = ld_ref[dr, 2 * hp + 1]
            q_exp = idx_q + 1.0 if dr == 0 else c - idx_q
            k_exp = c - 1.0 - idx_k if dr == 0 else idx_k
            qdec_ref[dr, hp] = jnp.exp(jnp.where(lane_q < RET_V_DIM, lg0, lg1) * q_exp)
            kdec_ref[dr, hp] = jnp.exp(jnp.where(low_k, lg0, lg1) * k_exp)
            cdec_ref[dr, hp] = jnp.where(blk0, jnp.exp(lg0 * c), jnp.where(blk1, jnp.exp(lg1 * c), 0.0))

    def chunk(dr, n):
        r0 = pl.multiple_of(n * c, c)
        qc = q_ref[pl.ds(r0, c), :]
        kc = k_ref[pl.ds(r0, c), :]
        vc = v_ref[pl.ds(r0, c), :]
        ys = []
        for hp in range(n_pairs):
            qp = qc[:, hp * LANES:(hp + 1) * LANES]
            kp = kc[:, hp * LANES:(hp + 1) * LANES]
            vp = vc[:, hp * pair_v:(hp + 1) * pair_v]
            zk = jnp.zeros_like(kp)
            zv = jnp.zeros_like(vp)
            k2 = jnp.concatenate([jnp.where(low_k16, kp, zk), jnp.where(low_k16, zk, kp)], axis=0)
            v2 = jnp.concatenate([jnp.where(low_v16, vp, zv), jnp.where(low_v16, zv, vp)], axis=0)
            s = _dot_nt(qp, k2) * jnp.concatenate([intra_ref[dr, 2 * hp], intra_ref[dr, 2 * hp + 1]], axis=1)
            y_intra = _dot(s.astype(BF16), v2)
            state = st_ref[hp]
            y_inter = _dot(qp, state.astype(BF16)) * qdec_ref[dr, hp]
            ys.append(y_intra + y_inter)
            kd = (kp.astype(F32) * kdec_ref[dr, hp]).astype(BF16)
            st_ref[hp] = cdec_ref[dr, hp] * state + diag * _dot_tn(kd, vp)
        return r0, ys

    st_ref[...] = jnp.zeros_like(st_ref)

    def fwd(n, carry):
        r0, ys = chunk(0, n)
        for hp in range(n_pairs):
            yf_ref[pl.ds(r0, c), hp * pair_v:(hp + 1) * pair_v] = ys[hp]
        return carry

    lax.fori_loop(0, nc, fwd, 0, unroll=RET_UNROLL)
    st_ref[...] = jnp.zeros_like(st_ref)

    def bwd(i, carry):
        r0, ys = chunk(1, nc - 1 - i)
        for hp in range(n_pairs):
            y = ys[hp] + yf_ref[pl.ds(r0, c), hp * pair_v:(hp + 1) * pair_v]
            for half in range(2):
                cols = slice(hp * pair_v + half * RET_V_DIM, hp * pair_v + (half + 1) * RET_V_DIM)
                yh = y[:, half * RET_V_DIM:(half + 1) * RET_V_DIM]
                mu = jnp.mean(yh, axis=-1, keepdims=True)
                dev = yh - mu
                var = jnp.mean(dev * dev, axis=-1, keepdims=True)
                o_ref[pl.ds(r0, c), cols] = (dev * lax.rsqrt(var + EPS) * gn_ref[:, cols]).astype(o_ref.dtype)
        return carry

    lax.fori_loop(0, nc, bwd, 0, unroll=RET_UNROLL)


def _ret_scan(log_decay, q, k, v, ret_norm, batch, seq):
    c = RET_CHUNK
    n_pairs = RET_HEADS // 2
    return pl.pallas_call(
        _ret_scan_kernel,
        grid=(batch,),
        in_specs=[
            pl.BlockSpec(memory_space=pltpu.SMEM),
            pl.BlockSpec((seq, RET_QK_W), lambda b: (b, 0)),
            pl.BlockSpec((seq, RET_QK_W), lambda b: (b, 0)),
            pl.BlockSpec((seq, RET_V_W), lambda b: (b, 0)),
            _resident((1, RET_V_W)),
        ],
        out_specs=pl.BlockSpec((seq, RET_V_W), lambda b: (b, 0)),
        out_shape=jax.ShapeDtypeStruct((batch * seq, RET_V_W), BF16),
        scratch_shapes=[
            pltpu.VMEM((seq, RET_V_W), F32),
            pltpu.VMEM((n_pairs, LANES, 2 * RET_V_DIM), F32),
            pltpu.VMEM((2, RET_HEADS, c, c), F32),
            pltpu.VMEM((2, n_pairs, c, 2 * RET_V_DIM), F32),
            pltpu.VMEM((2, n_pairs, c, LANES), F32),
            pltpu.VMEM((2, n_pairs, LANES, 2 * RET_V_DIM), F32),
        ],
        compiler_params=_params(1),
        name="ret_scan",
    )(log_decay, q, k, v, ret_norm)


def _top2_route(h, w_hi_lo):
    h_hi = h.astype(BF16)
    h_lo = (h - h_hi.astype(F32)).astype(BF16)
    hh = _dot(h_hi, w_hi_lo)
    logits = hh[:, :LANES] + (hh[:, LANES:] + _dot(h_lo, w_hi_lo[:, :LANES]))
    lane = lax.broadcasted_iota(jnp.int32, logits.shape, 1).astype(F32)
    lg = jnp.where(lane < N_EXPERTS, logits, -jnp.inf)
    m1 = jnp.max(lg, axis=-1, keepdims=True)
    i1 = jnp.min(jnp.where(lg == m1, lane, float(LANES)), axis=-1, keepdims=True)
    lg2 = jnp.where(lane == i1, -jnp.inf, lg)
    m2 = jnp.max(lg2, axis=-1, keepdims=True)
    i2 = jnp.min(jnp.where(lg2 == m2, lane, float(LANES)), axis=-1, keepdims=True)
    e = jnp.exp(m2 - m1)
    w1 = 1.0 / (1.0 + e)
    w2 = e / (1.0 + e)
    return jnp.where(lane == 0, i1, jnp.where(lane == 1, i2, jnp.where(lane == 2, w1, jnp.where(lane == 3, w2, 0.0))))


def _merge_kernel(x_ref, mod_ref, g_ref, wg_ref, ymla_ref, yret_ref,
                  o0_ref, l0_ref, o1_ref, l1_ref, o2_ref, l2_ref,
                  wbm_ref, wbd_ref, wbr_ref, wout_ref, gf_ref, wr_ref, out_ref, *rest, route):
    nat_o, nat_l = rest[-2:]
    x = x_ref[...]
    mod = mod_ref[0]
    h = _norm_mod(x, g_ref[...], mod[1:2], mod[0:1]).astype(BF16)
    gates = _dot(h, wg_ref[0])
    d_model = x.shape[1]
    tm = x.shape[0]

    o_nat = [o0_ref[0, 0].astype(F32)]
    l_nat = [l0_ref[0, 0]]
    for g, (o_ref, l_ref) in enumerate(((o1_ref, l1_ref), (o2_ref, l2_ref))):
        dil = DIL_PAIRS[g + 1][1]
        n = tm // dil
        n_lane_blocks = DIL_GROUP_W // LANES
        for r in range(dil):
            o_r = o_ref[0, r].astype(F32)
            l_r = l_ref[0, r]
            for j in range(n_lane_blocks):
                nat_o[g, j, pl.ds(r, n, stride=dil), :] = o_r[:, j * LANES:(j + 1) * LANES]
                nat_l[g, j, pl.ds(r, n, stride=dil), :] = l_r[:, j * LANES:(j + 1) * LANES]
        o_nat.append(jnp.concatenate([nat_o[g, j] for j in range(n_lane_blocks)], axis=1))
        l_nat.append(jnp.concatenate([nat_l[g, j] for j in range(n_lane_blocks)], axis=1))
    m = jnp.maximum(jnp.maximum(l_nat[0], l_nat[1]), l_nat[2])
    ws = [jnp.exp(l - m) for l in l_nat]
    y_dil = (ws[0] * o_nat[0] + ws[1] * o_nat[1] + ws[2] * o_nat[2]) / (ws[0] + ws[1] + ws[2])

    y_ret = (_silu(gates[:, :RET_V_W]) * yret_ref[...].astype(F32)).astype(BF16)
    ga = gates[:, RET_V_W:RET_V_W + d_model]
    gb = gates[:, RET_V_W + d_model:RET_V_W + 2 * d_model]
    gc = gates[:, RET_V_W + 2 * d_model:]
    merged = (jax.nn.sigmoid(ga) * _dot(ymla_ref[...], wbm_ref[0])
              + jax.nn.sigmoid(gb) * _dot(y_dil.astype(BF16), wbd_ref[0])
              + jax.nn.sigmoid(gc) * _dot(y_ret, wbr_ref[0]))
    x_new = x + mod[2:3] * _dot(merged.astype(BF16), wout_ref[0])
    out_ref[...] = x_new
    if route:
        h_ref, route_ref, ids_ref = rest[:3]
        h2 = _norm_mod(x_new, gf_ref[...], mod[4:5], mod[3:4])
        h_ref[...] = h2
        r = _top2_route(h2, wr_ref[...])
        route_ref[...] = r
        ids_ref[...] = r.T[:ids_ref.shape[0]]


def _merge_out(x, mod, gain, wg, y_mla, y_ret, dil_outs, wbm, wbd, wbr, wout, gain_ffn, w_router,
               layer, batch, seq, route):
    t, d_model = x.shape
    tm = MERGE_TILE
    per_b = seq // tm
    row = lambda i: (i, 0)
    dil_specs, dil_args = [], []
    for (_, dil), (o, lse) in zip(DIL_PAIRS, dil_outs):
        spec = pl.BlockSpec((1, dil, tm // dil, DIL_GROUP_W), lambda i: (i // per_b, 0, i % per_b, 0))
        dil_specs += [spec, spec]
        dil_args += [o, lse]
    out_specs = [pl.BlockSpec((tm, d_model), row)]
    out_shape = [jax.ShapeDtypeStruct((t, d_model), F32)]
    if route:
        out_specs += [pl.BlockSpec((tm, d_model), row), pl.BlockSpec((tm, LANES), row),
                      pl.BlockSpec((SUBLANES, tm), lambda i: (0, i))]
        out_shape += [jax.ShapeDtypeStruct((t, d_model), F32), jax.ShapeDtypeStruct((t, LANES), F32),
                      jax.ShapeDtypeStruct((SUBLANES, t), F32)]
    wr32 = jnp.zeros((d_model, LANES), F32).at[:, :N_EXPERTS].set(w_router)
    wr_hi = wr32.astype(BF16)
    wr = jnp.concatenate([wr_hi, (wr32 - wr_hi.astype(F32)).astype(BF16)], axis=1)
    return pl.pallas_call(
        functools.partial(_merge_kernel, route=route),
        grid=(t // tm,),
        in_specs=[
            pl.BlockSpec((tm, d_model), row),
            pl.BlockSpec((1, ADA_CHUNKS, d_model), lambda i: (i // per_b, 0, 0)),
            _resident((1, d_model)),
            _layer_resident(wg, layer),
            pl.BlockSpec((tm, y_mla.shape[1]), row),
            pl.BlockSpec((tm, RET_V_W), row),
            *dil_specs,
            _layer_resident(wbm, layer), _layer_resident(wbd, layer), _layer_resident(wbr, layer),
            _layer_resident(wout, layer),
            _resident((1, d_model)),
            _resident(wr.shape),
        ],
        out_specs=out_specs,
        out_shape=out_shape,
        scratch_shapes=[pltpu.VMEM((2, DIL_GROUP_W // LANES, tm, LANES), F32),
                        pltpu.VMEM((2, DIL_GROUP_W // LANES, tm, LANES), F32)],
        compiler_params=_params(1),
        name="merge_out",
    )(x, mod, gain, wg, y_mla, y_ret, *dil_args, wbm, wbd, wbr, wout, gain_ffn, wr)


def _residual_out(y, fg_ref, final):
    return _rms(y, fg_ref[...]) if final else y


def _ffn_kernel(x_ref, mod_ref, g_ref, fg_ref, w1_ref, w3_ref, w2_ref, out_ref, *, n_chunks, final):
    x = x_ref[...]
    mod = mod_ref[0]
    h = _norm_mod(x, g_ref[...], mod[4:5], mod[3:4]).astype(BF16)
    d_ff = w1_ref.shape[2]
    fc = d_ff // n_chunks
    acc = jnp.zeros(x.shape, F32)
    for j in range(n_chunks):
        a = _dot(h, w1_ref[0, :, j * fc:(j + 1) * fc])
        b = _dot(h, w3_ref[0, :, j * fc:(j + 1) * fc])
        acc = acc + _dot((_silu(a) * b).astype(BF16), w2_ref[0, j * fc:(j + 1) * fc, :])
    out_ref[...] = _residual_out(x + mod[5:6] * acc, fg_ref, final)


def _ffn_dense(x, mod, gain, final_gain, w1, w3, w2, layer, seq, final):
    t, d_model = x.shape
    tm = ROW_TILE
    per_b = seq // tm
    row = lambda i: (i, 0)
    return pl.pallas_call(
        functools.partial(_ffn_kernel, n_chunks=1, final=final),
        grid=(t // tm,),
        in_specs=[
            pl.BlockSpec((tm, d_model), row),
            pl.BlockSpec((1, ADA_CHUNKS, d_model), lambda i: (i // per_b, 0, 0)),
            _resident((1, d_model)),
            _resident((1, d_model)),
            _layer_resident(w1, layer), _layer_resident(w3, layer), _layer_resident(w2, layer),
        ],
        out_specs=pl.BlockSpec((tm, d_model), row),
        out_shape=jax.ShapeDtypeStruct((t, d_model), F32),
        compiler_params=_params(1),
        name="ffn_dense",
    )(x, mod, gain, final_gain, w1, w3, w2)


def _row_copy(src_hbm, dst, src_row, dst_row, sem):
    return pltpu.make_async_copy(src_hbm.at[pl.ds(src_row, 1), :], dst.at[pl.ds(dst_row, 1), :], sem)


def _expert_kernel(be_ref, nact_ref, tok_ref, tok_next_ref, h_hbm, w1_ref, w3_ref, w2_ref, y_ref,
                   xbuf, sems, *, n_chunks):
    i = pl.program_id(0)
    nact = nact_ref[0]
    bm = y_ref.shape[0]
    slot = i % 2

    def wait_block(s):
        pltpu.make_async_copy(h_hbm.at[pl.ds(0, bm), :], xbuf.at[s], sems.at[s]).wait()

    @pl.when(i == 0)
    def _():
        def issue(r, carry):
            _row_copy(h_hbm, xbuf.at[0], tok_ref[0, 0, r], r, sems.at[0]).start()
            return carry

        lax.fori_loop(0, bm, issue, 0, unroll=8)

    @pl.when(i >= nact)
    def _():
        y_ref[...] = jnp.zeros_like(y_ref)

    @pl.when(i < nact)
    def _():
        wait_block(slot)
        xb = xbuf[slot].astype(BF16)
        for r in range(bm):
            _row_copy(h_hbm, xbuf.at[1 - slot], tok_next_ref[0, 0, r], r, sems.at[1 - slot]).start()
        d_ff = w1_ref.shape[2]
        fc = d_ff // n_chunks
        acc = jnp.zeros(y_ref.shape, F32)
        for j in range(n_chunks):
            a = _dot(xb, w1_ref[0, :, j * fc:(j + 1) * fc])
            b = _dot(xb, w3_ref[0, :, j * fc:(j + 1) * fc])
            acc = acc + _dot((_silu(a) * b).astype(BF16), w2_ref[0, j * fc:(j + 1) * fc, :])
        y_ref[...] = acc

    @pl.when(i == nact - 1)
    def _():
        wait_block(1 - slot)


def _moe_expert(block_expert, nact, row_tok, h, w1, w3, w2, expert_base):
    nb, _, bm = row_tok.shape
    d_model = h.shape[1]
    d_ff = w1.shape[2]
    wsel = lambda i, be, na: (expert_base + be[jnp.minimum(i, na[0] - 1)], 0, 0)
    return pl.pallas_call(
        functools.partial(_expert_kernel, n_chunks=2),
        grid_spec=pltpu.PrefetchScalarGridSpec(
            num_scalar_prefetch=2,
            grid=(nb,),
            in_specs=[
                pl.BlockSpec((1, 1, bm), lambda i, be, na: (i, 0, 0), memory_space=pltpu.SMEM),
                pl.BlockSpec((1, 1, bm), lambda i, be, na: (jnp.minimum(i + 1, nb - 1), 0, 0),
                             memory_space=pltpu.SMEM),
                pl.BlockSpec(memory_space=pl.ANY),
                pl.BlockSpec((1, d_model, d_ff), wsel, pipeline_mode=pl.Buffered(1)),
                pl.BlockSpec((1, d_model, d_ff), wsel, pipeline_mode=pl.Buffered(1)),
                pl.BlockSpec((1, d_ff, d_model), wsel, pipeline_mode=pl.Buffered(1)),
            ],
            out_specs=pl.BlockSpec((bm, d_model), lambda i, be, na: (i, 0)),
            scratch_shapes=[pltpu.VMEM((2, bm, d_model), F32), pltpu.SemaphoreType.DMA((2,))],
        ),
        out_shape=jax.ShapeDtypeStruct((nb * bm, d_model), F32),
        compiler_params=_params(1),
        name="moe_expert",
    )(block_expert, nact, row_tok, row_tok, h, w1, w3, w2)


def _combine_kernel(dest_ref, dest_next_ref, x_ref, mod_ref, route_ref, fg_ref, yb_hbm, out_ref, buf, sems,
                    *, final):
    tm = x_ref.shape[0]
    n_rows = TOP_K * tm
    i = pl.program_id(0)
    slot = i % 2

    def issue(rows_ref, s):
        def body(r, carry):
            _row_copy(yb_hbm, buf.at[s], rows_ref[0, 0, r], r, sems.at[s]).start()
            return carry

        lax.fori_loop(0, n_rows, body, 0, unroll=8)

    @pl.when(i == 0)
    def _():
        issue(dest_ref, 0)

    @pl.when(i + 1 < pl.num_programs(0))
    def _():
        for r in range(n_rows):
            _row_copy(yb_hbm, buf.at[1 - slot], dest_next_ref[0, 0, r], r, sems.at[1 - slot]).start()

    pltpu.make_async_copy(yb_hbm.at[pl.ds(0, n_rows), :], buf.at[slot], sems.at[slot]).wait()
    route = route_ref[...]
    y = route[:, 2:3] * buf[slot, :tm, :] + route[:, 3:4] * buf[slot, tm:, :]
    out_ref[...] = _residual_out(x_ref[...] + mod_ref[0][5:6] * y, fg_ref, final)


def _moe_combine(dest, x, mod, route, final_gain, yb, seq, final):
    t, d_model = x.shape
    tm = COMBINE_TILE
    per_b = seq // tm
    row = lambda i: (i, 0)
    return pl.pallas_call(
        functools.partial(_combine_kernel, final=final),
        grid=(t // tm,),
        in_specs=[
            pl.BlockSpec((1, 1, TOP_K * tm), lambda i: (i, 0, 0), memory_space=pltpu.SMEM),
            pl.BlockSpec((1, 1, TOP_K * tm), lambda i: (jnp.minimum(i + 1, t // tm - 1), 0, 0),
                         memory_space=pltpu.SMEM),
            pl.BlockSpec((tm, d_model), row),
            pl.BlockSpec((1, ADA_CHUNKS, d_model), lambda i: (i // per_b, 0, 0)),
            pl.BlockSpec((tm, LANES), row),
            _resident((1, d_model)),
            pl.BlockSpec(memory_space=pl.ANY),
        ],
        out_specs=pl.BlockSpec((tm, d_model), row),
        out_shape=jax.ShapeDtypeStruct((t, d_model), F32),
        scratch_shapes=[pltpu.VMEM((2, TOP_K * tm, d_model), F32), pltpu.SemaphoreType.DMA((2,))],
        compiler_params=_params(1),
        name="moe_combine",
    )(dest, dest, x, mod, route, final_gain, yb)


def _moe(x, h, route, expert_ids, mod, final_gain, w1, w3, w2, expert_base, seq, final):
    t, d_model = x.shape
    bm = MOE_BLOCK
    e_flat = expert_ids[:TOP_K].astype(jnp.int32).reshape(1, -1)
    n_assign = e_flat.shape[1]
    onehot = e_flat == jnp.arange(N_EXPERTS, dtype=jnp.int32)[:, None]
    csum = jnp.cumsum(onehot.astype(jnp.int32), axis=1)
    counts = csum[:, -1]
    padded = (counts + bm - 1) // bm * bm
    pends = jnp.cumsum(padded)
    row_of = (pends - padded)[:, None] + csum - 1
    dest = jnp.sum(jnp.where(onehot, row_of, 0), axis=0)
    nb = n_assign // bm + N_EXPERTS
    row_tok = jnp.zeros((nb * bm,), jnp.int32).at[dest].set(jnp.arange(n_assign, dtype=jnp.int32) % t)
    block_start = jnp.arange(nb, dtype=jnp.int32) * bm
    block_expert = jnp.minimum(
        jnp.sum((pends[None, :] <= block_start[:, None]).astype(jnp.int32), axis=1), N_EXPERTS - 1)
    nact = (pends[-1:] // bm).astype(jnp.int32)

    yb = _moe_expert(block_expert, nact, row_tok.reshape(nb, 1, bm), h, w1, w3, w2, expert_base)
    tmc = COMBINE_TILE
    dest_tiles = dest.reshape(TOP_K, t // tmc, tmc).transpose(1, 0, 2).reshape(t // tmc, 1, TOP_K * tmc)
    return _moe_combine(dest_tiles, x, mod, route, final_gain, yb, seq, final)


def _rope_tables(positions):
    inv_freq = ROPE_THETA ** (-jnp.arange(0, ROPE_DIM, 2, dtype=F32) / ROPE_DIM)
    ang = positions.astype(F32)[..., None] * inv_freq
    cos, sin = jnp.cos(ang), jnp.sin(ang)
    reps = LANES // ROPE_DIM
    cos_t = jnp.tile(jnp.concatenate([cos, cos], axis=-1), (1, 1, reps))
    sin_t = jnp.tile(jnp.concatenate([-sin, sin], axis=-1), (1, 1, reps))
    return cos_t, sin_t


def _split_w_in_kernel(wt_ref, mla_ref, dil_ref, ret_ref, gate_ref):
    n_cols = wt_ref.shape[1]
    main = (n_cols // LANES) * LANES
    w = wt_ref[0, :main, :].T
    if main < n_cols:
        last = wt_ref[0, n_cols - LANES:, :].T
        w = jnp.concatenate([w, last[:, LANES - (n_cols - main):]], axis=1)
    d_model = gate_ref.shape[2] // 4
    n_mla = MLA_Q_RANK + MLA_KV_RANK
    kr = w[:, n_mla:n_mla + MLA_ROPE]
    mla_ref[0] = jnp.concatenate([w[:, :n_mla], kr, kr], axis=1).astype(BF16)
    tail = w[:, n_mla + MLA_ROPE:]
    dil_w = 3 * len(DIL_PAIRS) * DIL_GROUP_W
    gw = DIL_GROUP_W
    n_g = len(DIL_PAIRS)
    dil_ref[0] = jnp.concatenate(
        [tail[:, (m * n_g + g) * gw:(m * n_g + g + 1) * gw] for g in range(n_g) for m in range(3)],
        axis=1).astype(BF16)
    ret_w = 2 * RET_QK_W + RET_V_W
    ret_ref[0] = tail[:, dil_w:dil_w + ret_w].astype(BF16)
    gate_ref[0] = tail[:, dil_w + ret_w:dil_w + ret_w + RET_V_W + 3 * d_model].astype(BF16)


def _split_w_in(w_in):
    depth, d_model, n_cols = w_in.shape
    rb = LANES
    widths = (MLA_Q_RANK + MLA_KV_RANK + 2 * MLA_ROPE, 3 * len(DIL_PAIRS) * DIL_GROUP_W,
              2 * RET_QK_W + RET_V_W, RET_V_W + 3 * d_model)
    assert sum(widths) - MLA_ROPE == n_cols
    return pl.pallas_call(
        _split_w_in_kernel,
        grid=(depth, d_model // rb),
        in_specs=[pl.BlockSpec((1, n_cols, rb), lambda l, i: (l, 0, i))],
        out_specs=[pl.BlockSpec((1, rb, n), lambda l, i: (l, i, 0)) for n in widths],
        out_shape=[jax.ShapeDtypeStruct((depth, d_model, n), BF16) for n in widths],
        compiler_params=_params(2),
        name="split_w_in",
    )(jnp.swapaxes(w_in, 1, 2))


def _split_w_uq(w_uq):
    depth = w_uq.shape[0]
    w = w_uq.astype(BF16).reshape(depth, MLA_Q_RANK, MLA_HEADS, MLA_QK)
    nope = w[..., :MLA_NOPE].reshape(depth, MLA_Q_RANK, MLA_HEADS * MLA_NOPE)
    rope = w[..., MLA_NOPE:].reshape(depth, MLA_Q_RANK, MLA_HEADS * MLA_ROPE)
    return jnp.concatenate([nope, rope], axis=-1)


def kernel(x, c, positions, ada_w, ada_b, norm_mix, norm_ffn, w_in, mla_q_norm, mla_w_uq, mla_kv_norm,
           mla_w_ukv, ret_log_decay, ret_norm, w_br_mla, w_br_dil, w_br_ret, w_out, ffn_w1, ffn_w3, ffn_w2,
           moe_router, moe_w1, moe_w3, moe_w2, final_norm):
    batch, seq, d_model = x.shape
    depth = ada_w.shape[0]
    cos_t, sin_t = _rope_tables(positions)
    mod_all = _ada(c, ada_w, ada_b)
    assert depth >= 1
    fgain = final_norm.reshape(1, d_model)
    xt = x.reshape(batch * seq, d_model)
    moe_w1_b = moe_w1.astype(BF16).reshape((-1,) + moe_w1.shape[2:])
    moe_w3_b = moe_w3.astype(BF16).reshape((-1,) + moe_w3.shape[2:])
    moe_w2_b = moe_w2.astype(BF16).reshape((-1,) + moe_w2.shape[2:])
    w_mla, w_dil, w_ret, w_gate = _split_w_in(w_in)
    w_uq = _split_w_uq(mla_w_uq)
    w_ukv = mla_w_ukv.astype(BF16)
    wbm, wbd, wbr, wout = (w.astype(BF16) for w in (w_br_mla, w_br_dil, w_br_ret, w_out))
    ffn_w1_b, ffn_w3_b, ffn_w2_b = (w.astype(BF16) for w in (ffn_w1, ffn_w3, ffn_w2))
    for layer in range(depth):
        mod = mod_all[layer]
        gmix = norm_mix[layer].reshape(1, d_model)

        q, k, v = _mla_prep(xt, mod, gmix, w_mla, mla_q_norm[layer].reshape(1, -1),
                            mla_kv_norm[layer].reshape(1, -1), w_uq, w_ukv, cos_t, sin_t, layer, batch, seq)
        y_mla = _mla_attn(q, k, v)

        dil_qkv = _dil_prep(xt, mod, gmix, w_dil, cos_t, sin_t, layer, batch, seq)
        dil_outs = [_dil_attn(*dil_qkv[3 * g:3 * g + 3], window, dil)
                    for g, (window, dil) in enumerate(DIL_PAIRS)]

        rq, rk, rv = _ret_prep(xt, mod, gmix, w_ret, cos_t, sin_t, layer, seq)
        y_ret = _ret_scan(ret_log_decay[layer].astype(F32), rq, rk, rv,
                          ret_norm[layer].reshape(1, -1).astype(F32), batch, seq)

        gffn = norm_ffn[layer].reshape(1, d_model)
        i = layer // 2
        is_moe = layer % 2 == 1
        merged = _merge_out(xt, mod, gmix, w_gate, y_mla, y_ret, dil_outs, wbm, wbd, wbr, wout, gffn,
                            moe_router[i] if is_moe else jnp.zeros((d_model, N_EXPERTS), F32),
                            layer, batch, seq, is_moe)
        last = layer == depth - 1
        if is_moe:
            xt, h, route, expert_ids = merged
            xt = _moe(xt, h, route, expert_ids, mod, fgain, moe_w1_b, moe_w3_b, moe_w2_b, i * N_EXPERTS, seq,
                      last)
        else:
            xt = _ffn_dense(merged[0], mod, gffn, fgain, ffn_w1_b, ffn_w3_b, ffn_w2_b, i, seq, last)
    return xt.reshape(batch, seq, d_model)
```

```python
import functools

import jax
import jax.numpy as jnp
from jax import lax
from jax.experimental import pallas as pl
from jax.experimental.pallas import tpu as pltpu

F32 = jnp.float32
BF16 = jnp.bfloat16

EPS = 1e-6
NEG_INF = -1e30
ROPE_THETA = 10000.0
ROPE_DIM = 64
ADA_CHUNKS = 6

MLA_HEADS = 8
MLA_Q_RANK = 768
MLA_KV_RANK = 512
MLA_NOPE = 128
MLA_ROPE = ROPE_DIM
MLA_V = 128
MLA_QK = MLA_NOPE + MLA_ROPE

DIL_PAIRS = ((128, 1), (512, 4), (2048, 16))
DIL_HEADS = 8
DIL_HEAD_DIM = ROPE_DIM
DIL_GROUP_W = DIL_HEADS * DIL_HEAD_DIM
DIL_QBLOCK = 128

RET_HEADS = 8
RET_QK_DIM = ROPE_DIM
RET_V_DIM = 2 * RET_QK_DIM
RET_CHUNK = 256
RET_UNROLL = 4
RET_QK_W = RET_HEADS * RET_QK_DIM
RET_V_W = RET_HEADS * RET_V_DIM

N_EXPERTS = 8
TOP_K = 2
MOE_BLOCK = 256

LANES = 128
SUBLANES = 8
V7X_VMEM_LIMIT_BYTES = 56 * 1024 * 1024

ROW_TILE = 1024
MERGE_TILE = 512
COMBINE_TILE = 256
MLA_Q_TILE = 1024
MLA_HEADS_PER_STEP = 8
MLA_KEY_CHUNK = 1024


def _params(n_grid_dims):
    return pltpu.CompilerParams(
        dimension_semantics=("arbitrary",) * n_grid_dims,
        vmem_limit_bytes=V7X_VMEM_LIMIT_BYTES,
    )


def _resident(shape):
    zeros = (0,) * len(shape)
    return pl.BlockSpec(shape, lambda *_: zeros, pipeline_mode=pl.Buffered(1))


def _layer_resident(stacked, layer):
    idx = (layer,) + (0,) * (stacked.ndim - 1)
    return pl.BlockSpec((1,) + stacked.shape[1:], lambda *_: idx, pipeline_mode=pl.Buffered(1))


def _silu(v):
    return v * jax.nn.sigmoid(v)


def _norm_mod(x, gain, scale, shift):
    ms = jnp.mean(x * x, axis=-1, keepdims=True)
    y = x * lax.rsqrt(ms + EPS) * gain
    return y * (1.0 + scale) + shift


def _rms(v, gain):
    ms = jnp.mean(v * v, axis=-1, keepdims=True)
    return v * lax.rsqrt(ms + EPS) * gain


def _rope(t, cos_t, sin_t):
    lane = lax.broadcasted_iota(jnp.int32, (t.shape[0], LANES), 1)
    first_half = (lane % ROPE_DIM) < (ROPE_DIM // 2)
    out = []
    for j in range(t.shape[1] // LANES):
        c = t[:, j * LANES:(j + 1) * LANES]
        rot = jnp.where(first_half,
                        pltpu.roll(c, LANES - ROPE_DIM // 2, 1),
                        pltpu.roll(c, ROPE_DIM // 2, 1))
        out.append(c * cos_t + rot * sin_t)
    return out[0] if len(out) == 1 else jnp.concatenate(out, axis=1)


def _dot(a, b):
    return jnp.dot(a, b, preferred_element_type=F32)


def _dot_nt(a, b):
    return lax.dot_general(a, b, (((1,), (1,)), ((), ())), preferred_element_type=F32)


def _dot_tn(a, b):
    return lax.dot_general(a, b, (((0,), (0,)), ((), ())), preferred_element_type=F32)


def _ada_kernel(c_ref, w_ref, b_ref, o_ref):
    o_ref[0] = jnp.dot(_silu(c_ref[...]), w_ref[0], preferred_element_type=F32,
                       precision=lax.Precision.HIGHEST) + b_ref[0]


def _ada(c, ada_w, ada_b):
    depth, d_model, n = ada_w.shape
    batch = c.shape[0]
    tn = n // 4
    out = pl.pallas_call(
        _ada_kernel,
        grid=(depth, n // tn),
        in_specs=[
            pl.BlockSpec((batch, d_model), lambda l, j: (0, 0)),
            pl.BlockSpec((1, d_model, tn), lambda l, j: (l, 0, j)),
            pl.BlockSpec((1, 1, tn), lambda l, j: (l, 0, j)),
        ],
        out_specs=pl.BlockSpec((1, batch, tn), lambda l, j: (l, 0, j)),
        out_shape=jax.ShapeDtypeStruct((depth, batch, n), F32),
        compiler_params=_params(2),
        name="ada",
    )(c, ada_w, ada_b.reshape(depth, 1, n))
    return out.reshape(depth, batch, ADA_CHUNKS, d_model)


def _mla_prep_kernel(x_ref, mod_ref, g_ref, wa_ref, qn_ref, kvn_ref, wuq_ref, wukv_ref, cos_ref, sin_ref,
                     q_ref, k_ref, v_ref):
    mod = mod_ref[0]
    h = _norm_mod(x_ref[...], g_ref[...], mod[1:2], mod[0:1]).astype(BF16)
    p = _dot(h, wa_ref[0])
    cq = _rms(p[:, :MLA_Q_RANK], qn_ref[...]).astype(BF16)
    ckv = _rms(p[:, MLA_Q_RANK:MLA_Q_RANK + MLA_KV_RANK], kvn_ref[...]).astype(BF16)
    cos_t = cos_ref[0]
    sin_t = sin_ref[0]
    kr = _rope(p[:, MLA_Q_RANK + MLA_KV_RANK:], cos_t, sin_t)
    q = _dot(cq, wuq_ref[0])
    kv = _dot(ckv, wukv_ref[0])
    nope_w = MLA_HEADS * MLA_NOPE
    qr = _rope(q[:, nope_w:], cos_t, sin_t)
    scale = MLA_QK ** -0.5
    kr_b = kr[:, :MLA_ROPE].astype(BF16)
    for hd in range(MLA_HEADS):
        q_ref[0, hd, :, :MLA_NOPE] = (q[:, hd * MLA_NOPE:(hd + 1) * MLA_NOPE] * scale).astype(BF16)
        q_ref[0, hd, :, MLA_NOPE:] = (qr[:, hd * MLA_ROPE:(hd + 1) * MLA_ROPE] * scale).astype(BF16)
        base = hd * (MLA_NOPE + MLA_V)
        k_ref[0, hd, :, :MLA_NOPE] = kv[:, base:base + MLA_NOPE].astype(BF16)
        k_ref[0, hd, :, MLA_NOPE:] = kr_b
        v_ref[0, hd] = kv[:, base + MLA_NOPE:base + MLA_NOPE + MLA_V].astype(BF16)


def _mla_prep(x, mod, gain, wa, qn, kvn, wuq, wukv, cos_t, sin_t, layer, batch, seq):
    t, d_model = x.shape
    tm = ROW_TILE
    per_b = seq // tm
    row = lambda i: (i, 0)
    hs = lambda i: (i // per_b, 0, i % per_b, 0)
    return pl.pallas_call(
        _mla_prep_kernel,
        grid=(t // tm,),
        in_specs=[
            pl.BlockSpec((tm, d_model), row),
            pl.BlockSpec((1, ADA_CHUNKS, d_model), lambda i: (i // per_b, 0, 0)),
            _resident((1, d_model)),
            _layer_resident(wa, layer),
            _resident((1, MLA_Q_RANK)),
            _resident((1, MLA_KV_RANK)),
            _layer_resident(wuq, layer),
            _layer_resident(wukv, layer),
            pl.BlockSpec((1, tm, LANES), lambda i: (i // per_b, i % per_b, 0)),
            pl.BlockSpec((1, tm, LANES), lambda i: (i // per_b, i % per_b, 0)),
        ],
        out_specs=[
            pl.BlockSpec((1, MLA_HEADS, tm, MLA_QK), hs),
            pl.BlockSpec((1, MLA_HEADS, tm, MLA_QK), hs),
            pl.BlockSpec((1, MLA_HEADS, tm, MLA_V), hs),
        ],
        out_shape=[
            jax.ShapeDtypeStruct((batch, MLA_HEADS, seq, MLA_QK), BF16),
            jax.ShapeDtypeStruct((batch, MLA_HEADS, seq, MLA_QK), BF16),
            jax.ShapeDtypeStruct((batch, MLA_HEADS, seq, MLA_V), BF16),
        ],
        compiler_params=_params(1),
        name="mla_prep",
    )(x, mod, gain, wa, qn, kvn, wuq, wukv, cos_t, sin_t)


def _mla_attn_kernel(q_ref, k_ref, v_ref, o_ref, *, heads_per_step, key_chunk):
    seq = k_ref.shape[2]
    ones = jnp.ones((key_chunk, LANES), BF16)
    for hd in range(heads_per_step):
        q = q_ref[0, hd]
        m = acc = None
        for c in range(seq // key_chunk):
            rows = slice(c * key_chunk, (c + 1) * key_chunk)
            s = _dot_nt(q, k_ref[0, hd, rows, :])
            v_ext = jnp.concatenate([v_ref[0, hd, rows, :], ones], axis=1)
            m_c = jnp.max(s, axis=-1, keepdims=True)
            if c == 0:
                m = m_c
                acc = _dot(jnp.exp(s - m).astype(BF16), v_ext)
            else:
                m_new = jnp.maximum(m, m_c)
                acc = jnp.exp(m - m_new) * acc + _dot(jnp.exp(s - m_new).astype(BF16), v_ext)
                m = m_new
        o_ref[:, hd * MLA_V:(hd + 1) * MLA_V] = (acc[:, :MLA_V] / acc[:, MLA_V:MLA_V + 1]).astype(o_ref.dtype)


def _mla_attn(q, k, v):
    batch, heads, seq, _ = q.shape
    tq = MLA_Q_TILE
    nq = seq // tq
    hps = MLA_HEADS_PER_STEP
    return pl.pallas_call(
        functools.partial(_mla_attn_kernel, heads_per_step=hps, key_chunk=MLA_KEY_CHUNK),
        grid=(batch, heads // hps, nq),
        in_specs=[
            pl.BlockSpec((1, hps, tq, MLA_QK), lambda b, h, i: (b, h, i, 0)),
            pl.BlockSpec((1, hps, seq, MLA_QK), lambda b, h, i: (b, h, 0, 0)),
            pl.BlockSpec((1, hps, seq, MLA_V), lambda b, h, i: (b, h, 0, 0)),
        ],
        out_specs=pl.BlockSpec((tq, hps * MLA_V), lambda b, h, i: (b * nq + i, h)),
        out_shape=jax.ShapeDtypeStruct((batch * seq, heads * MLA_V), BF16),
        compiler_params=_params(3),
        name="mla_attn",
    )(q, k, v)


def _dil_prep_kernel(x_ref, mod_ref, g_ref, w_ref, cos_ref, sin_ref, *refs):
    out_refs = refs[:3 * len(DIL_PAIRS)]
    h_scr = refs[3 * len(DIL_PAIRS)]
    mod = mod_ref[0]
    h32 = _norm_mod(x_ref[...], g_ref[...], mod[1:2], mod[0:1])
    n_lane_blocks = h32.shape[1] // LANES
    for j in range(n_lane_blocks):
        h_scr[j] = h32[:, j * LANES:(j + 1) * LANES]
    tm = h32.shape[0]
    gw = DIL_GROUP_W
    for g, (_, dil) in enumerate(DIL_PAIRS):
        n = tm // dil
        if dil == 1:
            hp, cos_t, sin_t = h32, cos_ref[0], sin_ref[0]
        else:
            hp = jnp.concatenate(
                [jnp.concatenate([h_scr[j, pl.ds(r, n, stride=dil), :] for j in range(n_lane_blocks)], axis=1)
                 for r in range(dil)], axis=0)
            cos_t = jnp.concatenate([cos_ref[0, pl.ds(r, n, stride=dil), :] for r in range(dil)], axis=0)
            sin_t = jnp.concatenate([sin_ref[0, pl.ds(r, n, stride=dil), :] for r in range(dil)], axis=0)
        pg = _dot(hp.astype(BF16), w_ref[0, :, g * 3 * gw:(g + 1) * 3 * gw])
        qg = (_rope(pg[:, :gw], cos_t, sin_t) * (DIL_HEAD_DIM ** -0.5)).astype(BF16)
        kg = _rope(pg[:, gw:2 * gw], cos_t, sin_t).astype(BF16)
        vg = pg[:, 2 * gw:].astype(BF16)
        q_ref, k_ref, v_ref = out_refs[3 * g:3 * g + 3]
        for r in range(dil):
            q_ref[0, r] = qg[r * n:(r + 1) * n]
            k_ref[0, r] = kg[r * n:(r + 1) * n]
            v_ref[0, r] = vg[r * n:(r + 1) * n]


def _dil_prep(x, mod, gain, w, cos_t, sin_t, layer, batch, seq):
    t, d_model = x.shape
    tm = ROW_TILE
    per_b = seq // tm
    out_specs, out_shape = [], []
    for _, dil in DIL_PAIRS:
        for _ in range(3):
            out_specs.append(pl.BlockSpec((1, dil, tm // dil, DIL_GROUP_W),
                                          lambda i: (i // per_b, 0, i % per_b, 0)))
            out_shape.append(jax.ShapeDtypeStruct((batch, dil, seq // dil, DIL_GROUP_W), BF16))
    return pl.pallas_call(
        _dil_prep_kernel,
        grid=(t // tm,),
        in_specs=[
            pl.BlockSpec((tm, d_model), lambda i: (i, 0)),
            pl.BlockSpec((1, ADA_CHUNKS, d_model), lambda i: (i // per_b, 0, 0)),
            _resident((1, d_model)),
            _layer_resident(w, layer),
            pl.BlockSpec((1, tm, LANES), lambda i: (i // per_b, i % per_b, 0)),
            pl.BlockSpec((1, tm, LANES), lambda i: (i // per_b, i % per_b, 0)),
        ],
        out_specs=out_specs,
        out_shape=out_shape,
        scratch_shapes=[pltpu.VMEM((d_model // LANES, tm, LANES), F32)],
        compiler_params=_params(1),
        name="dil_prep",
    )(x, mod, gain, w, cos_t, sin_t)


def _dil_attn_kernel(q_ref, k_ref, v_ref, o_ref, lse_ref, *, dil, segment, band):
    qb_rows = DIL_QBLOCK
    kw = 2 * qb_rows
    short = segment < kw
    per_window = kw // segment if short else 1
    nb = 1 if short else segment // qb_rows
    low_k = lax.broadcasted_iota(jnp.int32, (kw, LANES), 1).astype(F32).astype(BF16) < DIL_HEAD_DIM
    low_q = lax.broadcasted_iota(jnp.int32, (qb_rows, LANES), 1) < DIL_HEAD_DIM
    row_i = lax.broadcasted_iota(jnp.int32, (qb_rows, kw), 0)
    col_i = lax.broadcasted_iota(jnp.int32, (qb_rows, kw), 1)
    zeros_k = jnp.zeros((kw, LANES), BF16)
    ones_k = jnp.ones((kw, LANES), BF16)
    denom_cols = jnp.concatenate([jnp.where(low_k, ones_k, zeros_k), jnp.where(low_k, zeros_k, ones_k)], axis=0)

    def tile(idx, carry):
        r = idx // nb
        if short:
            q0 = 0
            r0 = (r // per_window) * per_window
            qt = q_ref[0, r]
            kt = jnp.concatenate([k_ref[0, r0 + j] for j in range(per_window)], axis=0)
            vt = jnp.concatenate([v_ref[0, r0 + j] for j in range(per_window)], axis=0)
            jq = (r - r0) * segment + row_i
            jk = col_i
        else:
            q0 = pl.multiple_of((idx % nb) * qb_rows, qb_rows)
            ks = pl.multiple_of(jnp.clip(q0 - band, 0, segment - kw), band)
            qt = q_ref[0, r, pl.ds(q0, qb_rows), :]
            kt = k_ref[0, r, pl.ds(ks, kw), :]
            vt = v_ref[0, r, pl.ds(ks, kw), :]
            jq = q0 + row_i
            jk = ks + col_i
        valid = jnp.logical_and(jnp.abs(jq - jk) <= band, jq // segment == jk // segment)
        for hp in range(DIL_GROUP_W // LANES):
            cols = slice(hp * LANES, (hp + 1) * LANES)
            qp, kp, vp = qt[:, cols], kt[:, cols], vt[:, cols]
            k2 = jnp.concatenate([jnp.where(low_k, kp, zeros_k), jnp.where(low_k, zeros_k, kp)], axis=0)
            v2 = jnp.concatenate([jnp.where(low_k, vp, zeros_k), jnp.where(low_k, zeros_k, vp)], axis=0)
            s = _dot_nt(qp, k2)
            s0 = jnp.where(valid, s[:, :kw], NEG_INF)
            s1 = jnp.where(valid, s[:, kw:], NEG_INF)
            m0 = jnp.max(s0, axis=-1, keepdims=True)
            m1 = jnp.max(s1, axis=-1, keepdims=True)
            p = jnp.concatenate([jnp.exp(s0 - m0), jnp.exp(s1 - m1)], axis=1).astype(BF16)
            pv = _dot(p, jnp.concatenate([v2, denom_cols], axis=1))
            den = pv[:, LANES:]
            o_ref[0, r, pl.ds(q0, qb_rows), cols] = (pv[:, :LANES] / den).astype(o_ref.dtype)
            lse_ref[0, r, pl.ds(q0, qb_rows), cols] = jnp.where(low_q, m0, m1) + jnp.log(den)
        return carry

    lax.fori_loop(0, dil * nb, tile, 0, unroll=8)


def _dil_attn(q, k, v, window, dil):
    batch, _, seg, gw = q.shape
    band = window // (2 * dil)
    assert band * 2 == DIL_QBLOCK and seg % DIL_QBLOCK == 0
    assert seg >= 2 * DIL_QBLOCK or (seg == DIL_QBLOCK and dil % 2 == 0)
    spec = pl.BlockSpec((1, dil, seg, gw), lambda b: (b, 0, 0, 0))
    return pl.pallas_call(
        functools.partial(_dil_attn_kernel, dil=dil, segment=seg, band=band),
        grid=(batch,),
        in_specs=[spec, spec, spec],
        out_specs=[spec, spec],
        out_shape=[jax.ShapeDtypeStruct(q.shape, BF16), jax.ShapeDtypeStruct(q.shape, F32)],
        compiler_params=_params(1),
        name=f"dil_attn_d{dil}",
    )(q, k, v)


def _ret_prep_kernel(x_ref, mod_ref, g_ref, w_ref, cos_ref, sin_ref, q_ref, k_ref, v_ref):
    mod = mod_ref[0]
    h = _norm_mod(x_ref[...], g_ref[...], mod[1:2], mod[0:1]).astype(BF16)
    p = _dot(h, w_ref[0])
    cos_t, sin_t = cos_ref[0], sin_ref[0]
    q_ref[...] = _rope(p[:, :RET_QK_W], cos_t, sin_t).astype(BF16)
    k_ref[...] = (_rope(p[:, RET_QK_W:2 * RET_QK_W], cos_t, sin_t) * (RET_QK_DIM ** -0.5)).astype(BF16)
    v_ref[...] = p[:, 2 * RET_QK_W:].astype(BF16)


def _ret_prep(x, mod, gain, w, cos_t, sin_t, layer, seq):
    t, d_model = x.shape
    tm = ROW_TILE
    per_b = seq // tm
    row = lambda i: (i, 0)
    return pl.pallas_call(
        _ret_prep_kernel,
        grid=(t // tm,),
        in_specs=[
            pl.BlockSpec((tm, d_model), row),
            pl.BlockSpec((1, ADA_CHUNKS, d_model), lambda i: (i // per_b, 0, 0)),
            _resident((1, d_model)),
            _layer_resident(w, layer),
            pl.BlockSpec((1, tm, LANES), lambda i: (i // per_b, i % per_b, 0)),
            pl.BlockSpec((1, tm, LANES), lambda i: (i // per_b, i % per_b, 0)),
        ],
        out_specs=[pl.BlockSpec((tm, RET_QK_W), row), pl.BlockSpec((tm, RET_QK_W), row),
                   pl.BlockSpec((tm, RET_V_W), row)],
        out_shape=[jax.ShapeDtypeStruct((t, RET_QK_W), BF16), jax.ShapeDtypeStruct((t, RET_QK_W), BF16),
                   jax.ShapeDtypeStruct((t, RET_V_W), BF16)],
        compiler_params=_params(1),
        name="ret_prep",
    )(x, mod, gain, w, cos_t, sin_t)


def _ret_scan_kernel(ld_ref, q_ref, k_ref, v_ref, gn_ref, o_ref,
                     yf_ref, st_ref, intra_ref, qdec_ref, kdec_ref, cdec_ref):
    c = RET_CHUNK
    seq = q_ref.shape[0]
    nc = seq // c
    n_pairs = RET_HEADS // 2
    pair_v = 2 * RET_V_DIM
    ia = lax.broadcasted_iota(jnp.int32, (c, c), 0)
    ib = lax.broadcasted_iota(jnp.int32, (c, c), 1)
    diff = (ia - ib).astype(F32)
    idx_q = lax.broadcasted_iota(jnp.int32, (c, pair_v), 0).astype(F32)
    lane_q = lax.broadcasted_iota(jnp.int32, (c, pair_v), 1)
    idx_k = lax.broadcasted_iota(jnp.int32, (c, LANES), 0).astype(F32)
    lane_k = lax.broadcasted_iota(jnp.int32, (c, LANES), 1)
    low_k = lane_k < RET_QK_DIM
    low_k16 = lane_k.astype(F32).astype(BF16) < RET_QK_DIM
    low_v16 = lane_q.astype(F32).astype(BF16) < RET_V_DIM
    srow =lax.broadcasted_iota(jnp.int32, (LANES, pair_v), 0)
    scol = lax.broadcasted_iota(jnp.int32, (LANES, pair_v), 1)
    blk0 = jnp.logical_and(srow < RET_QK_DIM, scol < RET_V_DIM)
    blk1 = jnp.logical_and(srow >= RET_QK_DIM, scol >= RET_V_DIM)
    diag = jnp.where(jnp.logical_or(blk0, blk1), 1.0, 0.0).astype(F32)

    for dr in range(2):
        for hd in range(RET_HEADS):
            lg = ld_ref[dr, hd]
            if dr == 0:
                mask = diff >= 0
                dist = jnp.where(mask, diff, 0.0)
            else:
                mask = diff < 0
                dist = jnp.where(mask, -diff, 0.0)
            intra_ref[dr, hd] = jnp.where(mask, jnp.exp(lg * dist), 0.0)
        for hp in range(n_pairs):
            lg0 = ld_ref[dr, 2 * hp]
            lg1 = ld_ref[dr, 2 * hp + 1]
            q_exp = idx_q + 1.0 if dr == 0 else c - idx_q
            k_exp = c - 1.0 - idx_k if dr == 0 else idx_k
            qdec_ref[dr, hp] = jnp.exp(jnp.where(lane_q < RET_V_DIM, lg0, lg1) * q_exp)
            kdec_ref[dr, hp] = jnp.exp(jnp.where(low_k, lg0, lg1) * k_exp)
            cdec_ref[dr, hp] = jnp.where(blk0, jnp.exp(lg0 * c), jnp.where(blk1, jnp.exp(lg1 * c), 0.0))

    def chunk(dr, n):
        r0 = pl.multiple_of(n * c, c)
        qc = q_ref[pl.ds(r0, c), :]
        kc = k_ref[pl.ds(r0, c), :]
        vc = v_ref[pl.ds(r0, c), :]
        ys = []
        for hp in range(n_pairs):
            qp = qc[:, hp * LANES:(hp + 1) * LANES]
            kp = kc[:, hp * LANES:(hp + 1) * LANES]
            vp = vc[:, hp * pair_v:(hp + 1) * pair_v]
            zk = jnp.zeros_like(kp)
            zv = jnp.zeros_like(vp)
            k2 = jnp.concatenate([jnp.where(low_k16, kp, zk), jnp.where(low_k16, zk, kp)], axis=0)
            v2 = jnp.concatenate([jnp.where(low_v16, vp, zv), jnp.where(low_v16, zv, vp)], axis=0)
            s = _dot_nt(qp, k2) * jnp.concatenate([intra_ref[dr, 2 * hp], intra_ref[dr, 2 * hp + 1]], axis=1)
            y_intra = _dot(s.astype(BF16), v2)
            state = st_ref[hp]
            y_inter = _dot(qp, state.astype(BF16)) * qdec_ref[dr, hp]
            ys.append(y_intra + y_inter)
            kd = (kp.astype(F32) * kdec_ref[dr, hp]).astype(BF16)
            st_ref[hp] = cdec_ref[dr, hp] * state + diag * _dot_tn(kd, vp)
        return r0, ys

    st_ref[...] = jnp.zeros_like(st_ref)

    def fwd(n, carry):
        r0, ys = chunk(0, n)
        for hp in range(n_pairs):
            yf_ref[pl.ds(r0, c), hp * pair_v:(hp + 1) * pair_v] = ys[hp]
        return carry

    lax.fori_loop(0, nc, fwd, 0, unroll=RET_UNROLL)
    st_ref[...] = jnp.zeros_like(st_ref)

    def bwd(i, carry):
        r0, ys = chunk(1, nc - 1 - i)
        for hp in range(n_pairs):
            y = ys[hp] + yf_ref[pl.ds(r0, c), hp * pair_v:(hp + 1) * pair_v]
            for half in range(2):
                cols = slice(hp * pair_v + half * RET_V_DIM, hp * pair_v + (half + 1) * RET_V_DIM)
                yh = y[:, half * RET_V_DIM:(half + 1) * RET_V_DIM]
                mu = jnp.mean(yh, axis=-1, keepdims=True)
                dev = yh - mu
                var = jnp.mean(dev * dev, axis=-1, keepdims=True)
                o_ref[pl.ds(r0, c), cols] = (dev * lax.rsqrt(var + EPS) * gn_ref[:, cols]).astype(o_ref.dtype)
        return carry

    lax.fori_loop(0, nc, bwd, 0, unroll=RET_UNROLL)


def _ret_scan(log_decay, q, k, v, ret_norm, batch, seq):
    c = RET_CHUNK
    n_pairs = RET_HEADS // 2
    return pl.pallas_call(
        _ret_scan_kernel,
        grid=(batch,),
        in_specs=[
            pl.BlockSpec(memory_space=pltpu.SMEM),
            pl.BlockSpec((seq, RET_QK_W), lambda b: (b, 0)),
            pl.BlockSpec((seq, RET_QK_W), lambda b: (b, 0)),
            pl.BlockSpec((seq, RET_V_W), lambda b: (b, 0)),
            _resident((1, RET_V_W)),
        ],
        out_specs=pl.BlockSpec((seq, RET_V_W), lambda b: (b, 0)),
        out_shape=jax.ShapeDtypeStruct((batch * seq, RET_V_W), BF16),
        scratch_shapes=[
            pltpu.VMEM((seq, RET_V_W), F32),
            pltpu.VMEM((n_pairs, LANES, 2 * RET_V_DIM), F32),
            pltpu.VMEM((2, RET_HEADS, c, c), F32),
            pltpu.VMEM((2, n_pairs, c, 2 * RET_V_DIM), F32),
            pltpu.VMEM((2, n_pairs, c, LANES), F32),
            pltpu.VMEM((2, n_pairs, LANES, 2 * RET_V_DIM), F32),
        ],
        compiler_params=_params(1),
        name="ret_scan",
    )(log_decay, q, k, v, ret_norm)


def _top2_route(h, w_hi_lo):
    h_hi = h.astype(BF16)
    h_lo = (h - h_hi.astype(F32)).astype(BF16)
    hh = _dot(h_hi, w_hi_lo)
    logits = hh[:, :LANES] + (hh[:, LANES:] + _dot(h_lo, w_hi_lo[:, :LANES]))
    lane = lax.broadcasted_iota(jnp.int32, logits.shape, 1).astype(F32)
    lg = jnp.where(lane < N_EXPERTS, logits, -jnp.inf)
    m1 = jnp.max(lg, axis=-1, keepdims=True)
    i1 = jnp.min(jnp.where(lg == m1, lane, float(LANES)), axis=-1, keepdims=True)
    lg2 = jnp.where(lane == i1, -jnp.inf, lg)
    m2 = jnp.max(lg2, axis=-1, keepdims=True)
    i2 = jnp.min(jnp.where(lg2 == m2, lane, float(LANES)), axis=-1, keepdims=True)
    e = jnp.exp(m2 - m1)
    w1 = 1.0 / (1.0 + e)
    w2 = e / (1.0 + e)
    return jnp.where(lane == 0, i1, jnp.where(lane == 1, i2, jnp.where(lane == 2, w1, jnp.where(lane == 3, w2, 0.0))))


def _merge_kernel(x_ref, mod_ref, g_ref, wg_ref, ymla_ref, yret_ref,
                  o0_ref, l0_ref, o1_ref, l1_ref, o2_ref, l2_ref,
                  wbm_ref, wbd_ref, wbr_ref, wout_ref, gf_ref, wr_ref, out_ref, *rest, route):
    nat_o, nat_l = rest[-2:]
    x = x_ref[...]
    mod = mod_ref[0]
    h = _norm_mod(x, g_ref[...], mod[1:2], mod[0:1]).astype(BF16)
    gates = _dot(h, wg_ref[0])
    d_model = x.shape[1]
    tm = x.shape[0]

    o_nat = [o0_ref[0, 0].astype(F32)]
    l_nat = [l0_ref[0, 0]]
    for g, (o_ref, l_ref) in enumerate(((o1_ref, l1_ref), (o2_ref, l2_ref))):
        dil = DIL_PAIRS[g + 1][1]
        n = tm // dil
        n_lane_blocks = DIL_GROUP_W // LANES
        for r in range(dil):
            o_r = o_ref[0, r].astype(F32)
            l_r = l_ref[0, r]
            for j in range(n_lane_blocks):
                nat_o[g, j, pl.ds(r, n, stride=dil), :] = o_r[:, j * LANES:(j + 1) * LANES]
                nat_l[g, j, pl.ds(r, n, stride=dil), :] = l_r[:, j * LANES:(j + 1) * LANES]
        o_nat.append(jnp.concatenate([nat_o[g, j] for j in range(n_lane_blocks)], axis=1))
        l_nat.append(jnp.concatenate([nat_l[g, j] for j in range(n_lane_blocks)], axis=1))
    m = jnp.maximum(jnp.maximum(l_nat[0], l_nat[1]), l_nat[2])
    ws = [jnp.exp(l - m) for l in l_nat]
    y_dil = (ws[0] * o_nat[0] + ws[1] * o_nat[1] + ws[2] * o_nat[2]) / (ws[0] + ws[1] + ws[2])

    y_ret = (_silu(gates[:, :RET_V_W]) * yret_ref[...].astype(F32)).astype(BF16)
    ga = gates[:, RET_V_W:RET_V_W + d_model]
    gb = gates[:, RET_V_W + d_model:RET_V_W + 2 * d_model]
    gc = gates[:, RET_V_W + 2 * d_model:]
    merged = (jax.nn.sigmoid(ga) * _dot(ymla_ref[...], wbm_ref[0])
              + jax.nn.sigmoid(gb) * _dot(y_dil.astype(BF16), wbd_ref[0])
              + jax.nn.sigmoid(gc) * _dot(y_ret, wbr_ref[0]))
    x_new = x + mod[2:3] * _dot(merged.astype(BF16), wout_ref[0])
    out_ref[...] = x_new
    if route:
        h_ref, route_ref, ids_ref = rest[:3]
        h2 = _norm_mod(x_new, gf_ref[...], mod[4:5], mod[3:4])
        h_ref[...] = h2
        r = _top2_route(h2, wr_ref[...])
        route_ref[...] = r
        ids_ref[...] = r.T[:ids_ref.shape[0]]


def _merge_out(x, mod, gain, wg, y_mla, y_ret, dil_outs, wbm, wbd, wbr, wout, gain_ffn, w_router,
               layer, batch, seq, route):
    t, d_model = x.shape
    tm = MERGE_TILE
    per_b = seq // tm
    row = lambda i: (i, 0)
    dil_specs, dil_args = [], []
    for (_, dil), (o, lse) in zip(DIL_PAIRS, dil_outs):
        spec = pl.BlockSpec((1, dil, tm // dil, DIL_GROUP_W), lambda i: (i // per_b, 0, i % per_b, 0))
        dil_specs += [spec, spec]
        dil_args += [o, lse]
    out_specs = [pl.BlockSpec((tm, d_model), row)]
    out_shape = [jax.ShapeDtypeStruct((t, d_model), F32)]
    if route:
        out_specs += [pl.BlockSpec((tm, d_model), row), pl.BlockSpec((tm, LANES), row),
                      pl.BlockSpec((SUBLANES, tm), lambda i: (0, i))]
        out_shape += [jax.ShapeDtypeStruct((t, d_model), F32), jax.ShapeDtypeStruct((t, LANES), F32),
                      jax.ShapeDtypeStruct((SUBLANES, t), F32)]
    wr32 = jnp.zeros((d_model, LANES), F32).at[:, :N_EXPERTS].set(w_router)
    wr_hi = wr32.astype(BF16)
    wr = jnp.concatenate([wr_hi, (wr32 - wr_hi.astype(F32)).astype(BF16)], axis=1)
    return pl.pallas_call(
        functools.partial(_merge_kernel, route=route),
        grid=(t // tm,),
        in_specs=[
            pl.BlockSpec((tm, d_model), row),
            pl.BlockSpec((1, ADA_CHUNKS, d_model), lambda i: (i // per_b, 0, 0)),
            _resident((1, d_model)),
            _layer_resident(wg, layer),
            pl.BlockSpec((tm, y_mla.shape[1]), row),
            pl.BlockSpec((tm, RET_V_W), row),
            *dil_specs,
            _layer_resident(wbm, layer), _layer_resident(wbd, layer), _layer_resident(wbr, layer),
            _layer_resident(wout, layer),
            _resident((1, d_model)),
            _resident(wr.shape),
        ],
        out_specs=out_specs,
        out_shape=out_shape,
        scratch_shapes=[pltpu.VMEM((2, DIL_GROUP_W // LANES, tm, LANES), F32),
                        pltpu.VMEM((2, DIL_GROUP_W // LANES, tm, LANES), F32)],
        compiler_params=_params(1),
        name="merge_out",
    )(x, mod, gain, wg, y_mla, y_ret, *dil_args, wbm, wbd, wbr, wout, gain_ffn, wr)


def _residual_out(y, fg_ref, final):
    return _rms(y, fg_ref[...]) if final else y


def _ffn_kernel(x_ref, mod_ref, g_ref, fg_ref, w1_ref, w3_ref, w2_ref, out_ref, *, n_chunks, final):
    x = x_ref[...]
    mod = mod_ref[0]
    h = _norm_mod(x, g_ref[...], mod[4:5], mod[3:4]).astype(BF16)
    d_ff = w1_ref.shape[2]
    fc = d_ff // n_chunks
    acc = jnp.zeros(x.shape, F32)
    for j in range(n_chunks):
        a = _dot(h, w1_ref[0, :, j * fc:(j + 1) * fc])
        b = _dot(h, w3_ref[0, :, j * fc:(j + 1) * fc])
        acc = acc + _dot((_silu(a) * b).astype(BF16), w2_ref[0, j * fc:(j + 1) * fc, :])
    out_ref[...] = _residual_out(x + mod[5:6] * acc, fg_ref, final)


def _ffn_dense(x, mod, gain, final_gain, w1, w3, w2, layer, seq, final):
    t, d_model = x.shape
    tm = ROW_TILE
    per_b = seq // tm
    row = lambda i: (i, 0)
    return pl.pallas_call(
        functools.partial(_ffn_kernel, n_chunks=1, final=final),
        grid=(t // tm,),
        in_specs=[
            pl.BlockSpec((tm, d_model), row),
            pl.BlockSpec((1, ADA_CHUNKS, d_model), lambda i: (i // per_b, 0, 0)),
            _resident((1, d_model)),
            _resident((1, d_model)),
            _layer_resident(w1, layer), _layer_resident(w3, layer), _layer_resident(w2, layer),
        ],
        out_specs=pl.BlockSpec((tm, d_model), row),
        out_shape=jax.ShapeDtypeStruct((t, d_model), F32),
        compiler_params=_params(1),
        name="ffn_dense",
    )(x, mod, gain, final_gain, w1, w3, w2)


def _row_copy(src_hbm, dst, src_row, dst_row, sem):
    return pltpu.make_async_copy(src_hbm.at[pl.ds(src_row, 1), :], dst.at[pl.ds(dst_row, 1), :], sem)


def _expert_kernel(be_ref, nact_ref, tok_ref, tok_next_ref, h_hbm, w1_ref, w3_ref, w2_ref, y_ref,
                   xbuf, sems, *, n_chunks):
    i = pl.program_id(0)
    nact = nact_ref[0]
    bm = y_ref.shape[0]
    slot = i % 2

    def wait_block(s):
        pltpu.make_async_copy(h_hbm.at[pl.ds(0, bm), :], xbuf.at[s], sems.at[s]).wait()

    @pl.when(i == 0)
    def _():
        def issue(r, carry):
            _row_copy(h_hbm, xbuf.at[0], tok_ref[0, 0, r], r, sems.at[0]).start()
            return carry

        lax.fori_loop(0, bm, issue, 0, unroll=8)

    @pl.when(i >= nact)
    def _():
        y_ref[...] = jnp.zeros_like(y_ref)

    @pl.when(i < nact)
    def _():
        wait_block(slot)
        xb = xbuf[slot].astype(BF16)
        for r in range(bm):
            _row_copy(h_hbm, xbuf.at[1 - slot], tok_next_ref[0, 0, r], r, sems.at[1 - slot]).start()
        d_ff = w1_ref.shape[2]
        fc = d_ff // n_chunks
        acc = jnp.zeros(y_ref.shape, F32)
        for j in range(n_chunks):
            a = _dot(xb, w1_ref[0, :, j * fc:(j + 1) * fc])
            b = _dot(xb, w3_ref[0, :, j * fc:(j + 1) * fc])
            acc = acc + _dot((_silu(a) * b).astype(BF16), w2_ref[0, j * fc:(j + 1) * fc, :])
        y_ref[...] = acc

    @pl.when(i == nact - 1)
    def _():
        wait_block(1 - slot)


def _moe_expert(block_expert, nact, row_tok, h, w1, w3, w2, expert_base):
    nb, _, bm = row_tok.shape
    d_model = h.shape[1]
    d_ff = w1.shape[2]
    wsel = lambda i, be, na: (expert_base + be[jnp.minimum(i, na[0] - 1)], 0, 0)
    return pl.pallas_call(
        functools.partial(_expert_kernel, n_chunks=2),
        grid_spec=pltpu.PrefetchScalarGridSpec(
            num_scalar_prefetch=2,
            grid=(nb,),
            in_specs=[
                pl.BlockSpec((1, 1, bm), lambda i, be, na: (i, 0, 0), memory_space=pltpu.SMEM),
                pl.BlockSpec((1, 1, bm), lambda i, be, na: (jnp.minimum(i + 1, nb - 1), 0, 0),
                             memory_space=pltpu.SMEM),
                pl.BlockSpec(memory_space=pl.ANY),
                pl.BlockSpec((1, d_model, d_ff), wsel, pipeline_mode=pl.Buffered(1)),
                pl.BlockSpec((1, d_model, d_ff), wsel, pipeline_mode=pl.Buffered(1)),
                pl.BlockSpec((1, d_ff, d_model), wsel, pipeline_mode=pl.Buffered(1)),
            ],
            out_specs=pl.BlockSpec((bm, d_model), lambda i, be, na: (i, 0)),
            scratch_shapes=[pltpu.VMEM((2, bm, d_model), F32), pltpu.SemaphoreType.DMA((2,))],
        ),
        out_shape=jax.ShapeDtypeStruct((nb * bm, d_model), F32),
        compiler_params=_params(1),
        name="moe_expert",
    )(block_expert, nact, row_tok, row_tok, h, w1, w3, w2)


def _combine_kernel(dest_ref, dest_next_ref, x_ref, mod_ref, route_ref, fg_ref, yb_hbm, out_ref, buf, sems,
                    *, final):
    tm = x_ref.shape[0]
    n_rows = TOP_K * tm
    i = pl.program_id(0)
    slot = i % 2

    def issue(rows_ref, s):
        def body(r, carry):
            _row_copy(yb_hbm, buf.at[s], rows_ref[0, 0, r], r, sems.at[s]).start()
            return carry

        lax.fori_loop(0, n_rows, body, 0, unroll=8)

    @pl.when(i == 0)
    def _():
        issue(dest_ref, 0)

    @pl.when(i + 1 < pl.num_programs(0))
    def _():
        for r in range(n_rows):
            _row_copy(yb_hbm, buf.at[1 - slot], dest_next_ref[0, 0, r], r, sems.at[1 - slot]).start()

    pltpu.make_async_copy(yb_hbm.at[pl.ds(0, n_rows), :], buf.at[slot], sems.at[slot]).wait()
    route = route_ref[...]
    y = route[:, 2:3] * buf[slot, :tm, :] + route[:, 3:4] * buf[slot, tm:, :]
    out_ref[...] = _residual_out(x_ref[...] + mod_ref[0][5:6] * y, fg_ref, final)


def _moe_combine(dest, x, mod, route, final_gain, yb, seq, final):
    t, d_model = x.shape
    tm = COMBINE_TILE
    per_b = seq // tm
    row = lambda i: (i, 0)
    return pl.pallas_call(
        functools.partial(_combine_kernel, final=final),
        grid=(t // tm,),
        in_specs=[
            pl.BlockSpec((1, 1, TOP_K * tm), lambda i: (i, 0, 0), memory_space=pltpu.SMEM),
            pl.BlockSpec((1, 1, TOP_K * tm), lambda i: (jnp.minimum(i + 1, t // tm - 1), 0, 0),
                         memory_space=pltpu.SMEM),
            pl.BlockSpec((tm, d_model), row),
            pl.BlockSpec((1, ADA_CHUNKS, d_model), lambda i: (i // per_b, 0, 0)),
            pl.BlockSpec((tm, LANES), row),
            _resident((1, d_model)),
            pl.BlockSpec(memory_space=pl.ANY),
        ],
        out_specs=pl.BlockSpec((tm, d_model), row),
        out_shape=jax.ShapeDtypeStruct((t, d_model), F32),
        scratch_shapes=[pltpu.VMEM((2, TOP_K * tm, d_model), F32), pltpu.SemaphoreType.DMA((2,))],
        compiler_params=_params(1),
        name="moe_combine",
    )(dest, dest, x, mod, route, final_gain, yb)


def _moe(x, h, route, expert_ids, mod, final_gain, w1, w3, w2, expert_base, seq, final):
    t, d_model = x.shape
    bm = MOE_BLOCK
    e_flat = expert_ids[:TOP_K].astype(jnp.int32).reshape(1, -1)
    n_assign = e_flat.shape[1]
    onehot = e_flat == jnp.arange(N_EXPERTS, dtype=jnp.int32)[:, None]
    csum = jnp.cumsum(onehot.astype(jnp.int32), axis=1)
    counts = csum[:, -1]
    padded = (counts + bm - 1) // bm * bm
    pends = jnp.cumsum(padded)
    row_of = (pends - padded)[:, None] + csum - 1
    dest = jnp.sum(jnp.where(onehot, row_of, 0), axis=0)
    nb = n_assign // bm + N_EXPERTS
    row_tok = jnp.zeros((nb * bm,), jnp.int32).at[dest].set(
        jnp.arange(n_assign, dtype=jnp.int32) % t, unique_indices=True, mode="promise_in_bounds")
    block_start = jnp.arange(nb, dtype=jnp.int32) * bm
    block_expert = jnp.minimum(
        jnp.sum((pends[None, :] <= block_start[:, None]).astype(jnp.int32), axis=1), N_EXPERTS - 1)
    nact = (pends[-1:] // bm).astype(jnp.int32)

    yb = _moe_expert(block_expert, nact, row_tok.reshape(nb, 1, bm), h, w1, w3, w2, expert_base)
    tmc = COMBINE_TILE
    dest_tiles = dest.reshape(TOP_K, t // tmc, tmc).transpose(1, 0, 2).reshape(t // tmc, 1, TOP_K * tmc)
    return _moe_combine(dest_tiles, x, mod, route, final_gain, yb, seq, final)


def _rope_tables(positions):
    inv_freq = ROPE_THETA ** (-jnp.arange(0, ROPE_DIM, 2, dtype=F32) / ROPE_DIM)
    ang = positions.astype(F32)[..., None] * inv_freq
    cos, sin = jnp.cos(ang), jnp.sin(ang)
    reps = LANES // ROPE_DIM
    cos_t = jnp.tile(jnp.concatenate([cos, cos], axis=-1), (1, 1, reps))
    sin_t = jnp.tile(jnp.concatenate([-sin, sin], axis=-1), (1, 1, reps))
    return cos_t, sin_t


def _split_w_in_kernel(wt_ref, mla_ref, dil_ref, ret_ref, gate_ref):
    n_cols = wt_ref.shape[1]
    main = (n_cols // LANES) * LANES
    w = wt_ref[0, :main, :].T
    if main < n_cols:
        last = wt_ref[0, n_cols - LANES:, :].T
        w = jnp.concatenate([w, last[:, LANES - (n_cols - main):]], axis=1)
    d_model = gate_ref.shape[2] // 4
    n_mla = MLA_Q_RANK + MLA_KV_RANK
    kr = w[:, n_mla:n_mla + MLA_ROPE]
    mla_ref[0] = jnp.concatenate([w[:, :n_mla], kr, kr], axis=1).astype(BF16)
    tail = w[:, n_mla + MLA_ROPE:]
    dil_w = 3 * len(DIL_PAIRS) * DIL_GROUP_W
    gw = DIL_GROUP_W
    n_g = len(DIL_PAIRS)
    dil_ref[0] = jnp.concatenate(
        [tail[:, (m * n_g + g) * gw:(m * n_g + g + 1) * gw] for g in range(n_g) for m in range(3)],
        axis=1).astype(BF16)
    ret_w = 2 * RET_QK_W + RET_V_W
    ret_ref[0] = tail[:, dil_w:dil_w + ret_w].astype(BF16)
    gate_ref[0] = tail[:, dil_w + ret_w:dil_w + ret_w + RET_V_W + 3 * d_model].astype(BF16)


def _split_w_in(w_in):
    depth, d_model, n_cols = w_in.shape
    rb = LANES
    widths = (MLA_Q_RANK + MLA_KV_RANK + 2 * MLA_ROPE, 3 * len(DIL_PAIRS) * DIL_GROUP_W,
              2 * RET_QK_W + RET_V_W, RET_V_W + 3 * d_model)
    assert sum(widths) - MLA_ROPE == n_cols
    return pl.pallas_call(
        _split_w_in_kernel,
        grid=(depth, d_model // rb),
        in_specs=[pl.BlockSpec((1, n_cols, rb), lambda l, i: (l, 0, i))],
        out_specs=[pl.BlockSpec((1, rb, n), lambda l, i: (l, i, 0)) for n in widths],
        out_shape=[jax.ShapeDtypeStruct((depth, d_model, n), BF16) for n in widths],
        compiler_params=_params(2),
        name="split_w_in",
    )(jnp.swapaxes(w_in, 1, 2))


def _split_w_uq(w_uq):
    depth = w_uq.shape[0]
    w = w_uq.astype(BF16).reshape(depth, MLA_Q_RANK, MLA_HEADS, MLA_QK)
    nope = w[..., :MLA_NOPE].reshape(depth, MLA_Q_RANK, MLA_HEADS * MLA_NOPE)
    rope = w[..., MLA_NOPE:].reshape(depth, MLA_Q_RANK, MLA_HEADS * MLA_ROPE)
    return jnp.concatenate([nope, rope], axis=-1)


def kernel(x, c, positions, ada_w, ada_b, norm_mix, norm_ffn, w_in, mla_q_norm, mla_w_uq, mla_kv_norm,
           mla_w_ukv, ret_log_decay, ret_norm, w_br_mla, w_br_dil, w_br_ret, w_out, ffn_w1, ffn_w3, ffn_w2,
           moe_router, moe_w1, moe_w3, moe_w2, final_norm):
    batch, seq, d_model = x.shape
    depth = ada_w.shape[0]
    cos_t, sin_t = _rope_tables(positions)
    mod_all = _ada(c, ada_w, ada_b)
    assert depth >= 1
    fgain = final_norm.reshape(1, d_model)
    xt = x.reshape(batch * seq, d_model)
    moe_w1_b = moe_w1.astype(BF16).reshape((-1,) + moe_w1.shape[2:])
    moe_w3_b = moe_w3.astype(BF16).reshape((-1,) + moe_w3.shape[2:])
    moe_w2_b = moe_w2.astype(BF16).reshape((-1,) + moe_w2.shape[2:])
    w_mla, w_dil, w_ret, w_gate = _split_w_in(w_in)
    w_uq = _split_w_uq(mla_w_uq)
    w_ukv = mla_w_ukv.astype(BF16)
    wbm, wbd, wbr, wout = (w.astype(BF16) for w in (w_br_mla, w_br_dil, w_br_ret, w_out))
    ffn_w1_b, ffn_w3_b, ffn_w2_b = (w.astype(BF16) for w in (ffn_w1, ffn_w3, ffn_w2))
    for layer in range(depth):
        mod = mod_all[layer]
        gmix = norm_mix[layer].reshape(1, d_model)

        q, k, v = _mla_prep(xt, mod, gmix, w_mla, mla_q_norm[layer].reshape(1, -1),
                            mla_kv_norm[layer].reshape(1, -1), w_uq, w_ukv, cos_t, sin_t, layer, batch, seq)
        y_mla = _mla_attn(q, k, v)

        dil_qkv = _dil_prep(xt, mod, gmix, w_dil, cos_t, sin_t, layer, batch, seq)
        dil_outs = [_dil_attn(*dil_qkv[3 * g:3 * g + 3], window, dil)
                    for g, (window, dil) in enumerate(DIL_PAIRS)]

        rq, rk, rv = _ret_prep(xt, mod, gmix, w_ret, cos_t, sin_t, layer, seq)
        y_ret = _ret_scan(ret_log_decay[layer].astype(F32), rq, rk, rv,
                          ret_norm[layer].reshape(1, -1).astype(F32), batch, seq)

        gffn = norm_ffn[layer].reshape(1, d_model)
        i = layer // 2
        is_moe = layer % 2 == 1
        merged = _merge_out(xt, mod, gmix, w_gate, y_mla, y_ret, dil_outs, wbm, wbd, wbr, wout, gffn,
                            moe_router[i] if is_moe else jnp.zeros((d_model, N_EXPERTS), F32),
                            layer, batch, seq, is_moe)
        last = layer == depth - 1
        if is_moe:
            xt, h, route, expert_ids = merged
            xt = _moe(xt, h, route, expert_ids, mod, fgain, moe_w1_b, moe_w3_b, moe_w2_b, i * N_EXPERTS, seq,
                      last)
        else:
            xt = _ffn_dense(merged[0], mod, gffn, fgain, ffn_w1_b, ffn_w3_b, ffn_w2_b, i, seq, last)
    return xt.reshape(batch, seq, d_model)
```

```python
import functools

import jax
import jax.numpy as jnp
from jax import lax
from jax.experimental import pallas as pl
from jax.experimental.pallas import tpu as pltpu

F32 = jnp.float32
BF16 = jnp.bfloat16

EPS = 1e-6
NEG_INF = -1e30
ROPE_THETA = 10000.0
ROPE_DIM = 64
ADA_CHUNKS = 6

MLA_HEADS = 8
MLA_Q_RANK = 768
MLA_KV_RANK = 512
MLA_NOPE = 128
MLA_ROPE = ROPE_DIM
MLA_V = 128
MLA_QK = MLA_NOPE + MLA_ROPE

DIL_PAIRS = ((128, 1), (512, 4), (2048, 16))
DIL_HEADS = 8
DIL_HEAD_DIM = ROPE_DIM
DIL_GROUP_W = DIL_HEADS * DIL_HEAD_DIM
DIL_QBLOCK = 128

RET_HEADS = 8
RET_QK_DIM = ROPE_DIM
RET_V_DIM = 2 * RET_QK_DIM
RET_CHUNK = 256
RET_UNROLL = 4
RET_QK_W = RET_HEADS * RET_QK_DIM
RET_V_W = RET_HEADS * RET_V_DIM

N_EXPERTS = 8
TOP_K = 2
MOE_BLOCK = 256

LANES = 128
SUBLANES = 8
V7X_VMEM_LIMIT_BYTES = 56 * 1024 * 1024

ROW_TILE = 1024
MERGE_TILE = 512
COMBINE_TILE = 256
MLA_Q_TILE = 1024
MLA_HEADS_PER_STEP = 8
MLA_KEY_CHUNK = 1024


def _params(n_grid_dims):
    return pltpu.CompilerParams(
        dimension_semantics=("arbitrary",) * n_grid_dims,
        vmem_limit_bytes=V7X_VMEM_LIMIT_BYTES,
    )


def _resident(shape):
    zeros = (0,) * len(shape)
    return pl.BlockSpec(shape, lambda *_: zeros, pipeline_mode=pl.Buffered(1))


def _layer_resident(stacked, layer):
    idx = (layer,) + (0,) * (stacked.ndim - 1)
    return pl.BlockSpec((1,) + stacked.shape[1:], lambda *_: idx, pipeline_mode=pl.Buffered(1))


def _silu(v):
    return v * jax.nn.sigmoid(v)


def _norm_mod(x, gain, scale, shift):
    ms = jnp.mean(x * x, axis=-1, keepdims=True)
    y = x * lax.rsqrt(ms + EPS) * gain
    return y * (1.0 + scale) + shift


def _rms(v, gain):
    ms = jnp.mean(v * v, axis=-1, keepdims=True)
    return v * lax.rsqrt(ms + EPS) * gain


def _rope(t, cos_t, sin_t):
    lane = lax.broadcasted_iota(jnp.int32, (t.shape[0], LANES), 1)
    first_half = (lane % ROPE_DIM) < (ROPE_DIM // 2)
    out = []
    for j in range(t.shape[1] // LANES):
        c = t[:, j * LANES:(j + 1) * LANES]
        rot = jnp.where(first_half,
                        pltpu.roll(c, LANES - ROPE_DIM // 2, 1),
                        pltpu.roll(c, ROPE_DIM // 2, 1))
        out.append(c * cos_t + rot * sin_t)
    return out[0] if len(out) == 1 else jnp.concatenate(out, axis=1)


def _dot(a, b):
    return jnp.dot(a, b, preferred_element_type=F32)


def _dot_nt(a, b):
    return lax.dot_general(a, b, (((1,), (1,)), ((), ())), preferred_element_type=F32)


def _dot_tn(a, b):
    return lax.dot_general(a, b, (((0,), (0,)), ((), ())), preferred_element_type=F32)


def _ada_kernel(c_ref, w_ref, b_ref, o_ref):
    o_ref[0] = jnp.dot(_silu(c_ref[...]), w_ref[0], preferred_element_type=F32,
                       precision=lax.Precision.HIGHEST) + b_ref[0]


def _ada(c, ada_w, ada_b):
    depth, d_model, n = ada_w.shape
    batch = c.shape[0]
    tn = n // 4
    out = pl.pallas_call(
        _ada_kernel,
        grid=(depth, n // tn),
        in_specs=[
            pl.BlockSpec((batch, d_model), lambda l, j: (0, 0)),
            pl.BlockSpec((1, d_model, tn), lambda l, j: (l, 0, j)),
            pl.BlockSpec((1, 1, tn), lambda l, j: (l, 0, j)),
        ],
        out_specs=pl.BlockSpec((1, batch, tn), lambda l, j: (l, 0, j)),
        out_shape=jax.ShapeDtypeStruct((depth, batch, n), F32),
        compiler_params=_params(2),
        name="ada",
    )(c, ada_w, ada_b.reshape(depth, 1, n))
    return out.reshape(depth, batch, ADA_CHUNKS, d_model)


def _mla_prep_kernel(x_ref, mod_ref, g_ref, wa_ref, qn_ref, kvn_ref, wuq_ref, wukv_ref, cos_ref, sin_ref,
                     q_ref, k_ref, v_ref):
    mod = mod_ref[0]
    h = _norm_mod(x_ref[...], g_ref[...], mod[1:2], mod[0:1]).astype(BF16)
    p = _dot(h, wa_ref[0])
    cq = _rms(p[:, :MLA_Q_RANK], qn_ref[...]).astype(BF16)
    ckv = _rms(p[:, MLA_Q_RANK:MLA_Q_RANK + MLA_KV_RANK], kvn_ref[...]).astype(BF16)
    cos_t = cos_ref[0]
    sin_t = sin_ref[0]
    kr = _rope(p[:, MLA_Q_RANK + MLA_KV_RANK:], cos_t, sin_t)
    q = _dot(cq, wuq_ref[0])
    kv = _dot(ckv, wukv_ref[0])
    nope_w = MLA_HEADS * MLA_NOPE
    qr = _rope(q[:, nope_w:], cos_t, sin_t)
    scale = MLA_QK ** -0.5
    kr_b = kr[:, :MLA_ROPE].astype(BF16)
    for hd in range(MLA_HEADS):
        q_ref[0, hd, :, :MLA_NOPE] = (q[:, hd * MLA_NOPE:(hd + 1) * MLA_NOPE] * scale).astype(BF16)
        q_ref[0, hd, :, MLA_NOPE:] = (qr[:, hd * MLA_ROPE:(hd + 1) * MLA_ROPE] * scale).astype(BF16)
        base = hd * (MLA_NOPE + MLA_V)
        k_ref[0, hd, :, :MLA_NOPE] = kv[:, base:base + MLA_NOPE].astype(BF16)
        k_ref[0, hd, :, MLA_NOPE:] = kr_b
        v_ref[0, hd] = kv[:, base + MLA_NOPE:base + MLA_NOPE + MLA_V].astype(BF16)


def _mla_prep(x, mod, gain, wa, qn, kvn, wuq, wukv, cos_t, sin_t, layer, batch, seq):
    t, d_model = x.shape
    tm = ROW_TILE
    per_b = seq // tm
    row = lambda i: (i, 0)
    hs = lambda i: (i // per_b, 0, i % per_b, 0)
    return pl.pallas_call(
        _mla_prep_kernel,
        grid=(t // tm,),
        in_specs=[
            pl.BlockSpec((tm, d_model), row),
            pl.BlockSpec((1, ADA_CHUNKS, d_model), lambda i: (i // per_b, 0, 0)),
            _resident((1, d_model)),
            _layer_resident(wa, layer),
            _resident((1, MLA_Q_RANK)),
            _resident((1, MLA_KV_RANK)),
            _layer_resident(wuq, layer),
            _layer_resident(wukv, layer),
            pl.BlockSpec((1, tm, LANES), lambda i: (i // per_b, i % per_b, 0)),
            pl.BlockSpec((1, tm, LANES), lambda i: (i // per_b, i % per_b, 0)),
        ],
        out_specs=[
            pl.BlockSpec((1, MLA_HEADS, tm, MLA_QK), hs),
            pl.BlockSpec((1, MLA_HEADS, tm, MLA_QK), hs),
            pl.BlockSpec((1, MLA_HEADS, tm, MLA_V), hs),
        ],
        out_shape=[
            jax.ShapeDtypeStruct((batch, MLA_HEADS, seq, MLA_QK), BF16),
            jax.ShapeDtypeStruct((batch, MLA_HEADS, seq, MLA_QK), BF16),
            jax.ShapeDtypeStruct((batch, MLA_HEADS, seq, MLA_V), BF16),
        ],
        compiler_params=_params(1),
        name="mla_prep",
    )(x, mod, gain, wa, qn, kvn, wuq, wukv, cos_t, sin_t)


def _mla_attn_kernel(q_ref, k_ref, v_ref, o_ref, *, heads_per_step, key_chunk):
    seq = k_ref.shape[2]
    ones = jnp.ones((key_chunk, LANES), BF16)
    for hd in range(heads_per_step):
        q = q_ref[0, hd]
        m = acc = None
        for c in range(seq // key_chunk):
            rows = slice(c * key_chunk, (c + 1) * key_chunk)
            s = _dot_nt(q, k_ref[0, hd, rows, :])
            v_ext = jnp.concatenate([v_ref[0, hd, rows, :], ones], axis=1)
            m_c = jnp.max(s, axis=-1, keepdims=True)
            if c == 0:
                m = m_c
                acc = _dot(jnp.exp(s - m).astype(BF16), v_ext)
            else:
                m_new = jnp.maximum(m, m_c)
                acc = jnp.exp(m - m_new) * acc + _dot(jnp.exp(s - m_new).astype(BF16), v_ext)
                m = m_new
        o_ref[:, hd * MLA_V:(hd + 1) * MLA_V] = (acc[:, :MLA_V] / acc[:, MLA_V:MLA_V + 1]).astype(o_ref.dtype)


def _mla_attn(q, k, v):
    batch, heads, seq, _ = q.shape
    tq = MLA_Q_TILE
    nq = seq // tq
    hps = MLA_HEADS_PER_STEP
    return pl.pallas_call(
        functools.partial(_mla_attn_kernel, heads_per_step=hps, key_chunk=MLA_KEY_CHUNK),
        grid=(batch, heads // hps, nq),
        in_specs=[
            pl.BlockSpec((1, hps, tq, MLA_QK), lambda b, h, i: (b, h, i, 0)),
            pl.BlockSpec((1, hps, seq, MLA_QK), lambda b, h, i: (b, h, 0, 0)),
            pl.BlockSpec((1, hps, seq, MLA_V), lambda b, h, i: (b, h, 0, 0)),
        ],
        out_specs=pl.BlockSpec((tq, hps * MLA_V), lambda b, h, i: (b * nq + i, h)),
        out_shape=jax.ShapeDtypeStruct((batch * seq, heads * MLA_V), BF16),
        compiler_params=_params(3),
        name="mla_attn",
    )(q, k, v)


def _dil_prep_kernel(x_ref, mod_ref, g_ref, w_ref, cos_ref, sin_ref, *refs):
    out_refs = refs[:3 * len(DIL_PAIRS)]
    h_scr = refs[3 * len(DIL_PAIRS)]
    mod = mod_ref[0]
    h32 = _norm_mod(x_ref[...], g_ref[...], mod[1:2], mod[0:1])
    n_lane_blocks = h32.shape[1] // LANES
    for j in range(n_lane_blocks):
        h_scr[j] = h32[:, j * LANES:(j + 1) * LANES]
    tm = h32.shape[0]
    gw = DIL_GROUP_W
    for g, (_, dil) in enumerate(DIL_PAIRS):
        n = tm // dil
        if dil == 1:
            hp, cos_t, sin_t = h32, cos_ref[0], sin_ref[0]
        else:
            hp = jnp.concatenate(
                [jnp.concatenate([h_scr[j, pl.ds(r, n, stride=dil), :] for j in range(n_lane_blocks)], axis=1)
                 for r in range(dil)], axis=0)
            cos_t = jnp.concatenate([cos_ref[0, pl.ds(r, n, stride=dil), :] for r in range(dil)], axis=0)
            sin_t = jnp.concatenate([sin_ref[0, pl.ds(r, n, stride=dil), :] for r in range(dil)], axis=0)
        pg = _dot(hp.astype(BF16), w_ref[0, :, g * 3 * gw:(g + 1) * 3 * gw])
        qg = (_rope(pg[:, :gw], cos_t, sin_t) * (DIL_HEAD_DIM ** -0.5)).astype(BF16)
        kg = _rope(pg[:, gw:2 * gw], cos_t, sin_t).astype(BF16)
        vg = pg[:, 2 * gw:].astype(BF16)
        q_ref, k_ref, v_ref = out_refs[3 * g:3 * g + 3]
        for r in range(dil):
            q_ref[0, r] = qg[r * n:(r + 1) * n]
            k_ref[0, r] = kg[r * n:(r + 1) * n]
            v_ref[0, r] = vg[r * n:(r + 1) * n]


def _dil_prep(x, mod, gain, w, cos_t, sin_t, layer, batch, seq):
    t, d_model = x.shape
    tm = ROW_TILE
    per_b = seq // tm
    out_specs, out_shape = [], []
    for _, dil in DIL_PAIRS:
        for _ in range(3):
            out_specs.append(pl.BlockSpec((1, dil, tm // dil, DIL_GROUP_W),
                                          lambda i: (i // per_b, 0, i % per_b, 0)))
            out_shape.append(jax.ShapeDtypeStruct((batch, dil, seq // dil, DIL_GROUP_W), BF16))
    return pl.pallas_call(
        _dil_prep_kernel,
        grid=(t // tm,),
        in_specs=[
            pl.BlockSpec((tm, d_model), lambda i: (i, 0)),
            pl.BlockSpec((1, ADA_CHUNKS, d_model), lambda i: (i // per_b, 0, 0)),
            _resident((1, d_model)),
            _layer_resident(w, layer),
            pl.BlockSpec((1, tm, LANES), lambda i: (i // per_b, i % per_b, 0)),
            pl.BlockSpec((1, tm, LANES), lambda i: (i // per_b, i % per_b, 0)),
        ],
        out_specs=out_specs,
        out_shape=out_shape,
        scratch_shapes=[pltpu.VMEM((d_model // LANES, tm, LANES), F32)],
        compiler_params=_params(1),
        name="dil_prep",
    )(x, mod, gain, w, cos_t, sin_t)


def _dil_attn_kernel(q_ref, k_ref, v_ref, o_ref, lse_ref, *, dil, segment, band):
    qb_rows = DIL_QBLOCK
    kw = 2 * qb_rows
    short = segment < kw
    per_window = kw // segment if short else 1
    nb = 1 if short else segment // qb_rows
    low_k = lax.broadcasted_iota(jnp.int32, (kw, LANES), 1).astype(F32).astype(BF16) < DIL_HEAD_DIM
    low_q = lax.broadcasted_iota(jnp.int32, (qb_rows, LANES), 1) < DIL_HEAD_DIM
    row_i = lax.broadcasted_iota(jnp.int32, (qb_rows, kw), 0)
    col_i = lax.broadcasted_iota(jnp.int32, (qb_rows, kw), 1)
    zeros_k = jnp.zeros((kw, LANES), BF16)
    ones_k = jnp.ones((kw, LANES), BF16)
    denom_cols = jnp.concatenate([jnp.where(low_k, ones_k, zeros_k), jnp.where(low_k, zeros_k, ones_k)], axis=0)

    def tile(idx, carry):
        r = idx // nb
        if short:
            q0 = 0
            r0 = (r // per_window) * per_window
            qt = q_ref[0, r]
            kt = jnp.concatenate([k_ref[0, r0 + j] for j in range(per_window)], axis=0)
            vt = jnp.concatenate([v_ref[0, r0 + j] for j in range(per_window)], axis=0)
            jq = (r - r0) * segment + row_i
            jk = col_i
        else:
            q0 = pl.multiple_of((idx % nb) * qb_rows, qb_rows)
            ks = pl.multiple_of(jnp.clip(q0 - band, 0, segment - kw), band)
            qt = q_ref[0, r, pl.ds(q0, qb_rows), :]
            kt = k_ref[0, r, pl.ds(ks, kw), :]
            vt = v_ref[0, r, pl.ds(ks, kw), :]
            jq = q0 + row_i
            jk = ks + col_i
        valid = jnp.logical_and(jnp.abs(jq - jk) <= band, jq // segment == jk // segment)
        for hp in range(DIL_GROUP_W // LANES):
            cols = slice(hp * LANES, (hp + 1) * LANES)
            qp, kp, vp = qt[:, cols], kt[:, cols], vt[:, cols]
            k2 = jnp.concatenate([jnp.where(low_k, kp, zeros_k), jnp.where(low_k, zeros_k, kp)], axis=0)
            v2 = jnp.concatenate([jnp.where(low_k, vp, zeros_k), jnp.where(low_k, zeros_k, vp)], axis=0)
            s = _dot_nt(qp, k2)
            s0 = jnp.where(valid, s[:, :kw], NEG_INF)
            s1 = jnp.where(valid, s[:, kw:], NEG_INF)
            m0 = jnp.max(s0, axis=-1, keepdims=True)
            m1 = jnp.max(s1, axis=-1, keepdims=True)
            p = jnp.concatenate([jnp.exp(s0 - m0), jnp.exp(s1 - m1)], axis=1).astype(BF16)
            pv = _dot(p, jnp.concatenate([v2, denom_cols], axis=1))
            den = pv[:, LANES:]
            o_ref[0, r, pl.ds(q0, qb_rows), cols] = (pv[:, :LANES] / den).astype(o_ref.dtype)
            lse_ref[0, r, pl.ds(q0, qb_rows), cols] = jnp.where(low_q, m0, m1) + jnp.log(den)
        return carry

    lax.fori_loop(0, dil * nb, tile, 0, unroll=8)


def _dil_attn(q, k, v, window, dil):
    batch, _, seg, gw = q.shape
    band = window // (2 * dil)
    assert band * 2 == DIL_QBLOCK and seg % DIL_QBLOCK == 0
    assert seg >= 2 * DIL_QBLOCK or (seg == DIL_QBLOCK and dil % 2 == 0)
    spec = pl.BlockSpec((1, dil, seg, gw), lambda b: (b, 0, 0, 0))
    return pl.pallas_call(
        functools.partial(_dil_attn_kernel, dil=dil, segment=seg, band=band),
        grid=(batch,),
        in_specs=[spec, spec, spec],
        out_specs=[spec, spec],
        out_shape=[jax.ShapeDtypeStruct(q.shape, BF16), jax.ShapeDtypeStruct(q.shape, F32)],
        compiler_params=_params(1),
        name=f"dil_attn_d{dil}",
    )(q, k, v)


def _ret_prep_kernel(x_ref, mod_ref, g_ref, w_ref, cos_ref, sin_ref, q_ref, k_ref, v_ref):
    mod = mod_ref[0]
    h = _norm_mod(x_ref[...], g_ref[...], mod[1:2], mod[0:1]).astype(BF16)
    p = _dot(h, w_ref[0])
    cos_t, sin_t = cos_ref[0], sin_ref[0]
    q_ref[...] = _rope(p[:, :RET_QK_W], cos_t, sin_t).astype(BF16)
    k_ref[...] = (_rope(p[:, RET_QK_W:2 * RET_QK_W], cos_t, sin_t) * (RET_QK_DIM ** -0.5)).astype(BF16)
    v_ref[...] = p[:, 2 * RET_QK_W:].astype(BF16)


def _ret_prep(x, mod, gain, w, cos_t, sin_t, layer, seq):
    t, d_model = x.shape
    tm = ROW_TILE
    per_b = seq // tm
    row = lambda i: (i, 0)
    return pl.pallas_call(
        _ret_prep_kernel,
        grid=(t // tm,),
        in_specs=[
            pl.BlockSpec((tm, d_model), row),
            pl.BlockSpec((1, ADA_CHUNKS, d_model), lambda i: (i // per_b, 0, 0)),
            _resident((1, d_model)),
            _layer_resident(w, layer),
            pl.BlockSpec((1, tm, LANES), lambda i: (i // per_b, i % per_b, 0)),
            pl.BlockSpec((1, tm, LANES), lambda i: (i // per_b, i % per_b, 0)),
        ],
        out_specs=[pl.BlockSpec((tm, RET_QK_W), row), pl.BlockSpec((tm, RET_QK_W), row),
                   pl.BlockSpec((tm, RET_V_W), row)],
        out_shape=[jax.ShapeDtypeStruct((t, RET_QK_W), BF16), jax.ShapeDtypeStruct((t, RET_QK_W), BF16),
                   jax.ShapeDtypeStruct((t, RET_V_W), BF16)],
        compiler_params=_params(1),
        name="ret_prep",
    )(x, mod, gain, w, cos_t, sin_t)


def _ret_scan_kernel(ld_ref, q_ref, k_ref, v_ref, gn_ref, o_ref,
                     yf_ref, st_ref, intra_ref, qdec_ref, kdec_ref, cdec_ref):
    c = RET_CHUNK
    seq = q_ref.shape[0]
    nc = seq // c
    n_pairs = RET_HEADS // 2
    pair_v = 2 * RET_V_DIM
    ia = lax.broadcasted_iota(jnp.int32, (c, c), 0)
    ib = lax.broadcasted_iota(jnp.int32, (c, c), 1)
    diff = (ia - ib).astype(F32)
    idx_q = lax.broadcasted_iota(jnp.int32, (c, pair_v), 0).astype(F32)
    lane_q = lax.broadcasted_iota(jnp.int32, (c, pair_v), 1)
    idx_k = lax.broadcasted_iota(jnp.int32, (c, LANES), 0).astype(F32)
    lane_k = lax.broadcasted_iota(jnp.int32, (c, LANES), 1)
    low_k = lane_k < RET_QK_DIM
    low_k16 = lane_k.astype(F32).astype(BF16) < RET_QK_DIM
    low_v16 = lane_q.astype(F32).astype(BF16) < RET_V_DIM
    srow =lax.broadcasted_iota(jnp.int32, (LANES, pair_v), 0)
    scol = lax.broadcasted_iota(jnp.int32, (LANES, pair_v), 1)
    blk0 = jnp.logical_and(srow < RET_QK_DIM, scol < RET_V_DIM)
    blk1 = jnp.logical_and(srow >= RET_QK_DIM, scol >= RET_V_DIM)
    diag = jnp.where(jnp.logical_or(blk0, blk1), 1.0, 0.0).astype(F32)

    for dr in range(2):
        for hd in range(RET_HEADS):
            lg = ld_ref[dr, hd]
            if dr == 0:
                mask = diff >= 0
                dist = jnp.where(mask, diff, 0.0)
            else:
                mask = diff < 0
                dist = jnp.where(mask, -diff, 0.0)
            intra_ref[dr, hd] = jnp.where(mask, jnp.exp(lg * dist), 0.0)
        for hp in range(n_pairs):
            lg0 = ld_ref[dr, 2 * hp]
            lg1 = ld_ref[dr, 2 * hp + 1]
            q_exp = idx_q + 1.0 if dr == 0 else c - idx_q
            k_exp = c - 1.0 - idx_k if dr == 0 else idx_k
            qdec_ref[dr, hp] = jnp.exp(jnp.where(lane_q < RET_V_DIM, lg0, lg1) * q_exp)
            kdec_ref[dr, hp] = jnp.exp(jnp.where(low_k, lg0, lg1) * k_exp)
            cdec_ref[dr, hp] = jnp.where(blk0, jnp.exp(lg0 * c), jnp.where(blk1, jnp.exp(lg1 * c), 0.0))

    def chunk(dr, n):
        r0 = pl.multiple_of(n * c, c)
        qc = q_ref[pl.ds(r0, c), :]
        kc = k_ref[pl.ds(r0, c), :]
        vc = v_ref[pl.ds(r0, c), :]
        ys = []
        for hp in range(n_pairs):
            qp = qc[:, hp * LANES:(hp + 1) * LANES]
            kp = kc[:, hp * LANES:(hp + 1) * LANES]
            vp = vc[:, hp * pair_v:(hp + 1) * pair_v]
            zk = jnp.zeros_like(kp)
            zv = jnp.zeros_like(vp)
            k2 = jnp.concatenate([jnp.where(low_k16, kp, zk), jnp.where(low_k16, zk, kp)], axis=0)
            v2 = jnp.concatenate([jnp.where(low_v16, vp, zv), jnp.where(low_v16, zv, vp)], axis=0)
            s = _dot_nt(qp, k2) * jnp.concatenate([intra_ref[dr, 2 * hp], intra_ref[dr, 2 * hp + 1]], axis=1)
            y_intra = _dot(s.astype(BF16), v2)
            state = st_ref[hp]
            y_inter = _dot(qp, state.astype(BF16)) * qdec_ref[dr, hp]
            ys.append(y_intra + y_inter)
            kd = (kp.astype(F32) * kdec_ref[dr, hp]).astype(BF16)
            st_ref[hp] = cdec_ref[dr, hp] * state + diag * _dot_tn(kd, vp)
        return r0, ys

    st_ref[...] = jnp.zeros_like(st_ref)

    def fwd(n, carry):
        r0, ys = chunk(0, n)
        for hp in range(n_pairs):
            yf_ref[pl.ds(r0, c), hp * pair_v:(hp + 1) * pair_v] = ys[hp]
        return carry

    lax.fori_loop(0, nc, fwd, 0, unroll=RET_UNROLL)
    st_ref[...] = jnp.zeros_like(st_ref)

    def bwd(i, carry):
        r0, ys = chunk(1, nc - 1 - i)
        for hp in range(n_pairs):
            y = ys[hp] + yf_ref[pl.ds(r0, c), hp * pair_v:(hp + 1) * pair_v]
            for half in range(2):
                cols = slice(hp * pair_v + half * RET_V_DIM, hp * pair_v + (half + 1) * RET_V_DIM)
                yh = y[:, half * RET_V_DIM:(half + 1) * RET_V_DIM]
                mu = jnp.mean(yh, axis=-1, keepdims=True)
                dev = yh - mu
                var = jnp.mean(dev * dev, axis=-1, keepdims=True)
                o_ref[pl.ds(r0, c), cols] = (dev * lax.rsqrt(var + EPS) * gn_ref[:, cols]).astype(o_ref.dtype)
        return carry

    lax.fori_loop(0, nc, bwd, 0, unroll=RET_UNROLL)


def _ret_scan(log_decay, q, k, v, ret_norm, batch, seq):
    c = RET_CHUNK
    n_pairs = RET_HEADS // 2
    return pl.pallas_call(
        _ret_scan_kernel,
        grid=(batch,),
        in_specs=[
            pl.BlockSpec(memory_space=pltpu.SMEM),
            pl.BlockSpec((seq, RET_QK_W), lambda b: (b, 0)),
            pl.BlockSpec((seq, RET_QK_W), lambda b: (b, 0)),
            pl.BlockSpec((seq, RET_V_W), lambda b: (b, 0)),
            _resident((1, RET_V_W)),
        ],
        out_specs=pl.BlockSpec((seq, RET_V_W), lambda b: (b, 0)),
        out_shape=jax.ShapeDtypeStruct((batch * seq, RET_V_W), BF16),
        scratch_shapes=[
            pltpu.VMEM((seq, RET_V_W), F32),
            pltpu.VMEM((n_pairs, LANES, 2 * RET_V_DIM), F32),
            pltpu.VMEM((2, RET_HEADS, c, c), F32),
            pltpu.VMEM((2, n_pairs, c, 2 * RET_V_DIM), F32),
            pltpu.VMEM((2, n_pairs, c, LANES), F32),
            pltpu.VMEM((2, n_pairs, LANES, 2 * RET_V_DIM), F32),
        ],
        compiler_params=_params(1),
        name="ret_scan",
    )(log_decay, q, k, v, ret_norm)


def _top2_route(h, w_hi_lo):
    h_hi = h.astype(BF16)
    h_lo = (h - h_hi.astype(F32)).astype(BF16)
    hh = _dot(h_hi, w_hi_lo)
    logits = hh[:, :LANES] + (hh[:, LANES:] + _dot(h_lo, w_hi_lo[:, :LANES]))
    lane = lax.broadcasted_iota(jnp.int32, logits.shape, 1).astype(F32)
    lg = jnp.where(lane < N_EXPERTS, logits, -jnp.inf)
    m1 = jnp.max(lg, axis=-1, keepdims=True)
    i1 = jnp.min(jnp.where(lg == m1, lane, float(LANES)), axis=-1, keepdims=True)
    lg2 = jnp.where(lane == i1, -jnp.inf, lg)
    m2 = jnp.max(lg2, axis=-1, keepdims=True)
    i2 = jnp.min(jnp.where(lg2 == m2, lane, float(LANES)), axis=-1, keepdims=True)
    e = jnp.exp(m2 - m1)
    w1 = 1.0 / (1.0 + e)
    w2 = e / (1.0 + e)
    return jnp.where(lane == 0, i1, jnp.where(lane == 1, i2, jnp.where(lane == 2, w1, jnp.where(lane == 3, w2, 0.0))))


def _merge_kernel(x_ref, mod_ref, g_ref, wg_ref, ymla_ref, yret_ref,
                  o0_ref, l0_ref, o1_ref, l1_ref, o2_ref, l2_ref,
                  wbm_ref, wbd_ref, wbr_ref, wout_ref, gf_ref, wr_ref, out_ref, *rest, route):
    nat_o, nat_l = rest[-2:]
    x = x_ref[...]
    mod = mod_ref[0]
    h = _norm_mod(x, g_ref[...], mod[1:2], mod[0:1]).astype(BF16)
    gates = _dot(h, wg_ref[0])
    d_model = x.shape[1]
    tm = x.shape[0]

    o_nat = [o0_ref[0, 0].astype(F32)]
    l_nat = [l0_ref[0, 0]]
    for g, (o_ref, l_ref) in enumerate(((o1_ref, l1_ref), (o2_ref, l2_ref))):
        dil = DIL_PAIRS[g + 1][1]
        n = tm // dil
        n_lane_blocks = DIL_GROUP_W // LANES
        for r in range(dil):
            o_r = o_ref[0, r].astype(F32)
            l_r = l_ref[0, r]
            for j in range(n_lane_blocks):
                nat_o[g, j, pl.ds(r, n, stride=dil), :] = o_r[:, j * LANES:(j + 1) * LANES]
                nat_l[g, j, pl.ds(r, n, stride=dil), :] = l_r[:, j * LANES:(j + 1) * LANES]
        o_nat.append(jnp.concatenate([nat_o[g, j] for j in range(n_lane_blocks)], axis=1))
        l_nat.append(jnp.concatenate([nat_l[g, j] for j in range(n_lane_blocks)], axis=1))
    m = jnp.maximum(jnp.maximum(l_nat[0], l_nat[1]), l_nat[2])
    ws = [jnp.exp(l - m) for l in l_nat]
    y_dil = (ws[0] * o_nat[0] + ws[1] * o_nat[1] + ws[2] * o_nat[2]) / (ws[0] + ws[1] + ws[2])

    y_ret = (_silu(gates[:, :RET_V_W]) * yret_ref[...].astype(F32)).astype(BF16)
    ga = gates[:, RET_V_W:RET_V_W + d_model]
    gb = gates[:, RET_V_W + d_model:RET_V_W + 2 * d_model]
    gc = gates[:, RET_V_W + 2 * d_model:]
    merged = (jax.nn.sigmoid(ga) * _dot(ymla_ref[...], wbm_ref[0])
              + jax.nn.sigmoid(gb) * _dot(y_dil.astype(BF16), wbd_ref[0])
              + jax.nn.sigmoid(gc) * _dot(y_ret, wbr_ref[0]))
    x_new = x + mod[2:3] * _dot(merged.astype(BF16), wout_ref[0])
    out_ref[...] = x_new
    if route:
        h_ref, route_ref, ids_ref = rest[:3]
        h2 = _norm_mod(x_new, gf_ref[...], mod[4:5], mod[3:4])
        h_ref[...] = h2
        r = _top2_route(h2, wr_ref[...])
        route_ref[...] = r
        ids_ref[...] = r.T[:ids_ref.shape[0]]


def _merge_out(x, mod, gain, wg, y_mla, y_ret, dil_outs, wbm, wbd, wbr, wout, gain_ffn, w_router,
               layer, batch, seq, route):
    t, d_model = x.shape
    tm = MERGE_TILE
    per_b = seq // tm
    row = lambda i: (i, 0)
    dil_specs, dil_args = [], []
    for (_, dil), (o, lse) in zip(DIL_PAIRS, dil_outs):
        spec = pl.BlockSpec((1, dil, tm // dil, DIL_GROUP_W), lambda i: (i // per_b, 0, i % per_b, 0))
        dil_specs += [spec, spec]
        dil_args += [o, lse]
    out_specs = [pl.BlockSpec((tm, d_model), row)]
    out_shape = [jax.ShapeDtypeStruct((t, d_model), F32)]
    if route:
        out_specs += [pl.BlockSpec((tm, d_model), row), pl.BlockSpec((tm, LANES), row),
                      pl.BlockSpec((SUBLANES, tm), lambda i: (0, i))]
        out_shape += [jax.ShapeDtypeStruct((t, d_model), F32), jax.ShapeDtypeStruct((t, LANES), F32),
                      jax.ShapeDtypeStruct((SUBLANES, t), F32)]
    wr32 = jnp.zeros((d_model, LANES), F32).at[:, :N_EXPERTS].set(w_router)
    wr_hi = wr32.astype(BF16)
    wr = jnp.concatenate([wr_hi, (wr32 - wr_hi.astype(F32)).astype(BF16)], axis=1)
    return pl.pallas_call(
        functools.partial(_merge_kernel, route=route),
        grid=(t // tm,),
        in_specs=[
            pl.BlockSpec((tm, d_model), row),
            pl.BlockSpec((1, ADA_CHUNKS, d_model), lambda i: (i // per_b, 0, 0)),
            _resident((1, d_model)),
            _layer_resident(wg, layer),
            pl.BlockSpec((tm, y_mla.shape[1]), row),
            pl.BlockSpec((tm, RET_V_W), row),
            *dil_specs,
            _layer_resident(wbm, layer), _layer_resident(wbd, layer), _layer_resident(wbr, layer),
            _layer_resident(wout, layer),
            _resident((1, d_model)),
            _resident(wr.shape),
        ],
        out_specs=out_specs,
        out_shape=out_shape,
        scratch_shapes=[pltpu.VMEM((2, DIL_GROUP_W // LANES, tm, LANES), F32),
                        pltpu.VMEM((2, DIL_GROUP_W // LANES, tm, LANES), F32)],
        compiler_params=_params(1),
        name="merge_out",
    )(x, mod, gain, wg, y_mla, y_ret, *dil_args, wbm, wbd, wbr, wout, gain_ffn, wr)


def _residual_out(y, fg_ref, final):
    return _rms(y, fg_ref[...]) if final else y


def _ffn_kernel(x_ref, mod_ref, g_ref, fg_ref, w1_ref, w3_ref, w2_ref, out_ref, *, n_chunks, final):
    x = x_ref[...]
    mod = mod_ref[0]
    h = _norm_mod(x, g_ref[...], mod[4:5], mod[3:4]).astype(BF16)
    d_ff = w1_ref.shape[2]
    fc = d_ff // n_chunks
    acc = jnp.zeros(x.shape, F32)
    for j in range(n_chunks):
        a = _dot(h, w1_ref[0, :, j * fc:(j + 1) * fc])
        b = _dot(h, w3_ref[0, :, j * fc:(j + 1) * fc])
        acc = acc + _dot((_silu(a) * b).astype(BF16), w2_ref[0, j * fc:(j + 1) * fc, :])
    out_ref[...] = _residual_out(x + mod[5:6] * acc, fg_ref, final)


def _ffn_dense(x, mod, gain, final_gain, w1, w3, w2, layer, seq, final):
    t, d_model = x.shape
    tm = ROW_TILE
    per_b = seq // tm
    row = lambda i: (i, 0)
    return pl.pallas_call(
        functools.partial(_ffn_kernel, n_chunks=1, final=final),
        grid=(t // tm,),
        in_specs=[
            pl.BlockSpec((tm, d_model), row),
            pl.BlockSpec((1, ADA_CHUNKS, d_model), lambda i: (i // per_b, 0, 0)),
            _resident((1, d_model)),
            _resident((1, d_model)),
            _layer_resident(w1, layer), _layer_resident(w3, layer), _layer_resident(w2, layer),
        ],
        out_specs=pl.BlockSpec((tm, d_model), row),
        out_shape=jax.ShapeDtypeStruct((t, d_model), F32),
        compiler_params=_params(1),
        name="ffn_dense",
    )(x, mod, gain, final_gain, w1, w3, w2)


def _row_copy(src_hbm, dst, src_row, dst_row, sem):
    return pltpu.make_async_copy(src_hbm.at[pl.ds(src_row, 1), :], dst.at[pl.ds(dst_row, 1), :], sem)


def _expert_kernel(be_ref, nact_ref, tok_ref, tok_next_ref, h_hbm, w1_ref, w3_ref, w2_ref, y_ref,
                   xbuf, sems, *, n_chunks):
    i = pl.program_id(0)
    nact = nact_ref[0]
    bm = y_ref.shape[0]
    slot = i % 2

    def wait_block(s):
        pltpu.make_async_copy(h_hbm.at[pl.ds(0, bm), :], xbuf.at[s], sems.at[s]).wait()

    @pl.when(i == 0)
    def _():
        def issue(r, carry):
            _row_copy(h_hbm, xbuf.at[0], tok_ref[0, 0, r], r, sems.at[0]).start()
            return carry

        lax.fori_loop(0, bm, issue, 0, unroll=8)

    @pl.when(i >= nact)
    def _():
        y_ref[...] = jnp.zeros_like(y_ref)

    @pl.when(i < nact)
    def _():
        wait_block(slot)
        xb = xbuf[slot].astype(BF16)
        for r in range(bm):
            _row_copy(h_hbm, xbuf.at[1 - slot], tok_next_ref[0, 0, r], r, sems.at[1 - slot]).start()
        d_ff = w1_ref.shape[2]
        fc = d_ff // n_chunks
        acc = jnp.zeros(y_ref.shape, F32)
        for j in range(n_chunks):
            a = _dot(xb, w1_ref[0, :, j * fc:(j + 1) * fc])
            b = _dot(xb, w3_ref[0, :, j * fc:(j + 1) * fc])
            acc = acc + _dot((_silu(a) * b).astype(BF16), w2_ref[0, j * fc:(j + 1) * fc, :])
        y_ref[...] = acc

    @pl.when(i == nact - 1)
    def _():
        wait_block(1 - slot)


def _moe_expert(block_expert, nact, row_tok, h, w1, w3, w2, expert_base):
    nb, _, bm = row_tok.shape
    d_model = h.shape[1]
    d_ff = w1.shape[2]
    wsel = lambda i, be, na: (expert_base + be[jnp.minimum(i, na[0] - 1)], 0, 0)
    return pl.pallas_call(
        functools.partial(_expert_kernel, n_chunks=2),
        grid_spec=pltpu.PrefetchScalarGridSpec(
            num_scalar_prefetch=2,
            grid=(nb,),
            in_specs=[
                pl.BlockSpec((1, 1, bm), lambda i, be, na: (i, 0, 0), memory_space=pltpu.SMEM),
                pl.BlockSpec((1, 1, bm), lambda i, be, na: (jnp.minimum(i + 1, nb - 1), 0, 0),
                             memory_space=pltpu.SMEM),
                pl.BlockSpec(memory_space=pl.ANY),
                pl.BlockSpec((1, d_model, d_ff), wsel, pipeline_mode=pl.Buffered(1)),
                pl.BlockSpec((1, d_model, d_ff), wsel, pipeline_mode=pl.Buffered(1)),
                pl.BlockSpec((1, d_ff, d_model), wsel, pipeline_mode=pl.Buffered(1)),
            ],
            out_specs=pl.BlockSpec((bm, d_model), lambda i, be, na: (i, 0)),
            scratch_shapes=[pltpu.VMEM((2, bm, d_model), F32), pltpu.SemaphoreType.DMA((2,))],
        ),
        out_shape=jax.ShapeDtypeStruct((nb * bm, d_model), F32),
        compiler_params=_params(1),
        name="moe_expert",
    )(block_expert, nact, row_tok, row_tok, h, w1, w3, w2)


def _combine_kernel(dest_ref, dest_next_ref, x_ref, mod_ref, route_ref, fg_ref, yb_hbm, out_ref, buf, sems,
                    *, final):
    tm = x_ref.shape[0]
    n_rows = TOP_K * tm
    i = pl.program_id(0)
    slot = i % 2

    def issue(rows_ref, s):
        def body(r, carry):
            _row_copy(yb_hbm, buf.at[s], rows_ref[0, 0, r], r, sems.at[s]).start()
            return carry

        lax.fori_loop(0, n_rows, body, 0, unroll=8)

    @pl.when(i == 0)
    def _():
        issue(dest_ref, 0)

    @pl.when(i + 1 < pl.num_programs(0))
    def _():
        for r in range(n_rows):
            _row_copy(yb_hbm, buf.at[1 - slot], dest_next_ref[0, 0, r], r, sems.at[1 - slot]).start(priority=r % 2)

    pltpu.make_async_copy(yb_hbm.at[pl.ds(0, n_rows), :], buf.at[slot], sems.at[slot]).wait()
    route = route_ref[...]
    y = route[:, 2:3] * buf[slot, :tm, :] + route[:, 3:4] * buf[slot, tm:, :]
    out_ref[...] = _residual_out(x_ref[...] + mod_ref[0][5:6] * y, fg_ref, final)


def _moe_combine(dest, x, mod, route, final_gain, yb, seq, final):
    t, d_model = x.shape
    tm = COMBINE_TILE
    per_b = seq // tm
    row = lambda i: (i, 0)
    return pl.pallas_call(
        functools.partial(_combine_kernel, final=final),
        grid=(t // tm,),
        in_specs=[
            pl.BlockSpec((1, 1, TOP_K * tm), lambda i: (i, 0, 0), memory_space=pltpu.SMEM),
            pl.BlockSpec((1, 1, TOP_K * tm), lambda i: (jnp.minimum(i + 1, t // tm - 1), 0, 0),
                         memory_space=pltpu.SMEM),
            pl.BlockSpec((tm, d_model), row),
            pl.BlockSpec((1, ADA_CHUNKS, d_model), lambda i: (i // per_b, 0, 0)),
            pl.BlockSpec((tm, LANES), row),
            _resident((1, d_model)),
            pl.BlockSpec(memory_space=pl.ANY),
        ],
        out_specs=pl.BlockSpec((tm, d_model), row),
        out_shape=jax.ShapeDtypeStruct((t, d_model), F32),
        scratch_shapes=[pltpu.VMEM((2, TOP_K * tm, d_model), F32), pltpu.SemaphoreType.DMA((2,))],
        compiler_params=_params(1),
        name="moe_combine",
    )(dest, dest, x, mod, route, final_gain, yb)


def _moe(x, h, route, expert_ids, mod, final_gain, w1, w3, w2, expert_base, seq, final):
    t, d_model = x.shape
    bm = MOE_BLOCK
    e_flat = expert_ids[:TOP_K].astype(jnp.int32).reshape(1, -1)
    n_assign = e_flat.shape[1]
    onehot = e_flat == jnp.arange(N_EXPERTS, dtype=jnp.int32)[:, None]
    csum = jnp.cumsum(onehot.astype(jnp.int32), axis=1)
    counts = csum[:, -1]
    padded = (counts + bm - 1) // bm * bm
    pends = jnp.cumsum(padded)
    row_of = (pends - padded)[:, None] + csum - 1
    dest = jnp.sum(jnp.where(onehot, row_of, 0), axis=0)
    nb = n_assign // bm + N_EXPERTS
    row_tok = jnp.zeros((nb * bm,), jnp.int32).at[dest].set(jnp.arange(n_assign, dtype=jnp.int32) % t)
    block_start = jnp.arange(nb, dtype=jnp.int32) * bm
    block_expert = jnp.minimum(
        jnp.sum((pends[None, :] <= block_start[:, None]).astype(jnp.int32), axis=1), N_EXPERTS - 1)
    nact = (pends[-1:] // bm).astype(jnp.int32)

    yb = _moe_expert(block_expert, nact, row_tok.reshape(nb, 1, bm), h, w1, w3, w2, expert_base)
    tmc = COMBINE_TILE
    dest_tiles = dest.reshape(TOP_K, t // tmc, tmc).transpose(1, 0, 2).reshape(t // tmc, 1, TOP_K * tmc)
    return _moe_combine(dest_tiles, x, mod, route, final_gain, yb, seq, final)


def _rope_tables(positions):
    inv_freq = ROPE_THETA ** (-jnp.arange(0, ROPE_DIM, 2, dtype=F32) / ROPE_DIM)
    ang = positions.astype(F32)[..., None] * inv_freq
    cos, sin = jnp.cos(ang), jnp.sin(ang)
    reps = LANES // ROPE_DIM
    cos_t = jnp.tile(jnp.concatenate([cos, cos], axis=-1), (1, 1, reps))
    sin_t = jnp.tile(jnp.concatenate([-sin, sin], axis=-1), (1, 1, reps))
    return cos_t, sin_t


def _split_w_in_kernel(wt_ref, mla_ref, dil_ref, ret_ref, gate_ref):
    n_cols = wt_ref.shape[1]
    main = (n_cols // LANES) * LANES
    w = wt_ref[0, :main, :].T
    if main < n_cols:
        last = wt_ref[0, n_cols - LANES:, :].T
        w = jnp.concatenate([w, last[:, LANES - (n_cols - main):]], axis=1)
    d_model = gate_ref.shape[2] // 4
    n_mla = MLA_Q_RANK + MLA_KV_RANK
    kr = w[:, n_mla:n_mla + MLA_ROPE]
    mla_ref[0] = jnp.concatenate([w[:, :n_mla], kr, kr], axis=1).astype(BF16)
    tail = w[:, n_mla + MLA_ROPE:]
    dil_w = 3 * len(DIL_PAIRS) * DIL_GROUP_W
    gw = DIL_GROUP_W
    n_g = len(DIL_PAIRS)
    dil_ref[0] = jnp.concatenate(
        [tail[:, (m * n_g + g) * gw:(m * n_g + g + 1) * gw] for g in range(n_g) for m in range(3)],
        axis=1).astype(BF16)
    ret_w = 2 * RET_QK_W + RET_V_W
    ret_ref[0] = tail[:, dil_w:dil_w + ret_w].astype(BF16)
    gate_ref[0] = tail[:, dil_w + ret_w:dil_w + ret_w + RET_V_W + 3 * d_model].astype(BF16)


def _split_w_in(w_in):
    depth, d_model, n_cols = w_in.shape
    rb = LANES
    widths = (MLA_Q_RANK + MLA_KV_RANK + 2 * MLA_ROPE, 3 * len(DIL_PAIRS) * DIL_GROUP_W,
              2 * RET_QK_W + RET_V_W, RET_V_W + 3 * d_model)
    assert sum(widths) - MLA_ROPE == n_cols
    return pl.pallas_call(
        _split_w_in_kernel,
        grid=(depth, d_model // rb),
        in_specs=[pl.BlockSpec((1, n_cols, rb), lambda l, i: (l, 0, i))],
        out_specs=[pl.BlockSpec((1, rb, n), lambda l, i: (l, i, 0)) for n in widths],
        out_shape=[jax.ShapeDtypeStruct((depth, d_model, n), BF16) for n in widths],
        compiler_params=_params(2),
        name="split_w_in",
    )(jnp.swapaxes(w_in, 1, 2))


def _split_w_uq(w_uq):
    depth = w_uq.shape[0]
    w = w_uq.astype(BF16).reshape(depth, MLA_Q_RANK, MLA_HEADS, MLA_QK)
    nope = w[..., :MLA_NOPE].reshape(depth, MLA_Q_RANK, MLA_HEADS * MLA_NOPE)
    rope = w[..., MLA_NOPE:].reshape(depth, MLA_Q_RANK, MLA_HEADS * MLA_ROPE)
    return jnp.concatenate([nope, rope], axis=-1)


def kernel(x, c, positions, ada_w, ada_b, norm_mix, norm_ffn, w_in, mla_q_norm, mla_w_uq, mla_kv_norm,
           mla_w_ukv, ret_log_decay, ret_norm, w_br_mla, w_br_dil, w_br_ret, w_out, ffn_w1, ffn_w3, ffn_w2,
           moe_router, moe_w1, moe_w3, moe_w2, final_norm):
    batch, seq, d_model = x.shape
    depth = ada_w.shape[0]
    cos_t, sin_t = _rope_tables(positions)
    mod_all = _ada(c, ada_w, ada_b)
    assert depth >= 1
    fgain = final_norm.reshape(1, d_model)
    xt = x.reshape(batch * seq, d_model)
    moe_w1_b = moe_w1.astype(BF16).reshape((-1,) + moe_w1.shape[2:])
    moe_w3_b = moe_w3.astype(BF16).reshape((-1,) + moe_w3.shape[2:])
    moe_w2_b = moe_w2.astype(BF16).reshape((-1,) + moe_w2.shape[2:])
    w_mla, w_dil, w_ret, w_gate = _split_w_in(w_in)
    w_uq = _split_w_uq(mla_w_uq)
    w_ukv = mla_w_ukv.astype(BF16)
    wbm, wbd, wbr, wout = (w.astype(BF16) for w in (w_br_mla, w_br_dil, w_br_ret, w_out))
    ffn_w1_b, ffn_w3_b, ffn_w2_b = (w.astype(BF16) for w in (ffn_w1, ffn_w3, ffn_w2))
    for layer in range(depth):
        mod = mod_all[layer]
        gmix = norm_mix[layer].reshape(1, d_model)

        q, k, v = _mla_prep(xt, mod, gmix, w_mla, mla_q_norm[layer].reshape(1, -1),
                            mla_kv_norm[layer].reshape(1, -1), w_uq, w_ukv, cos_t, sin_t, layer, batch, seq)
        y_mla = _mla_attn(q, k, v)

        dil_qkv = _dil_prep(xt, mod, gmix, w_dil, cos_t, sin_t, layer, batch, seq)
        dil_outs = [_dil_attn(*dil_qkv[3 * g:3 * g + 3], window, dil)
                    for g, (window, dil) in enumerate(DIL_PAIRS)]

        rq, rk, rv = _ret_prep(xt, mod, gmix, w_ret, cos_t, sin_t, layer, seq)
        y_ret = _ret_scan(ret_log_decay[layer].astype(F32), rq, rk, rv,
                          ret_norm[layer].reshape(1, -1).astype(F32), batch, seq)

        gffn = norm_ffn[layer].reshape(1, d_model)
        i = layer // 2
        is_moe = layer % 2 == 1
        merged = _merge_out(xt, mod, gmix, w_gate, y_mla, y_ret, dil_outs, wbm, wbd, wbr, wout, gffn,
                            moe_router[i] if is_moe else jnp.zeros((d_model, N_EXPERTS), F32),
                            layer, batch, seq, is_moe)
        last = layer == depth - 1
        if is_moe:
            xt, h, route, expert_ids = merged
            xt = _moe(xt, h, route, expert_ids, mod, fgain, moe_w1_b, moe_w3_b, moe_w2_b, i * N_EXPERTS, seq,
                      last)
        else:
            xt = _ffn_dense(merged[0], mod, gffn, fgain, ffn_w1_b, ffn_w3_b, ffn_w2_b, i, seq, last)
    return xt.reshape(batch, seq, d_model)
```
